```python
import jax, jax.numpy as jnp
from jax import lax
import numpy as np

D_MODEL = 2048
BATCH = 1
SEQ = 8192
DEPTH = 4

LN_EPS = 1e-5
DN_ALPHA = (2 * DEPTH) ** 0.25
DN_BETA = (8 * DEPTH) ** -0.25

RW_HEADS = 16
RW_HEAD_DIM = 64
RW_WIDTH = RW_HEADS * RW_HEAD_DIM
RW_DECAY_RANK = 64
RW_A_RANK = 64
RW_GATE_RANK = 160
RW_GN_EPS = 64e-5

AT_GROUPS = ((128, 1), (512, 4), (2048, 16))
AT_HEADS_PER_GROUP = 6
AT_HEADS = AT_HEADS_PER_GROUP * len(AT_GROUPS)
AT_HEAD_DIM = 64
AT_WIDTH = AT_HEADS * AT_HEAD_DIM
AT_OUT_WIDTH = AT_HEADS_PER_GROUP * AT_HEAD_DIM
AT_BLOCK = 128
ROPE_DIMS = AT_HEAD_DIM // 4
ROPE_THETA = 500000.0

ML_HEADS = 8
ML_HEAD_DIM = 128
ML_WIDTH = ML_HEADS * ML_HEAD_DIM
ML_CONV = 4
ML_CHUNK = 64
ML_NORM_EPS = 1e-6

N_BRANCHES = 3
IN_SPLITS = (RW_WIDTH,) * 4 + (AT_WIDTH,) * 3 + (ML_WIDTH,) * 4 + (ML_HEADS,) * 2 + (N_BRANCHES * D_MODEL,)
IN_COLS = sum(IN_SPLITS)
BRANCH_ROWS = RW_WIDTH + AT_OUT_WIDTH + ML_WIDTH

MOE_GROUPS = 4
MOE_EXPERTS_PER_GROUP = 8
MOE_EXPERTS = MOE_GROUPS * MOE_EXPERTS_PER_GROUP
MOE_TOP_K = 2
MOE_FF = 512
MOE_BLOCK = 128

kernel_name = 'hybrid_rwkv7_dilated_mlstm_hmoe_deepnorm'


def layer_norm(x, g, b):
    xf = x.astype(jnp.float32)
    mu = xf.mean(-1, keepdims=True)
    var = jnp.mean(jnp.square(xf - mu), -1, keepdims=True)
    return ((xf - mu) * lax.rsqrt(var + LN_EPS) * g + b).astype(x.dtype)


def token_shift(u):
    return jnp.pad(u, ((0, 0), (1, 0), (0, 0)))[:, :-1]


def causal_conv(u, w):
    K, C = w.shape
    return lax.conv_general_dilated(u, w[:, None, :].astype(u.dtype), window_strides=(1,),
                                    padding=[(K - 1, 0)], dimension_numbers=('NWC', 'WIO', 'NWC'),
                                    feature_group_count=C)


def rope_partial(u, positions):
    half = ROPE_DIMS // 2
    inv_freq = ROPE_THETA ** (-jnp.arange(half, dtype=jnp.float32) * 2.0 / ROPE_DIMS)
    ang = positions.astype(jnp.float32)[..., None] * inv_freq
    cos = jnp.cos(ang)[:, :, None, :]
    sin = jnp.sin(ang)[:, :, None, :]
    u1 = u[..., :half].astype(jnp.float32)
    u2 = u[..., half:ROPE_DIMS].astype(jnp.float32)
    rot = jnp.concatenate([u1 * cos - u2 * sin, u2 * cos + u1 * sin], -1).astype(u.dtype)
    return jnp.concatenate([rot, u[..., ROPE_DIMS:]], -1)


def rwkv7_time_mix(r_in, k_in, v_in, z_in, mu, w0, w1, w2, a0, a1, a2, g1, g2, k_k, k_a, r_k, gn_g, gn_b):
    f32 = jnp.float32
    B, S, _ = r_in.shape
    H, N = RW_HEADS, RW_HEAD_DIM
    lerp = lambda u, m: u + (token_shift(u) - u) * m
    r, k, v = lerp(r_in, mu[0]), lerp(k_in, mu[1]), lerp(v_in, mu[2])
    z_diff = token_shift(z_in) - z_in
    xw, xa, xg = z_in + z_diff * mu[3], z_in + z_diff * mu[4], z_in + z_diff * mu[5]
    w_log = -jax.nn.softplus(-(w0 + jnp.tanh(xw @ w1) @ w2).astype(f32)) - 0.5
    decay = jnp.exp(-jnp.exp(w_log))
    a = jax.nn.sigmoid((a0 + (xa @ a1) @ a2).astype(f32))
    g = jax.nn.sigmoid(xg @ g1) @ g2
    heads = lambda u: u.astype(f32).reshape(B, S, H, N)
    kk = heads(k * k_k)
    kk = kk / jnp.maximum(jnp.linalg.norm(kk, axis=-1, keepdims=True), 1e-12)
    k_mod = k.astype(f32) * (1.0 + (a - 1.0) * k_a)
    r_h, w_h, k_h, v_h, a_h = heads(r), heads(decay), heads(k_mod), heads(v), heads(a)
    b_h = kk * a_h

    def step(state, inp):
        r_t, w_t, k_t, v_t, kk_t, b_t = inp
        sa = jnp.einsum('bhvk,bhk->bhv', state, kk_t)
        state = state * w_t[:, :, None, :] - sa[..., None] * b_t[:, :, None, :] + v_t[..., None] * k_t[:, :, None, :]
        return state, jnp.einsum('bhvk,bhk->bhv', state, r_t)

    xs = tuple(jnp.moveaxis(u, 1, 0) for u in (r_h, w_h, k_h, v_h, kk, b_h))
    _, out = lax.scan(step, jnp.zeros((B, H, N, N), f32), xs)
    out = jnp.moveaxis(out, 0, 1)
    mu_o = out.mean(-1, keepdims=True)
    var_o = jnp.mean(jnp.square(out - mu_o), -1, keepdims=True)
    out = ((out - mu_o) * lax.rsqrt(var_o + RW_GN_EPS)).reshape(B, S, RW_WIDTH) * gn_g + gn_b
    bonus = jnp.sum(r_h * k_h * r_k, -1, keepdims=True) * v_h
    out = out + bonus.reshape(B, S, RW_WIDTH)
    return (out * g).astype(r_in.dtype)


def dilated_window_attention(q, k, v, window, dilation):
    B, S, H, Dh = q.shape
    n_back = window // dilation
    L = S // dilation
    nb = -(-L // AT_BLOCK)
    Lp = nb * AT_BLOCK

    def to_blocks(u):
        u = u.reshape(B, L, dilation, H, Dh)
        u = jnp.pad(u, ((0, 0), (0, Lp - L), (0, 0), (0, 0), (0, 0)))
        return u.reshape(B, nb, AT_BLOCK, dilation, H, Dh)

    qb, kb, vb = to_blocks(q), to_blocks(k), to_blocks(v)
    prev = lambda u: jnp.pad(u, ((0, 0), (1, 0), (0, 0), (0, 0), (0, 0), (0, 0)))[:, :-1]
    kc = jnp.concatenate([prev(kb), kb], axis=2)
    vc = jnp.concatenate([prev(vb), vb], axis=2)
    s = jnp.einsum('bnidhe,bnjdhe->bndhij', qb, kc).astype(jnp.float32)
    i = jnp.arange(AT_BLOCK)[:, None]
    j = jnp.arange(2 * AT_BLOCK)[None, :]
    rel = i + AT_BLOCK - j
    allowed = (rel >= 0) & (rel <= n_back)
    first = (jnp.arange(nb) == 0)[:, None, None] & (j < AT_BLOCK)[None]
    mask = allowed[None] & ~first
    s = jnp.where(mask[None, :, None, None], s, -jnp.inf)
    m = s.max(-1, keepdims=True)
    p = jnp.exp(s - m)
    den = p.sum(-1, keepdims=True)
    o = jnp.einsum('bndhij,bnjdhe->bnidhe', (p / den).astype(v.dtype), vc)
    lse = (m + jnp.log(den))[..., 0]
    o = o.reshape(B, Lp, dilation, H, Dh)[:, :L].reshape(B, S, H, Dh)
    lse = jnp.transpose(lse, (0, 1, 4, 2, 3)).reshape(B, Lp, dilation, H)[:, :L].reshape(B, S, H)
    return o, lse


def mlstm_chunkwise(q, k, v, i_pre, f_pre):
    B, S, H, Dh = q.shape
    L = ML_CHUNK
    nc = S // L
    chunks = lambda u: jnp.moveaxis(u.reshape((B, nc, L) + u.shape[2:]), 3, 1)
    qc = chunks(q) * (Dh ** -0.5)
    kc, vc = chunks(k), chunks(v)
    lf = jax.nn.log_sigmoid(chunks(f_pre))
    li = chunks(i_pre)
    b = jnp.cumsum(lf, -1)
    b_last = b[..., -1]
    causal = jnp.tril(jnp.ones((L, L), bool))
    logD = jnp.where(causal, b[..., :, None] - b[..., None, :] + li[..., None, :], -jnp.inf)
    w_end = b_last[..., None] - b + li

    def step(carry, inp):
        C, n, m = carry
        k_c, v_c, w_c, bl = inp
        m_new = jnp.maximum(bl + m, w_c.max(-1))
        dec = jnp.exp(bl + m - m_new)
        wts = jnp.exp(w_c - m_new[..., None])
        C_new = dec[..., None, None] * C + jnp.einsum('bhj,bhjv,bhjk->bhvk', wts, v_c, k_c)
        n_new = dec[..., None] * n + jnp.einsum('bhj,bhjk->bhk', wts, k_c)
        return (C_new, n_new, m_new), (C, n, m)

    init = (jnp.zeros((B, H, Dh, Dh), jnp.float32), jnp.zeros((B, H, Dh), jnp.float32),
            jnp.full((B, H), -jnp.inf, jnp.float32))
    xs = (jnp.moveaxis(kc, 2, 0), jnp.moveaxis(vc, 2, 0), jnp.moveaxis(w_end, 2, 0), jnp.moveaxis(b_last, 2, 0))
    _, (C_st, n_st, m_st) = lax.scan(step, init, xs)
    C_st = jnp.moveaxis(C_st, 0, 2)
    n_st = jnp.moveaxis(n_st, 0, 2)
    m_st = jnp.moveaxis(m_st, 0, 2)
    a_log = b + m_st[..., None]
    m_s = jnp.maximum(a_log, logD.max(-1))
    inter_w = jnp.exp(a_log - m_s)
    qk = jnp.einsum('bhcsd,bhcjd->bhcsj', qc, kc) * jnp.exp(logD - m_s[..., None])
    num = inter_w[..., None] * jnp.einsum('bhcvk,bhcsk->bhcsv', C_st, qc) + jnp.einsum('bhcsj,bhcjv->bhcsv', qk, vc)
    den = inter_w * jnp.einsum('bhck,bhcsk->bhcs', n_st, qc) + qk.sum(-1)
    h = num / jnp.maximum(jnp.abs(den), jnp.exp(-m_s))[..., None]
    return jnp.moveaxis(h, 1, 3).reshape(B, S, H, Dh)


def hybrid_mixer(x, positions, w_in, rw_mu, rw_w0, rw_w1, rw_w2, rw_a0, rw_a1, rw_a2, rw_g1, rw_g2,
                 rw_kk, rw_ka, rw_rk, rw_gn_g, rw_gn_b, ml_conv, ml_gate_b, ml_ln_g, w_branch, w_out):
    f32 = jnp.float32
    B, S, _ = x.shape
    split_idx = tuple(int(c) for c in np.cumsum(IN_SPLITS)[:-1])
    (rw_r, rw_k, rw_v, rw_z, at_q, at_k, at_v, ml_q, ml_k, ml_v, ml_o, ml_i, ml_f,
     gate_pre) = jnp.split(x @ w_in, split_idx, axis=-1)

    y_rw = rwkv7_time_mix(rw_r, rw_k, rw_v, rw_z, rw_mu, rw_w0, rw_w1, rw_w2, rw_a0, rw_a1, rw_a2,
                          rw_g1, rw_g2, rw_kk, rw_ka, rw_rk, rw_gn_g, rw_gn_b)

    heads = lambda u: u.reshape(B, S, AT_HEADS, AT_HEAD_DIM)
    q = rope_partial(heads(at_q), positions) * (AT_HEAD_DIM ** -0.5)
    k = rope_partial(heads(at_k), positions)
    v = heads(at_v)
    outs, lses = [], []
    for g, (window, dilation) in enumerate(AT_GROUPS):
        sl = slice(g * AT_HEADS_PER_GROUP, (g + 1) * AT_HEADS_PER_GROUP)
        o, lse = dilated_window_attention(q[:, :, sl], k[:, :, sl], v[:, :, sl], window, dilation)
        outs.append(o)
        lses.append(lse)
    wts = jax.nn.softmax(jnp.stack(lses), axis=0)
    y_at = jnp.einsum('gbsh,gbshe->bshe', wts, jnp.stack(outs).astype(f32))
    y_at = y_at.reshape(B, S, AT_OUT_WIDTH).astype(x.dtype)

    qk = jax.nn.silu(causal_conv(jnp.concatenate([ml_q, ml_k], -1), ml_conv))
    mq, mk = jnp.split(qk, 2, axis=-1)
    mh = lambda u: u.astype(f32).reshape(B, S, ML_HEADS, ML_HEAD_DIM)
    h = mlstm_chunkwise(mh(mq), mh(mk), mh(ml_v), ml_i.astype(f32) + ml_gate_b[0], ml_f.astype(f32) + ml_gate_b[1])
    h = h * jax.nn.sigmoid(mh(ml_o))
    mu_h = h.mean(-1, keepdims=True)
    var_h = jnp.mean(jnp.square(h - mu_h), -1, keepdims=True)
    y_ml = (((h - mu_h) * lax.rsqrt(var_h + ML_NORM_EPS)).reshape(B, S, ML_WIDTH) * ml_ln_g).astype(x.dtype)

    gates = jax.nn.sigmoid(gate_pre.astype(f32)).reshape(B, S, N_BRANCHES, D_MODEL).astype(x.dtype)
    wb_rw, wb_at, wb_ml = jnp.split(w_branch, (RW_WIDTH, RW_WIDTH + AT_OUT_WIDTH), axis=0)
    y = gates[:, :, 0] * (y_rw @ wb_rw) + gates[:, :, 1] * (y_at @ wb_at) + gates[:, :, 2] * (y_ml @ wb_ml)
    return y @ w_out


def hierarchical_moe(x, w_group, b_group, w_expert, b_expert, w1, w3, w2):
    B, S, D = x.shape
    T = B * S
    A = T * MOE_TOP_K
    xt = x.reshape(T, D)
    rows = jnp.arange(T)
    g_logits = (xt @ w_group).astype(jnp.float32) + b_group
    g_prob = jax.nn.softmax(g_logits, -1)
    g_idx = jnp.argmax(g_logits, -1)
    g_w = g_prob[rows, g_idx][:, None]
    e_logits = ((xt @ w_expert).astype(jnp.float32) + b_expert).reshape(T, MOE_GROUPS, MOE_EXPERTS_PER_GROUP)
    top_l, top_i = lax.top_k(e_logits[rows, g_idx], MOE_TOP_K)
    top_w = jax.nn.softmax(top_l, -1) * g_w
    e_flat = (g_idx[:, None] * MOE_EXPERTS_PER_GROUP + top_i).reshape(A)
    order = jnp.argsort(e_flat)
    e_sorted = e_flat[order]
    counts = jnp.bincount(e_flat, length=MOE_EXPERTS)
    starts = jnp.cumsum(counts) - counts
    pcounts = (counts + MOE_BLOCK - 1) // MOE_BLOCK * MOE_BLOCK
    pends = jnp.cumsum(pcounts)
    pstarts = pends - pcounts
    pos_sorted = pstarts[e_sorted] + (jnp.arange(A) - starts[e_sorted])
    pos = jnp.zeros((A,), jnp.int32).at[order].set(pos_sorted.astype(jnp.int32))
    P = A + MOE_EXPERTS * MOE_BLOCK
    nblk = P // MOE_BLOCK
    buf = jnp.zeros((P, D), x.dtype).at[pos].set(xt[jnp.arange(A) // MOE_TOP_K])
    blk_expert = jnp.clip(jnp.searchsorted(pends, jnp.arange(nblk) * MOE_BLOCK, side='right'), 0, MOE_EXPERTS - 1)

    def expert_block(args):
        xb, e = args
        return (jax.nn.silu(xb @ w1[e]) * (xb @ w3[e])) @ w2[e]

    yb = lax.map(expert_block, (buf.reshape(nblk, MOE_BLOCK, D), blk_expert))
    y = yb.reshape(P, D)[pos].reshape(T, MOE_TOP_K, D)
    return jnp.einsum('tkd,tk->td', y, top_w.astype(y.dtype)).reshape(B, S, D)


def setup_inputs(seed: int = 0) -> dict:
    key = jax.random.key(seed)
    ks = iter(jax.random.split(key, 48))
    L, D = DEPTH, D_MODEL
    f32 = jnp.float32
    nrm = lambda shape, scale: jax.random.normal(next(ks), shape, f32) * scale
    unif = lambda shape, lo, hi: jax.random.uniform(next(ks), shape, f32, lo, hi)
    x = nrm((BATCH, SEQ, D), 1.0)
    positions = jax.random.randint(next(ks), (BATCH, 1), 0, 4096, jnp.int32) + jnp.arange(SEQ, dtype=jnp.int32)[None]
    w_in = nrm((L, D, IN_COLS), D ** -0.5)
    rw_mu = unif((L, 6, RW_WIDTH), 0.0, 1.0)
    rw_w0 = unif((L, RW_WIDTH), -4.0, 0.0)
    rw_w1 = nrm((L, RW_WIDTH, RW_DECAY_RANK), RW_WIDTH ** -0.5)
    rw_w2 = nrm((L, RW_DECAY_RANK, RW_WIDTH), 0.1 * RW_DECAY_RANK ** -0.5)
    rw_a0 = nrm((L, RW_WIDTH), 0.1)
    rw_a1 = nrm((L, RW_WIDTH, RW_A_RANK), RW_WIDTH ** -0.5)
    rw_a2 = nrm((L, RW_A_RANK, RW_WIDTH), 0.1 * RW_A_RANK ** -0.5)
    rw_g1 = nrm((L, RW_WIDTH, RW_GATE_RANK), RW_WIDTH ** -0.5)
    rw_g2 = nrm((L, RW_GATE_RANK, RW_WIDTH), RW_GATE_RANK ** -0.5)
    rw_kk = 0.85 + nrm((L, RW_WIDTH), 0.05)
    rw_ka = 1.0 + nrm((L, RW_WIDTH), 0.05)
    rw_rk = nrm((L, RW_HEADS, RW_HEAD_DIM), 0.1)
    rw_gn_g = 1.0 + nrm((L, RW_WIDTH), 0.05)
    rw_gn_b = nrm((L, RW_WIDTH), 0.02)
    ml_conv = nrm((L, ML_CONV, 2 * ML_WIDTH), ML_CONV ** -0.5)
    ml_gate_b = jnp.stack([nrm((L, ML_HEADS), 0.1),
                           jnp.linspace(3.0, 6.0, ML_HEADS, dtype=f32)[None] + nrm((L, ML_HEADS), 0.1)], axis=1)
    ml_ln_g = 1.0 + nrm((L, ML_WIDTH), 0.05)
    w_branch = jnp.concatenate([nrm((L, RW_WIDTH, D), DN_BETA * RW_WIDTH ** -0.5),
                                nrm((L, AT_OUT_WIDTH, D), DN_BETA * AT_OUT_WIDTH ** -0.5),
                                nrm((L, ML_WIDTH, D), DN_BETA * ML_WIDTH ** -0.5)], axis=1)
    w_out = nrm((L, D, D), DN_BETA * D ** -0.5)
    ln1_g = 1.0 + nrm((L, D), 0.05)
    ln1_b = nrm((L, D), 0.02)
    moe_w_group = nrm((L, D, MOE_GROUPS), D ** -0.5)
    moe_b_group = nrm((L, MOE_GROUPS), 0.01)
    moe_w_expert = nrm((L, D, MOE_EXPERTS), D ** -0.5)
    moe_b_expert = nrm((L, MOE_EXPERTS), 0.01)
    moe_w1 = nrm((L, MOE_EXPERTS, D, MOE_FF), D ** -0.5)
    moe_w3 = nrm((L, MOE_EXPERTS, D, MOE_FF), D ** -0.5)
    moe_w2 = nrm((L, MOE_EXPERTS, MOE_FF, D), DN_BETA * MOE_FF ** -0.5)
    ln2_g = 1.0 + nrm((L, D), 0.05)
    ln2_b = nrm((L, D), 0.02)
    return {'x': x, 'positions': positions, 'w_in': w_in, 'rw_mu': rw_mu, 'rw_w0': rw_w0, 'rw_w1': rw_w1,
            'rw_w2': rw_w2, 'rw_a0': rw_a0, 'rw_a1': rw_a1, 'rw_a2': rw_a2, 'rw_g1': rw_g1, 'rw_g2': rw_g2,
            'rw_kk': rw_kk, 'rw_ka': rw_ka, 'rw_rk': rw_rk, 'rw_gn_g': rw_gn_g, 'rw_gn_b': rw_gn_b,
            'ml_conv': ml_conv, 'ml_gate_b': ml_gate_b, 'ml_ln_g': ml_ln_g, 'w_branch': w_branch,
            'w_out': w_out, 'ln1_g': ln1_g, 'ln1_b': ln1_b, 'moe_w_group': moe_w_group,
            'moe_b_group': moe_b_group, 'moe_w_expert': moe_w_expert, 'moe_b_expert': moe_b_expert,
            'moe_w1': moe_w1, 'moe_w3': moe_w3, 'moe_w2': moe_w2, 'ln2_g': ln2_g, 'ln2_b': ln2_b}


def reference(x, positions, w_in, rw_mu, rw_w0, rw_w1, rw_w2, rw_a0, rw_a1, rw_a2, rw_g1, rw_g2,
              rw_kk, rw_ka, rw_rk, rw_gn_g, rw_gn_b, ml_conv, ml_gate_b, ml_ln_g, w_branch, w_out,
              ln1_g, ln1_b, moe_w_group, moe_b_group, moe_w_expert, moe_b_expert, moe_w1, moe_w3, moe_w2,
              ln2_g, ln2_b):
    for l in range(DEPTH):
        h = hybrid_mixer(x, positions, w_in[l], rw_mu[l], rw_w0[l], rw_w1[l], rw_w2[l], rw_a0[l], rw_a1[l],
                         rw_a2[l], rw_g1[l], rw_g2[l], rw_kk[l], rw_ka[l], rw_rk[l], rw_gn_g[l], rw_gn_b[l],
                         ml_conv[l], ml_gate_b[l], ml_ln_g[l], w_branch[l], w_out[l])
        x = layer_norm(DN_ALPHA * x + h, ln1_g[l], ln1_b[l])
        h = hierarchical_moe(x, moe_w_group[l], moe_b_group[l], moe_w_expert[l], moe_b_expert[l],
                             moe_w1[l], moe_w3[l], moe_w2[l])
        x = layer_norm(DN_ALPHA * x + h, ln2_g[l], ln2_b[l])
    return x
```

```python
import functools

import jax
import jax.numpy as jnp
from jax import lax
from jax.experimental import pallas as pl
from jax.experimental.pallas import tpu as pltpu

F32 = jnp.float32
BF16 = jnp.bfloat16

D_MODEL = 2048
DEPTH = 4
LN_EPS = 1e-5
DN_ALPHA = (2 * DEPTH) ** 0.25

RW_HEADS = 16
RW_HEAD_DIM = 64
RW_WIDTH = RW_HEADS * RW_HEAD_DIM
RW_GN_EPS = 64e-5

AT_GROUPS = ((128, 1), (512, 4), (2048, 16))
AT_HEADS_PER_GROUP = 6
AT_HEADS = AT_HEADS_PER_GROUP * len(AT_GROUPS)
AT_HEAD_DIM = 64
AT_WIDTH = AT_HEADS * AT_HEAD_DIM
AT_OUT_WIDTH = AT_HEADS_PER_GROUP * AT_HEAD_DIM
AT_BLOCK = 128
ROPE_DIMS = AT_HEAD_DIM // 4
ROPE_THETA = 500000.0

ML_HEADS = 8
ML_HEAD_DIM = 128
ML_WIDTH = ML_HEADS * ML_HEAD_DIM
ML_CHUNK = 64
ML_NORM_EPS = 1e-6

N_BRANCHES = 3
COL_RW = 0
COL_AT = 4 * RW_WIDTH
COL_ML = COL_AT + 3 * AT_WIDTH
COL_IF = COL_ML + 4 * ML_WIDTH
COL_GATE = COL_IF + 2 * ML_HEADS
IN_COLS = COL_GATE + N_BRANCHES * D_MODEL

MOE_GROUPS = 4
MOE_EXPERTS_PER_GROUP = 8
MOE_EXPERTS = MOE_GROUPS * MOE_EXPERTS_PER_GROUP
MOE_TOP_K = 2
MOE_FF = 512
MOE_ROWS = 256

LANES = 128
VMEM_LIMIT = 56 * 1024 * 1024


def _params(*sem):
    return pltpu.CompilerParams(dimension_semantics=sem, vmem_limit_bytes=VMEM_LIMIT)


def _mm_kernel(a_ref, w_ref, o_ref, wb_ref, *, precision):
    @pl.when(pl.program_id(1) == 0)
    def _():
        wb_ref[...] = w_ref[...].astype(wb_ref.dtype)

    o_ref[...] = jnp.dot(a_ref[...], wb_ref[...], preferred_element_type=F32,
                         precision=precision).astype(o_ref.dtype)


def _mm(a, w, *, layer=None, col0=0, n_cols=None, tm=1024, tn=None, out_dtype=F32, exact=False, name="mm"):
    m, k = a.shape
    n_total = w.shape[-1]
    n_cols = n_total - col0 if n_cols is None else n_cols
    tn = n_cols if tn is None else tn
    tm = min(tm, m)
    assert m % tm == 0 and n_cols % tn == 0 and col0 % tn == 0
    cb0 = col0 // tn
    if layer is None:
        w_spec = pl.BlockSpec((k, tn), lambda j, i: (0, cb0 + j))
    else:
        w_spec = pl.BlockSpec((None, k, tn), lambda j, i: (layer, 0, cb0 + j))
    op_dtype = F32 if exact else BF16
    return pl.pallas_call(
        functools.partial(_mm_kernel, precision=lax.Precision.HIGHEST if exact else None),
        grid=(n_cols // tn, m // tm),
        in_specs=[pl.BlockSpec((tm, k), lambda j, i: (i, 0)), w_spec],
        out_specs=pl.BlockSpec((tm, tn), lambda j, i: (i, j)),
        out_shape=jax.ShapeDtypeStruct((m, n_cols), out_dtype),
        scratch_shapes=[pltpu.VMEM((k, tn), op_dtype)],
        compiler_params=_params("arbitrary", "arbitrary"),
        name=name,
    )(a.astype(op_dtype), w)


def _add_ln_kernel(x_ref, h_ref, g_ref, b_ref, o_ref, ob_ref):
    y = DN_ALPHA * x_ref[...] + h_ref[...]
    mu = jnp.mean(y, axis=-1, keepdims=True)
    d = y - mu
    var = jnp.mean(d * d, axis=-1, keepdims=True)
    out = d * lax.rsqrt(var + LN_EPS) * g_ref[...] + b_ref[...]
    o_ref[...] = out
    ob_ref[...] = out.astype(BF16)


def _add_ln(x, h, g, b, tm=256):
    m, d = x.shape
    row = pl.BlockSpec((tm, d), lambda i: (i, 0))
    vec = pl.BlockSpec((1, d), lambda i: (0, 0))
    return pl.pallas_call(
        _add_ln_kernel,
        grid=(m // tm,),
        in_specs=[row, row, vec, vec],
        out_specs=[row, row],
        out_shape=[jax.ShapeDtypeStruct((m, d), F32), jax.ShapeDtypeStruct((m, d), BF16)],
        compiler_params=_params("arbitrary"),
        name="add_ln",
    )(x, h, g.reshape(1, d), b.reshape(1, d))


RW_TB = 128


def _rwkv_kernel(r_ref, w_ref, k_ref, kk_ref, b_ref, vt_ref, o_ref, s_ref):
    @pl.when(pl.program_id(0) == 0)
    def _():
        s_ref[...] = jnp.zeros_like(s_ref)

    lane = lax.broadcasted_iota(jnp.int32, (RW_HEAD_DIM, RW_TB), 1)

    def step(t, carry):
        onehot = lane == t
        row = pl.ds(t, 1)
        for h in range(RW_HEADS):
            state = s_ref[h]
            sa = jnp.sum(state * kk_ref[h, row, :], axis=1, keepdims=True)
            v_col = jnp.sum(jnp.where(onehot, vt_ref[h], 0.0), axis=1, keepdims=True)
            state = state * w_ref[h, row, :] - sa * b_ref[h, row, :] + v_col * k_ref[h, row, :]
            s_ref[h] = state
            out = jnp.sum(state * r_ref[h, row, :], axis=1, keepdims=True)
            o_ref[h] = jnp.where(onehot, out, o_ref[h])
        return carry

    lax.fori_loop(0, RW_TB, step, 0)


def _rwkv_scan(r, w, k, kk, b, vt):
    h, s, n = r.shape
    rows = pl.BlockSpec((h, RW_TB, n), lambda i: (0, i, 0))
    cols = pl.BlockSpec((h, n, RW_TB), lambda i: (0, 0, i))
    return pl.pallas_call(
        _rwkv_kernel,
        grid=(s // RW_TB,),
        in_specs=[rows] * 5 + [cols],
        out_specs=cols,
        out_shape=jax.ShapeDtypeStruct((h, n, s), F32),
        scratch_shapes=[pltpu.VMEM((h, n, n), F32)],
        compiler_params=_params("arbitrary"),
        name="rwkv_scan",
    )(r, w, k, kk, b, vt)


def _token_shift(u):
    return jnp.pad(u, ((1, 0), (0, 0)))[:-1]


def _rwkv7(rw, l, p):
    s = rw.shape[0]
    r_in, k_in, v_in, z_in = (rw[:, i * RW_WIDTH:(i + 1) * RW_WIDTH] for i in range(4))
    mu = p['rw_mu'][l]
    lerp = lambda u, m: u + (_token_shift(u) - u) * m
    r, k, v = lerp(r_in, mu[0]), lerp(k_in, mu[1]), lerp(v_in, mu[2])
    z_diff = _token_shift(z_in) - z_in
    xw, xa, xg = z_in + z_diff * mu[3], z_in + z_diff * mu[4], z_in + z_diff * mu[5]
    w_lr = _mm(jnp.tanh(_mm(xw, p['rw_w1'], layer=l, name="mm_rw_lora_in")), p['rw_w2'], layer=l,
               name="mm_rw_lora_out")
    w_log = -jax.nn.softplus(-(p['rw_w0'][l] + w_lr)) - 0.5
    decay = jnp.exp(-jnp.exp(w_log))
    a = jax.nn.sigmoid(p['rw_a0'][l] + _mm(_mm(xa, p['rw_a1'], layer=l, name="mm_rw_lora_in"), p['rw_a2'],
                                           layer=l, name="mm_rw_lora_out"))
    g = _mm(jax.nn.sigmoid(_mm(xg, p['rw_g1'], layer=l, name="mm_rw_lora_in")), p['rw_g2'], layer=l,
            name="mm_rw_lora_out")
    heads = lambda u: u.reshape(s, RW_HEADS, RW_HEAD_DIM)
    kk = heads(k * p['rw_kk'][l])
    kk = kk / jnp.maximum(jnp.sqrt(jnp.sum(kk * kk, axis=-1, keepdims=True)), 1e-12)
    k_mod = k * (1.0 + (a - 1.0) * p['rw_ka'][l])
    r_h, w_h, k_h, v_h, a_h = heads(r), heads(decay), heads(k_mod), heads(v), heads(a)
    b_h = kk * a_h
    hsn = lambda u: jnp.transpose(u, (1, 0, 2))
    out_t = _rwkv_scan(hsn(r_h), hsn(w_h), hsn(k_h), hsn(kk), hsn(b_h), jnp.transpose(v_h, (1, 2, 0)))
    out = jnp.transpose(out_t, (2, 0, 1))
    mu_o = out.mean(-1, keepdims=True)
    var_o = jnp.mean(jnp.square(out - mu_o), -1, keepdims=True)
    out = ((out - mu_o) * lax.rsqrt(var_o + RW_GN_EPS)).reshape(s, RW_WIDTH) * p['rw_gn_g'][l] + p['rw_gn_b'][l]
    bonus = jnp.sum(r_h * k_h * p['rw_rk'][l], -1, keepdims=True) * v_h
    return (out + bonus.reshape(s, RW_WIDTH)) * g


def _attn_kernel(q_ref, kp_ref, kc_ref, vp_ref, vc_ref, o_ref, lse_ref):
    n = pl.program_id(1)
    qi = lax.broadcasted_iota(jnp.int32, (AT_BLOCK, AT_BLOCK), 0)
    kj = lax.broadcasted_iota(jnp.int32, (AT_BLOCK, AT_BLOCK), 1)
    mask_c = kj <= qi
    mask_p = (kj >= qi) & (n > 0)
    nt = (((1,), (1,)), ((), ()))
    outs = []
    lse_all = jnp.zeros((AT_BLOCK, LANES), F32)
    for h in range(AT_HEADS_PER_GROUP):
        sl = slice(h * AT_HEAD_DIM, (h + 1) * AT_HEAD_DIM)
        q = q_ref[:, sl]
        s_c = jnp.where(mask_c, lax.dot_general(q, kc_ref[:, sl], nt, preferred_element_type=F32), -jnp.inf)
        s_p = jnp.where(mask_p, lax.dot_general(q, kp_ref[:, sl], nt, preferred_element_type=F32), -jnp.inf)
        m = jnp.maximum(jnp.max(s_c, axis=-1, keepdims=True), jnp.max(s_p, axis=-1, keepdims=True))
        p_c = jnp.exp(s_c - m)
        p_p = jnp.exp(s_p - m)
        den = jnp.sum(p_c, axis=-1, keepdims=True) + jnp.sum(p_p, axis=-1, keepdims=True)
        o = (jnp.dot((p_c / den).astype(BF16), vc_ref[:, sl], preferred_element_type=F32)
             + jnp.dot((p_p / den).astype(BF16), vp_ref[:, sl], preferred_element_type=F32))
        outs.append(o)
        lse_all = jnp.where(kj == h, m + jnp.log(den), lse_all)
    o_ref[...] = jnp.concatenate(outs, axis=-1)
    lse_ref[...] = lse_all


def _dilated_attention(q, k, v, g, dilation):
    s = q.shape[0]
    length = s // dilation
    nb = length // AT_BLOCK
    assert nb * AT_BLOCK * dilation == s
    per_row = AT_WIDTH // AT_OUT_WIDTH
    view = lambda u: u.reshape(length, dilation * AT_WIDTH)
    cur = pl.BlockSpec((AT_BLOCK, AT_OUT_WIDTH), lambda r, n: (n, r * per_row + g))
    prev = pl.BlockSpec((AT_BLOCK, AT_OUT_WIDTH), lambda r, n: (jnp.maximum(n - 1, 0), r * per_row + g))
    o, lse = pl.pallas_call(
        _attn_kernel,
        grid=(dilation, nb),
        in_specs=[cur, prev, cur, prev, cur],
        out_specs=[pl.BlockSpec((AT_BLOCK, AT_OUT_WIDTH), lambda r, n: (n, r)),
                   pl.BlockSpec((AT_BLOCK, LANES), lambda r, n: (n, r))],
        out_shape=[jax.ShapeDtypeStruct((length, dilation * AT_OUT_WIDTH), F32),
                   jax.ShapeDtypeStruct((length, dilation * LANES), F32)],
        compiler_params=_params("arbitrary", "arbitrary"),
        name=f"attn_d{dilation}",
    )(view(q), view(k), view(k), view(v), view(v))
    return o.reshape(s, AT_OUT_WIDTH), lse.reshape(s, LANES)


def _rope(u, cos, sin):
    half = ROPE_DIMS // 2
    u1, u2 = u[..., :half], u[..., half:ROPE_DIMS]
    return jnp.concatenate([u1 * cos - u2 * sin, u2 * cos + u1 * sin, u[..., ROPE_DIMS:]], -1)


def _attention(at, positions):
    s = at.shape[0]
    half = ROPE_DIMS // 2
    inv_freq = ROPE_THETA ** (-jnp.arange(half, dtype=F32) * 2.0 / ROPE_DIMS)
    ang = positions.astype(F32).reshape(s, 1, 1) * inv_freq
    cos, sin = jnp.cos(ang), jnp.sin(ang)
    heads = lambda u: u.reshape(s, AT_HEADS, AT_HEAD_DIM)
    q = (_rope(heads(at[:, :AT_WIDTH]), cos, sin) * (AT_HEAD_DIM ** -0.5)).reshape(s, AT_WIDTH).astype(BF16)
    k = _rope(heads(at[:, AT_WIDTH:2 * AT_WIDTH]), cos, sin).reshape(s, AT_WIDTH).astype(BF16)
    v = at[:, 2 * AT_WIDTH:].astype(BF16)
    outs, lses = [], []
    for g, (_, dilation) in enumerate(AT_GROUPS):
        o, lse = _dilated_attention(q, k, v, g, dilation)
        outs.append(o.reshape(s, AT_HEADS_PER_GROUP, AT_HEAD_DIM))
        lses.append(lse[:, :AT_HEADS_PER_GROUP])
    wts = jax.nn.softmax(jnp.stack(lses), axis=0)
    y = jnp.sum(wts[..., None] * jnp.stack(outs), axis=0)
    return y.reshape(s, AT_OUT_WIDTH)


ML_CHUNKS_PER_STEP = 8
ML_ROWS = ML_CHUNKS_PER_STEP * ML_CHUNK


def _mlstm_kernel(q_ref, k_ref, v_ref, og_ref, bcol_ref, icol_ref, brow_ref, irow_ref, g_ref, y_ref,
                  ct_ref, n_ref, m_ref):
    @pl.when(pl.program_id(0) == 0)
    def _():
        ct_ref[...] = jnp.zeros_like(ct_ref)
        n_ref[...] = jnp.zeros_like(n_ref)
        m_ref[...] = jnp.full_like(m_ref, -jnp.inf)

    si = lax.broadcasted_iota(jnp.int32, (ML_CHUNK, ML_CHUNK), 0)
    ji = lax.broadcasted_iota(jnp.int32, (ML_CHUNK, ML_CHUNK), 1)
    causal = ji <= si
    nt = (((1,), (1,)), ((), ()))

    def chunk(c, carry):
        rows = pl.ds(pl.multiple_of(c * ML_CHUNK, ML_CHUNK), ML_CHUNK)
        for h in range(ML_HEADS):
            cols = slice(h * ML_HEAD_DIM, (h + 1) * ML_HEAD_DIM)
            q = q_ref[rows, cols] * (ML_HEAD_DIM ** -0.5)
            k = k_ref[rows, cols]
            v = v_ref[rows, cols]
            b_col = bcol_ref[rows, h:h + 1]
            i_col = icol_ref[rows, h:h + 1]
            b_row = brow_ref[c, h:h + 1, :]
            i_row = irow_ref[c, h:h + 1, :]
            m_prev = m_ref[h, 0:1, 0:1]
            n_prev = n_ref[h, 0:1, :]
            ct_prev = ct_ref[h]
            b_last = b_col[ML_CHUNK - 1:ML_CHUNK, :]
            log_d = jnp.where(causal, b_col - b_row + i_row, -jnp.inf)
            a_log = b_col + m_prev
            m_s = jnp.maximum(a_log, jnp.max(log_d, axis=-1, keepdims=True))
            inter_w = jnp.exp(a_log - m_s)
            qb, kb, vb = q.astype(BF16), k.astype(BF16), v.astype(BF16)
            qk = lax.dot_general(qb, kb, nt, preferred_element_type=F32) * jnp.exp(log_d - m_s)
            num = (inter_w * jnp.dot(qb, ct_prev.astype(BF16), preferred_element_type=F32)
                   + jnp.dot(qk.astype(BF16), vb, preferred_element_type=F32))
            den = inter_w * jnp.sum(q * n_prev, axis=-1, keepdims=True) + jnp.sum(qk, axis=-1, keepdims=True)
            hid = num / jnp.maximum(jnp.abs(den), jnp.exp(-m_s))
            w_end = b_last - b_col + i_col
            m_new = jnp.maximum(b_last + m_prev, jnp.max(w_end, axis=0, keepdims=True))
            dec = jnp.exp(b_last + m_prev - m_new)
            wts = jnp.exp(w_end - m_new)
            ct_ref[h] = dec * ct_prev + jnp.dot(jnp.transpose(k).astype(BF16), (v * wts).astype(BF16),
                                                preferred_element_type=F32)
            n_ref[h, 0:1, :] = dec * n_prev + jnp.sum(k * wts, axis=0, keepdims=True)
            m_ref[h, 0:1, 0:1] = m_new
            hid = hid * jax.nn.sigmoid(og_ref[rows, cols])
            mu = jnp.mean(hid, axis=-1, keepdims=True)
            dev = hid - mu
            var = jnp.mean(dev * dev, axis=-1, keepdims=True)
            y_ref[rows, cols] = dev * lax.rsqrt(var + ML_NORM_EPS) * g_ref[:, cols]
        return carry

    lax.fori_loop(0, ML_CHUNKS_PER_STEP, chunk, 0)


def _mlstm(ml, gates_if, l, p):
    s = ml.shape[0]
    nc = s // ML_CHUNK
    u = ml[:, :2 * ML_WIDTH]
    wc = p['ml_conv'][l]
    taps = wc.shape[0]
    up = jnp.pad(u, ((taps - 1, 0), (0, 0)))
    conv = sum(up[i:i + s] * wc[i] for i in range(taps))
    qk = jax.nn.silu(conv)
    i_pre = gates_if[:, :ML_HEADS] + p['ml_gate_b'][l, 0]
    f_pre = gates_if[:, ML_HEADS:] + p['ml_gate_b'][l, 1]
    lf = jax.nn.log_sigmoid(f_pre)
    b_cum = jnp.cumsum(lf.reshape(nc, ML_CHUNK, ML_HEADS), axis=1)
    b_col = b_cum.reshape(s, ML_HEADS)
    b_row = jnp.transpose(b_cum, (0, 2, 1))
    i_row = jnp.transpose(i_pre.reshape(nc, ML_CHUNK, ML_HEADS), (0, 2, 1))
    col = lambda j: pl.BlockSpec((ML_ROWS, ML_WIDTH), lambda i: (i, j))
    gcol = pl.BlockSpec((ML_ROWS, ML_HEADS), lambda i: (i, 0))
    grow = pl.BlockSpec((ML_CHUNKS_PER_STEP, ML_HEADS, ML_CHUNK), lambda i: (i, 0, 0))
    return pl.pallas_call(
        _mlstm_kernel,
        grid=(s // ML_ROWS,),
        in_specs=[col(0), col(1), col(2), col(3), gcol, gcol, grow, grow,
                  pl.BlockSpec((1, ML_WIDTH), lambda i: (0, 0))],
        out_specs=pl.BlockSpec((ML_ROWS, ML_WIDTH), lambda i: (i, 0)),
        out_shape=jax.ShapeDtypeStruct((s, ML_WIDTH), F32),
        scratch_shapes=[pltpu.VMEM((ML_HEADS, ML_HEAD_DIM, ML_HEAD_DIM), F32),
                        pltpu.VMEM((ML_HEADS, 8, ML_HEAD_DIM), F32),
                        pltpu.VMEM((ML_HEADS, 8, LANES), F32)],
        compiler_params=_params("arbitrary"),
        name="mlstm",
    )(qk, qk, ml, ml, b_col, i_pre, b_row, i_row, p['ml_ln_g'][l].reshape(1, ML_WIDTH))


def _expert_kernel(be_ref, new_ref, used_ref, x_ref, w1_ref, w3_ref, w2_ref, o_ref, w1b, w3b, w2b):
    i = pl.program_id(0)

    @pl.when(new_ref[i] == 1)
    def _():
        w1b[...] = w1_ref[...].astype(BF16)
        w3b[...] = w3_ref[...].astype(BF16)
        w2b[...] = w2_ref[...].astype(BF16)

    @pl.when(used_ref[i] == 1)
    def _():
        x = x_ref[...]
        h1 = jnp.dot(x, w1b[...], preferred_element_type=F32)
        h3 = jnp.dot(x, w3b[...], preferred_element_type=F32)
        hid = (h1 * jax.nn.sigmoid(h1) * h3).astype(BF16)
        o_ref[...] = jnp.dot(hid, w2b[...], preferred_element_type=F32)

    @pl.when(used_ref[i] == 0)
    def _():
        o_ref[...] = jnp.zeros_like(o_ref)


def _experts(buf, blk_expert, blk_new, blk_used, w1, w3, w2, l):
    rows, d = buf.shape
    nblk = rows // MOE_ROWS
    up = pl.BlockSpec((None, None, d, MOE_FF), lambda i, be, nw, us: (l, be[i], 0, 0))
    down = pl.BlockSpec((None, None, MOE_FF, d), lambda i, be, nw, us: (l, be[i], 0, 0))
    blk = pl.BlockSpec((MOE_ROWS, d), lambda i, be, nw, us: (i, 0))
    return pl.pallas_call(
        _expert_kernel,
        grid_spec=pltpu.PrefetchScalarGridSpec(
            num_scalar_prefetch=3,
            grid=(nblk,),
            in_specs=[blk, up, up, down],
            out_specs=blk,
            scratch_shapes=[pltpu.VMEM((d, MOE_FF), BF16), pltpu.VMEM((d, MOE_FF), BF16),
                            pltpu.VMEM((MOE_FF, d), BF16)]),
        out_shape=jax.ShapeDtypeStruct((rows, d), F32),
        compiler_params=_params("arbitrary"),
        name="experts",
    )(blk_expert, blk_new, blk_used, buf, w1, w3, w2)


def _moe(x, xb, l, p):
    t, d = x.shape
    n_assign = t * MOE_TOP_K
    w_router = jnp.concatenate([p['moe_w_group'][l], p['moe_w_expert'][l]], axis=1)
    w_router = jnp.pad(w_router, ((0, 0), (0, LANES - w_router.shape[1])))
    logits = _mm(x, w_router, tm=512, exact=True, name="mm_router")
    g_logits = logits[:, :MOE_GROUPS] + p['moe_b_group'][l]
    g_prob = jax.nn.softmax(g_logits, -1)
    g_idx = jnp.argmax(g_logits, -1)
    g_w = jnp.take_along_axis(g_prob, g_idx[:, None], axis=1)
    e_logits = (logits[:, MOE_GROUPS:MOE_GROUPS + MOE_EXPERTS] + p['moe_b_expert'][l]).reshape(
        t, MOE_GROUPS, MOE_EXPERTS_PER_GROUP)
    e_sel = jnp.take_along_axis(e_logits, g_idx[:, None, None], axis=1)[:, 0]
    top_l, top_i = lax.top_k(e_sel, MOE_TOP_K)
    top_w = jax.nn.softmax(top_l, -1) * g_w
    e_flat = (g_idx[:, None] * MOE_EXPERTS_PER_GROUP + top_i).reshape(n_assign).astype(jnp.int32)
    onehot = (e_flat[:, None] == jnp.arange(MOE_EXPERTS, dtype=jnp.int32)[None]).astype(jnp.int32)
    rank = jnp.take_along_axis(jnp.cumsum(onehot, axis=0), e_flat[:, None], axis=1)[:, 0] - 1
    counts = jnp.sum(onehot, axis=0)
    pcounts = (counts + MOE_ROWS - 1) // MOE_ROWS * MOE_ROWS
    pends = jnp.cumsum(pcounts)
    pstarts = pends - pcounts
    pos = pstarts[e_flat] + rank
    n_rows = n_assign + MOE_EXPERTS * MOE_ROWS
    nblk = n_rows // MOE_ROWS
    src = jnp.zeros((n_rows,), jnp.int32).at[pos].set(jnp.arange(n_assign, dtype=jnp.int32) // MOE_TOP_K)
    blk_start = jnp.arange(nblk, dtype=jnp.int32) * MOE_ROWS
    blk_expert = jnp.clip(jnp.searchsorted(pends, blk_start, side='right'), 0, MOE_EXPERTS - 1).astype(jnp.int32)
    blk_used = (blk_start < pends[-1]).astype(jnp.int32)
    blk_new = jnp.concatenate([jnp.ones((1,), jnp.int32), (blk_expert[1:] != blk_expert[:-1]).astype(jnp.int32)])
    yb = _experts(xb[src], blk_expert, blk_new, blk_used, p['moe_w1'], p['moe_w3'], p['moe_w2'], l)
    y = yb[pos].reshape(t, MOE_TOP_K, d)
    return jnp.sum(y * top_w[..., None], axis=1)


def _mixer(xb, positions, l, p):
    w_in = p['w_in']
    proj = _mm(xb, w_in, layer=l, col0=0, n_cols=COL_IF, tn=896, name="mm_in")
    gates_if = _mm(xb, w_in[l, :, COL_IF:COL_GATE], name="mm_in_if")
    gate_pre = _mm(xb, w_in[l, :, COL_GATE:], tn=1024, name="mm_in_gate")
    y_rw = _rwkv7(proj[:, COL_RW:COL_AT], l, p)
    y_at = _attention(proj[:, COL_AT:COL_ML], positions)
    y_ml = _mlstm(proj[:, COL_ML:COL_IF], gates_if, l, p)
    gates = jax.nn.sigmoid(gate_pre)
    wb = p['w_branch'][l]
    y = (gates[:, :D_MODEL] * _mm(y_rw, wb[:RW_WIDTH], tn=1024, name="mm_branch_rw")
         + gates[:, D_MODEL:2 * D_MODEL] * _mm(y_at, wb[RW_WIDTH:RW_WIDTH + AT_OUT_WIDTH], tn=1024,
                                               name="mm_branch_at")
         + gates[:, 2 * D_MODEL:] * _mm(y_ml, wb[RW_WIDTH + AT_OUT_WIDTH:], tn=1024, name="mm_branch_ml"))
    return _mm(y, p['w_out'], layer=l, tn=1024, name="mm_out")


def kernel(x, positions, w_in, rw_mu, rw_w0, rw_w1, rw_w2, rw_a0, rw_a1, rw_a2, rw_g1, rw_g2, rw_kk, rw_ka, rw_rk, rw_gn_g, rw_gn_b, ml_conv, ml_gate_b, ml_ln_g, w_branch, w_out, ln1_g, ln1_b, moe_w_group, moe_b_group, moe_w_expert, moe_b_expert, moe_w1, moe_w3, moe_w2, ln2_g, ln2_b):
    p = dict(w_in=w_in, rw_mu=rw_mu, rw_w0=rw_w0, rw_w1=rw_w1, rw_w2=rw_w2, rw_a0=rw_a0, rw_a1=rw_a1,
             rw_a2=rw_a2, rw_g1=rw_g1, rw_g2=rw_g2, rw_kk=rw_kk, rw_ka=rw_ka, rw_rk=rw_rk, rw_gn_g=rw_gn_g,
             rw_gn_b=rw_gn_b, ml_conv=ml_conv, ml_gate_b=ml_gate_b, ml_ln_g=ml_ln_g, w_branch=w_branch,
             w_out=w_out, moe_w_group=moe_w_group, moe_b_group=moe_b_group, moe_w_expert=moe_w_expert,
             moe_b_expert=moe_b_expert, moe_w1=moe_w1, moe_w3=moe_w3, moe_w2=moe_w2)
    batch, seq, d = x.shape
    assert batch == 1
    xf = x.reshape(seq, d)
    xb = xf.astype(BF16)
    pos = positions.reshape(seq)
    for l in range(w_in.shape[0]):
        h = _mixer(xb, pos, l, p)
        xf, xb = _add_ln(xf, h, ln1_g[l], ln1_b[l])
        h = _moe(xf, xb, l, p)
        xf, xb = _add_ln(xf, h, ln2_g[l], ln2_b[l])
    return xf.reshape(batch, seq, d)
```

```python
import functools

import jax
import jax.numpy as jnp
from jax import lax
from jax.experimental import pallas as pl
from jax.experimental.pallas import tpu as pltpu

F32 = jnp.float32
BF16 = jnp.bfloat16

D_MODEL = 2048
DEPTH = 4
LN_EPS = 1e-5
DN_ALPHA = (2 * DEPTH) ** 0.25

RW_HEADS = 16
RW_HEAD_DIM = 64
RW_WIDTH = RW_HEADS * RW_HEAD_DIM
RW_GN_EPS = 64e-5

AT_GROUPS = ((128, 1), (512, 4), (2048, 16))
AT_HEADS_PER_GROUP = 6
AT_HEADS = AT_HEADS_PER_GROUP * len(AT_GROUPS)
AT_HEAD_DIM = 64
AT_WIDTH = AT_HEADS * AT_HEAD_DIM
AT_OUT_WIDTH = AT_HEADS_PER_GROUP * AT_HEAD_DIM
AT_BLOCK = 128
ROPE_DIMS = AT_HEAD_DIM // 4
ROPE_THETA = 500000.0

ML_HEADS = 8
ML_HEAD_DIM = 128
ML_WIDTH = ML_HEADS * ML_HEAD_DIM
ML_CHUNK = 64
ML_NORM_EPS = 1e-6

N_BRANCHES = 3
COL_RW = 0
COL_AT = 4 * RW_WIDTH
COL_ML = COL_AT + 3 * AT_WIDTH
COL_IF = COL_ML + 4 * ML_WIDTH
COL_GATE = COL_IF + 2 * ML_HEADS
IN_COLS = COL_GATE + N_BRANCHES * D_MODEL

MOE_GROUPS = 4
MOE_EXPERTS_PER_GROUP = 8
MOE_EXPERTS = MOE_GROUPS * MOE_EXPERTS_PER_GROUP
MOE_TOP_K = 2
MOE_FF = 512
MOE_ROWS = 256

LANES = 128
VMEM_LIMIT = 56 * 1024 * 1024


def _params(*sem):
    return pltpu.CompilerParams(dimension_semantics=sem, vmem_limit_bytes=VMEM_LIMIT)


def _mm_kernel(a_ref, w_ref, o_ref, wb_ref, *, precision):
    @pl.when(pl.program_id(1) == 0)
    def _():
        wb_ref[...] = w_ref[...].astype(wb_ref.dtype)

    o_ref[...] = jnp.dot(a_ref[...], wb_ref[...], preferred_element_type=F32,
                         precision=precision).astype(o_ref.dtype)


def _mm(a, w, *, layer=None, col0=0, n_cols=None, tm=1024, tn=None, out_dtype=F32, exact=False, name="mm"):
    m, k = a.shape
    n_total = w.shape[-1]
    n_cols = n_total - col0 if n_cols is None else n_cols
    tn = n_cols if tn is None else tn
    tm = min(tm, m)
    assert m % tm == 0 and n_cols % tn == 0 and col0 % tn == 0
    cb0 = col0 // tn
    if layer is None:
        w_spec = pl.BlockSpec((k, tn), lambda j, i: (0, cb0 + j))
    else:
        w_spec = pl.BlockSpec((None, k, tn), lambda j, i: (layer, 0, cb0 + j))
    op_dtype = F32 if exact else BF16
    return pl.pallas_call(
        functools.partial(_mm_kernel, precision=lax.Precision.HIGHEST if exact else None),
        grid=(n_cols // tn, m // tm),
        in_specs=[pl.BlockSpec((tm, k), lambda j, i: (i, 0)), w_spec],
        out_specs=pl.BlockSpec((tm, tn), lambda j, i: (i, j)),
        out_shape=jax.ShapeDtypeStruct((m, n_cols), out_dtype),
        scratch_shapes=[pltpu.VMEM((k, tn), op_dtype)],
        compiler_params=_params("arbitrary", "arbitrary"),
        name=name,
    )(a.astype(op_dtype), w)


def _add_ln_kernel(x_ref, h_ref, g_ref, b_ref, o_ref, ob_ref):
    y = DN_ALPHA * x_ref[...] + h_ref[...]
    mu = jnp.mean(y, axis=-1, keepdims=True)
    d = y - mu
    var = jnp.mean(d * d, axis=-1, keepdims=True)
    out = d * lax.rsqrt(var + LN_EPS) * g_ref[...] + b_ref[...]
    o_ref[...] = out
    ob_ref[...] = out.astype(BF16)


def _add_ln(x, h, g, b, tm=256):
    m, d = x.shape
    row = pl.BlockSpec((tm, d), lambda i: (i, 0))
    vec = pl.BlockSpec((1, d), lambda i: (0, 0))
    return pl.pallas_call(
        _add_ln_kernel,
        grid=(m // tm,),
        in_specs=[row, row, vec, vec],
        out_specs=[row, row],
        out_shape=[jax.ShapeDtypeStruct((m, d), F32), jax.ShapeDtypeStruct((m, d), BF16)],
        compiler_params=_params("arbitrary"),
        name="add_ln",
    )(x, h, g.reshape(1, d), b.reshape(1, d))


RW_CHUNK = 64
RW_PAIR = 2 * RW_HEAD_DIM
RW_SUB = 16


def _bdot(a, b):
    return jnp.dot(a.astype(BF16), b.astype(BF16), preferred_element_type=F32)


def _rwkv_kernel(r_ref, lw_ref, k_ref, v_ref, kk_ref, b_ref, o_ref, zt_ref):
    @pl.when(pl.program_id(0) == 0)
    def _():
        zt_ref[...] = jnp.zeros_like(zt_ref)

    c, n2 = RW_CHUNK, RW_PAIR
    nt = (((1,), (1,)), ((), ()))
    lw = lw_ref[...]
    tri = (lax.broadcasted_iota(jnp.int32, (c, c), 1) <= lax.broadcasted_iota(jnp.int32, (c, c), 0)).astype(BF16)
    lw_hi = lw.astype(BF16)
    rem = lw - lw_hi.astype(F32)
    lw_mid = rem.astype(BF16)
    lw_lo = (rem - lw_mid.astype(F32)).astype(BF16)
    g_in = (jnp.dot(tri, lw_hi, preferred_element_type=F32) + jnp.dot(tri, lw_mid, preferred_element_type=F32)
            + jnp.dot(tri, lw_lo, preferred_element_type=F32))
    g_ex = g_in - lw
    g_last = g_in[c - 1:c, :]
    e_neg = jnp.exp(-g_in)
    e_end = jnp.exp(g_last - g_in)
    kkd = kk_ref[...] * jnp.exp(g_ex)
    rd = r_ref[...] * jnp.exp(g_in)
    kinv = k_ref[...] * e_neg
    binv = b_ref[...] * e_neg
    kd = k_ref[...] * e_end
    bd = b_ref[...] * e_end
    gam_last = jnp.exp(g_last)

    row = lax.broadcasted_iota(jnp.int32, (n2, n2), 0)
    col = lax.broadcasted_iota(jnp.int32, (n2, n2), 1)
    t_idx, s_idx = row % c, col % c
    strict = t_idx > s_idx
    incl = t_idx >= s_idx
    diag_blk = (row // RW_SUB) == (col // RW_SUB)
    eye = (row == col).astype(F32)
    first_head = lax.broadcasted_iota(jnp.int32, (c, n2), 1) < RW_HEAD_DIM

    def embed(x):
        return jnp.concatenate([jnp.where(first_head, x, 0.0), jnp.where(first_head, 0.0, x)], axis=0)

    pairs = range(RW_HEADS // 2)
    sls = [slice(p * n2, (p + 1) * n2) for p in pairs]
    each = lambda f, *xs: [f(*a) for a in zip(*xs)]
    bdot = lambda xs, ys: each(_bdot, xs, ys)
    v_e = [embed(v_ref[:, sl]) for sl in sls]
    v_b = [x.astype(BF16) for x in v_e]
    lhs = [jnp.concatenate([embed(kkd[:, sl]), embed(rd[:, sl])], axis=0).astype(BF16) for sl in sls]
    rhs = [jnp.concatenate([embed(kinv[:, sl]), embed(binv[:, sl])], axis=0).astype(BF16) for sl in sls]
    zt = [zt_ref[p] for p in pairs]
    zt_b = [x.astype(BF16) for x in zt]
    aa = each(lambda a, b: lax.dot_general(a, b, nt, preferred_element_type=F32), lhs, rhs)
    x1 = each(lambda a, z: lax.dot_general(a[:n2], z, nt, preferred_element_type=F32), lhs, zt_b)
    o_z = each(lambda a, z: lax.dot_general(a[n2:], z, nt, preferred_element_type=F32), lhs, zt_b)
    a_kkk = [jnp.where(strict, x[:n2, :n2], 0.0).astype(BF16) for x in aa]
    a_kkb = [jnp.where(strict, x[:n2, n2:], 0.0) for x in aa]
    a_rk = [jnp.where(incl, x[n2:, :n2], 0.0).astype(BF16) for x in aa]
    a_rb = [jnp.where(incl, x[n2:, n2:], 0.0).astype(BF16) for x in aa]
    nd = [jnp.where(diag_blk, x, 0.0) for x in a_kkb]
    off = [jnp.where(diag_blk, 0.0, x) for x in a_kkb]
    nd2 = bdot(nd, nd)
    y = each(lambda z, av: z + av, x1, bdot(a_kkk, v_b))
    o_v = bdot(a_rk, v_b)
    nd4 = bdot(nd2, nd2)
    p1 = bdot([eye - x for x in nd], [eye + x for x in nd2])
    nd8 = bdot(nd4, nd4)
    p2 = bdot(p1, [eye + x for x in nd4])
    d_inv = bdot(p2, [eye + x for x in nd8])
    e1 = bdot(d_inv, off)
    e2 = bdot(e1, e1)
    t_inv = bdot(bdot([eye - x for x in e1], [eye + x for x in e2]), d_inv)
    u = bdot(t_inv, y)
    o_u = bdot(a_rb, u)
    vu_t = [jnp.transpose(jnp.concatenate([a, b], axis=0)) for a, b in zip(v_e, u)]
    kb = [jnp.concatenate([embed(kd[:, sl]), -embed(bd[:, sl])], axis=0) for sl in sls]
    z_up = bdot(vu_t, kb)
    for p in pairs:
        o_e = o_z[p] + o_v[p] - o_u[p]
        o_ref[:, sls[p]] = o_e[:c] + o_e[c:]
        zt_ref[p] = zt[p] * gam_last[:, sls[p]] + z_up[p]


def _rwkv_scan(r, lw, k, v, kk, b):
    s, width = r.shape
    blk = pl.BlockSpec((RW_CHUNK, width), lambda i: (i, 0))
    return pl.pallas_call(
        _rwkv_kernel,
        grid=(s // RW_CHUNK,),
        in_specs=[blk] * 6,
        out_specs=blk,
        out_shape=jax.ShapeDtypeStruct((s, width), F32),
        scratch_shapes=[pltpu.VMEM((RW_HEADS // 2, RW_PAIR, RW_PAIR), F32)],
        compiler_params=_params("arbitrary"),
        name="rwkv_scan",
    )(r, lw, k, v, kk, b)


def _token_shift(u):
    return jnp.pad(u, ((1, 0), (0, 0)))[:-1]


def _rwkv7(rw, l, p):
    s = rw.shape[0]
    r_in, k_in, v_in, z_in = (rw[:, i * RW_WIDTH:(i + 1) * RW_WIDTH] for i in range(4))
    mu = p['rw_mu'][l]
    lerp = lambda u, m: u + (_token_shift(u) - u) * m
    r, k, v = lerp(r_in, mu[0]), lerp(k_in, mu[1]), lerp(v_in, mu[2])
    z_diff = _token_shift(z_in) - z_in
    xw, xa, xg = z_in + z_diff * mu[3], z_in + z_diff * mu[4], z_in + z_diff * mu[5]
    w_lr = _mm(jnp.tanh(_mm(xw, p['rw_w1'], layer=l, name="mm_rw_lora_in")), p['rw_w2'], layer=l,
               name="mm_rw_lora_out")
    w_log = -jax.nn.softplus(-(p['rw_w0'][l] + w_lr)) - 0.5
    log_decay = -jnp.exp(w_log)
    a = jax.nn.sigmoid(p['rw_a0'][l] + _mm(_mm(xa, p['rw_a1'], layer=l, name="mm_rw_lora_in"), p['rw_a2'],
                                           layer=l, name="mm_rw_lora_out"))
    g = _mm(jax.nn.sigmoid(_mm(xg, p['rw_g1'], layer=l, name="mm_rw_lora_in")), p['rw_g2'], layer=l,
            name="mm_rw_lora_out")
    heads = lambda u: u.reshape(s, RW_HEADS, RW_HEAD_DIM)
    kk = heads(k * p['rw_kk'][l])
    kk = kk / jnp.maximum(jnp.sqrt(jnp.sum(kk * kk, axis=-1, keepdims=True)), 1e-12)
    k_mod = k * (1.0 + (a - 1.0) * p['rw_ka'][l])
    r_h, k_h, v_h = heads(r), heads(k_mod), heads(v)
    b = (kk * heads(a)).reshape(s, RW_WIDTH)
    out = heads(_rwkv_scan(r, log_decay, k_mod, v, kk.reshape(s, RW_WIDTH), b))
    mu_o = out.mean(-1, keepdims=True)
    var_o = jnp.mean(jnp.square(out - mu_o), -1, keepdims=True)
    out = ((out - mu_o) * lax.rsqrt(var_o + RW_GN_EPS)).reshape(s, RW_WIDTH) * p['rw_gn_g'][l] + p['rw_gn_b'][l]
    bonus = jnp.sum(r_h * k_h * p['rw_rk'][l], -1, keepdims=True) * v_h
    return (out + bonus.reshape(s, RW_WIDTH)) * g


def _attn_kernel(q_ref, kp_ref, kc_ref, vp_ref, vc_ref, o_ref, lse_ref):
    n = pl.program_id(1)
    qi = lax.broadcasted_iota(jnp.int32, (AT_BLOCK, AT_BLOCK), 0)
    kj = lax.broadcasted_iota(jnp.int32, (AT_BLOCK, AT_BLOCK), 1)
    mask_c = kj <= qi
    mask_p = (kj >= qi) & (n > 0)
    nt = (((1,), (1,)), ((), ()))
    outs = []
    lse_all = jnp.zeros((AT_BLOCK, LANES), F32)
    for h in range(AT_HEADS_PER_GROUP):
        sl = slice(h * AT_HEAD_DIM, (h + 1) * AT_HEAD_DIM)
        q = q_ref[:, sl]
        s_c = jnp.where(mask_c, lax.dot_general(q, kc_ref[:, sl], nt, preferred_element_type=F32), -jnp.inf)
        s_p = jnp.where(mask_p, lax.dot_general(q, kp_ref[:, sl], nt, preferred_element_type=F32), -jnp.inf)
        m = jnp.maximum(jnp.max(s_c, axis=-1, keepdims=True), jnp.max(s_p, axis=-1, keepdims=True))
        p_c = jnp.exp(s_c - m)
        p_p = jnp.exp(s_p - m)
        den = jnp.sum(p_c, axis=-1, keepdims=True) + jnp.sum(p_p, axis=-1, keepdims=True)
        o = (jnp.dot((p_c / den).astype(BF16), vc_ref[:, sl], preferred_element_type=F32)
             + jnp.dot((p_p / den).astype(BF16), vp_ref[:, sl], preferred_element_type=F32))
        outs.append(o)
        lse_all = jnp.where(kj == h, m + jnp.log(den), lse_all)
    o_ref[...] = jnp.concatenate(outs, axis=-1)
    lse_ref[...] = lse_all


def _dilated_attention(q, k, v, g, dilation):
    s = q.shape[0]
    length = s // dilation
    nb = length // AT_BLOCK
    assert nb * AT_BLOCK * dilation == s
    per_row = AT_WIDTH // AT_OUT_WIDTH
    view = lambda u: u.reshape(length, dilation * AT_WIDTH)
    cur = pl.BlockSpec((AT_BLOCK, AT_OUT_WIDTH), lambda r, n: (n, r * per_row + g))
    prev = pl.BlockSpec((AT_BLOCK, AT_OUT_WIDTH), lambda r, n: (jnp.maximum(n - 1, 0), r * per_row + g))
    o, lse = pl.pallas_call(
        _attn_kernel,
        grid=(dilation, nb),
        in_specs=[cur, prev, cur, prev, cur],
        out_specs=[pl.BlockSpec((AT_BLOCK, AT_OUT_WIDTH), lambda r, n: (n, r)),
                   pl.BlockSpec((AT_BLOCK, LANES), lambda r, n: (n, r))],
        out_shape=[jax.ShapeDtypeStruct((length, dilation * AT_OUT_WIDTH), F32),
                   jax.ShapeDtypeStruct((length, dilation * LANES), F32)],
        compiler_params=_params("arbitrary", "arbitrary"),
        name=f"attn_d{dilation}",
    )(view(q), view(k), view(k), view(v), view(v))
    return o.reshape(s, AT_OUT_WIDTH), lse.reshape(s, LANES)


def _rope(u, cos, sin):
    half = ROPE_DIMS // 2
    u1, u2 = u[..., :half], u[..., half:ROPE_DIMS]
    return jnp.concatenate([u1 * cos - u2 * sin, u2 * cos + u1 * sin, u[..., ROPE_DIMS:]], -1)


def _attention(at, positions):
    s = at.shape[0]
    half = ROPE_DIMS // 2
    inv_freq = ROPE_THETA ** (-jnp.arange(half, dtype=F32) * 2.0 / ROPE_DIMS)
    ang = positions.astype(F32).reshape(s, 1, 1) * inv_freq
    cos, sin = jnp.cos(ang), jnp.sin(ang)
    heads = lambda u: u.reshape(s, AT_HEADS, AT_HEAD_DIM)
    q = (_rope(heads(at[:, :AT_WIDTH]), cos, sin) * (AT_HEAD_DIM ** -0.5)).reshape(s, AT_WIDTH).astype(BF16)
    k = _rope(heads(at[:, AT_WIDTH:2 * AT_WIDTH]), cos, sin).reshape(s, AT_WIDTH).astype(BF16)
    v = at[:, 2 * AT_WIDTH:].astype(BF16)
    outs, lses = [], []
    for g, (_, dilation) in enumerate(AT_GROUPS):
        o, lse = _dilated_attention(q, k, v, g, dilation)
        outs.append(o.reshape(s, AT_HEADS_PER_GROUP, AT_HEAD_DIM))
        lses.append(lse[:, :AT_HEADS_PER_GROUP])
    wts = jax.nn.softmax(jnp.stack(lses), axis=0)
    y = jnp.sum(wts[..., None] * jnp.stack(outs), axis=0)
    return y.reshape(s, AT_OUT_WIDTH)


ML_CHUNKS_PER_STEP = 8
ML_ROWS = ML_CHUNKS_PER_STEP * ML_CHUNK


def _mlstm_kernel(q_ref, k_ref, v_ref, og_ref, bcol_ref, icol_ref, brow_ref, irow_ref, g_ref, y_ref,
                  ct_ref, n_ref, m_ref):
    @pl.when(pl.program_id(0) == 0)
    def _():
        ct_ref[...] = jnp.zeros_like(ct_ref)
        n_ref[...] = jnp.zeros_like(n_ref)
        m_ref[...] = jnp.full_like(m_ref, -jnp.inf)

    si = lax.broadcasted_iota(jnp.int32, (ML_CHUNK, ML_CHUNK), 0)
    ji = lax.broadcasted_iota(jnp.int32, (ML_CHUNK, ML_CHUNK), 1)
    causal = ji <= si
    nt = (((1,), (1,)), ((), ()))

    def chunk(c, carry):
        rows = pl.ds(pl.multiple_of(c * ML_CHUNK, ML_CHUNK), ML_CHUNK)
        for h in range(ML_HEADS):
            cols = slice(h * ML_HEAD_DIM, (h + 1) * ML_HEAD_DIM)
            q = q_ref[rows, cols] * (ML_HEAD_DIM ** -0.5)
            k = k_ref[rows, cols]
            v = v_ref[rows, cols]
            b_col = bcol_ref[rows, h:h + 1]
            i_col = icol_ref[rows, h:h + 1]
            b_row = brow_ref[c, h:h + 1, :]
            i_row = irow_ref[c, h:h + 1, :]
            m_prev = m_ref[h, 0:1, 0:1]
            n_prev = n_ref[h, 0:1, :]
            ct_prev = ct_ref[h]
            b_last = b_col[ML_CHUNK - 1:ML_CHUNK, :]
            log_d = jnp.where(causal, b_col - b_row + i_row, -jnp.inf)
            a_log = b_col + m_prev
            m_s = jnp.maximum(a_log, jnp.max(log_d, axis=-1, keepdims=True))
            inter_w = jnp.exp(a_log - m_s)
            qb, kb, vb = q.astype(BF16), k.astype(BF16), v.astype(BF16)
            qk = lax.dot_general(qb, kb, nt, preferred_element_type=F32) * jnp.exp(log_d - m_s)
            num = (inter_w * jnp.dot(qb, ct_prev.astype(BF16), preferred_element_type=F32)
                   + jnp.dot(qk.astype(BF16), vb, preferred_element_type=F32))
            den = inter_w * jnp.sum(q * n_prev, axis=-1, keepdims=True) + jnp.sum(qk, axis=-1, keepdims=True)
            hid = num / jnp.maximum(jnp.abs(den), jnp.exp(-m_s))
            w_end = b_last - b_col + i_col
            m_new = jnp.maximum(b_last + m_prev, jnp.max(w_end, axis=0, keepdims=True))
            dec = jnp.exp(b_last + m_prev - m_new)
            wts = jnp.exp(w_end - m_new)
            ct_ref[h] = dec * ct_prev + jnp.dot(jnp.transpose(k).astype(BF16), (v * wts).astype(BF16),
                                                preferred_element_type=F32)
            n_ref[h, 0:1, :] = dec * n_prev + jnp.sum(k * wts, axis=0, keepdims=True)
            m_ref[h, 0:1, 0:1] = m_new
            hid = hid * jax.nn.sigmoid(og_ref[rows, cols])
            mu = jnp.mean(hid, axis=-1, keepdims=True)
            dev = hid - mu
            var = jnp.mean(dev * dev, axis=-1, keepdims=True)
            y_ref[rows, cols] = dev * lax.rsqrt(var + ML_NORM_EPS) * g_ref[:, cols]
        return carry

    lax.fori_loop(0, ML_CHUNKS_PER_STEP, chunk, 0)


def _mlstm(ml, gates_if, l, p):
    s = ml.shape[0]
    nc = s // ML_CHUNK
    u = ml[:, :2 * ML_WIDTH]
    wc = p['ml_conv'][l]
    taps = wc.shape[0]
    up = jnp.pad(u, ((taps - 1, 0), (0, 0)))
    conv = sum(up[i:i + s] * wc[i] for i in range(taps))
    qk = jax.nn.silu(conv)
    i_pre = gates_if[:, :ML_HEADS] + p['ml_gate_b'][l, 0]
    f_pre = gates_if[:, ML_HEADS:] + p['ml_gate_b'][l, 1]
    lf = jax.nn.log_sigmoid(f_pre)
    b_cum = jnp.cumsum(lf.reshape(nc, ML_CHUNK, ML_HEADS), axis=1)
    b_col = b_cum.reshape(s, ML_HEADS)
    b_row = jnp.transpose(b_cum, (0, 2, 1))
    i_row = jnp.transpose(i_pre.reshape(nc, ML_CHUNK, ML_HEADS), (0, 2, 1))
    col = lambda j: pl.BlockSpec((ML_ROWS, ML_WIDTH), lambda i: (i, j))
    gcol = pl.BlockSpec((ML_ROWS, ML_HEADS), lambda i: (i, 0))
    grow = pl.BlockSpec((ML_CHUNKS_PER_STEP, ML_HEADS, ML_CHUNK), lambda i: (i, 0, 0))
    return pl.pallas_call(
        _mlstm_kernel,
        grid=(s // ML_ROWS,),
        in_specs=[col(0), col(1), col(2), col(3), gcol, gcol, grow, grow,
                  pl.BlockSpec((1, ML_WIDTH), lambda i: (0, 0))],
        out_specs=pl.BlockSpec((ML_ROWS, ML_WIDTH), lambda i: (i, 0)),
        out_shape=jax.ShapeDtypeStruct((s, ML_WIDTH), F32),
        scratch_shapes=[pltpu.VMEM((ML_HEADS, ML_HEAD_DIM, ML_HEAD_DIM), F32),
                        pltpu.VMEM((ML_HEADS, 8, ML_HEAD_DIM), F32),
                        pltpu.VMEM((ML_HEADS, 8, LANES), F32)],
        compiler_params=_params("arbitrary"),
        name="mlstm",
    )(qk, qk, ml, ml, b_col, i_pre, b_row, i_row, p['ml_ln_g'][l].reshape(1, ML_WIDTH))


def _expert_kernel(be_ref, new_ref, used_ref, x_ref, w1_ref, w3_ref, w2_ref, o_ref, w1b, w3b, w2b):
    i = pl.program_id(0)

    @pl.when(new_ref[i] == 1)
    def _():
        w1b[...] = w1_ref[...].astype(BF16)
        w3b[...] = w3_ref[...].astype(BF16)
        w2b[...] = w2_ref[...].astype(BF16)

    @pl.when(used_ref[i] == 1)
    def _():
        x = x_ref[...]
        h1 = jnp.dot(x, w1b[...], preferred_element_type=F32)
        h3 = jnp.dot(x, w3b[...], preferred_element_type=F32)
        hid = (h1 * jax.nn.sigmoid(h1) * h3).astype(BF16)
        o_ref[...] = jnp.dot(hid, w2b[...], preferred_element_type=F32)

    @pl.when(used_ref[i] == 0)
    def _():
        o_ref[...] = jnp.zeros_like(o_ref)


def _experts(buf, blk_expert, blk_new, blk_used, w1, w3, w2, l):
    rows, d = buf.shape
    nblk = rows // MOE_ROWS
    up = pl.BlockSpec((None, None, d, MOE_FF), lambda i, be, nw, us: (l, be[i], 0, 0))
    down = pl.BlockSpec((None, None, MOE_FF, d), lambda i, be, nw, us: (l, be[i], 0, 0))
    blk = pl.BlockSpec((MOE_ROWS, d), lambda i, be, nw, us: (i, 0))
    return pl.pallas_call(
        _expert_kernel,
        grid_spec=pltpu.PrefetchScalarGridSpec(
            num_scalar_prefetch=3,
            grid=(nblk,),
            in_specs=[blk, up, up, down],
            out_specs=blk,
            scratch_shapes=[pltpu.VMEM((d, MOE_FF), BF16), pltpu.VMEM((d, MOE_FF), BF16),
                            pltpu.VMEM((MOE_FF, d), BF16)]),
        out_shape=jax.ShapeDtypeStruct((rows, d), F32),
        compiler_params=_params("arbitrary"),
        name="experts",
    )(blk_expert, blk_new, blk_used, buf, w1, w3, w2)


def _moe(x, xb, l, p):
    t, d = x.shape
    n_assign = t * MOE_TOP_K
    w_router = jnp.concatenate([p['moe_w_group'][l], p['moe_w_expert'][l]], axis=1)
    w_router = jnp.pad(w_router, ((0, 0), (0, LANES - w_router.shape[1])))
    logits = _mm(x, w_router, tm=512, exact=True, name="mm_router")
    g_logits = logits[:, :MOE_GROUPS] + p['moe_b_group'][l]
    g_prob = jax.nn.softmax(g_logits, -1)
    g_idx = jnp.argmax(g_logits, -1)
    g_w = jnp.take_along_axis(g_prob, g_idx[:, None], axis=1)
    e_logits = (logits[:, MOE_GROUPS:MOE_GROUPS + MOE_EXPERTS] + p['moe_b_expert'][l]).reshape(
        t, MOE_GROUPS, MOE_EXPERTS_PER_GROUP)
    e_sel = jnp.take_along_axis(e_logits, g_idx[:, None, None], axis=1)[:, 0]
    top_l, top_i = lax.top_k(e_sel, MOE_TOP_K)
    top_w = jax.nn.softmax(top_l, -1) * g_w
    e_flat = (g_idx[:, None] * MOE_EXPERTS_PER_GROUP + top_i).reshape(n_assign).astype(jnp.int32)
    onehot = (e_flat[:, None] == jnp.arange(MOE_EXPERTS, dtype=jnp.int32)[None]).astype(jnp.int32)
    rank = jnp.take_along_axis(jnp.cumsum(onehot, axis=0), e_flat[:, None], axis=1)[:, 0] - 1
    counts = jnp.sum(onehot, axis=0)
    pcounts = (counts + MOE_ROWS - 1) // MOE_ROWS * MOE_ROWS
    pends = jnp.cumsum(pcounts)
    pstarts = pends - pcounts
    pos = pstarts[e_flat] + rank
    n_rows = n_assign + MOE_EXPERTS * MOE_ROWS
    nblk = n_rows // MOE_ROWS
    src = jnp.zeros((n_rows,), jnp.int32).at[pos].set(jnp.arange(n_assign, dtype=jnp.int32) // MOE_TOP_K)
    blk_start = jnp.arange(nblk, dtype=jnp.int32) * MOE_ROWS
    blk_expert = jnp.clip(jnp.searchsorted(pends, blk_start, side='right'), 0, MOE_EXPERTS - 1).astype(jnp.int32)
    blk_used = (blk_start < pends[-1]).astype(jnp.int32)
    blk_new = jnp.concatenate([jnp.ones((1,), jnp.int32), (blk_expert[1:] != blk_expert[:-1]).astype(jnp.int32)])
    yb = _experts(xb[src], blk_expert, blk_new, blk_used, p['moe_w1'], p['moe_w3'], p['moe_w2'], l)
    y = yb[pos].reshape(t, MOE_TOP_K, d)
    return jnp.sum(y * top_w[..., None], axis=1)


def _mixer(xb, positions, l, p):
    w_in = p['w_in']
    proj = _mm(xb, w_in, layer=l, col0=0, n_cols=COL_IF, tn=896, name="mm_in")
    gates_if = _mm(xb, w_in[l, :, COL_IF:COL_GATE], name="mm_in_if")
    gate_pre = _mm(xb, w_in[l, :, COL_GATE:], tn=1024, name="mm_in_gate")
    y_rw = _rwkv7(proj[:, COL_RW:COL_AT], l, p)
    y_at = _attention(proj[:, COL_AT:COL_ML], positions)
    y_ml = _mlstm(proj[:, COL_ML:COL_IF], gates_if, l, p)
    gates = jax.nn.sigmoid(gate_pre)
    wb = p['w_branch'][l]
    y = (gates[:, :D_MODEL] * _mm(y_rw, wb[:RW_WIDTH], tn=1024, name="mm_branch_rw")
         + gates[:, D_MODEL:2 * D_MODEL] * _mm(y_at, wb[RW_WIDTH:RW_WIDTH + AT_OUT_WIDTH], tn=1024,
                                               name="mm_branch_at")
         + gates[:, 2 * D_MODEL:] * _mm(y_ml, wb[RW_WIDTH + AT_OUT_WIDTH:], tn=1024, name="mm_branch_ml"))
    return _mm(y, p['w_out'], layer=l, tn=1024, name="mm_out")


def kernel(x, positions, w_in, rw_mu, rw_w0, rw_w1, rw_w2, rw_a0, rw_a1, rw_a2, rw_g1, rw_g2, rw_kk, rw_ka, rw_rk, rw_gn_g, rw_gn_b, ml_conv, ml_gate_b, ml_ln_g, w_branch, w_out, ln1_g, ln1_b, moe_w_group, moe_b_group, moe_w_expert, moe_b_expert, moe_w1, moe_w3, moe_w2, ln2_g, ln2_b):
    p = dict(w_in=w_in, rw_mu=rw_mu, rw_w0=rw_w0, rw_w1=rw_w1, rw_w2=rw_w2, rw_a0=rw_a0, rw_a1=rw_a1,
             rw_a2=rw_a2, rw_g1=rw_g1, rw_g2=rw_g2, rw_kk=rw_kk, rw_ka=rw_ka, rw_rk=rw_rk, rw_gn_g=rw_gn_g,
             rw_gn_b=rw_gn_b, ml_conv=ml_conv, ml_gate_b=ml_gate_b, ml_ln_g=ml_ln_g, w_branch=w_branch,
             w_out=w_out, moe_w_group=moe_w_group, moe_b_group=moe_b_group, moe_w_expert=moe_w_expert,
             moe_b_expert=moe_b_expert, moe_w1=moe_w1, moe_w3=moe_w3, moe_w2=moe_w2)
    batch, seq, d = x.shape
    assert batch == 1
    xf = x.reshape(seq, d)
    xb = xf.astype(BF16)
    pos = positions.reshape(seq)
    for l in range(w_in.shape[0]):
        h = _mixer(xb, pos, l, p)
        xf, xb = _add_ln(xf, h, ln1_g[l], ln1_b[l])
        h = _moe(xf, xb, l, p)
        xf, xb = _add_ln(xf, h, ln2_g[l], ln2_b[l])
    return xf.reshape(batch, seq, d)
```

```python
import functools

import jax
import jax.numpy as jnp
from jax import lax
from jax.experimental import pallas as pl
from jax.experimental.pallas import tpu as pltpu

F32 = jnp.float32
BF16 = jnp.bfloat16

D_MODEL = 2048
DEPTH = 4
LN_EPS = 1e-5
DN_ALPHA = (2 * DEPTH) ** 0.25

RW_HEADS = 16
RW_HEAD_DIM = 64
RW_WIDTH = RW_HEADS * RW_HEAD_DIM
RW_GN_EPS = 64e-5

AT_GROUPS = ((128, 1), (512, 4), (2048, 16))
AT_HEADS_PER_GROUP = 6
AT_HEADS = AT_HEADS_PER_GROUP * len(AT_GROUPS)
AT_HEAD_DIM = 64
AT_WIDTH = AT_HEADS * AT_HEAD_DIM
AT_OUT_WIDTH = AT_HEADS_PER_GROUP * AT_HEAD_DIM
AT_BLOCK = 128
ROPE_DIMS = AT_HEAD_DIM // 4
ROPE_THETA = 500000.0

ML_HEADS = 8
ML_HEAD_DIM = 128
ML_WIDTH = ML_HEADS * ML_HEAD_DIM
ML_CHUNK = 64
ML_NORM_EPS = 1e-6

N_BRANCHES = 3
COL_RW = 0
COL_AT = 4 * RW_WIDTH
COL_ML = COL_AT + 3 * AT_WIDTH
COL_IF = COL_ML + 4 * ML_WIDTH
COL_GATE = COL_IF + 2 * ML_HEADS
IN_COLS = COL_GATE + N_BRANCHES * D_MODEL

MOE_GROUPS = 4
MOE_EXPERTS_PER_GROUP = 8
MOE_EXPERTS = MOE_GROUPS * MOE_EXPERTS_PER_GROUP
MOE_TOP_K = 2
MOE_FF = 512
MOE_ROWS = 256

LANES = 128
SUBLANES = 8
VMEM_LIMIT = 56 * 1024 * 1024

NT_DIMS = (((1,), (1,)), ((), ()))


def _params(*sem):
    return pltpu.CompilerParams(dimension_semantics=sem, vmem_limit_bytes=VMEM_LIMIT)


def _bdot(a, b):
    return jnp.dot(a.astype(BF16), b.astype(BF16), preferred_element_type=F32)


def _bdot_nt(a, b):
    return lax.dot_general(a.astype(BF16), b.astype(BF16), NT_DIMS, preferred_element_type=F32)


def _each(f, *xs):
    return [f(*a) for a in zip(*xs)]


def _shift_rows(x, prev_row):
    first = lax.broadcasted_iota(jnp.int32, x.shape, 0) == 0
    return jnp.where(first, prev_row, pltpu.roll(x, 1, axis=0))


def _pair_sum(x, first_head):
    lo = jnp.sum(jnp.where(first_head, x, 0.0), axis=-1, keepdims=True)
    hi = jnp.sum(jnp.where(first_head, 0.0, x), axis=-1, keepdims=True)
    return jnp.where(first_head, lo, hi)


def _mm_kernel(a_ref, w_ref, o_ref, wb_ref, *, precision):
    @pl.when(pl.program_id(1) == 0)
    def _():
        wb_ref[...] = w_ref[...].astype(wb_ref.dtype)

    o_ref[...] = jnp.dot(a_ref[...], wb_ref[...], preferred_element_type=F32,
                         precision=precision).astype(o_ref.dtype)


def _mm(a, w, *, layer=None, col0=0, n_cols=None, tm=1024, tn=None, out_dtype=F32, exact=False, name="mm"):
    m, k = a.shape
    n_total = w.shape[-1]
    n_cols = n_total - col0 if n_cols is None else n_cols
    tn = n_cols if tn is None else tn
    tm = min(tm, m)
    assert m % tm == 0 and n_cols % tn == 0 and col0 % tn == 0
    cb0 = col0 // tn
    if layer is None:
        w_spec = pl.BlockSpec((k, tn), lambda j, i: (0, cb0 + j))
    else:
        w_spec = pl.BlockSpec((None, k, tn), lambda j, i: (layer, 0, cb0 + j))
    op_dtype = F32 if exact else BF16
    return pl.pallas_call(
        functools.partial(_mm_kernel, precision=lax.Precision.HIGHEST if exact else None),
        grid=(n_cols // tn, m // tm),
        in_specs=[pl.BlockSpec((tm, k), lambda j, i: (i, 0)), w_spec],
        out_specs=pl.BlockSpec((tm, tn), lambda j, i: (i, j)),
        out_shape=jax.ShapeDtypeStruct((m, n_cols), out_dtype),
        scratch_shapes=[pltpu.VMEM((k, tn), op_dtype)],
        compiler_params=_params("arbitrary", "arbitrary"),
        name=name,
    )(a.astype(op_dtype), w)


def _layer_norm(y, g, b):
    mu = jnp.mean(y, axis=-1, keepdims=True)
    d = y - mu
    var = jnp.mean(d * d, axis=-1, keepdims=True)
    return d * lax.rsqrt(var + LN_EPS) * g + b


def _add_ln_kernel(x_ref, h_ref, g_ref, b_ref, o_ref, ob_ref):
    out = _layer_norm(DN_ALPHA * x_ref[...] + h_ref[...], g_ref[...], b_ref[...])
    o_ref[...] = out
    ob_ref[...] = out.astype(BF16)


def _add_ln(x, h, g, b, tm=256):
    m, d = x.shape
    row = pl.BlockSpec((tm, d), lambda i: (i, 0))
    vec = pl.BlockSpec((1, d), lambda i: (0, 0))
    return pl.pallas_call(
        _add_ln_kernel,
        grid=(m // tm,),
        in_specs=[row, row, vec, vec],
        out_specs=[row, row],
        out_shape=[jax.ShapeDtypeStruct((m, d), F32), jax.ShapeDtypeStruct((m, d), BF16)],
        compiler_params=_params("arbitrary"),
        name="add_ln",
    )(x, h, g.reshape(1, d), b.reshape(1, d))


def _out_ln_kernel(y_ref, w_ref, x_ref, g_ref, b_ref, o_ref, ob_ref):
    h = jnp.dot(y_ref[...], w_ref[...], preferred_element_type=F32)
    out = _layer_norm(DN_ALPHA * x_ref[...] + h, g_ref[...], b_ref[...])
    o_ref[...] = out
    ob_ref[...] = out.astype(BF16)


def _out_ln(y, w, x, g, b, tm=256):
    m, d = x.shape
    row = pl.BlockSpec((tm, d), lambda i: (i, 0))
    vec = pl.BlockSpec((1, d), lambda i: (0, 0))
    return pl.pallas_call(
        _out_ln_kernel,
        grid=(m // tm,),
        in_specs=[row, pl.BlockSpec((d, d), lambda i: (0, 0)), row, vec, vec],
        out_specs=[row, row],
        out_shape=[jax.ShapeDtypeStruct((m, d), F32), jax.ShapeDtypeStruct((m, d), BF16)],
        compiler_params=_params("arbitrary"),
        name="out_ln",
    )(y, w, x, g.reshape(1, d), b.reshape(1, d))


RW_CHUNK = 64
RW_PAIR = 2 * RW_HEAD_DIM
RW_SUB = 16
RW_PREP_ROWS = 256
(RV_MU, RV_W0, RV_A0, RV_KK, RV_KA, RV_GN_G, RV_GN_B, RV_RK, RV_ROWS) = (0, 6, 7, 8, 9, 10, 11, 12, 16)


def _rwkv_prep_kernel(r_ref, k_ref, v_ref, z_ref, rp_ref, kp_ref, vp_ref, zp_ref, vec_ref,
                      w1_ref, w2_ref, a1_ref, a2_ref, g1_ref, g2_ref,
                      ro_ref, lw_ref, ko_ref, vo_ref, kk_ref, b_ref, g_ref):
    first_block = pl.program_id(0) == 0
    vec = lambda j: vec_ref[j:j + 1, :]
    prev = lambda ref: jnp.where(first_block, 0.0, ref[SUBLANES - 1:SUBLANES, :])
    lerp = lambda u, up, j: u + (_shift_rows(u, up) - u) * vec(RV_MU + j)
    r = lerp(r_ref[...], prev(rp_ref), 0)
    k = lerp(k_ref[...], prev(kp_ref), 1)
    v = lerp(v_ref[...], prev(vp_ref), 2)
    z = z_ref[...]
    z_diff = _shift_rows(z, prev(zp_ref)) - z
    xw, xa, xg = z + z_diff * vec(RV_MU + 3), z + z_diff * vec(RV_MU + 4), z + z_diff * vec(RV_MU + 5)
    w_pre = vec(RV_W0) + _bdot(jnp.tanh(_bdot(xw, w1_ref[...])), w2_ref[...])
    softplus = jnp.maximum(-w_pre, 0.0) + jnp.log(1.0 + jnp.exp(-jnp.abs(w_pre)))
    lw_ref[...] = -jnp.exp(-softplus - 0.5)
    a = jax.nn.sigmoid(vec(RV_A0) + _bdot(_bdot(xa, a1_ref[...]), a2_ref[...]))
    g_ref[...] = _bdot(jax.nn.sigmoid(_bdot(xg, g1_ref[...])), g2_ref[...])
    kk = k * vec(RV_KK)
    first_head = lax.broadcasted_iota(jnp.int32, (kk.shape[0], RW_PAIR), 1) < RW_HEAD_DIM
    for p in range(RW_HEADS // 2):
        sl = slice(p * RW_PAIR, (p + 1) * RW_PAIR)
        kk_p = kk[:, sl]
        kk_p = kk_p / jnp.maximum(jnp.sqrt(_pair_sum(kk_p * kk_p, first_head)), 1e-12)
        kk_ref[:, sl] = kk_p
        b_ref[:, sl] = kk_p * a[:, sl]
    ro_ref[...] = r
    ko_ref[...] = k * (1.0 + (a - 1.0) * vec(RV_KA))
    vo_ref[...] = v


def _rwkv_prep(rw, vecs, l, p):
    s = rw.shape[0]
    tm = min(RW_PREP_ROWS, s)
    per = tm // SUBLANES
    cur = lambda j: pl.BlockSpec((tm, RW_WIDTH), lambda i: (i, j))
    prev = lambda j: pl.BlockSpec((SUBLANES, RW_WIDTH), lambda i: (jnp.maximum(i * per - 1, 0), j))
    lora = lambda w: pl.BlockSpec((None,) + w.shape[1:], lambda i: (l, 0, 0))
    ws = [p['rw_w1'], p['rw_w2'], p['rw_a1'], p['rw_a2'], p['rw_g1'], p['rw_g2']]
    out = pl.BlockSpec((tm, RW_WIDTH), lambda i: (i, 0))
    return pl.pallas_call(
        _rwkv_prep_kernel,
        grid=(s // tm,),
        in_specs=[cur(0), cur(1), cur(2), cur(3), prev(0), prev(1), prev(2), prev(3),
                  pl.BlockSpec((RV_ROWS, RW_WIDTH), lambda i: (0, 0))] + [lora(w) for w in ws],
        out_specs=[out] * 7,
        out_shape=[jax.ShapeDtypeStruct((s, RW_WIDTH), F32)] * 7,
        compiler_params=_params("arbitrary"),
        name="rwkv_prep",
    )(rw, rw, rw, rw, rw, rw, rw, rw, vecs, *ws)


def _rwkv_kernel(r_ref, lw_ref, k_ref, v_ref, kk_ref, b_ref, g_ref, vec_ref, o_ref, zt_ref):
    @pl.when(pl.program_id(0) == 0)
    def _():
        zt_ref[...] = jnp.zeros_like(zt_ref)

    c, n2 = RW_CHUNK, RW_PAIR
    lw = lw_ref[...]
    tri = (lax.broadcasted_iota(jnp.int32, (c, c), 1) <= lax.broadcasted_iota(jnp.int32, (c, c), 0)).astype(BF16)
    lw_hi = lw.astype(BF16)
    rem = lw - lw_hi.astype(F32)
    lw_mid = rem.astype(BF16)
    lw_lo = (rem - lw_mid.astype(F32)).astype(BF16)
    g_in = (jnp.dot(tri, lw_hi, preferred_element_type=F32) + jnp.dot(tri, lw_mid, preferred_element_type=F32)
            + jnp.dot(tri, lw_lo, preferred_element_type=F32))
    g_ex = g_in - lw
    g_last = g_in[c - 1:c, :]
    e_neg = jnp.exp(-g_in)
    e_end = jnp.exp(g_last - g_in)
    kkd = kk_ref[...] * jnp.exp(g_ex)
    rd = r_ref[...] * jnp.exp(g_in)
    kinv = k_ref[...] * e_neg
    binv = b_ref[...] * e_neg
    kd = k_ref[...] * e_end
    bd = b_ref[...] * e_end
    gam_last = jnp.exp(g_last)

    row = lax.broadcasted_iota(jnp.int32, (n2, n2), 0)
    col = lax.broadcasted_iota(jnp.int32, (n2, n2), 1)
    t_idx, s_idx = row % c, col % c
    strict = t_idx > s_idx
    incl = t_idx >= s_idx
    diag_blk = (row // RW_SUB) == (col // RW_SUB)
    eye = (row == col).astype(F32)
    first_head = lax.broadcasted_iota(jnp.int32, (c, n2), 1) < RW_HEAD_DIM

    def embed(x):
        return jnp.concatenate([jnp.where(first_head, x, 0.0), jnp.where(first_head, 0.0, x)], axis=0)

    pairs = range(RW_HEADS // 2)
    sls = [slice(p * n2, (p + 1) * n2) for p in pairs]
    bdot = lambda xs, ys: _each(_bdot, xs, ys)
    v_e = [embed(v_ref[:, sl]) for sl in sls]
    v_b = [x.astype(BF16) for x in v_e]
    lhs = [jnp.concatenate([embed(kkd[:, sl]), embed(rd[:, sl])], axis=0).astype(BF16) for sl in sls]
    rhs = [jnp.concatenate([embed(kinv[:, sl]), embed(binv[:, sl])], axis=0).astype(BF16) for sl in sls]
    zt = [zt_ref[p] for p in pairs]
    zt_b = [x.astype(BF16) for x in zt]
    aa = _each(_bdot_nt, lhs, rhs)
    x1 = _each(lambda a, z: _bdot_nt(a[:n2], z), lhs, zt_b)
    o_z = _each(lambda a, z: _bdot_nt(a[n2:], z), lhs, zt_b)
    a_kkk = [jnp.where(strict, x[:n2, :n2], 0.0).astype(BF16) for x in aa]
    a_kkb = [jnp.where(strict, x[:n2, n2:], 0.0) for x in aa]
    a_rk = [jnp.where(incl, x[n2:, :n2], 0.0).astype(BF16) for x in aa]
    a_rb = [jnp.where(incl, x[n2:, n2:], 0.0).astype(BF16) for x in aa]
    nd = [jnp.where(diag_blk, x, 0.0) for x in a_kkb]
    off = [jnp.where(diag_blk, 0.0, x) for x in a_kkb]
    nd2 = bdot(nd, nd)
    y = _each(lambda z, av: z + av, x1, bdot(a_kkk, v_b))
    o_v = bdot(a_rk, v_b)
    nd4 = bdot(nd2, nd2)
    p1 = bdot([eye - x for x in nd], [eye + x for x in nd2])
    nd8 = bdot(nd4, nd4)
    p2 = bdot(p1, [eye + x for x in nd4])
    d_inv = bdot(p2, [eye + x for x in nd8])
    e1 = bdot(d_inv, off)
    e2 = bdot(e1, e1)
    t_inv = bdot(bdot([eye - x for x in e1], [eye + x for x in e2]), d_inv)
    u = bdot(t_inv, y)
    o_u = bdot(a_rb, u)
    vu_t = [jnp.transpose(jnp.concatenate([a, b], axis=0)) for a, b in zip(v_e, u)]
    kb = [jnp.concatenate([embed(kd[:, sl]), -embed(bd[:, sl])], axis=0) for sl in sls]
    z_up = bdot(vu_t, kb)
    inv_n = 1.0 / RW_HEAD_DIM
    for p in pairs:
        sl = sls[p]
        zt_ref[p] = zt[p] * gam_last[:, sl] + z_up[p]
        o_e = o_z[p] + o_v[p] - o_u[p]
        out = o_e[:c] + o_e[c:]
        mu = _pair_sum(out, first_head) * inv_n
        dev = out - mu
        var = _pair_sum(dev * dev, first_head) * inv_n
        normed = dev * lax.rsqrt(var + RW_GN_EPS) * vec_ref[RV_GN_G:RV_GN_G + 1, sl] + vec_ref[RV_GN_B:RV_GN_B + 1, sl]
        bonus = _pair_sum(r_ref[:, sl] * k_ref[:, sl] * vec_ref[RV_RK:RV_RK + 1, sl], first_head) * v_ref[:, sl]
        o_ref[:, sl] = ((normed + bonus) * g_ref[:, sl]).astype(o_ref.dtype)


def _rwkv_scan(r, lw, k, v, kk, b, g, vecs):
    s, width = r.shape
    blk = pl.BlockSpec((RW_CHUNK, width), lambda i: (i, 0))
    return pl.pallas_call(
        _rwkv_kernel,
        grid=(s // RW_CHUNK,),
        in_specs=[blk] * 7 + [pl.BlockSpec((RV_ROWS, width), lambda i: (0, 0))],
        out_specs=blk,
        out_shape=jax.ShapeDtypeStruct((s, width), BF16),
        scratch_shapes=[pltpu.VMEM((RW_HEADS // 2, RW_PAIR, RW_PAIR), F32)],
        compiler_params=_params("arbitrary"),
        name="rwkv_scan",
    )(r, lw, k, v, kk, b, g, vecs)


def _rwkv7(rw, l, p):
    rows = [p['rw_mu'][l], p['rw_w0'][l][None], p['rw_a0'][l][None], p['rw_kk'][l][None], p['rw_ka'][l][None],
            p['rw_gn_g'][l][None], p['rw_gn_b'][l][None], p['rw_rk'][l].reshape(1, RW_WIDTH)]
    vecs = jnp.concatenate(rows + [jnp.zeros((RV_ROWS - RV_RK - 1, RW_WIDTH), F32)], axis=0)
    r, lw, k, v, kk, b, g = _rwkv_prep(rw, vecs, l, p)
    return _rwkv_scan(r, lw, k, v, kk, b, g, vecs)


AT_PREP_ROWS = 512
AT_SLABS = AT_WIDTH // LANES


def _rope_kernel(q_ref, k_ref, v_ref, c_ref, s1_ref, s2_ref, qo_ref, ko_ref, vo_ref):
    cos, s_dn, s_up = c_ref[...], s1_ref[...], s2_ref[...]

    def rot(x, scale):
        outs = []
        for j in range(AT_SLABS):
            xs = x[:, j * LANES:(j + 1) * LANES]
            y = xs * cos + pltpu.roll(xs, LANES - ROPE_DIMS // 2, axis=1) * s_dn + pltpu.roll(xs, ROPE_DIMS // 2, axis=1) * s_up
            outs.append(y * scale if scale != 1.0 else y)
        return jnp.concatenate(outs, axis=-1)

    qo_ref[...] = rot(q_ref[...], AT_HEAD_DIM ** -0.5).astype(BF16)
    ko_ref[...] = rot(k_ref[...], 1.0).astype(BF16)
    vo_ref[...] = v_ref[...].astype(BF16)


def _rope_tables(positions):
    s = positions.shape[0]
    half = ROPE_DIMS // 2
    inv_freq = ROPE_THETA ** (-jnp.arange(half, dtype=F32) * 2.0 / ROPE_DIMS)
    ang = positions.astype(F32)[:, None] * inv_freq
    cos, sin = jnp.cos(ang), jnp.sin(ang)
    rest = AT_HEAD_DIM - ROPE_DIMS
    zeros, ones = jnp.zeros((s, rest), F32), jnp.ones((s, rest), F32)
    z_half = jnp.zeros((s, half), F32)
    head = lambda parts: jnp.tile(jnp.concatenate(parts, axis=1), (1, LANES // AT_HEAD_DIM))
    return head([cos, cos, ones]), head([-sin, z_half, zeros]), head([z_half, sin, zeros])


def _rope(at, tables):
    s = at.shape[0]
    tm = min(AT_PREP_ROWS, s)
    col = lambda j: pl.BlockSpec((tm, AT_WIDTH), lambda i: (i, j))
    tab = pl.BlockSpec((tm, LANES), lambda i: (i, 0))
    return pl.pallas_call(
        _rope_kernel,
        grid=(s // tm,),
        in_specs=[col(0), col(1), col(2), tab, tab, tab],
        out_specs=[col(0)] * 3,
        out_shape=[jax.ShapeDtypeStruct((s, AT_WIDTH), BF16)] * 3,
        compiler_params=_params("arbitrary"),
        name="attn_rope",
    )(at, at, at, *tables)


def _attn_kernel(q_ref, kp_ref, kc_ref, vp_ref, vc_ref, o_ref, lse_ref):
    n = pl.program_id(1)
    qi = lax.broadcasted_iota(jnp.int32, (AT_BLOCK, AT_BLOCK), 0)
    kj = lax.broadcasted_iota(jnp.int32, (AT_BLOCK, AT_BLOCK), 1)
    mask_c = kj <= qi
    mask_p = (kj >= qi) & (n > 0)
    heads = range(AT_HEADS_PER_GROUP)
    sls = [slice(h * AT_HEAD_DIM, (h + 1) * AT_HEAD_DIM) for h in heads]
    q = [q_ref[:, sl] for sl in sls]
    s_c = [jnp.where(mask_c, lax.dot_general(a, kc_ref[:, sl], NT_DIMS, preferred_element_type=F32), -jnp.inf)
           for a, sl in zip(q, sls)]
    s_p = [jnp.where(mask_p, lax.dot_general(a, kp_ref[:, sl], NT_DIMS, preferred_element_type=F32), -jnp.inf)
           for a, sl in zip(q, sls)]
    m = [jnp.maximum(jnp.max(a, axis=-1, keepdims=True), jnp.max(b, axis=-1, keepdims=True)) for a, b in zip(s_c, s_p)]
    p_c = [jnp.exp(a - mm) for a, mm in zip(s_c, m)]
    p_p = [jnp.exp(a - mm) for a, mm in zip(s_p, m)]
    den = [jnp.sum(a, axis=-1, keepdims=True) + jnp.sum(b, axis=-1, keepdims=True) for a, b in zip(p_c, p_p)]
    o_c = [jnp.dot((a / d).astype(BF16), vc_ref[:, sl], preferred_element_type=F32) for a, d, sl in zip(p_c, den, sls)]
    o_p = [jnp.dot((a / d).astype(BF16), vp_ref[:, sl], preferred_element_type=F32) for a, d, sl in zip(p_p, den, sls)]
    o_ref[...] = jnp.concatenate([a + b for a, b in zip(o_c, o_p)], axis=-1)
    lse_ref[...] = jnp.concatenate(
        [jnp.broadcast_to(mm + jnp.log(d), (AT_BLOCK, AT_HEAD_DIM)) for mm, d in zip(m, den)], axis=-1)


def _dilated_attention(q, k, v, g, dilation):
    s = q.shape[0]
    length = s // dilation
    nb = length // AT_BLOCK
    assert nb * AT_BLOCK * dilation == s
    per_row = AT_WIDTH // AT_OUT_WIDTH
    view = lambda u: u.reshape(length, dilation * AT_WIDTH)
    cur = pl.BlockSpec((AT_BLOCK, AT_OUT_WIDTH), lambda r, n: (n, r * per_row + g))
    prev = pl.BlockSpec((AT_BLOCK, AT_OUT_WIDTH), lambda r, n: (jnp.maximum(n - 1, 0), r * per_row + g))
    out = pl.BlockSpec((AT_BLOCK, AT_OUT_WIDTH), lambda r, n: (n, r))
    o, lse = pl.pallas_call(
        _attn_kernel,
        grid=(dilation, nb),
        in_specs=[cur, prev, cur, prev, cur],
        out_specs=[out, out],
        out_shape=[jax.ShapeDtypeStruct((length, dilation * AT_OUT_WIDTH), F32)] * 2,
        compiler_params=_params("arbitrary", "arbitrary"),
        name=f"attn_d{dilation}",
    )(view(q), view(k), view(k), view(v), view(v))
    return o.reshape(s, AT_OUT_WIDTH), lse.reshape(s, AT_OUT_WIDTH)


def _attn_merge_kernel(o0, o1, o2, l0, l1, l2, y_ref):
    m = jnp.maximum(jnp.maximum(l0[...], l1[...]), l2[...])
    e0, e1, e2 = jnp.exp(l0[...] - m), jnp.exp(l1[...] - m), jnp.exp(l2[...] - m)
    tot = e0 + e1 + e2
    y_ref[...] = ((e0 / tot) * o0[...] + (e1 / tot) * o1[...] + (e2 / tot) * o2[...]).astype(BF16)


def _attention(at, tables):
    s = at.shape[0]
    q, k, v = _rope(at, tables)
    outs, lses = [], []
    for g, (_, dilation) in enumerate(AT_GROUPS):
        o, lse = _dilated_attention(q, k, v, g, dilation)
        outs.append(o)
        lses.append(lse)
    tm = min(AT_PREP_ROWS, s)
    blk = pl.BlockSpec((tm, AT_OUT_WIDTH), lambda i: (i, 0))
    return pl.pallas_call(
        _attn_merge_kernel,
        grid=(s // tm,),
        in_specs=[blk] * 6,
        out_specs=blk,
        out_shape=jax.ShapeDtypeStruct((s, AT_OUT_WIDTH), BF16),
        compiler_params=_params("arbitrary"),
        name="attn_merge",
    )(*outs, *lses)


ML_CHUNKS_PER_STEP = 8
ML_ROWS = ML_CHUNKS_PER_STEP * ML_CHUNK


def _mlstm_kernel(u_ref, v_ref, og_ref, cw_ref, bcol_ref, icol_ref, brow_ref, irow_ref, g_ref, y_ref,
                  qk_ref, tail_ref, ct_ref, n_ref, m_ref):
    @pl.when(pl.program_id(0) == 0)
    def _():
        tail_ref[...] = jnp.zeros_like(tail_ref)
        ct_ref[...] = jnp.zeros_like(ct_ref)
        n_ref[...] = jnp.zeros_like(n_ref)
        m_ref[...] = jnp.full_like(m_ref, -jnp.inf)

    u = u_ref[...]
    taps = cw_ref.shape[0]
    tail = tail_ref[...]
    row8 = lax.broadcasted_iota(jnp.int32, tail.shape, 0)
    conv = u * cw_ref[taps - 1:taps, :]
    for back in range(1, taps):
        shifted = pltpu.roll(u, back, axis=0)
        head = jnp.where(row8 < back, pltpu.roll(tail, back, axis=0), shifted[:SUBLANES])
        shifted = jnp.concatenate([head, shifted[SUBLANES:]], axis=0)
        conv = conv + shifted * cw_ref[taps - 1 - back:taps - back, :]
    qk_ref[...] = conv * jax.nn.sigmoid(conv)
    tail_ref[...] = u[ML_ROWS - SUBLANES:, :]

    si = lax.broadcasted_iota(jnp.int32, (ML_CHUNK, ML_CHUNK), 0)
    ji = lax.broadcasted_iota(jnp.int32, (ML_CHUNK, ML_CHUNK), 1)
    causal = ji <= si
    heads = range(ML_HEADS)
    cols = [slice(h * ML_HEAD_DIM, (h + 1) * ML_HEAD_DIM) for h in heads]

    def chunk(c, carry):
        rows = pl.ds(pl.multiple_of(c * ML_CHUNK, ML_CHUNK), ML_CHUNK)
        q = [qk_ref[rows, cl] * (ML_HEAD_DIM ** -0.5) for cl in cols]
        k = [qk_ref[rows, pl.ds(ML_WIDTH + h * ML_HEAD_DIM, ML_HEAD_DIM)] for h in heads]
        v = [v_ref[rows, cl] for cl in cols]
        b_col = [bcol_ref[rows, h:h + 1] for h in heads]
        i_col = [icol_ref[rows, h:h + 1] for h in heads]
        b_row = [brow_ref[c, h:h + 1, :] for h in heads]
        i_row = [irow_ref[c, h:h + 1, :] for h in heads]
        m_prev = [m_ref[h, 0:1, 0:1] for h in heads]
        n_prev = [n_ref[h, 0:1, :] for h in heads]
        ct_prev = [ct_ref[h] for h in heads]
        qb = [x.astype(BF16) for x in q]
        kb = [x.astype(BF16) for x in k]
        vb = [x.astype(BF16) for x in v]
        s_qk = _each(_bdot_nt, qb, kb)
        inter = _each(_bdot, qb, ct_prev)
        k_t = [jnp.transpose(x) for x in k]
        log_d = [jnp.where(causal, bc - br + ir, -jnp.inf) for bc, br, ir in zip(b_col, b_row, i_row)]
        a_log = [bc + mp for bc, mp in zip(b_col, m_prev)]
        m_s = [jnp.maximum(al, jnp.max(ld, axis=-1, keepdims=True)) for al, ld in zip(a_log, log_d)]
        inter_w = [jnp.exp(al - ms) for al, ms in zip(a_log, m_s)]
        qk = [s * jnp.exp(ld - ms) for s, ld, ms in zip(s_qk, log_d, m_s)]
        intra = _each(_bdot, qk, vb)
        b_last = [bc[ML_CHUNK - 1:ML_CHUNK, :] for bc in b_col]
        w_end = [bl - bc + ic for bl, bc, ic in zip(b_last, b_col, i_col)]
        m_new = [jnp.maximum(bl + mp, jnp.max(we, axis=0, keepdims=True)) for bl, mp, we in zip(b_last, m_prev, w_end)]
        dec = [jnp.exp(bl + mp - mn) for bl, mp, mn in zip(b_last, m_prev, m_new)]
        wts = [jnp.exp(we - mn) for we, mn in zip(w_end, m_new)]
        ct_up = _each(_bdot, k_t, [x * w for x, w in zip(v, wts)])
        for h in heads:
            ct_ref[h] = dec[h] * ct_prev[h] + ct_up[h]
            n_ref[h, 0:1, :] = dec[h] * n_prev[h] + jnp.sum(k[h] * wts[h], axis=0, keepdims=True)
            m_ref[h, 0:1, 0:1] = m_new[h]
            num = inter_w[h] * inter[h] + intra[h]
            den = (inter_w[h] * jnp.sum(q[h] * n_prev[h], axis=-1, keepdims=True)
                   + jnp.sum(qk[h], axis=-1, keepdims=True))
            hid = num / jnp.maximum(jnp.abs(den), jnp.exp(-m_s[h]))
            hid = hid * jax.nn.sigmoid(og_ref[rows, cols[h]])
            mu = jnp.mean(hid, axis=-1, keepdims=True)
            dev = hid - mu
            var = jnp.mean(dev * dev, axis=-1, keepdims=True)
            y_ref[rows, cols[h]] = (dev * lax.rsqrt(var + ML_NORM_EPS) * g_ref[:, cols[h]]).astype(y_ref.dtype)
        return carry

    lax.fori_loop(0, ML_CHUNKS_PER_STEP, chunk, 0)


def _mlstm(ml, gates_if, l, p):
    s = ml.shape[0]
    nc = s // ML_CHUNK
    i_pre = gates_if[:, :ML_HEADS] + p['ml_gate_b'][l, 0]
    f_pre = gates_if[:, ML_HEADS:] + p['ml_gate_b'][l, 1]
    lf = jax.nn.log_sigmoid(f_pre)
    b_cum = jnp.cumsum(lf.reshape(nc, ML_CHUNK, ML_HEADS), axis=1)
    b_col = b_cum.reshape(s, ML_HEADS)
    b_row = jnp.transpose(b_cum, (0, 2, 1))
    i_row = jnp.transpose(i_pre.reshape(nc, ML_CHUNK, ML_HEADS), (0, 2, 1))
    col = lambda j: pl.BlockSpec((ML_ROWS, ML_WIDTH), lambda i: (i, j))
    gcol = pl.BlockSpec((ML_ROWS, ML_HEADS), lambda i: (i, 0))
    grow = pl.BlockSpec((ML_CHUNKS_PER_STEP, ML_HEADS, ML_CHUNK), lambda i: (i, 0, 0))
    conv_w = p['ml_conv'][l]
    return pl.pallas_call(
        _mlstm_kernel,
        grid=(s // ML_ROWS,),
        in_specs=[pl.BlockSpec((ML_ROWS, 2 * ML_WIDTH), lambda i: (i, 0)), col(2), col(3),
                  pl.BlockSpec(conv_w.shape, lambda i: (0, 0)), gcol, gcol, grow, grow,
                  pl.BlockSpec((1, ML_WIDTH), lambda i: (0, 0))],
        out_specs=pl.BlockSpec((ML_ROWS, ML_WIDTH), lambda i: (i, 0)),
        out_shape=jax.ShapeDtypeStruct((s, ML_WIDTH), BF16),
        scratch_shapes=[pltpu.VMEM((ML_ROWS, 2 * ML_WIDTH), F32),
                        pltpu.VMEM((SUBLANES, 2 * ML_WIDTH), F32),
                        pltpu.VMEM((ML_HEADS, ML_HEAD_DIM, ML_HEAD_DIM), F32),
                        pltpu.VMEM((ML_HEADS, SUBLANES, ML_HEAD_DIM), F32),
                        pltpu.VMEM((ML_HEADS, SUBLANES, LANES), F32)],
        compiler_params=_params("arbitrary"),
        name="mlstm",
    )(ml, ml, ml, conv_w, b_col, i_pre, b_row, i_row, p['ml_ln_g'][l].reshape(1, ML_WIDTH))


def _merge_kernel(x_ref, yr_ref, ya_ref, ym_ref, g0_ref, g1_ref, g2_ref, w0_ref, w1_ref, w2_ref, o_ref):
    x = x_ref[...]
    gate = lambda g_ref: jax.nn.sigmoid(jnp.dot(x, g_ref[...], preferred_element_type=F32))
    y = (gate(g0_ref) * jnp.dot(yr_ref[...], w0_ref[...], preferred_element_type=F32)
         + gate(g1_ref) * jnp.dot(ya_ref[...], w1_ref[...], preferred_element_type=F32)
         + gate(g2_ref) * jnp.dot(ym_ref[...], w2_ref[...], preferred_element_type=F32))
    o_ref[...] = y.astype(o_ref.dtype)


def _merge(xb, y_rw, y_at, y_ml, w_gate, wb_rw, wb_at, wb_ml, tm=512, tn=512):
    m, d = xb.shape
    nj = d // tn
    act = lambda a: pl.BlockSpec((tm, a.shape[1]), lambda j, i: (i, 0))
    gate = lambda b: pl.BlockSpec((d, tn), lambda j, i: (0, b * nj + j))
    wsp = lambda w: pl.BlockSpec((w.shape[0], tn), lambda j, i: (0, j))
    return pl.pallas_call(
        _merge_kernel,
        grid=(nj, m // tm),
        in_specs=[act(xb), act(y_rw), act(y_at), act(y_ml), gate(0), gate(1), gate(2),
                  wsp(wb_rw), wsp(wb_at), wsp(wb_ml)],
        out_specs=pl.BlockSpec((tm, tn), lambda j, i: (i, j)),
        out_shape=jax.ShapeDtypeStruct((m, d), BF16),
        compiler_params=_params("arbitrary", "arbitrary"),
        name="merge",
    )(xb, y_rw, y_at, y_ml, w_gate, w_gate, w_gate, wb_rw, wb_at, wb_ml)


def _expert_kernel(be_ref, new_ref, used_ref, x_ref, w1_ref, w3_ref, w2_ref, o_ref, w1b, w3b, w2b):
    i = pl.program_id(0)

    @pl.when(new_ref[i] == 1)
    def _():
        w1b[...] = w1_ref[...].astype(BF16)
        w3b[...] = w3_ref[...].astype(BF16)
        w2b[...] = w2_ref[...].astype(BF16)

    @pl.when(used_ref[i] == 1)
    def _():
        x = x_ref[...]
        h1 = jnp.dot(x, w1b[...], preferred_element_type=F32)
        h3 = jnp.dot(x, w3b[...], preferred_element_type=F32)
        hid = (h1 * jax.nn.sigmoid(h1) * h3).astype(BF16)
        o_ref[...] = jnp.dot(hid, w2b[...], preferred_element_type=F32)

    @pl.when(used_ref[i] == 0)
    def _():
        o_ref[...] = jnp.zeros_like(o_ref)


def _experts(buf, blk_expert, blk_new, blk_used, w1, w3, w2, l):
    rows, d = buf.shape
    nblk = rows // MOE_ROWS
    up = pl.BlockSpec((None, None, d, MOE_FF), lambda i, be, nw, us: (l, be[i], 0, 0))
    down = pl.BlockSpec((None, None, MOE_FF, d), lambda i, be, nw, us: (l, be[i], 0, 0))
    blk = pl.BlockSpec((MOE_ROWS, d), lambda i, be, nw, us: (i, 0))
    return pl.pallas_call(
        _expert_kernel,
        grid_spec=pltpu.PrefetchScalarGridSpec(
            num_scalar_prefetch=3,
            grid=(nblk,),
            in_specs=[blk, up, up, down],
            out_specs=blk,
            scratch_shapes=[pltpu.VMEM((d, MOE_FF), BF16), pltpu.VMEM((d, MOE_FF), BF16),
                            pltpu.VMEM((MOE_FF, d), BF16)]),
        out_shape=jax.ShapeDtypeStruct((rows, d), F32),
        compiler_params=_params("arbitrary"),
        name="experts",
    )(blk_expert, blk_new, blk_used, buf, w1, w3, w2)


def _moe(x, xb, l, p):
    t, d = x.shape
    n_assign = t * MOE_TOP_K
    w_router = jnp.concatenate([p['moe_w_group'][l], p['moe_w_expert'][l]], axis=1)
    w_router = jnp.pad(w_router, ((0, 0), (0, LANES - w_router.shape[1])))
    logits = _mm(x, w_router, tm=512, exact=True, name="mm_router")
    g_logits = logits[:, :MOE_GROUPS] + p['moe_b_group'][l]
    g_prob = jax.nn.softmax(g_logits, -1)
    g_idx = jnp.argmax(g_logits, -1)
    g_w = jnp.take_along_axis(g_prob, g_idx[:, None], axis=1)
    e_logits = (logits[:, MOE_GROUPS:MOE_GROUPS + MOE_EXPERTS] + p['moe_b_expert'][l]).reshape(
        t, MOE_GROUPS, MOE_EXPERTS_PER_GROUP)
    e_sel = jnp.take_along_axis(e_logits, g_idx[:, None, None], axis=1)[:, 0]
    top_l, top_i = lax.top_k(e_sel, MOE_TOP_K)
    top_w = jax.nn.softmax(top_l, -1) * g_w
    e_flat = (g_idx[:, None] * MOE_EXPERTS_PER_GROUP + top_i).reshape(n_assign).astype(jnp.int32)
    onehot = (e_flat[:, None] == jnp.arange(MOE_EXPERTS, dtype=jnp.int32)[None]).astype(jnp.int32)
    rank = jnp.take_along_axis(jnp.cumsum(onehot, axis=0), e_flat[:, None], axis=1)[:, 0] - 1
    counts = jnp.sum(onehot, axis=0)
    pcounts = (counts + MOE_ROWS - 1) // MOE_ROWS * MOE_ROWS
    pends = jnp.cumsum(pcounts)
    pstarts = pends - pcounts
    pos = pstarts[e_flat] + rank
    n_rows = n_assign + MOE_EXPERTS * MOE_ROWS
    nblk = n_rows // MOE_ROWS
    src = jnp.zeros((n_rows,), jnp.int32).at[pos].set(jnp.arange(n_assign, dtype=jnp.int32) // MOE_TOP_K)
    blk_start = jnp.arange(nblk, dtype=jnp.int32) * MOE_ROWS
    blk_expert = jnp.clip(jnp.searchsorted(pends, blk_start, side='right'), 0, MOE_EXPERTS - 1).astype(jnp.int32)
    blk_used = (blk_start < pends[-1]).astype(jnp.int32)
    blk_new = jnp.concatenate([jnp.ones((1,), jnp.int32), (blk_expert[1:] != blk_expert[:-1]).astype(jnp.int32)])
    yb = _experts(xb[src], blk_expert, blk_new, blk_used, p['moe_w1'], p['moe_w3'], p['moe_w2'], l)
    y = yb[pos].reshape(t, MOE_TOP_K, d)
    return jnp.sum(y * top_w[..., None], axis=1)


def _mixer(xf, xb, tables, l, p, ln_g, ln_b):
    w_in = p['w_in']
    bf = lambda w: w.astype(BF16)
    rw = _mm(xb, w_in, layer=l, col0=COL_RW, n_cols=COL_AT, tn=1024, name="mm_in_rw")
    at = _mm(xb, bf(w_in[l, :, COL_AT:COL_ML]), tn=AT_WIDTH, name="mm_in_at")
    ml = _mm(xb, bf(w_in[l, :, COL_ML:COL_IF]), tn=1024, name="mm_in_ml")
    gates_if = _mm(xb, w_in[l, :, COL_IF:COL_GATE], name="mm_in_if")
    y_rw = _rwkv7(rw, l, p)
    y_at = _attention(at, tables)
    y_ml = _mlstm(ml, gates_if, l, p)
    wb = p['w_branch'][l]
    y = _merge(xb, y_rw, y_at, y_ml, bf(w_in[l, :, COL_GATE:]), bf(wb[:RW_WIDTH]),
               bf(wb[RW_WIDTH:RW_WIDTH + AT_OUT_WIDTH]), bf(wb[RW_WIDTH + AT_OUT_WIDTH:]))
    return _out_ln(y, bf(p['w_out'][l]), xf, ln_g, ln_b)


def kernel(x, positions, w_in, rw_mu, rw_w0, rw_w1, rw_w2, rw_a0, rw_a1, rw_a2, rw_g1, rw_g2, rw_kk, rw_ka, rw_rk, rw_gn_g, rw_gn_b, ml_conv, ml_gate_b, ml_ln_g, w_branch, w_out, ln1_g, ln1_b, moe_w_group, moe_b_group, moe_w_expert, moe_b_expert, moe_w1, moe_w3, moe_w2, ln2_g, ln2_b):
    p = dict(w_in=w_in, rw_mu=rw_mu, rw_w0=rw_w0, rw_w1=rw_w1, rw_w2=rw_w2, rw_a0=rw_a0, rw_a1=rw_a1,
             rw_a2=rw_a2, rw_g1=rw_g1, rw_g2=rw_g2, rw_kk=rw_kk, rw_ka=rw_ka, rw_rk=rw_rk, rw_gn_g=rw_gn_g,
             rw_gn_b=rw_gn_b, ml_conv=ml_conv, ml_gate_b=ml_gate_b, ml_ln_g=ml_ln_g, w_branch=w_branch,
             w_out=w_out, moe_w_group=moe_w_group, moe_b_group=moe_b_group, moe_w_expert=moe_w_expert,
             moe_b_expert=moe_b_expert, moe_w1=moe_w1, moe_w3=moe_w3, moe_w2=moe_w2)
    batch, seq, d = x.shape
    assert batch == 1
    xf = x.reshape(seq, d)
    xb = xf.astype(BF16)
    tables = _rope_tables(positions.reshape(seq))
    for l in range(w_in.shape[0]):
        xf, xb = _mixer(xf, xb, tables, l, p, ln1_g[l], ln1_b[l])
        h = _moe(xf, xb, l, p)
        xf, xb = _add_ln(xf, h, ln2_g[l], ln2_b[l])
    return xf.reshape(batch, seq, d)
```

```python
import functools

import jax
import jax.numpy as jnp
from jax import lax
from jax.experimental import pallas as pl
from jax.experimental.pallas import tpu as pltpu

F32 = jnp.float32
BF16 = jnp.bfloat16

D_MODEL = 2048
DEPTH = 4
LN_EPS = 1e-5
DN_ALPHA = (2 * DEPTH) ** 0.25

RW_HEADS = 16
RW_HEAD_DIM = 64
RW_WIDTH = RW_HEADS * RW_HEAD_DIM
RW_GN_EPS = 64e-5

AT_GROUPS = ((128, 1), (512, 4), (2048, 16))
AT_HEADS_PER_GROUP = 6
AT_HEADS = AT_HEADS_PER_GROUP * len(AT_GROUPS)
AT_HEAD_DIM = 64
AT_WIDTH = AT_HEADS * AT_HEAD_DIM
AT_OUT_WIDTH = AT_HEADS_PER_GROUP * AT_HEAD_DIM
AT_BLOCK = 128
ROPE_DIMS = AT_HEAD_DIM // 4
ROPE_THETA = 500000.0

ML_HEADS = 8
ML_HEAD_DIM = 128
ML_WIDTH = ML_HEADS * ML_HEAD_DIM
ML_CHUNK = 64
ML_NORM_EPS = 1e-6

N_BRANCHES = 3
COL_RW = 0
COL_AT = 4 * RW_WIDTH
COL_ML = COL_AT + 3 * AT_WIDTH
COL_IF = COL_ML + 4 * ML_WIDTH
COL_GATE = COL_IF + 2 * ML_HEADS
IN_COLS = COL_GATE + N_BRANCHES * D_MODEL

MOE_GROUPS = 4
MOE_EXPERTS_PER_GROUP = 8
MOE_EXPERTS = MOE_GROUPS * MOE_EXPERTS_PER_GROUP
MOE_TOP_K = 2
MOE_FF = 512
MOE_ROWS = 256

LANES = 128
SUBLANES = 8
VMEM_LIMIT = 56 * 1024 * 1024

NT_DIMS = (((1,), (1,)), ((), ()))


def _params(*sem):
    return pltpu.CompilerParams(dimension_semantics=sem, vmem_limit_bytes=VMEM_LIMIT)


def _bdot(a, b):
    return jnp.dot(a.astype(BF16), b.astype(BF16), preferred_element_type=F32)


def _bdot_nt(a, b):
    return lax.dot_general(a.astype(BF16), b.astype(BF16), NT_DIMS, preferred_element_type=F32)


def _each(f, *xs):
    return [f(*a) for a in zip(*xs)]


def _shift_rows(x, prev_row):
    first = lax.broadcasted_iota(jnp.int32, x.shape, 0) == 0
    return jnp.where(first, prev_row, pltpu.roll(x, 1, axis=0))


def _pair_sum(x, first_head):
    lo = jnp.sum(jnp.where(first_head, x, 0.0), axis=-1, keepdims=True)
    hi = jnp.sum(jnp.where(first_head, 0.0, x), axis=-1, keepdims=True)
    return jnp.where(first_head, lo, hi)


def _mm_kernel(a_ref, w_ref, o_ref, wb_ref, *, precision):
    @pl.when(pl.program_id(1) == 0)
    def _():
        wb_ref[...] = w_ref[...].astype(wb_ref.dtype)

    o_ref[...] = jnp.dot(a_ref[...], wb_ref[...], preferred_element_type=F32,
                         precision=precision).astype(o_ref.dtype)


def _mm(a, w, *, layer=None, col0=0, n_cols=None, tm=1024, tn=None, out_dtype=F32, exact=False, name="mm"):
    m, k = a.shape
    n_total = w.shape[-1]
    n_cols = n_total - col0 if n_cols is None else n_cols
    tn = n_cols if tn is None else tn
    tm = min(tm, m)
    assert m % tm == 0 and n_cols % tn == 0 and col0 % tn == 0
    cb0 = col0 // tn
    if layer is None:
        w_spec = pl.BlockSpec((k, tn), lambda j, i: (0, cb0 + j))
    else:
        w_spec = pl.BlockSpec((None, k, tn), lambda j, i: (layer, 0, cb0 + j))
    op_dtype = F32 if exact else BF16
    return pl.pallas_call(
        functools.partial(_mm_kernel, precision=lax.Precision.HIGHEST if exact else None),
        grid=(n_cols // tn, m // tm),
        in_specs=[pl.BlockSpec((tm, k), lambda j, i: (i, 0)), w_spec],
        out_specs=pl.BlockSpec((tm, tn), lambda j, i: (i, j)),
        out_shape=jax.ShapeDtypeStruct((m, n_cols), out_dtype),
        scratch_shapes=[pltpu.VMEM((k, tn), op_dtype)],
        compiler_params=_params("arbitrary", "arbitrary"),
        name=name,
    )(a.astype(op_dtype), w)


def _layer_norm(y, g, b):
    mu = jnp.mean(y, axis=-1, keepdims=True)
    d = y - mu
    var = jnp.mean(d * d, axis=-1, keepdims=True)
    return d * lax.rsqrt(var + LN_EPS) * g + b


def _out_ln_kernel(y_ref, w_ref, x_ref, g_ref, b_ref, o_ref, ob_ref):
    h = jnp.dot(y_ref[...], w_ref[...], preferred_element_type=F32)
    out = _layer_norm(DN_ALPHA * x_ref[...] + h, g_ref[...], b_ref[...])
    o_ref[...] = out
    ob_ref[...] = out.astype(BF16)


def _out_ln(y, w, x, g, b, tm=256):
    m, d = x.shape
    row = pl.BlockSpec((tm, d), lambda i: (i, 0))
    vec = pl.BlockSpec((1, d), lambda i: (0, 0))
    return pl.pallas_call(
        _out_ln_kernel,
        grid=(m // tm,),
        in_specs=[row, pl.BlockSpec((d, d), lambda i: (0, 0)), row, vec, vec],
        out_specs=[row, row],
        out_shape=[jax.ShapeDtypeStruct((m, d), F32), jax.ShapeDtypeStruct((m, d), BF16)],
        compiler_params=_params("arbitrary"),
        name="out_ln",
    )(y, w, x, g.reshape(1, d), b.reshape(1, d))


RW_CHUNK = 64
RW_PAIR = 2 * RW_HEAD_DIM
RW_SUB = 16
RW_PREP_ROWS = 256
(RV_MU, RV_W0, RV_A0, RV_KK, RV_KA, RV_GN_G, RV_GN_B, RV_RK, RV_ROWS) = (0, 6, 7, 8, 9, 10, 11, 12, 16)


def _rwkv_prep_kernel(r_ref, k_ref, v_ref, z_ref, rp_ref, kp_ref, vp_ref, zp_ref, vec_ref,
                      w1_ref, w2_ref, a1_ref, a2_ref, g1_ref, g2_ref,
                      ro_ref, lw_ref, ko_ref, vo_ref, kk_ref, b_ref, g_ref):
    first_block = pl.program_id(0) == 0
    vec = lambda j: vec_ref[j:j + 1, :]
    prev = lambda ref: jnp.where(first_block, 0.0, ref[SUBLANES - 1:SUBLANES, :])
    lerp = lambda u, up, j: u + (_shift_rows(u, up) - u) * vec(RV_MU + j)
    r = lerp(r_ref[...], prev(rp_ref), 0)
    k = lerp(k_ref[...], prev(kp_ref), 1)
    v = lerp(v_ref[...], prev(vp_ref), 2)
    z = z_ref[...]
    z_diff = _shift_rows(z, prev(zp_ref)) - z
    xw, xa, xg = z + z_diff * vec(RV_MU + 3), z + z_diff * vec(RV_MU + 4), z + z_diff * vec(RV_MU + 5)
    w_pre = vec(RV_W0) + _bdot(jnp.tanh(_bdot(xw, w1_ref[...])), w2_ref[...])
    softplus = jnp.maximum(-w_pre, 0.0) + jnp.log(1.0 + jnp.exp(-jnp.abs(w_pre)))
    lw_ref[...] = -jnp.exp(-softplus - 0.5)
    a = jax.nn.sigmoid(vec(RV_A0) + _bdot(_bdot(xa, a1_ref[...]), a2_ref[...]))
    g_ref[...] = _bdot(jax.nn.sigmoid(_bdot(xg, g1_ref[...])), g2_ref[...])
    kk = k * vec(RV_KK)
    first_head = lax.broadcasted_iota(jnp.int32, (kk.shape[0], RW_PAIR), 1) < RW_HEAD_DIM
    for p in range(RW_HEADS // 2):
        sl = slice(p * RW_PAIR, (p + 1) * RW_PAIR)
        kk_p = kk[:, sl]
        kk_p = kk_p / jnp.maximum(jnp.sqrt(_pair_sum(kk_p * kk_p, first_head)), 1e-12)
        kk_ref[:, sl] = kk_p
        b_ref[:, sl] = kk_p * a[:, sl]
    ro_ref[...] = r
    ko_ref[...] = k * (1.0 + (a - 1.0) * vec(RV_KA))
    vo_ref[...] = v


def _rwkv_prep(rw, vecs, l, p):
    s = rw.shape[0]
    tm = min(RW_PREP_ROWS, s)
    per = tm // SUBLANES
    cur = lambda j: pl.BlockSpec((tm, RW_WIDTH), lambda i: (i, j))
    prev = lambda j: pl.BlockSpec((SUBLANES, RW_WIDTH), lambda i: (jnp.maximum(i * per - 1, 0), j))
    lora = lambda w: pl.BlockSpec((None,) + w.shape[1:], lambda i: (l, 0, 0))
    ws = [p['rw_w1'], p['rw_w2'], p['rw_a1'], p['rw_a2'], p['rw_g1'], p['rw_g2']]
    out = pl.BlockSpec((tm, RW_WIDTH), lambda i: (i, 0))
    return pl.pallas_call(
        _rwkv_prep_kernel,
        grid=(s // tm,),
        in_specs=[cur(0), cur(1), cur(2), cur(3), prev(0), prev(1), prev(2), prev(3),
                  pl.BlockSpec((RV_ROWS, RW_WIDTH), lambda i: (0, 0))] + [lora(w) for w in ws],
        out_specs=[out] * 7,
        out_shape=[jax.ShapeDtypeStruct((s, RW_WIDTH), F32)] * 7,
        compiler_params=_params("arbitrary"),
        name="rwkv_prep",
    )(rw, rw, rw, rw, rw, rw, rw, rw, vecs, *ws)


def _rwkv_kernel(r_ref, lw_ref, k_ref, v_ref, kk_ref, b_ref, g_ref, vec_ref, o_ref, zt_ref):
    @pl.when(pl.program_id(0) == 0)
    def _():
        zt_ref[...] = jnp.zeros_like(zt_ref)

    c, n2 = RW_CHUNK, RW_PAIR
    lw = lw_ref[...]
    tri = (lax.broadcasted_iota(jnp.int32, (c, c), 1) <= lax.broadcasted_iota(jnp.int32, (c, c), 0)).astype(BF16)
    lw_hi = lw.astype(BF16)
    rem = lw - lw_hi.astype(F32)
    lw_mid = rem.astype(BF16)
    lw_lo = (rem - lw_mid.astype(F32)).astype(BF16)
    g_in = (jnp.dot(tri, lw_hi, preferred_element_type=F32) + jnp.dot(tri, lw_mid, preferred_element_type=F32)
            + jnp.dot(tri, lw_lo, preferred_element_type=F32))
    g_ex = g_in - lw
    g_last = g_in[c - 1:c, :]
    e_neg = jnp.exp(-g_in)
    e_end = jnp.exp(g_last - g_in)
    kkd = kk_ref[...] * jnp.exp(g_ex)
    rd = r_ref[...] * jnp.exp(g_in)
    kinv = k_ref[...] * e_neg
    binv = b_ref[...] * e_neg
    kd = k_ref[...] * e_end
    bd = b_ref[...] * e_end
    gam_last = jnp.exp(g_last)

    row = lax.broadcasted_iota(jnp.int32, (n2, n2), 0)
    col = lax.broadcasted_iota(jnp.int32, (n2, n2), 1)
    t_idx, s_idx = row % c, col % c
    strict = t_idx > s_idx
    incl = t_idx >= s_idx
    diag_blk = (row // RW_SUB) == (col // RW_SUB)
    eye = (row == col).astype(F32)
    first_head = lax.broadcasted_iota(jnp.int32, (c, n2), 1) < RW_HEAD_DIM

    def embed(x):
        return jnp.concatenate([jnp.where(first_head, x, 0.0), jnp.where(first_head, 0.0, x)], axis=0)

    pairs = range(RW_HEADS // 2)
    sls = [slice(p * n2, (p + 1) * n2) for p in pairs]
    bdot = lambda xs, ys: _each(_bdot, xs, ys)
    v_e = [embed(v_ref[:, sl]) for sl in sls]
    v_b = [x.astype(BF16) for x in v_e]
    lhs = [jnp.concatenate([embed(kkd[:, sl]), embed(rd[:, sl])], axis=0).astype(BF16) for sl in sls]
    rhs = [jnp.concatenate([embed(kinv[:, sl]), embed(binv[:, sl])], axis=0).astype(BF16) for sl in sls]
    zt = [zt_ref[p] for p in pairs]
    zt_b = [x.astype(BF16) for x in zt]
    aa = _each(_bdot_nt, lhs, rhs)
    x1 = _each(lambda a, z: _bdot_nt(a[:n2], z), lhs, zt_b)
    o_z = _each(lambda a, z: _bdot_nt(a[n2:], z), lhs, zt_b)
    a_kkk = [jnp.where(strict, x[:n2, :n2], 0.0).astype(BF16) for x in aa]
    a_kkb = [jnp.where(strict, x[:n2, n2:], 0.0) for x in aa]
    a_rk = [jnp.where(incl, x[n2:, :n2], 0.0).astype(BF16) for x in aa]
    a_rb = [jnp.where(incl, x[n2:, n2:], 0.0).astype(BF16) for x in aa]
    nd = [jnp.where(diag_blk, x, 0.0) for x in a_kkb]
    off = [jnp.where(diag_blk, 0.0, x) for x in a_kkb]
    nd2 = bdot(nd, nd)
    y = _each(lambda z, av: z + av, x1, bdot(a_kkk, v_b))
    o_v = bdot(a_rk, v_b)
    nd4 = bdot(nd2, nd2)
    p1 = bdot([eye - x for x in nd], [eye + x for x in nd2])
    nd8 = bdot(nd4, nd4)
    p2 = bdot(p1, [eye + x for x in nd4])
    d_inv = bdot(p2, [eye + x for x in nd8])
    e1 = bdot(d_inv, off)
    e2 = bdot(e1, e1)
    t_inv = bdot(bdot([eye - x for x in e1], [eye + x for x in e2]), d_inv)
    u = bdot(t_inv, y)
    o_u = bdot(a_rb, u)
    vu_t = [jnp.transpose(jnp.concatenate([a, b], axis=0)) for a, b in zip(v_e, u)]
    kb = [jnp.concatenate([embed(kd[:, sl]), -embed(bd[:, sl])], axis=0) for sl in sls]
    z_up = bdot(vu_t, kb)
    inv_n = 1.0 / RW_HEAD_DIM
    for p in pairs:
        sl = sls[p]
        zt_ref[p] = zt[p] * gam_last[:, sl] + z_up[p]
        o_e = o_z[p] + o_v[p] - o_u[p]
        out = o_e[:c] + o_e[c:]
        mu = _pair_sum(out, first_head) * inv_n
        dev = out - mu
        var = _pair_sum(dev * dev, first_head) * inv_n
        normed = dev * lax.rsqrt(var + RW_GN_EPS) * vec_ref[RV_GN_G:RV_GN_G + 1, sl] + vec_ref[RV_GN_B:RV_GN_B + 1, sl]
        bonus = _pair_sum(r_ref[:, sl] * k_ref[:, sl] * vec_ref[RV_RK:RV_RK + 1, sl], first_head) * v_ref[:, sl]
        o_ref[:, sl] = ((normed + bonus) * g_ref[:, sl]).astype(o_ref.dtype)


def _rwkv_scan(r, lw, k, v, kk, b, g, vecs):
    s, width = r.shape
    blk = pl.BlockSpec((RW_CHUNK, width), lambda i: (i, 0))
    return pl.pallas_call(
        _rwkv_kernel,
        grid=(s // RW_CHUNK,),
        in_specs=[blk] * 7 + [pl.BlockSpec((RV_ROWS, width), lambda i: (0, 0))],
        out_specs=blk,
        out_shape=jax.ShapeDtypeStruct((s, width), BF16),
        scratch_shapes=[pltpu.VMEM((RW_HEADS // 2, RW_PAIR, RW_PAIR), F32)],
        compiler_params=_params("arbitrary"),
        name="rwkv_scan",
    )(r, lw, k, v, kk, b, g, vecs)


def _rwkv7(rw, l, p):
    rows = [p['rw_mu'][l], p['rw_w0'][l][None], p['rw_a0'][l][None], p['rw_kk'][l][None], p['rw_ka'][l][None],
            p['rw_gn_g'][l][None], p['rw_gn_b'][l][None], p['rw_rk'][l].reshape(1, RW_WIDTH)]
    vecs = jnp.concatenate(rows + [jnp.zeros((RV_ROWS - RV_RK - 1, RW_WIDTH), F32)], axis=0)
    r, lw, k, v, kk, b, g = _rwkv_prep(rw, vecs, l, p)
    return _rwkv_scan(r, lw, k, v, kk, b, g, vecs)


AT_PREP_ROWS = 512
AT_SLABS = AT_WIDTH // LANES


def _rope_kernel(q_ref, k_ref, v_ref, c_ref, s1_ref, s2_ref, qo_ref, ko_ref, vo_ref):
    cos, s_dn, s_up = c_ref[...], s1_ref[...], s2_ref[...]

    def rot(x, scale):
        outs = []
        for j in range(AT_SLABS):
            xs = x[:, j * LANES:(j + 1) * LANES]
            y = xs * cos + pltpu.roll(xs, LANES - ROPE_DIMS // 2, axis=1) * s_dn + pltpu.roll(xs, ROPE_DIMS // 2, axis=1) * s_up
            outs.append(y * scale if scale != 1.0 else y)
        return jnp.concatenate(outs, axis=-1)

    qo_ref[...] = rot(q_ref[...], AT_HEAD_DIM ** -0.5).astype(BF16)
    ko_ref[...] = rot(k_ref[...], 1.0).astype(BF16)
    vo_ref[...] = v_ref[...].astype(BF16)


def _rope_tables(positions):
    s = positions.shape[0]
    half = ROPE_DIMS // 2
    inv_freq = ROPE_THETA ** (-jnp.arange(half, dtype=F32) * 2.0 / ROPE_DIMS)
    ang = positions.astype(F32)[:, None] * inv_freq
    cos, sin = jnp.cos(ang), jnp.sin(ang)
    rest = AT_HEAD_DIM - ROPE_DIMS
    zeros, ones = jnp.zeros((s, rest), F32), jnp.ones((s, rest), F32)
    z_half = jnp.zeros((s, half), F32)
    head = lambda parts: jnp.tile(jnp.concatenate(parts, axis=1), (1, LANES // AT_HEAD_DIM))
    return head([cos, cos, ones]), head([-sin, z_half, zeros]), head([z_half, sin, zeros])


def _rope(at, tables):
    s = at.shape[0]
    tm = min(AT_PREP_ROWS, s)
    col = lambda j: pl.BlockSpec((tm, AT_WIDTH), lambda i: (i, j))
    tab = pl.BlockSpec((tm, LANES), lambda i: (i, 0))
    return pl.pallas_call(
        _rope_kernel,
        grid=(s // tm,),
        in_specs=[col(0), col(1), col(2), tab, tab, tab],
        out_specs=[col(0)] * 3,
        out_shape=[jax.ShapeDtypeStruct((s, AT_WIDTH), BF16)] * 3,
        compiler_params=_params("arbitrary"),
        name="attn_rope",
    )(at, at, at, *tables)


def _attn_kernel(q_ref, kp_ref, kc_ref, vp_ref, vc_ref, o_ref, lse_ref):
    n = pl.program_id(1)
    qi = lax.broadcasted_iota(jnp.int32, (AT_BLOCK, AT_BLOCK), 0)
    kj = lax.broadcasted_iota(jnp.int32, (AT_BLOCK, AT_BLOCK), 1)
    mask_c = kj <= qi
    mask_p = (kj >= qi) & (n > 0)
    heads = range(AT_HEADS_PER_GROUP)
    sls = [slice(h * AT_HEAD_DIM, (h + 1) * AT_HEAD_DIM) for h in heads]
    q = [q_ref[:, sl] for sl in sls]
    s_c = [jnp.where(mask_c, lax.dot_general(a, kc_ref[:, sl], NT_DIMS, preferred_element_type=F32), -jnp.inf)
           for a, sl in zip(q, sls)]
    s_p = [jnp.where(mask_p, lax.dot_general(a, kp_ref[:, sl], NT_DIMS, preferred_element_type=F32), -jnp.inf)
           for a, sl in zip(q, sls)]
    m = [jnp.maximum(jnp.max(a, axis=-1, keepdims=True), jnp.max(b, axis=-1, keepdims=True)) for a, b in zip(s_c, s_p)]
    p_c = [jnp.exp(a - mm) for a, mm in zip(s_c, m)]
    p_p = [jnp.exp(a - mm) for a, mm in zip(s_p, m)]
    den = [jnp.sum(a, axis=-1, keepdims=True) + jnp.sum(b, axis=-1, keepdims=True) for a, b in zip(p_c, p_p)]
    o_c = [jnp.dot((a / d).astype(BF16), vc_ref[:, sl], preferred_element_type=F32) for a, d, sl in zip(p_c, den, sls)]
    o_p = [jnp.dot((a / d).astype(BF16), vp_ref[:, sl], preferred_element_type=F32) for a, d, sl in zip(p_p, den, sls)]
    o_ref[...] = jnp.concatenate([a + b for a, b in zip(o_c, o_p)], axis=-1)
    lse_ref[...] = jnp.concatenate(
        [jnp.broadcast_to(mm + jnp.log(d), (AT_BLOCK, AT_HEAD_DIM)) for mm, d in zip(m, den)], axis=-1)


def _dilated_attention(q, k, v, g, dilation):
    s = q.shape[0]
    length = s // dilation
    nb = length // AT_BLOCK
    assert nb * AT_BLOCK * dilation == s
    per_row = AT_WIDTH // AT_OUT_WIDTH
    view = lambda u: u.reshape(length, dilation * AT_WIDTH)
    cur = pl.BlockSpec((AT_BLOCK, AT_OUT_WIDTH), lambda r, n: (n, r * per_row + g))
    prev = pl.BlockSpec((AT_BLOCK, AT_OUT_WIDTH), lambda r, n: (jnp.maximum(n - 1, 0), r * per_row + g))
    out = pl.BlockSpec((AT_BLOCK, AT_OUT_WIDTH), lambda r, n: (n, r))
    o, lse = pl.pallas_call(
        _attn_kernel,
        grid=(dilation, nb),
        in_specs=[cur, prev, cur, prev, cur],
        out_specs=[out, out],
        out_shape=[jax.ShapeDtypeStruct((length, dilation * AT_OUT_WIDTH), F32)] * 2,
        compiler_params=_params("arbitrary", "arbitrary"),
        name=f"attn_d{dilation}",
    )(view(q), view(k), view(k), view(v), view(v))
    return o.reshape(s, AT_OUT_WIDTH), lse.reshape(s, AT_OUT_WIDTH)


def _attn_merge_kernel(o0, o1, o2, l0, l1, l2, y_ref):
    m = jnp.maximum(jnp.maximum(l0[...], l1[...]), l2[...])
    e0, e1, e2 = jnp.exp(l0[...] - m), jnp.exp(l1[...] - m), jnp.exp(l2[...] - m)
    tot = e0 + e1 + e2
    y_ref[...] = ((e0 / tot) * o0[...] + (e1 / tot) * o1[...] + (e2 / tot) * o2[...]).astype(BF16)


def _attention(at, tables):
    s = at.shape[0]
    q, k, v = _rope(at, tables)
    outs, lses = [], []
    for g, (_, dilation) in enumerate(AT_GROUPS):
        o, lse = _dilated_attention(q, k, v, g, dilation)
        outs.append(o)
        lses.append(lse)
    tm = min(AT_PREP_ROWS, s)
    blk = pl.BlockSpec((tm, AT_OUT_WIDTH), lambda i: (i, 0))
    return pl.pallas_call(
        _attn_merge_kernel,
        grid=(s // tm,),
        in_specs=[blk] * 6,
        out_specs=blk,
        out_shape=jax.ShapeDtypeStruct((s, AT_OUT_WIDTH), BF16),
        compiler_params=_params("arbitrary"),
        name="attn_merge",
    )(*outs, *lses)


ML_CHUNKS_PER_STEP = 8
ML_ROWS = ML_CHUNKS_PER_STEP * ML_CHUNK


def _mlstm_kernel(u_ref, v_ref, og_ref, cw_ref, bcol_ref, icol_ref, brow_ref, irow_ref, g_ref, y_ref,
                  qk_ref, tail_ref, ct_ref, n_ref, m_ref):
    @pl.when(pl.program_id(0) == 0)
    def _():
        tail_ref[...] = jnp.zeros_like(tail_ref)
        ct_ref[...] = jnp.zeros_like(ct_ref)
        n_ref[...] = jnp.zeros_like(n_ref)
        m_ref[...] = jnp.full_like(m_ref, -jnp.inf)

    u = u_ref[...]
    taps = cw_ref.shape[0]
    tail = tail_ref[...]
    row8 = lax.broadcasted_iota(jnp.int32, tail.shape, 0)
    conv = u * cw_ref[taps - 1:taps, :]
    for back in range(1, taps):
        shifted = pltpu.roll(u, back, axis=0)
        head = jnp.where(row8 < back, pltpu.roll(tail, back, axis=0), shifted[:SUBLANES])
        shifted = jnp.concatenate([head, shifted[SUBLANES:]], axis=0)
        conv = conv + shifted * cw_ref[taps - 1 - back:taps - back, :]
    qk_ref[...] = conv * jax.nn.sigmoid(conv)
    tail_ref[...] = u[ML_ROWS - SUBLANES:, :]

    si = lax.broadcasted_iota(jnp.int32, (ML_CHUNK, ML_CHUNK), 0)
    ji = lax.broadcasted_iota(jnp.int32, (ML_CHUNK, ML_CHUNK), 1)
    causal = ji <= si
    heads = range(ML_HEADS)
    cols = [slice(h * ML_HEAD_DIM, (h + 1) * ML_HEAD_DIM) for h in heads]

    def chunk(c, carry):
        rows = pl.ds(pl.multiple_of(c * ML_CHUNK, ML_CHUNK), ML_CHUNK)
        q = [qk_ref[rows, cl] * (ML_HEAD_DIM ** -0.5) for cl in cols]
        k = [qk_ref[rows, pl.ds(ML_WIDTH + h * ML_HEAD_DIM, ML_HEAD_DIM)] for h in heads]
        v = [v_ref[rows, cl] for cl in cols]
        b_col = [bcol_ref[rows, h:h + 1] for h in heads]
        i_col = [icol_ref[rows, h:h + 1] for h in heads]
        b_row = [brow_ref[c, h:h + 1, :] for h in heads]
        i_row = [irow_ref[c, h:h + 1, :] for h in heads]
        m_prev = [m_ref[h, 0:1, 0:1] for h in heads]
        n_prev = [n_ref[h, 0:1, :] for h in heads]
        ct_prev = [ct_ref[h] for h in heads]
        qb = [x.astype(BF16) for x in q]
        kb = [x.astype(BF16) for x in k]
        vb = [x.astype(BF16) for x in v]
        s_qk = _each(_bdot_nt, qb, kb)
        inter = _each(_bdot, qb, ct_prev)
        k_t = [jnp.transpose(x) for x in k]
        log_d = [jnp.where(causal, bc - br + ir, -jnp.inf) for bc, br, ir in zip(b_col, b_row, i_row)]
        a_log = [bc + mp for bc, mp in zip(b_col, m_prev)]
        m_s = [jnp.maximum(al, jnp.max(ld, axis=-1, keepdims=True)) for al, ld in zip(a_log, log_d)]
        inter_w = [jnp.exp(al - ms) for al, ms in zip(a_log, m_s)]
        qk = [s * jnp.exp(ld - ms) for s, ld, ms in zip(s_qk, log_d, m_s)]
        intra = _each(_bdot, qk, vb)
        b_last = [bc[ML_CHUNK - 1:ML_CHUNK, :] for bc in b_col]
        w_end = [bl - bc + ic for bl, bc, ic in zip(b_last, b_col, i_col)]
        m_new = [jnp.maximum(bl + mp, jnp.max(we, axis=0, keepdims=True)) for bl, mp, we in zip(b_last, m_prev, w_end)]
        dec = [jnp.exp(bl + mp - mn) for bl, mp, mn in zip(b_last, m_prev, m_new)]
        wts = [jnp.exp(we - mn) for we, mn in zip(w_end, m_new)]
        ct_up = _each(_bdot, k_t, [x * w for x, w in zip(v, wts)])
        for h in heads:
            ct_ref[h] = dec[h] * ct_prev[h] + ct_up[h]
            n_ref[h, 0:1, :] = dec[h] * n_prev[h] + jnp.sum(k[h] * wts[h], axis=0, keepdims=True)
            m_ref[h, 0:1, 0:1] = m_new[h]
            num = inter_w[h] * inter[h] + intra[h]
            den = (inter_w[h] * jnp.sum(q[h] * n_prev[h], axis=-1, keepdims=True)
                   + jnp.sum(qk[h], axis=-1, keepdims=True))
            hid = num / jnp.maximum(jnp.abs(den), jnp.exp(-m_s[h]))
            hid = hid * jax.nn.sigmoid(og_ref[rows, cols[h]])
            mu = jnp.mean(hid, axis=-1, keepdims=True)
            dev = hid - mu
            var = jnp.mean(dev * dev, axis=-1, keepdims=True)
            y_ref[rows, cols[h]] = (dev * lax.rsqrt(var + ML_NORM_EPS) * g_ref[:, cols[h]]).astype(y_ref.dtype)
        return carry

    lax.fori_loop(0, ML_CHUNKS_PER_STEP, chunk, 0)


def _mlstm(ml, gates_if, l, p):
    s = ml.shape[0]
    nc = s // ML_CHUNK
    i_pre = gates_if[:, :ML_HEADS] + p['ml_gate_b'][l, 0]
    f_pre = gates_if[:, ML_HEADS:] + p['ml_gate_b'][l, 1]
    lf = jax.nn.log_sigmoid(f_pre)
    b_cum = jnp.cumsum(lf.reshape(nc, ML_CHUNK, ML_HEADS), axis=1)
    b_col = b_cum.reshape(s, ML_HEADS)
    b_row = jnp.transpose(b_cum, (0, 2, 1))
    i_row = jnp.transpose(i_pre.reshape(nc, ML_CHUNK, ML_HEADS), (0, 2, 1))
    col = lambda j: pl.BlockSpec((ML_ROWS, ML_WIDTH), lambda i: (i, j))
    gcol = pl.BlockSpec((ML_ROWS, ML_HEADS), lambda i: (i, 0))
    grow = pl.BlockSpec((ML_CHUNKS_PER_STEP, ML_HEADS, ML_CHUNK), lambda i: (i, 0, 0))
    conv_w = p['ml_conv'][l]
    return pl.pallas_call(
        _mlstm_kernel,
        grid=(s // ML_ROWS,),
        in_specs=[pl.BlockSpec((ML_ROWS, 2 * ML_WIDTH), lambda i: (i, 0)), col(2), col(3),
                  pl.BlockSpec(conv_w.shape, lambda i: (0, 0)), gcol, gcol, grow, grow,
                  pl.BlockSpec((1, ML_WIDTH), lambda i: (0, 0))],
        out_specs=pl.BlockSpec((ML_ROWS, ML_WIDTH), lambda i: (i, 0)),
        out_shape=jax.ShapeDtypeStruct((s, ML_WIDTH), BF16),
        scratch_shapes=[pltpu.VMEM((ML_ROWS, 2 * ML_WIDTH), F32),
                        pltpu.VMEM((SUBLANES, 2 * ML_WIDTH), F32),
                        pltpu.VMEM((ML_HEADS, ML_HEAD_DIM, ML_HEAD_DIM), F32),
                        pltpu.VMEM((ML_HEADS, SUBLANES, ML_HEAD_DIM), F32),
                        pltpu.VMEM((ML_HEADS, SUBLANES, LANES), F32)],
        compiler_params=_params("arbitrary"),
        name="mlstm",
    )(ml, ml, ml, conv_w, b_col, i_pre, b_row, i_row, p['ml_ln_g'][l].reshape(1, ML_WIDTH))


def _merge_kernel(x_ref, yr_ref, ya_ref, ym_ref, g0_ref, g1_ref, g2_ref, w0_ref, w1_ref, w2_ref, o_ref):
    x = x_ref[...]
    gate = lambda g_ref: jax.nn.sigmoid(jnp.dot(x, g_ref[...], preferred_element_type=F32))
    y = (gate(g0_ref) * jnp.dot(yr_ref[...], w0_ref[...], preferred_element_type=F32)
         + gate(g1_ref) * jnp.dot(ya_ref[...], w1_ref[...], preferred_element_type=F32)
         + gate(g2_ref) * jnp.dot(ym_ref[...], w2_ref[...], preferred_element_type=F32))
    o_ref[...] = y.astype(o_ref.dtype)


def _merge(xb, y_rw, y_at, y_ml, w_gate, wb_rw, wb_at, wb_ml, tm=512, tn=512):
    m, d = xb.shape
    nj = d // tn
    act = lambda a: pl.BlockSpec((tm, a.shape[1]), lambda j, i: (i, 0))
    gate = lambda b: pl.BlockSpec((d, tn), lambda j, i: (0, b * nj + j))
    wsp = lambda w: pl.BlockSpec((w.shape[0], tn), lambda j, i: (0, j))
    return pl.pallas_call(
        _merge_kernel,
        grid=(nj, m // tm),
        in_specs=[act(xb), act(y_rw), act(y_at), act(y_ml), gate(0), gate(1), gate(2),
                  wsp(wb_rw), wsp(wb_at), wsp(wb_ml)],
        out_specs=pl.BlockSpec((tm, tn), lambda j, i: (i, j)),
        out_shape=jax.ShapeDtypeStruct((m, d), BF16),
        compiler_params=_params("arbitrary", "arbitrary"),
        name="merge",
    )(xb, y_rw, y_at, y_ml, w_gate, w_gate, w_gate, wb_rw, wb_at, wb_ml)


ROW_SLABS = D_MODEL // LANES
MOE_LN_ROWS = 256


def _row_gather(idx_ref, base, n, src_hbm, dst, sem):
    def body(j, carry):
        pltpu.make_async_copy(src_hbm.at[idx_ref[base + j]], dst.at[j], sem).start()
        return carry
    lax.fori_loop(0, n, body, 0, unroll=8)


def _row_gather_wait(n, src_hbm, dst, sem):
    pltpu.make_async_copy(src_hbm.at[pl.ds(0, n)], dst.at[pl.ds(0, n)], sem).wait()


def _expert_kernel(be_ref, new_ref, used_ref, src_ref, x_hbm, w1_ref, w3_ref, w2_ref, o_ref,
                   xbuf, xs, w1b, w3b, w2b, sem):
    i = pl.program_id(0)
    nblk = pl.num_programs(0)
    slot = i % 2

    @pl.when(i == 0)
    def _():
        _row_gather(src_ref, 0, MOE_ROWS, x_hbm, xbuf.at[0], sem.at[0])

    nxt = jnp.minimum(i + 1, nblk - 1)

    @pl.when((i + 1 < nblk) & (used_ref[nxt] == 1))
    def _():
        _row_gather(src_ref, (i + 1) * MOE_ROWS, MOE_ROWS, x_hbm, xbuf.at[1 - slot], sem.at[1 - slot])

    @pl.when(new_ref[i] == 1)
    def _():
        w1b[...] = w1_ref[...].astype(BF16)
        w3b[...] = w3_ref[...].astype(BF16)
        w2b[...] = w2_ref[...].astype(BF16)

    @pl.when(used_ref[i] == 1)
    def _():
        _row_gather_wait(MOE_ROWS, x_hbm, xbuf.at[slot], sem.at[slot])
        rows = xbuf.at[slot]
        for s in range(ROW_SLABS):
            xs[:, s * LANES:(s + 1) * LANES] = rows[:, s, :].astype(BF16)
        x = xs[...]
        h1 = jnp.dot(x, w1b[...], preferred_element_type=F32)
        h3 = jnp.dot(x, w3b[...], preferred_element_type=F32)
        hid = (h1 * jax.nn.sigmoid(h1) * h3).astype(BF16)
        out = jnp.dot(hid, w2b[...], preferred_element_type=F32)
        for s in range(ROW_SLABS):
            o_ref[:, s, :] = out[:, s * LANES:(s + 1) * LANES]

    @pl.when(used_ref[i] == 0)
    def _():
        o_ref[...] = jnp.zeros_like(o_ref)


def _experts(x3, src, blk_expert, blk_new, blk_used, w1, w3, w2, l):
    rows = src.shape[0]
    d = D_MODEL
    nblk = rows // MOE_ROWS
    up = pl.BlockSpec((None, None, d, MOE_FF), lambda i, be, nw, us, sr: (l, be[i], 0, 0))
    down = pl.BlockSpec((None, None, MOE_FF, d), lambda i, be, nw, us, sr: (l, be[i], 0, 0))
    return pl.pallas_call(
        _expert_kernel,
        grid_spec=pltpu.PrefetchScalarGridSpec(
            num_scalar_prefetch=4,
            grid=(nblk,),
            in_specs=[pl.BlockSpec(memory_space=pl.ANY), up, up, down],
            out_specs=pl.BlockSpec((MOE_ROWS, ROW_SLABS, LANES), lambda i, be, nw, us, sr: (i, 0, 0)),
            scratch_shapes=[pltpu.VMEM((2, MOE_ROWS, ROW_SLABS, LANES), F32),
                            pltpu.VMEM((MOE_ROWS, d), BF16),
                            pltpu.VMEM((d, MOE_FF), BF16), pltpu.VMEM((d, MOE_FF), BF16),
                            pltpu.VMEM((MOE_FF, d), BF16),
                            pltpu.SemaphoreType.DMA((2,))]),
        out_shape=jax.ShapeDtypeStruct((rows, ROW_SLABS, LANES), F32),
        compiler_params=_params("arbitrary"),
        name="experts",
    )(blk_expert, blk_new, blk_used, src, x3, w1, w3, w2)


def _moe_ln_kernel(pos_ref, x_ref, w_ref, y_hbm, g_ref, b_ref, o_ref, ob_ref, ybuf, z_ref, sem):
    i = pl.program_id(0)
    nblk = pl.num_programs(0)
    tm = MOE_LN_ROWS
    slot = i % 2

    def gather(blk, dst, s):
        def body(j, carry):
            for k in range(MOE_TOP_K):
                row = pos_ref[(blk * tm + j) * MOE_TOP_K + k]
                pltpu.make_async_copy(y_hbm.at[row], dst.at[k * tm + j], s).start()
            return carry
        lax.fori_loop(0, tm, body, 0, unroll=4)

    @pl.when(i == 0)
    def _():
        gather(0, ybuf.at[0], sem.at[0])

    @pl.when(i + 1 < nblk)
    def _():
        gather(i + 1, ybuf.at[1 - slot], sem.at[1 - slot])

    _row_gather_wait(MOE_TOP_K * tm, y_hbm, ybuf.at[slot], sem.at[slot])
    rows = ybuf.at[slot]
    w0, w1 = w_ref[:, 0:1], w_ref[:, 1:2]
    for s in range(ROW_SLABS):
        cs = slice(s * LANES, (s + 1) * LANES)
        z_ref[:, cs] = DN_ALPHA * x_ref[:, cs] + (rows[0:tm, s, :] * w0 + rows[tm:2 * tm, s, :] * w1)
    out = _layer_norm(z_ref[...], g_ref[...], b_ref[...])
    o_ref[...] = out
    ob_ref[...] = out.astype(BF16)


def _moe_ln(x, y3, pos, top_w, g, b):
    t, d = x.shape
    tm = MOE_LN_ROWS
    row = pl.BlockSpec((tm, d), lambda i, pos: (i, 0))
    vec = pl.BlockSpec((1, d), lambda i, pos: (0, 0))
    return pl.pallas_call(
        _moe_ln_kernel,
        grid_spec=pltpu.PrefetchScalarGridSpec(
            num_scalar_prefetch=1,
            grid=(t // tm,),
            in_specs=[row, pl.BlockSpec((tm, MOE_TOP_K), lambda i, pos: (i, 0)),
                      pl.BlockSpec(memory_space=pl.ANY), vec, vec],
            out_specs=[row, row],
            scratch_shapes=[pltpu.VMEM((2, MOE_TOP_K * tm, ROW_SLABS, LANES), F32),
                            pltpu.VMEM((tm, d), F32),
                            pltpu.SemaphoreType.DMA((2,))]),
        out_shape=[jax.ShapeDtypeStruct((t, d), F32), jax.ShapeDtypeStruct((t, d), BF16)],
        compiler_params=_params("arbitrary"),
        name="moe_ln",
    )(pos, x, top_w, y3, g.reshape(1, d), b.reshape(1, d))


def _moe(x, l, p, ln_g, ln_b):
    t, d = x.shape
    n_assign = t * MOE_TOP_K
    w_router = jnp.concatenate([p['moe_w_group'][l], p['moe_w_expert'][l]], axis=1)
    w_router = jnp.pad(w_router, ((0, 0), (0, LANES - w_router.shape[1])))
    logits = _mm(x, w_router, tm=512, exact=True, name="mm_router")
    g_logits = logits[:, :MOE_GROUPS] + p['moe_b_group'][l]
    g_prob = jax.nn.softmax(g_logits, -1)
    g_idx = jnp.argmax(g_logits, -1)
    g_w = jnp.take_along_axis(g_prob, g_idx[:, None], axis=1)
    e_logits = (logits[:, MOE_GROUPS:MOE_GROUPS + MOE_EXPERTS] + p['moe_b_expert'][l]).reshape(
        t, MOE_GROUPS, MOE_EXPERTS_PER_GROUP)
    e_sel = jnp.take_along_axis(e_logits, g_idx[:, None, None], axis=1)[:, 0]
    top_l, top_i = lax.top_k(e_sel, MOE_TOP_K)
    top_w = jax.nn.softmax(top_l, -1) * g_w
    e_flat = (g_idx[:, None] * MOE_EXPERTS_PER_GROUP + top_i).reshape(n_assign).astype(jnp.int32)
    onehot = (e_flat[:, None] == jnp.arange(MOE_EXPERTS, dtype=jnp.int32)[None]).astype(jnp.int32)
    rank = jnp.take_along_axis(jnp.cumsum(onehot, axis=0), e_flat[:, None], axis=1)[:, 0] - 1
    counts = jnp.sum(onehot, axis=0)
    pcounts = (counts + MOE_ROWS - 1) // MOE_ROWS * MOE_ROWS
    pends = jnp.cumsum(pcounts)
    pstarts = pends - pcounts
    pos = pstarts[e_flat] + rank
    n_rows = n_assign + MOE_EXPERTS * MOE_ROWS
    nblk = n_rows // MOE_ROWS
    src = jnp.zeros((n_rows,), jnp.int32).at[pos].set(jnp.arange(n_assign, dtype=jnp.int32) // MOE_TOP_K)
    blk_start = jnp.arange(nblk, dtype=jnp.int32) * MOE_ROWS
    blk_expert = jnp.clip(jnp.searchsorted(pends, blk_start, side='right'), 0, MOE_EXPERTS - 1).astype(jnp.int32)
    blk_used = (blk_start < pends[-1]).astype(jnp.int32)
    blk_new = jnp.concatenate([jnp.ones((1,), jnp.int32), (blk_expert[1:] != blk_expert[:-1]).astype(jnp.int32)])
    y3 = _experts(x.reshape(t, ROW_SLABS, LANES), src, blk_expert, blk_new, blk_used,
                  p['moe_w1'], p['moe_w3'], p['moe_w2'], l)
    return _moe_ln(x, y3, pos.astype(jnp.int32), top_w, ln_g, ln_b)


def _mixer(xf, xb, tables, l, p, ln_g, ln_b):
    w_in = p['w_in']
    bf = lambda w: w.astype(BF16)
    rw = _mm(xb, w_in, layer=l, col0=COL_RW, n_cols=COL_AT, tn=1024, name="mm_in_rw")
    at = _mm(xb, bf(w_in[l, :, COL_AT:COL_ML]), tn=AT_WIDTH, name="mm_in_at")
    ml = _mm(xb, bf(w_in[l, :, COL_ML:COL_IF]), tn=1024, name="mm_in_ml")
    gates_if = _mm(xb, w_in[l, :, COL_IF:COL_GATE], name="mm_in_if")
    y_rw = _rwkv7(rw, l, p)
    y_at = _attention(at, tables)
    y_ml = _mlstm(ml, gates_if, l, p)
    wb = p['w_branch'][l]
    y = _merge(xb, y_rw, y_at, y_ml, bf(w_in[l, :, COL_GATE:]), bf(wb[:RW_WIDTH]),
               bf(wb[RW_WIDTH:RW_WIDTH + AT_OUT_WIDTH]), bf(wb[RW_WIDTH + AT_OUT_WIDTH:]))
    return _out_ln(y, bf(p['w_out'][l]), xf, ln_g, ln_b)


def kernel(x, positions, w_in, rw_mu, rw_w0, rw_w1, rw_w2, rw_a0, rw_a1, rw_a2, rw_g1, rw_g2, rw_kk, rw_ka, rw_rk, rw_gn_g, rw_gn_b, ml_conv, ml_gate_b, ml_ln_g, w_branch, w_out, ln1_g, ln1_b, moe_w_group, moe_b_group, moe_w_expert, moe_b_expert, moe_w1, moe_w3, moe_w2, ln2_g, ln2_b):
    p = dict(w_in=w_in, rw_mu=rw_mu, rw_w0=rw_w0, rw_w1=rw_w1, rw_w2=rw_w2, rw_a0=rw_a0, rw_a1=rw_a1,
             rw_a2=rw_a2, rw_g1=rw_g1, rw_g2=rw_g2, rw_kk=rw_kk, rw_ka=rw_ka, rw_rk=rw_rk, rw_gn_g=rw_gn_g,
             rw_gn_b=rw_gn_b, ml_conv=ml_conv, ml_gate_b=ml_gate_b, ml_ln_g=ml_ln_g, w_branch=w_branch,
             w_out=w_out, moe_w_group=moe_w_group, moe_b_group=moe_b_group, moe_w_expert=moe_w_expert,
             moe_b_expert=moe_b_expert, moe_w1=moe_w1, moe_w3=moe_w3, moe_w2=moe_w2)
    batch, seq, d = x.shape
    assert batch == 1
    xf = x.reshape(seq, d)
    xb = xf.astype(BF16)
    tables = _rope_tables(positions.reshape(seq))
    for l in range(w_in.shape[0]):
        xf, xb = _mixer(xf, xb, tables, l, p, ln1_g[l], ln1_b[l])
        xf, xb = _moe(xf, l, p, ln2_g[l], ln2_b[l])
    return xf.reshape(batch, seq, d)
```

```python
import functools

import jax
import jax.numpy as jnp
from jax import lax
from jax.experimental import pallas as pl
from jax.experimental.pallas import tpu as pltpu

F32 = jnp.float32
BF16 = jnp.bfloat16

D_MODEL = 2048
DEPTH = 4
LN_EPS = 1e-5
DN_ALPHA = (2 * DEPTH) ** 0.25

RW_HEADS = 16
RW_HEAD_DIM = 64
RW_WIDTH = RW_HEADS * RW_HEAD_DIM
RW_GN_EPS = 64e-5

AT_GROUPS = ((128, 1), (512, 4), (2048, 16))
AT_HEADS_PER_GROUP = 6
AT_HEADS = AT_HEADS_PER_GROUP * len(AT_GROUPS)
AT_HEAD_DIM = 64
AT_WIDTH = AT_HEADS * AT_HEAD_DIM
AT_OUT_WIDTH = AT_HEADS_PER_GROUP * AT_HEAD_DIM
AT_BLOCK = 128
ROPE_DIMS = AT_HEAD_DIM // 4
ROPE_THETA = 500000.0

ML_HEADS = 8
ML_HEAD_DIM = 128
ML_WIDTH = ML_HEADS * ML_HEAD_DIM
ML_CHUNK = 64
ML_NORM_EPS = 1e-6

N_BRANCHES = 3
COL_RW = 0
COL_AT = 4 * RW_WIDTH
COL_ML = COL_AT + 3 * AT_WIDTH
COL_IF = COL_ML + 4 * ML_WIDTH
COL_GATE = COL_IF + 2 * ML_HEADS
IN_COLS = COL_GATE + N_BRANCHES * D_MODEL

MOE_GROUPS = 4
MOE_EXPERTS_PER_GROUP = 8
MOE_EXPERTS = MOE_GROUPS * MOE_EXPERTS_PER_GROUP
MOE_TOP_K = 2
MOE_FF = 512
MOE_ROWS = 256

LANES = 128
SUBLANES = 8
VMEM_LIMIT = 56 * 1024 * 1024

NT_DIMS = (((1,), (1,)), ((), ()))


def _params(*sem):
    return pltpu.CompilerParams(dimension_semantics=sem, vmem_limit_bytes=VMEM_LIMIT)


def _bdot(a, b):
    return jnp.dot(a.astype(BF16), b.astype(BF16), preferred_element_type=F32)


def _bdot_nt(a, b):
    return lax.dot_general(a.astype(BF16), b.astype(BF16), NT_DIMS, preferred_element_type=F32)


def _each(f, *xs):
    return [f(*a) for a in zip(*xs)]


def _shift_rows(x, prev_row):
    first = lax.broadcasted_iota(jnp.int32, x.shape, 0) == 0
    return jnp.where(first, prev_row, pltpu.roll(x, 1, axis=0))


def _pair_sum(x, first_head):
    lo = jnp.sum(jnp.where(first_head, x, 0.0), axis=-1, keepdims=True)
    hi = jnp.sum(jnp.where(first_head, 0.0, x), axis=-1, keepdims=True)
    return jnp.where(first_head, lo, hi)


def _mm_kernel(a_ref, w_ref, o_ref, wb_ref, *, precision):
    @pl.when(pl.program_id(1) == 0)
    def _():
        wb_ref[...] = w_ref[...].astype(wb_ref.dtype)

    o_ref[...] = jnp.dot(a_ref[...], wb_ref[...], preferred_element_type=F32,
                         precision=precision).astype(o_ref.dtype)


def _mm(a, w, *, layer=None, col0=0, n_cols=None, tm=1024, tn=None, out_dtype=F32, exact=False, name="mm"):
    m, k = a.shape
    n_total = w.shape[-1]
    n_cols = n_total - col0 if n_cols is None else n_cols
    tn = n_cols if tn is None else tn
    tm = min(tm, m)
    assert m % tm == 0 and n_cols % tn == 0 and col0 % tn == 0
    cb0 = col0 // tn
    if layer is None:
        w_spec = pl.BlockSpec((k, tn), lambda j, i: (0, cb0 + j))
    else:
        w_spec = pl.BlockSpec((None, k, tn), lambda j, i: (layer, 0, cb0 + j))
    op_dtype = F32 if exact else BF16
    return pl.pallas_call(
        functools.partial(_mm_kernel, precision=lax.Precision.HIGHEST if exact else None),
        grid=(n_cols // tn, m // tm),
        in_specs=[pl.BlockSpec((tm, k), lambda j, i: (i, 0)), w_spec],
        out_specs=pl.BlockSpec((tm, tn), lambda j, i: (i, j)),
        out_shape=jax.ShapeDtypeStruct((m, n_cols), out_dtype),
        scratch_shapes=[pltpu.VMEM((k, tn), op_dtype)],
        compiler_params=_params("arbitrary", "arbitrary"),
        name=name,
    )(a.astype(op_dtype), w)


def _layer_norm(y, g, b):
    mu = jnp.mean(y, axis=-1, keepdims=True)
    d = y - mu
    var = jnp.mean(d * d, axis=-1, keepdims=True)
    return d * lax.rsqrt(var + LN_EPS) * g + b


def _out_ln_kernel(y_ref, w_ref, x_ref, g_ref, b_ref, o_ref, ob_ref):
    h = jnp.dot(y_ref[...], w_ref[...], preferred_element_type=F32)
    out = _layer_norm(DN_ALPHA * x_ref[...] + h, g_ref[...], b_ref[...])
    o_ref[...] = out
    ob_ref[...] = out.astype(BF16)


def _out_ln(y, w, x, g, b, tm=256):
    m, d = x.shape
    row = pl.BlockSpec((tm, d), lambda i: (i, 0))
    vec = pl.BlockSpec((1, d), lambda i: (0, 0))
    return pl.pallas_call(
        _out_ln_kernel,
        grid=(m // tm,),
        in_specs=[row, pl.BlockSpec((d, d), lambda i: (0, 0)), row, vec, vec],
        out_specs=[row, row],
        out_shape=[jax.ShapeDtypeStruct((m, d), F32), jax.ShapeDtypeStruct((m, d), BF16)],
        compiler_params=_params("arbitrary"),
        name="out_ln",
    )(y, w, x, g.reshape(1, d), b.reshape(1, d))


RW_CHUNK = 64
RW_PAIR = 2 * RW_HEAD_DIM
RW_SUB = 16
RW_PREP_ROWS = 256
(RV_MU, RV_W0, RV_A0, RV_KK, RV_KA, RV_GN_G, RV_GN_B, RV_RK, RV_ROWS) = (0, 6, 7, 8, 9, 10, 11, 12, 16)


def _rwkv_prep_kernel(r_ref, k_ref, v_ref, z_ref, rp_ref, kp_ref, vp_ref, zp_ref, vec_ref,
                      w1_ref, w2_ref, a1_ref, a2_ref, g1_ref, g2_ref,
                      ro_ref, lw_ref, ko_ref, vo_ref, kk_ref, b_ref, g_ref):
    first_block = pl.program_id(0) == 0
    vec = lambda j: vec_ref[j:j + 1, :]
    prev = lambda ref: jnp.where(first_block, 0.0, ref[SUBLANES - 1:SUBLANES, :])
    lerp = lambda u, up, j: u + (_shift_rows(u, up) - u) * vec(RV_MU + j)
    r = lerp(r_ref[...], prev(rp_ref), 0)
    k = lerp(k_ref[...], prev(kp_ref), 1)
    v = lerp(v_ref[...], prev(vp_ref), 2)
    z = z_ref[...]
    z_diff = _shift_rows(z, prev(zp_ref)) - z
    xw, xa, xg = z + z_diff * vec(RV_MU + 3), z + z_diff * vec(RV_MU + 4), z + z_diff * vec(RV_MU + 5)
    w_pre = vec(RV_W0) + _bdot(jnp.tanh(_bdot(xw, w1_ref[...])), w2_ref[...])
    softplus = jnp.maximum(-w_pre, 0.0) + jnp.log(1.0 + jnp.exp(-jnp.abs(w_pre)))
    lw_ref[...] = -jnp.exp(-softplus - 0.5)
    a = jax.nn.sigmoid(vec(RV_A0) + _bdot(_bdot(xa, a1_ref[...]), a2_ref[...]))
    g_ref[...] = _bdot(jax.nn.sigmoid(_bdot(xg, g1_ref[...])), g2_ref[...])
    kk = k * vec(RV_KK)
    first_head = lax.broadcasted_iota(jnp.int32, (kk.shape[0], RW_PAIR), 1) < RW_HEAD_DIM
    for p in range(RW_HEADS // 2):
        sl = slice(p * RW_PAIR, (p + 1) * RW_PAIR)
        kk_p = kk[:, sl]
        kk_p = kk_p / jnp.maximum(jnp.sqrt(_pair_sum(kk_p * kk_p, first_head)), 1e-12)
        kk_ref[:, sl] = kk_p
        b_ref[:, sl] = kk_p * a[:, sl]
    ro_ref[...] = r
    ko_ref[...] = k * (1.0 + (a - 1.0) * vec(RV_KA))
    vo_ref[...] = v


def _rwkv_prep(rw, vecs, l, p):
    s = rw.shape[0]
    tm = min(RW_PREP_ROWS, s)
    per = tm // SUBLANES
    cur = lambda j: pl.BlockSpec((tm, RW_WIDTH), lambda i: (i, j))
    prev = lambda j: pl.BlockSpec((SUBLANES, RW_WIDTH), lambda i: (jnp.maximum(i * per - 1, 0), j))
    lora = lambda w: pl.BlockSpec((None,) + w.shape[1:], lambda i: (l, 0, 0))
    ws = [p['rw_w1'], p['rw_w2'], p['rw_a1'], p['rw_a2'], p['rw_g1'], p['rw_g2']]
    out = pl.BlockSpec((tm, RW_WIDTH), lambda i: (i, 0))
    return pl.pallas_call(
        _rwkv_prep_kernel,
        grid=(s // tm,),
        in_specs=[cur(0), cur(1), cur(2), cur(3), prev(0), prev(1), prev(2), prev(3),
                  pl.BlockSpec((RV_ROWS, RW_WIDTH), lambda i: (0, 0))] + [lora(w) for w in ws],
        out_specs=[out] * 7,
        out_shape=[jax.ShapeDtypeStruct((s, RW_WIDTH), F32)] * 7,
        compiler_params=_params("arbitrary"),
        name="rwkv_prep",
    )(rw, rw, rw, rw, rw, rw, rw, rw, vecs, *ws)


def _rwkv_kernel(r_ref, lw_ref, k_ref, v_ref, kk_ref, b_ref, g_ref, vec_ref, o_ref, zt_ref):
    @pl.when(pl.program_id(0) == 0)
    def _():
        zt_ref[...] = jnp.zeros_like(zt_ref)

    c, n2 = RW_CHUNK, RW_PAIR
    lw = lw_ref[...]
    tri = (lax.broadcasted_iota(jnp.int32, (c, c), 1) <= lax.broadcasted_iota(jnp.int32, (c, c), 0)).astype(BF16)
    lw_hi = lw.astype(BF16)
    rem = lw - lw_hi.astype(F32)
    lw_mid = rem.astype(BF16)
    lw_lo = (rem - lw_mid.astype(F32)).astype(BF16)
    g_in = (jnp.dot(tri, lw_hi, preferred_element_type=F32) + jnp.dot(tri, lw_mid, preferred_element_type=F32)
            + jnp.dot(tri, lw_lo, preferred_element_type=F32))
    g_ex = g_in - lw
    g_last = g_in[c - 1:c, :]
    e_neg = jnp.exp(-g_in)
    e_end = jnp.exp(g_last - g_in)
    kkd = kk_ref[...] * jnp.exp(g_ex)
    rd = r_ref[...] * jnp.exp(g_in)
    kinv = k_ref[...] * e_neg
    binv = b_ref[...] * e_neg
    kd = k_ref[...] * e_end
    bd = b_ref[...] * e_end
    gam_last = jnp.exp(g_last)

    row = lax.broadcasted_iota(jnp.int32, (n2, n2), 0)
    col = lax.broadcasted_iota(jnp.int32, (n2, n2), 1)
    t_idx, s_idx = row % c, col % c
    strict = t_idx > s_idx
    incl = t_idx >= s_idx
    diag_blk = (row // RW_SUB) == (col // RW_SUB)
    eye = (row == col).astype(F32)
    first_head = lax.broadcasted_iota(jnp.int32, (c, n2), 1) < RW_HEAD_DIM

    def embed(x):
        return jnp.concatenate([jnp.where(first_head, x, 0.0), jnp.where(first_head, 0.0, x)], axis=0)

    pairs = range(RW_HEADS // 2)
    sls = [slice(p * n2, (p + 1) * n2) for p in pairs]
    bdot = lambda xs, ys: _each(_bdot, xs, ys)
    v_e = [embed(v_ref[:, sl]) for sl in sls]
    v_b = [x.astype(BF16) for x in v_e]
    lhs = [jnp.concatenate([embed(kkd[:, sl]), embed(rd[:, sl])], axis=0).astype(BF16) for sl in sls]
    rhs = [jnp.concatenate([embed(kinv[:, sl]), embed(binv[:, sl])], axis=0).astype(BF16) for sl in sls]
    zt = [zt_ref[p] for p in pairs]
    zt_b = [x.astype(BF16) for x in zt]
    aa = _each(_bdot_nt, lhs, rhs)
    x1 = _each(lambda a, z: _bdot_nt(a[:n2], z), lhs, zt_b)
    o_z = _each(lambda a, z: _bdot_nt(a[n2:], z), lhs, zt_b)
    a_kkk = [jnp.where(strict, x[:n2, :n2], 0.0).astype(BF16) for x in aa]
    a_kkb = [jnp.where(strict, x[:n2, n2:], 0.0) for x in aa]
    a_rk = [jnp.where(incl, x[n2:, :n2], 0.0).astype(BF16) for x in aa]
    a_rb = [jnp.where(incl, x[n2:, n2:], 0.0).astype(BF16) for x in aa]
    nd = [jnp.where(diag_blk, x, 0.0) for x in a_kkb]
    off = [jnp.where(diag_blk, 0.0, x) for x in a_kkb]
    nd2 = bdot(nd, nd)
    y = _each(lambda z, av: z + av, x1, bdot(a_kkk, v_b))
    o_v = bdot(a_rk, v_b)
    nd4 = bdot(nd2, nd2)
    p1 = bdot([eye - x for x in nd], [eye + x for x in nd2])
    nd8 = bdot(nd4, nd4)
    p2 = bdot(p1, [eye + x for x in nd4])
    d_inv = bdot(p2, [eye + x for x in nd8])
    e1 = bdot(d_inv, off)
    e2 = bdot(e1, e1)
    t_inv = bdot(bdot([eye - x for x in e1], [eye + x for x in e2]), d_inv)
    u = bdot(t_inv, y)
    o_u = bdot(a_rb, u)
    vu_t = [jnp.transpose(jnp.concatenate([a, b], axis=0)) for a, b in zip(v_e, u)]
    kb = [jnp.concatenate([embed(kd[:, sl]), -embed(bd[:, sl])], axis=0) for sl in sls]
    z_up = bdot(vu_t, kb)
    inv_n = 1.0 / RW_HEAD_DIM
    for p in pairs:
        sl = sls[p]
        zt_ref[p] = zt[p] * gam_last[:, sl] + z_up[p]
        o_e = o_z[p] + o_v[p] - o_u[p]
        out = o_e[:c] + o_e[c:]
        mu = _pair_sum(out, first_head) * inv_n
        dev = out - mu
        var = _pair_sum(dev * dev, first_head) * inv_n
        normed = dev * lax.rsqrt(var + RW_GN_EPS) * vec_ref[RV_GN_G:RV_GN_G + 1, sl] + vec_ref[RV_GN_B:RV_GN_B + 1, sl]
        bonus = _pair_sum(r_ref[:, sl] * k_ref[:, sl] * vec_ref[RV_RK:RV_RK + 1, sl], first_head) * v_ref[:, sl]
        o_ref[:, sl] = ((normed + bonus) * g_ref[:, sl]).astype(o_ref.dtype)


def _rwkv_scan(r, lw, k, v, kk, b, g, vecs):
    s, width = r.shape
    blk = pl.BlockSpec((RW_CHUNK, width), lambda i: (i, 0))
    return pl.pallas_call(
        _rwkv_kernel,
        grid=(s // RW_CHUNK,),
        in_specs=[blk] * 7 + [pl.BlockSpec((RV_ROWS, width), lambda i: (0, 0))],
        out_specs=blk,
        out_shape=jax.ShapeDtypeStruct((s, width), BF16),
        scratch_shapes=[pltpu.VMEM((RW_HEADS // 2, RW_PAIR, RW_PAIR), F32)],
        compiler_params=_params("arbitrary"),
        name="rwkv_scan",
    )(r, lw, k, v, kk, b, g, vecs)


def _rwkv7(rw, l, p):
    rows = [p['rw_mu'][l], p['rw_w0'][l][None], p['rw_a0'][l][None], p['rw_kk'][l][None], p['rw_ka'][l][None],
            p['rw_gn_g'][l][None], p['rw_gn_b'][l][None], p['rw_rk'][l].reshape(1, RW_WIDTH)]
    vecs = jnp.concatenate(rows + [jnp.zeros((RV_ROWS - RV_RK - 1, RW_WIDTH), F32)], axis=0)
    r, lw, k, v, kk, b, g = _rwkv_prep(rw, vecs, l, p)
    return _rwkv_scan(r, lw, k, v, kk, b, g, vecs)


AT_PREP_ROWS = 512
AT_SLABS = AT_WIDTH // LANES


def _rope_kernel(q_ref, k_ref, v_ref, c_ref, s1_ref, s2_ref, qo_ref, ko_ref, vo_ref):
    cos, s_dn, s_up = c_ref[...], s1_ref[...], s2_ref[...]

    def rot(x, scale):
        outs = []
        for j in range(AT_SLABS):
            xs = x[:, j * LANES:(j + 1) * LANES]
            y = xs * cos + pltpu.roll(xs, LANES - ROPE_DIMS // 2, axis=1) * s_dn + pltpu.roll(xs, ROPE_DIMS // 2, axis=1) * s_up
            outs.append(y * scale if scale != 1.0 else y)
        return jnp.concatenate(outs, axis=-1)

    qo_ref[...] = rot(q_ref[...], AT_HEAD_DIM ** -0.5).astype(BF16)
    ko_ref[...] = rot(k_ref[...], 1.0).astype(BF16)
    vo_ref[...] = v_ref[...].astype(BF16)


def _rope_tables(positions):
    s = positions.shape[0]
    half = ROPE_DIMS // 2
    inv_freq = ROPE_THETA ** (-jnp.arange(half, dtype=F32) * 2.0 / ROPE_DIMS)
    ang = positions.astype(F32)[:, None] * inv_freq
    cos, sin = jnp.cos(ang), jnp.sin(ang)
    rest = AT_HEAD_DIM - ROPE_DIMS
    zeros, ones = jnp.zeros((s, rest), F32), jnp.ones((s, rest), F32)
    z_half = jnp.zeros((s, half), F32)
    head = lambda parts: jnp.tile(jnp.concatenate(parts, axis=1), (1, LANES // AT_HEAD_DIM))
    return head([cos, cos, ones]), head([-sin, z_half, zeros]), head([z_half, sin, zeros])


def _rope(at, tables):
    s = at.shape[0]
    tm = min(AT_PREP_ROWS, s)
    col = lambda j: pl.BlockSpec((tm, AT_WIDTH), lambda i: (i, j))
    tab = pl.BlockSpec((tm, LANES), lambda i: (i, 0))
    return pl.pallas_call(
        _rope_kernel,
        grid=(s // tm,),
        in_specs=[col(0), col(1), col(2), tab, tab, tab],
        out_specs=[col(0)] * 3,
        out_shape=[jax.ShapeDtypeStruct((s, AT_WIDTH), BF16)] * 3,
        compiler_params=_params("arbitrary"),
        name="attn_rope",
    )(at, at, at, *tables)


def _attn_kernel(q_ref, kp_ref, kc_ref, vp_ref, vc_ref, o_ref, lse_ref):
    n = pl.program_id(1)
    qi = lax.broadcasted_iota(jnp.int32, (AT_BLOCK, AT_BLOCK), 0)
    kj = lax.broadcasted_iota(jnp.int32, (AT_BLOCK, AT_BLOCK), 1)
    mask_c = kj <= qi
    mask_p = (kj >= qi) & (n > 0)
    heads = range(AT_HEADS_PER_GROUP)
    sls = [slice(h * AT_HEAD_DIM, (h + 1) * AT_HEAD_DIM) for h in heads]
    q = [q_ref[:, sl] for sl in sls]
    s_c = [jnp.where(mask_c, lax.dot_general(a, kc_ref[:, sl], NT_DIMS, preferred_element_type=F32), -jnp.inf)
           for a, sl in zip(q, sls)]
    s_p = [jnp.where(mask_p, lax.dot_general(a, kp_ref[:, sl], NT_DIMS, preferred_element_type=F32), -jnp.inf)
           for a, sl in zip(q, sls)]
    m = [jnp.maximum(jnp.max(a, axis=-1, keepdims=True), jnp.max(b, axis=-1, keepdims=True)) for a, b in zip(s_c, s_p)]
    p_c = [jnp.exp(a - mm) for a, mm in zip(s_c, m)]
    p_p = [jnp.exp(a - mm) for a, mm in zip(s_p, m)]
    den = [jnp.sum(a, axis=-1, keepdims=True) + jnp.sum(b, axis=-1, keepdims=True) for a, b in zip(p_c, p_p)]
    o_c = [jnp.dot((a / d).astype(BF16), vc_ref[:, sl], preferred_element_type=F32) for a, d, sl in zip(p_c, den, sls)]
    o_p = [jnp.dot((a / d).astype(BF16), vp_ref[:, sl], preferred_element_type=F32) for a, d, sl in zip(p_p, den, sls)]
    o_ref[...] = jnp.concatenate([a + b for a, b in zip(o_c, o_p)], axis=-1)
    lse_ref[...] = jnp.concatenate(
        [jnp.broadcast_to(mm + jnp.log(d), (AT_BLOCK, AT_HEAD_DIM)) for mm, d in zip(m, den)], axis=-1)


def _dilated_attention(q, k, v, g, dilation):
    s = q.shape[0]
    length = s // dilation
    nb = length // AT_BLOCK
    assert nb * AT_BLOCK * dilation == s
    per_row = AT_WIDTH // AT_OUT_WIDTH
    view = lambda u: u.reshape(length, dilation * AT_WIDTH)
    cur = pl.BlockSpec((AT_BLOCK, AT_OUT_WIDTH), lambda r, n: (n, r * per_row + g))
    prev = pl.BlockSpec((AT_BLOCK, AT_OUT_WIDTH), lambda r, n: (jnp.maximum(n - 1, 0), r * per_row + g))
    out = pl.BlockSpec((AT_BLOCK, AT_OUT_WIDTH), lambda r, n: (n, r))
    o, lse = pl.pallas_call(
        _attn_kernel,
        grid=(dilation, nb),
        in_specs=[cur, prev, cur, prev, cur],
        out_specs=[out, out],
        out_shape=[jax.ShapeDtypeStruct((length, dilation * AT_OUT_WIDTH), F32)] * 2,
        compiler_params=_params("arbitrary", "arbitrary"),
        name=f"attn_d{dilation}",
    )(view(q), view(k), view(k), view(v), view(v))
    return o.reshape(s, AT_OUT_WIDTH), lse.reshape(s, AT_OUT_WIDTH)


def _attn_merge_kernel(o0, o1, o2, l0, l1, l2, y_ref):
    m = jnp.maximum(jnp.maximum(l0[...], l1[...]), l2[...])
    e0, e1, e2 = jnp.exp(l0[...] - m), jnp.exp(l1[...] - m), jnp.exp(l2[...] - m)
    tot = e0 + e1 + e2
    y_ref[...] = ((e0 / tot) * o0[...] + (e1 / tot) * o1[...] + (e2 / tot) * o2[...]).astype(BF16)


def _attention(at, tables):
    s = at.shape[0]
    q, k, v = _rope(at, tables)
    outs, lses = [], []
    for g, (_, dilation) in enumerate(AT_GROUPS):
        o, lse = _dilated_attention(q, k, v, g, dilation)
        outs.append(o)
        lses.append(lse)
    tm = min(AT_PREP_ROWS, s)
    blk = pl.BlockSpec((tm, AT_OUT_WIDTH), lambda i: (i, 0))
    return pl.pallas_call(
        _attn_merge_kernel,
        grid=(s // tm,),
        in_specs=[blk] * 6,
        out_specs=blk,
        out_shape=jax.ShapeDtypeStruct((s, AT_OUT_WIDTH), BF16),
        compiler_params=_params("arbitrary"),
        name="attn_merge",
    )(*outs, *lses)


ML_CHUNKS_PER_STEP = 8
ML_ROWS = ML_CHUNKS_PER_STEP * ML_CHUNK
ML_GROUP = 4


def _mlstm_kernel(u_ref, v_ref, og_ref, cw_ref, bb_ref, ib_ref, brow_ref, irow_ref, g_ref, y_ref,
                  qk_ref, ubuf_ref, ct_ref, m_ref):
    @pl.when(pl.program_id(0) == 0)
    def _():
        ubuf_ref[:SUBLANES, :] = jnp.zeros((SUBLANES, ubuf_ref.shape[1]), F32)
        ct_ref[...] = jnp.zeros_like(ct_ref)
        m_ref[...] = jnp.full_like(m_ref, -jnp.inf)

    taps = cw_ref.shape[0]
    ubuf_ref[SUBLANES:, :] = u_ref[...]
    conv = ubuf_ref[SUBLANES:, :] * cw_ref[taps - 1:taps, :]
    for back in range(1, taps):
        conv = conv + ubuf_ref[pl.ds(SUBLANES - back, ML_ROWS), :] * cw_ref[taps - 1 - back:taps - back, :]
    qk_ref[...] = conv * jax.nn.sigmoid(conv)
    ubuf_ref[:SUBLANES, :] = ubuf_ref[ML_ROWS:, :]

    dh, lc = ML_HEAD_DIM, ML_CHUNK
    si = lax.broadcasted_iota(jnp.int32, (lc, lc), 0)
    ji = lax.broadcasted_iota(jnp.int32, (lc, lc), 1)
    causal = ji <= si
    ones_b = jnp.ones((lc, dh), BF16)

    def group(c, rows, heads):
        cols = [slice(h * dh, (h + 1) * dh) for h in heads]
        q = [qk_ref[rows, cl] * (dh ** -0.5) for cl in cols]
        k = [qk_ref[rows, pl.ds(ML_WIDTH + h * dh, dh)] for h in heads]
        v = [v_ref[rows, cl] for cl in cols]
        bb = [bb_ref[rows, cl] for cl in cols]
        ib = [ib_ref[rows, cl] for cl in cols]
        b_row = [brow_ref[c, h:h + 1, :] for h in heads]
        i_row = [irow_ref[c, h:h + 1, :] for h in heads]
        m_prev = [m_ref[h, 0:1, :] for h in heads]
        ct_prev = [ct_ref[h] for h in heads]
        qb = [x.astype(BF16) for x in q]
        kb = [x.astype(BF16) for x in k]
        s_qk = _each(_bdot_nt, qb, kb)
        inter = _each(_bdot, qb, ct_prev)
        k_t = [jnp.transpose(x).astype(BF16) for x in k]
        log_d = [jnp.where(causal, b[:, :lc] - br + ir, -jnp.inf) for b, br, ir in zip(bb, b_row, i_row)]
        a_log = [b + mp for b, mp in zip(bb, m_prev)]
        m_s = [jnp.maximum(al, jnp.max(ld, axis=-1, keepdims=True)) for al, ld in zip(a_log, log_d)]
        inter_w = [jnp.exp(al - ms) for al, ms in zip(a_log, m_s)]
        qk = [s * jnp.exp(ld - ms[:, :lc]) for s, ld, ms in zip(s_qk, log_d, m_s)]
        intra = [_bdot(a, jnp.concatenate([x.astype(BF16), ones_b], axis=1)) for a, x in zip(qk, v)]
        b_last = [b[lc - 1:lc, :] for b in bb]
        w_end = [bl - b + i for bl, b, i in zip(b_last, bb, ib)]
        m_new = [jnp.maximum(bl + mp, jnp.max(we, axis=0, keepdims=True)) for bl, mp, we in zip(b_last, m_prev, w_end)]
        dec = [jnp.exp(bl + mp - mn) for bl, mp, mn in zip(b_last, m_prev, m_new)]
        wts = [jnp.exp(we - mn) for we, mn in zip(w_end, m_new)]
        upd = [_bdot(kt, jnp.concatenate([x * w, w], axis=1)) for kt, x, w in zip(k_t, v, wts)]
        for h, ctp, d, up, mn in zip(heads, ct_prev, dec, upd, m_new):
            ct_ref[h] = jnp.concatenate([d, d], axis=1) * ctp + up
            m_ref[h, 0:1, :] = mn
        num = [w * a[:, :dh] + b[:, :dh] for w, a, b in zip(inter_w, inter, intra)]
        den = [w * a[:, dh:] + b[:, dh:] for w, a, b in zip(inter_w, inter, intra)]
        hid = [n / jnp.maximum(jnp.abs(d), jnp.exp(-ms)) for n, d, ms in zip(num, den, m_s)]
        hid = [x * jax.nn.sigmoid(og_ref[rows, cl]) for x, cl in zip(hid, cols)]
        mu = [jnp.mean(x, axis=-1, keepdims=True) for x in hid]
        dev = [x - m for x, m in zip(hid, mu)]
        var = [jnp.mean(x * x, axis=-1, keepdims=True) for x in dev]
        for cl, x, vr in zip(cols, dev, var):
            y_ref[rows, cl] = (x * lax.rsqrt(vr + ML_NORM_EPS) * g_ref[:, cl]).astype(y_ref.dtype)

    def chunk(c, carry):
        rows = pl.ds(pl.multiple_of(c * lc, lc), lc)
        for h0 in range(0, ML_HEADS, ML_GROUP):
            group(c, rows, range(h0, h0 + ML_GROUP))
        return carry

    lax.fori_loop(0, ML_CHUNKS_PER_STEP, chunk, 0)


def _mlstm(ml, gates_if, l, p):
    s = ml.shape[0]
    nc = s // ML_CHUNK
    i_pre = gates_if[:, :ML_HEADS] + p['ml_gate_b'][l, 0]
    f_pre = gates_if[:, ML_HEADS:] + p['ml_gate_b'][l, 1]
    lf = jax.nn.log_sigmoid(f_pre)
    b_cum = jnp.cumsum(lf.reshape(nc, ML_CHUNK, ML_HEADS), axis=1)
    b_row = jnp.transpose(b_cum, (0, 2, 1))
    i_row = jnp.transpose(i_pre.reshape(nc, ML_CHUNK, ML_HEADS), (0, 2, 1))
    over_lanes = lambda a: jnp.repeat(a.reshape(s, ML_HEADS), ML_HEAD_DIM, axis=1)
    col = lambda j: pl.BlockSpec((ML_ROWS, ML_WIDTH), lambda i: (i, j))
    grow = pl.BlockSpec((ML_CHUNKS_PER_STEP, ML_HEADS, ML_CHUNK), lambda i: (i, 0, 0))
    conv_w = p['ml_conv'][l]
    return pl.pallas_call(
        _mlstm_kernel,
        grid=(s // ML_ROWS,),
        in_specs=[pl.BlockSpec((ML_ROWS, 2 * ML_WIDTH), lambda i: (i, 0)), col(2), col(3),
                  pl.BlockSpec(conv_w.shape, lambda i: (0, 0)), col(0), col(0), grow, grow,
                  pl.BlockSpec((1, ML_WIDTH), lambda i: (0, 0))],
        out_specs=pl.BlockSpec((ML_ROWS, ML_WIDTH), lambda i: (i, 0)),
        out_shape=jax.ShapeDtypeStruct((s, ML_WIDTH), BF16),
        scratch_shapes=[pltpu.VMEM((ML_ROWS, 2 * ML_WIDTH), F32),
                        pltpu.VMEM((ML_ROWS + SUBLANES, 2 * ML_WIDTH), F32),
                        pltpu.VMEM((ML_HEADS, ML_HEAD_DIM, 2 * ML_HEAD_DIM), F32),
                        pltpu.VMEM((ML_HEADS, SUBLANES, ML_HEAD_DIM), F32)],
        compiler_params=_params("arbitrary"),
        name="mlstm",
    )(ml, ml, ml, conv_w, over_lanes(b_cum), over_lanes(i_pre), b_row, i_row,
      p['ml_ln_g'][l].reshape(1, ML_WIDTH))


def _merge_kernel(x_ref, yr_ref, ya_ref, ym_ref, g0_ref, g1_ref, g2_ref, w0_ref, w1_ref, w2_ref, o_ref):
    x = x_ref[...]
    gate = lambda g_ref: jax.nn.sigmoid(jnp.dot(x, g_ref[...], preferred_element_type=F32))
    y = (gate(g0_ref) * jnp.dot(yr_ref[...], w0_ref[...], preferred_element_type=F32)
         + gate(g1_ref) * jnp.dot(ya_ref[...], w1_ref[...], preferred_element_type=F32)
         + gate(g2_ref) * jnp.dot(ym_ref[...], w2_ref[...], preferred_element_type=F32))
    o_ref[...] = y.astype(o_ref.dtype)


def _merge(xb, y_rw, y_at, y_ml, w_gate, wb_rw, wb_at, wb_ml, tm=512, tn=512):
    m, d = xb.shape
    nj = d // tn
    act = lambda a: pl.BlockSpec((tm, a.shape[1]), lambda j, i: (i, 0))
    gate = lambda b: pl.BlockSpec((d, tn), lambda j, i: (0, b * nj + j))
    wsp = lambda w: pl.BlockSpec((w.shape[0], tn), lambda j, i: (0, j))
    return pl.pallas_call(
        _merge_kernel,
        grid=(nj, m // tm),
        in_specs=[act(xb), act(y_rw), act(y_at), act(y_ml), gate(0), gate(1), gate(2),
                  wsp(wb_rw), wsp(wb_at), wsp(wb_ml)],
        out_specs=pl.BlockSpec((tm, tn), lambda j, i: (i, j)),
        out_shape=jax.ShapeDtypeStruct((m, d), BF16),
        compiler_params=_params("arbitrary", "arbitrary"),
        name="merge",
    )(xb, y_rw, y_at, y_ml, w_gate, w_gate, w_gate, wb_rw, wb_at, wb_ml)


MOE_LN_ROWS = 256


def _row_gather(idx_ref, base, n, src_hbm, dst, sem):
    def body(j, carry):
        pltpu.make_async_copy(src_hbm.at[pl.ds(idx_ref[base + j], 1)], dst.at[pl.ds(j, 1)], sem).start()
        return carry
    lax.fori_loop(0, n, body, 0, unroll=8)


def _row_gather_wait(n, src_hbm, dst, sem):
    pltpu.make_async_copy(src_hbm.at[pl.ds(0, n)], dst.at[pl.ds(0, n)], sem).wait()


def _expert_kernel(be_ref, new_ref, used_ref, src_ref, x_hbm, w1_ref, w3_ref, w2_ref, o_ref,
                   xbuf, w1b, w3b, w2b, sem):
    i = pl.program_id(0)
    nblk = pl.num_programs(0)
    slot = i % 2

    @pl.when(i == 0)
    def _():
        _row_gather(src_ref, 0, MOE_ROWS, x_hbm, xbuf.at[0], sem.at[0])

    nxt = jnp.minimum(i + 1, nblk - 1)

    @pl.when((i + 1 < nblk) & (used_ref[nxt] == 1))
    def _():
        _row_gather(src_ref, (i + 1) * MOE_ROWS, MOE_ROWS, x_hbm, xbuf.at[1 - slot], sem.at[1 - slot])

    @pl.when(new_ref[i] == 1)
    def _():
        w1b[...] = w1_ref[...].astype(BF16)
        w3b[...] = w3_ref[...].astype(BF16)
        w2b[...] = w2_ref[...].astype(BF16)

    @pl.when(used_ref[i] == 1)
    def _():
        _row_gather_wait(MOE_ROWS, x_hbm, xbuf.at[slot], sem.at[slot])
        x = xbuf[slot].astype(BF16)
        h1 = jnp.dot(x, w1b[...], preferred_element_type=F32)
        h3 = jnp.dot(x, w3b[...], preferred_element_type=F32)
        hid = (h1 * jax.nn.sigmoid(h1) * h3).astype(BF16)
        o_ref[...] = jnp.dot(hid, w2b[...], preferred_element_type=F32)

    @pl.when(used_ref[i] == 0)
    def _():
        o_ref[...] = jnp.zeros_like(o_ref)


def _experts(x, src, blk_expert, blk_new, blk_used, w1, w3, w2, l):
    rows = src.shape[0]
    d = x.shape[1]
    nblk = rows // MOE_ROWS
    up = pl.BlockSpec((None, None, d, MOE_FF), lambda i, be, nw, us, sr: (l, be[i], 0, 0))
    down = pl.BlockSpec((None, None, MOE_FF, d), lambda i, be, nw, us, sr: (l, be[i], 0, 0))
    return pl.pallas_call(
        _expert_kernel,
        grid_spec=pltpu.PrefetchScalarGridSpec(
            num_scalar_prefetch=4,
            grid=(nblk,),
            in_specs=[pl.BlockSpec(memory_space=pl.ANY), up, up, down],
            out_specs=pl.BlockSpec((MOE_ROWS, d), lambda i, be, nw, us, sr: (i, 0)),
            scratch_shapes=[pltpu.VMEM((2, MOE_ROWS, d), F32),
                            pltpu.VMEM((d, MOE_FF), BF16), pltpu.VMEM((d, MOE_FF), BF16),
                            pltpu.VMEM((MOE_FF, d), BF16),
                            pltpu.SemaphoreType.DMA((2,))]),
        out_shape=jax.ShapeDtypeStruct((rows, d), F32),
        compiler_params=_params("arbitrary"),
        name="experts",
    )(blk_expert, blk_new, blk_used, src, x, w1, w3, w2)


def _moe_ln_kernel(pos_ref, x_ref, w_ref, y_hbm, g_ref, b_ref, o_ref, ob_ref, ybuf, sem):
    i = pl.program_id(0)
    nblk = pl.num_programs(0)
    tm = MOE_LN_ROWS
    slot = i % 2

    def gather(blk, dst, s):
        def body(j, carry):
            for k in range(MOE_TOP_K):
                row = pos_ref[(blk * tm + j) * MOE_TOP_K + k]
                pltpu.make_async_copy(y_hbm.at[pl.ds(row, 1)], dst.at[pl.ds(k * tm + j, 1)], s).start()
            return carry
        lax.fori_loop(0, tm, body, 0, unroll=4)

    @pl.when(i == 0)
    def _():
        gather(0, ybuf.at[0], sem.at[0])

    @pl.when(i + 1 < nblk)
    def _():
        gather(i + 1, ybuf.at[1 - slot], sem.at[1 - slot])

    _row_gather_wait(MOE_TOP_K * tm, y_hbm, ybuf.at[slot], sem.at[slot])
    rows = ybuf.at[slot]
    h = rows[0:tm, :] * w_ref[:, 0:1] + rows[tm:2 * tm, :] * w_ref[:, 1:2]
    out = _layer_norm(DN_ALPHA * x_ref[...] + h, g_ref[...], b_ref[...])
    o_ref[...] = out
    ob_ref[...] = out.astype(BF16)


def _moe_ln(x, y, pos, top_w, g, b):
    t, d = x.shape
    tm = MOE_LN_ROWS
    row = pl.BlockSpec((tm, d), lambda i, pos: (i, 0))
    vec = pl.BlockSpec((1, d), lambda i, pos: (0, 0))
    return pl.pallas_call(
        _moe_ln_kernel,
        grid_spec=pltpu.PrefetchScalarGridSpec(
            num_scalar_prefetch=1,
            grid=(t // tm,),
            in_specs=[row, pl.BlockSpec((tm, MOE_TOP_K), lambda i, pos: (i, 0)),
                      pl.BlockSpec(memory_space=pl.ANY), vec, vec],
            out_specs=[row, row],
            scratch_shapes=[pltpu.VMEM((2, MOE_TOP_K * tm, d), F32),
                            pltpu.SemaphoreType.DMA((2,))]),
        out_shape=[jax.ShapeDtypeStruct((t, d), F32), jax.ShapeDtypeStruct((t, d), BF16)],
        compiler_params=_params("arbitrary"),
        name="moe_ln",
    )(pos, x, top_w, y, g.reshape(1, d), b.reshape(1, d))


def _moe(x, l, p, ln_g, ln_b):
    t, d = x.shape
    n_assign = t * MOE_TOP_K
    w_router = jnp.concatenate([p['moe_w_group'][l], p['moe_w_expert'][l]], axis=1)
    w_router = jnp.pad(w_router, ((0, 0), (0, LANES - w_router.shape[1])))
    logits = _mm(x, w_router, tm=512, exact=True, name="mm_router")
    g_logits = logits[:, :MOE_GROUPS] + p['moe_b_group'][l]
    g_prob = jax.nn.softmax(g_logits, -1)
    g_idx = jnp.argmax(g_logits, -1)
    g_w = jnp.take_along_axis(g_prob, g_idx[:, None], axis=1)
    e_logits = (logits[:, MOE_GROUPS:MOE_GROUPS + MOE_EXPERTS] + p['moe_b_expert'][l]).reshape(
        t, MOE_GROUPS, MOE_EXPERTS_PER_GROUP)
    e_sel = jnp.take_along_axis(e_logits, g_idx[:, None, None], axis=1)[:, 0]
    top_l, top_i = lax.top_k(e_sel, MOE_TOP_K)
    top_w = jax.nn.softmax(top_l, -1) * g_w
    e_flat = (g_idx[:, None] * MOE_EXPERTS_PER_GROUP + top_i).reshape(n_assign).astype(jnp.int32)
    onehot = (e_flat[:, None] == jnp.arange(MOE_EXPERTS, dtype=jnp.int32)[None]).astype(jnp.int32)
    rank = jnp.take_along_axis(jnp.cumsum(onehot, axis=0), e_flat[:, None], axis=1)[:, 0] - 1
    counts = jnp.sum(onehot, axis=0)
    pcounts = (counts + MOE_ROWS - 1) // MOE_ROWS * MOE_ROWS
    pends = jnp.cumsum(pcounts)
    pstarts = pends - pcounts
    pos = pstarts[e_flat] + rank
    n_rows = n_assign + MOE_EXPERTS * MOE_ROWS
    nblk = n_rows // MOE_ROWS
    src = jnp.zeros((n_rows,), jnp.int32).at[pos].set(jnp.arange(n_assign, dtype=jnp.int32) // MOE_TOP_K)
    blk_start = jnp.arange(nblk, dtype=jnp.int32) * MOE_ROWS
    blk_expert = jnp.clip(jnp.searchsorted(pends, blk_start, side='right'), 0, MOE_EXPERTS - 1).astype(jnp.int32)
    blk_used = (blk_start < pends[-1]).astype(jnp.int32)
    blk_new = jnp.concatenate([jnp.ones((1,), jnp.int32), (blk_expert[1:] != blk_expert[:-1]).astype(jnp.int32)])
    y = _experts(x, src, blk_expert, blk_new, blk_used, p['moe_w1'], p['moe_w3'], p['moe_w2'], l)
    return _moe_ln(x, y, pos.astype(jnp.int32), top_w, ln_g, ln_b)


def _mixer(xf, xb, tables, l, p, ln_g, ln_b):
    w_in = p['w_in']
    bf = lambda w: w.astype(BF16)
    rw = _mm(xb, w_in, layer=l, col0=COL_RW, n_cols=COL_AT, tn=1024, name="mm_in_rw")
    at = _mm(xb, bf(w_in[l, :, COL_AT:COL_ML]), tn=AT_WIDTH, name="mm_in_at")
    ml = _mm(xb, bf(w_in[l, :, COL_ML:COL_IF]), tn=1024, name="mm_in_ml")
    gates_if = _mm(xb, w_in[l, :, COL_IF:COL_GATE], name="mm_in_if")
    y_rw = _rwkv7(rw, l, p)
    y_at = _attention(at, tables)
    y_ml = _mlstm(ml, gates_if, l, p)
    wb = p['w_branch'][l]
    y = _merge(xb, y_rw, y_at, y_ml, bf(w_in[l, :, COL_GATE:]), bf(wb[:RW_WIDTH]),
               bf(wb[RW_WIDTH:RW_WIDTH + AT_OUT_WIDTH]), bf(wb[RW_WIDTH + AT_OUT_WIDTH:]))
    return _out_ln(y, bf(p['w_out'][l]), xf, ln_g, ln_b)


def kernel(x, positions, w_in, rw_mu, rw_w0, rw_w1, rw_w2, rw_a0, rw_a1, rw_a2, rw_g1, rw_g2, rw_kk, rw_ka, rw_rk, rw_gn_g, rw_gn_b, ml_conv, ml_gate_b, ml_ln_g, w_branch, w_out, ln1_g, ln1_b, moe_w_group, moe_b_group, moe_w_expert, moe_b_expert, moe_w1, moe_w3, moe_w2, ln2_g, ln2_b):
    p = dict(w_in=w_in, rw_mu=rw_mu, rw_w0=rw_w0, rw_w1=rw_w1, rw_w2=rw_w2, rw_a0=rw_a0, rw_a1=rw_a1,
             rw_a2=rw_a2, rw_g1=rw_g1, rw_g2=rw_g2, rw_kk=rw_kk, rw_ka=rw_ka, rw_rk=rw_rk, rw_gn_g=rw_gn_g,
             rw_gn_b=rw_gn_b, ml_conv=ml_conv, ml_gate_b=ml_gate_b, ml_ln_g=ml_ln_g, w_branch=w_branch,
             w_out=w_out, moe_w_group=moe_w_group, moe_b_group=moe_b_group, moe_w_expert=moe_w_expert,
             moe_b_expert=moe_b_expert, moe_w1=moe_w1, moe_w3=moe_w3, moe_w2=moe_w2)
    batch, seq, d = x.shape
    assert batch == 1
    xf = x.reshape(seq, d)
    xb = xf.astype(BF16)
    tables = _rope_tables(positions.reshape(seq))
    for l in range(w_in.shape[0]):
        xf, xb = _mixer(xf, xb, tables, l, p, ln1_g[l], ln1_b[l])
        xf, xb = _moe(xf, l, p, ln2_g[l], ln2_b[l])
    return xf.reshape(batch, seq, d)
```

```python
import functools

import jax
import jax.numpy as jnp
from jax import lax
from jax.experimental import pallas as pl
from jax.experimental.pallas import tpu as pltpu

F32 = jnp.float32
BF16 = jnp.bfloat16

D_MODEL = 2048
DEPTH = 4
LN_EPS = 1e-5
DN_ALPHA = (2 * DEPTH) ** 0.25

RW_HEADS = 16
RW_HEAD_DIM = 64
RW_WIDTH = RW_HEADS * RW_HEAD_DIM
RW_GN_EPS = 64e-5

AT_GROUPS = ((128, 1), (512, 4), (2048, 16))
AT_HEADS_PER_GROUP = 6
AT_HEADS = AT_HEADS_PER_GROUP * len(AT_GROUPS)
AT_HEAD_DIM = 64
AT_WIDTH = AT_HEADS * AT_HEAD_DIM
AT_OUT_WIDTH = AT_HEADS_PER_GROUP * AT_HEAD_DIM
AT_BLOCK = 128
ROPE_DIMS = AT_HEAD_DIM // 4
ROPE_THETA = 500000.0

ML_HEADS = 8
ML_HEAD_DIM = 128
ML_WIDTH = ML_HEADS * ML_HEAD_DIM
ML_CHUNK = 64
ML_NORM_EPS = 1e-6

N_BRANCHES = 3
COL_RW = 0
COL_AT = 4 * RW_WIDTH
COL_ML = COL_AT + 3 * AT_WIDTH
COL_IF = COL_ML + 4 * ML_WIDTH
COL_GATE = COL_IF + 2 * ML_HEADS
IN_COLS = COL_GATE + N_BRANCHES * D_MODEL

MOE_GROUPS = 4
MOE_EXPERTS_PER_GROUP = 8
MOE_EXPERTS = MOE_GROUPS * MOE_EXPERTS_PER_GROUP
MOE_TOP_K = 2
MOE_FF = 512
MOE_ROWS = 256

LANES = 128
SUBLANES = 8
VMEM_LIMIT = 56 * 1024 * 1024

NT_DIMS = (((1,), (1,)), ((), ()))


def _params(*sem):
    return pltpu.CompilerParams(dimension_semantics=sem, vmem_limit_bytes=VMEM_LIMIT)


def _bdot(a, b):
    return jnp.dot(a.astype(BF16), b.astype(BF16), preferred_element_type=F32)


def _bdot_nt(a, b):
    return lax.dot_general(a.astype(BF16), b.astype(BF16), NT_DIMS, preferred_element_type=F32)


def _each(f, *xs):
    return [f(*a) for a in zip(*xs)]


def _shift_rows(x, prev_row):
    first = lax.broadcasted_iota(jnp.int32, x.shape, 0) == 0
    return jnp.where(first, prev_row, pltpu.roll(x, 1, axis=0))


def _pair_sum(x, first_head):
    lo = jnp.sum(jnp.where(first_head, x, 0.0), axis=-1, keepdims=True)
    hi = jnp.sum(jnp.where(first_head, 0.0, x), axis=-1, keepdims=True)
    return jnp.where(first_head, lo, hi)


def _mm_kernel(a_ref, w_ref, o_ref, wb_ref, *, precision):
    @pl.when(pl.program_id(1) == 0)
    def _():
        wb_ref[...] = w_ref[...].astype(wb_ref.dtype)

    o_ref[...] = jnp.dot(a_ref[...], wb_ref[...], preferred_element_type=F32,
                         precision=precision).astype(o_ref.dtype)


def _mm(a, w, *, layer=None, col0=0, n_cols=None, tm=1024, tn=None, out_dtype=F32, exact=False, name="mm"):
    m, k = a.shape
    n_total = w.shape[-1]
    n_cols = n_total - col0 if n_cols is None else n_cols
    tn = n_cols if tn is None else tn
    tm = min(tm, m)
    assert m % tm == 0 and n_cols % tn == 0 and col0 % tn == 0
    cb0 = col0 // tn
    if layer is None:
        w_spec = pl.BlockSpec((k, tn), lambda j, i: (0, cb0 + j))
    else:
        w_spec = pl.BlockSpec((None, k, tn), lambda j, i: (layer, 0, cb0 + j))
    op_dtype = F32 if exact else BF16
    return pl.pallas_call(
        functools.partial(_mm_kernel, precision=lax.Precision.HIGHEST if exact else None),
        grid=(n_cols // tn, m // tm),
        in_specs=[pl.BlockSpec((tm, k), lambda j, i: (i, 0)), w_spec],
        out_specs=pl.BlockSpec((tm, tn), lambda j, i: (i, j)),
        out_shape=jax.ShapeDtypeStruct((m, n_cols), out_dtype),
        scratch_shapes=[pltpu.VMEM((k, tn), op_dtype)],
        compiler_params=_params("arbitrary", "arbitrary"),
        name=name,
    )(a.astype(op_dtype), w)


def _mm_t_kernel(a_ref, w_ref, o_ref, wb_ref):
    @pl.when(pl.program_id(1) == 0)
    def _():
        wb_ref[...] = jnp.transpose(w_ref[0]).astype(BF16)

    o_ref[...] = jnp.dot(a_ref[...], wb_ref[...], preferred_element_type=F32)


def _mm_t(a, w_t, *, layer, row0, n_rows, tm=1024, tn=None, name="mm_t"):
    m, k = a.shape
    tn = n_rows if tn is None else tn
    tm = min(tm, m)
    assert m % tm == 0 and n_rows % tn == 0 and row0 % SUBLANES == 0
    return pl.pallas_call(
        _mm_t_kernel,
        grid=(n_rows // tn, m // tm),
        in_specs=[pl.BlockSpec((tm, k), lambda j, i: (i, 0)),
                  pl.BlockSpec((pl.Element(1), pl.Element(tn), pl.Element(k)),
                               lambda j, i: (layer, pl.multiple_of(row0 + j * tn, SUBLANES), 0))],
        out_specs=pl.BlockSpec((tm, tn), lambda j, i: (i, j)),
        out_shape=jax.ShapeDtypeStruct((m, n_rows), F32),
        scratch_shapes=[pltpu.VMEM((k, tn), BF16)],
        compiler_params=_params("arbitrary", "arbitrary"),
        name=name,
    )(a, w_t)


def _layer_norm(y, g, b):
    mu = jnp.mean(y, axis=-1, keepdims=True)
    d = y - mu
    var = jnp.mean(d * d, axis=-1, keepdims=True)
    return d * lax.rsqrt(var + LN_EPS) * g + b


def _out_ln_kernel(y_ref, w_ref, x_ref, g_ref, b_ref, o_ref, ob_ref):
    h = jnp.dot(y_ref[...], w_ref[...], preferred_element_type=F32)
    out = _layer_norm(DN_ALPHA * x_ref[...] + h, g_ref[...], b_ref[...])
    o_ref[...] = out
    ob_ref[...] = out.astype(BF16)


def _out_ln(y, w, x, g, b, tm=256):
    m, d = x.shape
    row = pl.BlockSpec((tm, d), lambda i: (i, 0))
    vec = pl.BlockSpec((1, d), lambda i: (0, 0))
    return pl.pallas_call(
        _out_ln_kernel,
        grid=(m // tm,),
        in_specs=[row, pl.BlockSpec((d, d), lambda i: (0, 0)), row, vec, vec],
        out_specs=[row, row],
        out_shape=[jax.ShapeDtypeStruct((m, d), F32), jax.ShapeDtypeStruct((m, d), BF16)],
        compiler_params=_params("arbitrary"),
        name="out_ln",
    )(y, w, x, g.reshape(1, d), b.reshape(1, d))


RW_CHUNK = 64
RW_PAIR = 2 * RW_HEAD_DIM
RW_SUB = 16
RW_PREP_ROWS = 256
(RV_MU, RV_W0, RV_A0, RV_KK, RV_KA, RV_GN_G, RV_GN_B, RV_RK, RV_ROWS) = (0, 6, 7, 8, 9, 10, 11, 12, 16)


def _rwkv_prep_kernel(r_ref, k_ref, v_ref, z_ref, rp_ref, kp_ref, vp_ref, zp_ref, vec_ref,
                      w1_ref, w2_ref, a1_ref, a2_ref, g1_ref, g2_ref,
                      ro_ref, lw_ref, ko_ref, vo_ref, kk_ref, b_ref, g_ref):
    first_block = pl.program_id(0) == 0
    vec = lambda j: vec_ref[j:j + 1, :]
    prev = lambda ref: jnp.where(first_block, 0.0, ref[SUBLANES - 1:SUBLANES, :])
    lerp = lambda u, up, j: u + (_shift_rows(u, up) - u) * vec(RV_MU + j)
    r = lerp(r_ref[...], prev(rp_ref), 0)
    k = lerp(k_ref[...], prev(kp_ref), 1)
    v = lerp(v_ref[...], prev(vp_ref), 2)
    z = z_ref[...]
    z_diff = _shift_rows(z, prev(zp_ref)) - z
    xw, xa, xg = z + z_diff * vec(RV_MU + 3), z + z_diff * vec(RV_MU + 4), z + z_diff * vec(RV_MU + 5)
    w_pre = vec(RV_W0) + _bdot(jnp.tanh(_bdot(xw, w1_ref[...])), w2_ref[...])
    softplus = jnp.maximum(-w_pre, 0.0) + jnp.log(1.0 + jnp.exp(-jnp.abs(w_pre)))
    lw_ref[...] = -jnp.exp(-softplus - 0.5)
    a = jax.nn.sigmoid(vec(RV_A0) + _bdot(_bdot(xa, a1_ref[...]), a2_ref[...]))
    g_ref[...] = _bdot(jax.nn.sigmoid(_bdot(xg, g1_ref[...])), g2_ref[...])
    kk = k * vec(RV_KK)
    first_head = lax.broadcasted_iota(jnp.int32, (kk.shape[0], RW_PAIR), 1) < RW_HEAD_DIM
    for p in range(RW_HEADS // 2):
        sl = slice(p * RW_PAIR, (p + 1) * RW_PAIR)
        kk_p = kk[:, sl]
        kk_p = kk_p / jnp.maximum(jnp.sqrt(_pair_sum(kk_p * kk_p, first_head)), 1e-12)
        kk_ref[:, sl] = kk_p
        b_ref[:, sl] = kk_p * a[:, sl]
    ro_ref[...] = r
    ko_ref[...] = k * (1.0 + (a - 1.0) * vec(RV_KA))
    vo_ref[...] = v


def _rwkv_prep(rw, vecs, l, p):
    s = rw.shape[0]
    tm = min(RW_PREP_ROWS, s)
    per = tm // SUBLANES
    cur = lambda j: pl.BlockSpec((tm, RW_WIDTH), lambda i: (i, j))
    prev = lambda j: pl.BlockSpec((SUBLANES, RW_WIDTH), lambda i: (jnp.maximum(i * per - 1, 0), j))
    lora = lambda w: pl.BlockSpec((None,) + w.shape[1:], lambda i: (l, 0, 0))
    ws = [p['rw_w1'], p['rw_w2'], p['rw_a1'], p['rw_a2'], p['rw_g1'], p['rw_g2']]
    out = pl.BlockSpec((tm, RW_WIDTH), lambda i: (i, 0))
    return pl.pallas_call(
        _rwkv_prep_kernel,
        grid=(s // tm,),
        in_specs=[cur(0), cur(1), cur(2), cur(3), prev(0), prev(1), prev(2), prev(3),
                  pl.BlockSpec((RV_ROWS, RW_WIDTH), lambda i: (0, 0))] + [lora(w) for w in ws],
        out_specs=[out] * 7,
        out_shape=[jax.ShapeDtypeStruct((s, RW_WIDTH), F32)] * 7,
        compiler_params=_params("arbitrary"),
        name="rwkv_prep",
    )(rw, rw, rw, rw, rw, rw, rw, rw, vecs, *ws)


def _rwkv_kernel(r_ref, lw_ref, k_ref, v_ref, kk_ref, b_ref, g_ref, vec_ref, o_ref, zt_ref):
    @pl.when(pl.program_id(0) == 0)
    def _():
        zt_ref[...] = jnp.zeros_like(zt_ref)

    c, n2 = RW_CHUNK, RW_PAIR
    lw = lw_ref[...]
    tri = (lax.broadcasted_iota(jnp.int32, (c, c), 1) <= lax.broadcasted_iota(jnp.int32, (c, c), 0)).astype(BF16)
    lw_hi = lw.astype(BF16)
    rem = lw - lw_hi.astype(F32)
    lw_mid = rem.astype(BF16)
    lw_lo = (rem - lw_mid.astype(F32)).astype(BF16)
    g_in = (jnp.dot(tri, lw_hi, preferred_element_type=F32) + jnp.dot(tri, lw_mid, preferred_element_type=F32)
            + jnp.dot(tri, lw_lo, preferred_element_type=F32))
    g_ex = g_in - lw
    g_last = g_in[c - 1:c, :]
    e_neg = jnp.exp(-g_in)
    e_end = jnp.exp(g_last - g_in)
    kkd = kk_ref[...] * jnp.exp(g_ex)
    rd = r_ref[...] * jnp.exp(g_in)
    kinv = k_ref[...] * e_neg
    binv = b_ref[...] * e_neg
    kd = k_ref[...] * e_end
    bd = b_ref[...] * e_end
    gam_last = jnp.exp(g_last)

    row = lax.broadcasted_iota(jnp.int32, (n2, n2), 0)
    col = lax.broadcasted_iota(jnp.int32, (n2, n2), 1)
    t_idx, s_idx = row % c, col % c
    strict = t_idx > s_idx
    incl = t_idx >= s_idx
    diag_blk = (row // RW_SUB) == (col // RW_SUB)
    eye = (row == col).astype(F32)
    first_head = lax.broadcasted_iota(jnp.int32, (c, n2), 1) < RW_HEAD_DIM

    def embed(x):
        return jnp.concatenate([jnp.where(first_head, x, 0.0), jnp.where(first_head, 0.0, x)], axis=0)

    pairs = range(RW_HEADS // 2)
    sls = [slice(p * n2, (p + 1) * n2) for p in pairs]
    bdot = lambda xs, ys: _each(_bdot, xs, ys)
    v_e = [embed(v_ref[:, sl]) for sl in sls]
    v_b = [x.astype(BF16) for x in v_e]
    lhs = [jnp.concatenate([embed(kkd[:, sl]), embed(rd[:, sl])], axis=0).astype(BF16) for sl in sls]
    rhs = [jnp.concatenate([embed(kinv[:, sl]), embed(binv[:, sl])], axis=0).astype(BF16) for sl in sls]
    zt = [zt_ref[p] for p in pairs]
    zt_b = [x.astype(BF16) for x in zt]
    aa = _each(_bdot_nt, lhs, rhs)
    x1 = _each(lambda a, z: _bdot_nt(a[:n2], z), lhs, zt_b)
    o_z = _each(lambda a, z: _bdot_nt(a[n2:], z), lhs, zt_b)
    a_kkk = [jnp.where(strict, x[:n2, :n2], 0.0).astype(BF16) for x in aa]
    a_kkb = [jnp.where(strict, x[:n2, n2:], 0.0) for x in aa]
    a_rk = [jnp.where(incl, x[n2:, :n2], 0.0).astype(BF16) for x in aa]
    a_rb = [jnp.where(incl, x[n2:, n2:], 0.0).astype(BF16) for x in aa]
    nd = [jnp.where(diag_blk, x, 0.0) for x in a_kkb]
    off = [jnp.where(diag_blk, 0.0, x) for x in a_kkb]
    nd2 = bdot(nd, nd)
    y = _each(lambda z, av: z + av, x1, bdot(a_kkk, v_b))
    o_v = bdot(a_rk, v_b)
    nd4 = bdot(nd2, nd2)
    p1 = bdot([eye - x for x in nd], [eye + x for x in nd2])
    nd8 = bdot(nd4, nd4)
    p2 = bdot(p1, [eye + x for x in nd4])
    d_inv = bdot(p2, [eye + x for x in nd8])
    e1 = bdot(d_inv, off)
    e2 = bdot(e1, e1)
    t_inv = bdot(bdot([eye - x for x in e1], [eye + x for x in e2]), d_inv)
    u = bdot(t_inv, y)
    o_u = bdot(a_rb, u)
    vu_t = [jnp.transpose(jnp.concatenate([a, b], axis=0)) for a, b in zip(v_e, u)]
    kb = [jnp.concatenate([embed(kd[:, sl]), -embed(bd[:, sl])], axis=0) for sl in sls]
    z_up = bdot(vu_t, kb)
    inv_n = 1.0 / RW_HEAD_DIM
    for p in pairs:
        sl = sls[p]
        zt_ref[p] = zt[p] * gam_last[:, sl] + z_up[p]
        o_e = o_z[p] + o_v[p] - o_u[p]
        out = o_e[:c] + o_e[c:]
        mu = _pair_sum(out, first_head) * inv_n
        dev = out - mu
        var = _pair_sum(dev * dev, first_head) * inv_n
        normed = dev * lax.rsqrt(var + RW_GN_EPS) * vec_ref[RV_GN_G:RV_GN_G + 1, sl] + vec_ref[RV_GN_B:RV_GN_B + 1, sl]
        bonus = _pair_sum(r_ref[:, sl] * k_ref[:, sl] * vec_ref[RV_RK:RV_RK + 1, sl], first_head) * v_ref[:, sl]
        o_ref[:, sl] = ((normed + bonus) * g_ref[:, sl]).astype(o_ref.dtype)


def _rwkv_scan(r, lw, k, v, kk, b, g, vecs):
    s, width = r.shape
    blk = pl.BlockSpec((RW_CHUNK, width), lambda i: (i, 0))
    return pl.pallas_call(
        _rwkv_kernel,
        grid=(s // RW_CHUNK,),
        in_specs=[blk] * 7 + [pl.BlockSpec((RV_ROWS, width), lambda i: (0, 0))],
        out_specs=blk,
        out_shape=jax.ShapeDtypeStruct((s, width), BF16),
        scratch_shapes=[pltpu.VMEM((RW_HEADS // 2, RW_PAIR, RW_PAIR), F32)],
        compiler_params=_params("arbitrary"),
        name="rwkv_scan",
    )(r, lw, k, v, kk, b, g, vecs)


def _rwkv7(rw, l, p):
    rows = [p['rw_mu'][l], p['rw_w0'][l][None], p['rw_a0'][l][None], p['rw_kk'][l][None], p['rw_ka'][l][None],
            p['rw_gn_g'][l][None], p['rw_gn_b'][l][None], p['rw_rk'][l].reshape(1, RW_WIDTH)]
    vecs = jnp.concatenate(rows + [jnp.zeros((RV_ROWS - RV_RK - 1, RW_WIDTH), F32)], axis=0)
    r, lw, k, v, kk, b, g = _rwkv_prep(rw, vecs, l, p)
    return _rwkv_scan(r, lw, k, v, kk, b, g, vecs)


AT_PREP_ROWS = 512
AT_SLABS = AT_WIDTH // LANES


def _rope_kernel(q_ref, k_ref, c_ref, s1_ref, s2_ref, qo_ref, ko_ref):
    cos, s_dn, s_up = c_ref[...], s1_ref[...], s2_ref[...]

    def rot(x, scale):
        outs = []
        for j in range(AT_SLABS):
            xs = x[:, j * LANES:(j + 1) * LANES]
            y = xs * cos + pltpu.roll(xs, LANES - ROPE_DIMS // 2, axis=1) * s_dn + pltpu.roll(xs, ROPE_DIMS // 2, axis=1) * s_up
            outs.append(y * scale if scale != 1.0 else y)
        return jnp.concatenate(outs, axis=-1)

    qo_ref[...] = rot(q_ref[...], AT_HEAD_DIM ** -0.5)
    ko_ref[...] = rot(k_ref[...], 1.0)


def _rope_tables(positions):
    s = positions.shape[0]
    half = ROPE_DIMS // 2
    inv_freq = ROPE_THETA ** (-jnp.arange(half, dtype=F32) * 2.0 / ROPE_DIMS)
    ang = positions.astype(F32)[:, None] * inv_freq
    cos, sin = jnp.cos(ang), jnp.sin(ang)
    rest = AT_HEAD_DIM - ROPE_DIMS
    zeros, ones = jnp.zeros((s, rest), F32), jnp.ones((s, rest), F32)
    z_half = jnp.zeros((s, half), F32)
    head = lambda parts: jnp.tile(jnp.concatenate(parts, axis=1), (1, LANES // AT_HEAD_DIM))
    return head([cos, cos, ones]), head([-sin, z_half, zeros]), head([z_half, sin, zeros])


def _rope(at, tables):
    s = at.shape[0]
    tm = min(AT_PREP_ROWS, s)
    col = lambda j: pl.BlockSpec((tm, AT_WIDTH), lambda i: (i, j))
    tab = pl.BlockSpec((tm, LANES), lambda i: (i, 0))
    return pl.pallas_call(
        _rope_kernel,
        grid=(s // tm,),
        in_specs=[col(0), col(1), tab, tab, tab],
        out_specs=[col(0)] * 2,
        out_shape=[jax.ShapeDtypeStruct((s, AT_WIDTH), F32)] * 2,
        compiler_params=_params("arbitrary"),
        name="attn_rope",
    )(at, at, *tables)


AT_GROUP_SLABS = AT_OUT_WIDTH // LANES


def _attn_kernel(*refs, dilation):
    ns = AT_GROUP_SLABS
    q_refs, kp_refs, kc_refs, vp_refs, vc_refs = (refs[i * ns:(i + 1) * ns] for i in range(5))
    o_ref, lse_ref, o_s, lse_s = refs[5 * ns:]
    n = pl.program_id(0)
    qi = lax.broadcasted_iota(jnp.int32, (AT_BLOCK, AT_BLOCK), 0)
    kj = lax.broadcasted_iota(jnp.int32, (AT_BLOCK, AT_BLOCK), 1)
    mask_c = kj <= qi
    mask_p = (kj >= qi) & (n > 0)
    per_slab = LANES // AT_HEAD_DIM

    def residue(r, carry):
        rows = pl.ds(r, AT_BLOCK, stride=dilation) if dilation > 1 else slice(None)

        def heads_of(slab_refs):
            slabs = [ref[rows, :].astype(BF16) for ref in slab_refs]
            return [x[:, j * AT_HEAD_DIM:(j + 1) * AT_HEAD_DIM] for x in slabs for j in range(per_slab)]

        q, kc, kp, vc, vp = (heads_of(x) for x in (q_refs, kc_refs, kp_refs, vc_refs, vp_refs))
        s_c = [jnp.where(mask_c, lax.dot_general(a, b, NT_DIMS, preferred_element_type=F32), -jnp.inf)
               for a, b in zip(q, kc)]
        s_p = [jnp.where(mask_p, lax.dot_general(a, b, NT_DIMS, preferred_element_type=F32), -jnp.inf)
               for a, b in zip(q, kp)]
        m = [jnp.maximum(jnp.max(a, axis=-1, keepdims=True), jnp.max(b, axis=-1, keepdims=True))
             for a, b in zip(s_c, s_p)]
        p_c = [jnp.exp(a - mm) for a, mm in zip(s_c, m)]
        p_p = [jnp.exp(a - mm) for a, mm in zip(s_p, m)]
        den = [jnp.sum(a, axis=-1, keepdims=True) + jnp.sum(b, axis=-1, keepdims=True) for a, b in zip(p_c, p_p)]
        o_c = [jnp.dot((a / d).astype(BF16), v, preferred_element_type=F32) for a, d, v in zip(p_c, den, vc)]
        o_p = [jnp.dot((a / d).astype(BF16), v, preferred_element_type=F32) for a, d, v in zip(p_p, den, vp)]
        out = [a + b for a, b in zip(o_c, o_p)]
        lse = [jnp.broadcast_to(mm + jnp.log(d), (AT_BLOCK, AT_HEAD_DIM)) for mm, d in zip(m, den)]
        for t in range(ns):
            o_s[t, rows, :] = jnp.concatenate(out[t * per_slab:(t + 1) * per_slab], axis=-1)
            lse_s[t, rows, :] = jnp.concatenate(lse[t * per_slab:(t + 1) * per_slab], axis=-1)
        return carry

    if dilation > 1:
        lax.fori_loop(0, dilation, residue, 0)
    else:
        residue(0, 0)
    for t in range(ns):
        o_ref[:, t * LANES:(t + 1) * LANES] = o_s[t]
        lse_ref[:, t * LANES:(t + 1) * LANES] = lse_s[t]


def _dilated_attention(q, k, at, g, dilation):
    s = q.shape[0]
    rows = AT_BLOCK * dilation
    assert s % rows == 0
    ns = AT_GROUP_SLABS
    qk_col = g * ns
    v_col = 2 * (AT_WIDTH // LANES) + g * ns
    cur = lambda c: [pl.BlockSpec((rows, LANES), lambda n, t=t: (n, c + t)) for t in range(ns)]
    prev = lambda c: [pl.BlockSpec((rows, LANES), lambda n, t=t: (jnp.maximum(n - 1, 0), c + t)) for t in range(ns)]
    out = pl.BlockSpec((rows, AT_OUT_WIDTH), lambda n: (n, 0))
    return pl.pallas_call(
        functools.partial(_attn_kernel, dilation=dilation),
        grid=(s // rows,),
        in_specs=cur(qk_col) + prev(qk_col) + cur(qk_col) + prev(v_col) + cur(v_col),
        out_specs=[out, out],
        out_shape=[jax.ShapeDtypeStruct((s, AT_OUT_WIDTH), F32)] * 2,
        scratch_shapes=[pltpu.VMEM((ns, rows, LANES), F32)] * 2,
        compiler_params=_params("arbitrary"),
        name=f"attn_d{dilation}",
    )(*([q] * ns + [k] * (2 * ns) + [at] * (2 * ns)))


def _attn_merge_kernel(o0, o1, o2, l0, l1, l2, y_ref):
    m = jnp.maximum(jnp.maximum(l0[...], l1[...]), l2[...])
    e0, e1, e2 = jnp.exp(l0[...] - m), jnp.exp(l1[...] - m), jnp.exp(l2[...] - m)
    tot = e0 + e1 + e2
    y_ref[...] = ((e0 / tot) * o0[...] + (e1 / tot) * o1[...] + (e2 / tot) * o2[...]).astype(BF16)


def _attention(at, tables):
    s = at.shape[0]
    q, k = _rope(at, tables)
    outs, lses = [], []
    for g, (_, dilation) in enumerate(AT_GROUPS):
        o, lse = _dilated_attention(q, k, at, g, dilation)
        outs.append(o)
        lses.append(lse)
    tm = min(AT_PREP_ROWS, s)
    blk = pl.BlockSpec((tm, AT_OUT_WIDTH), lambda i: (i, 0))
    return pl.pallas_call(
        _attn_merge_kernel,
        grid=(s // tm,),
        in_specs=[blk] * 6,
        out_specs=blk,
        out_shape=jax.ShapeDtypeStruct((s, AT_OUT_WIDTH), BF16),
        compiler_params=_params("arbitrary"),
        name="attn_merge",
    )(*outs, *lses)


ML_CHUNKS_PER_STEP = 8
ML_ROWS = ML_CHUNKS_PER_STEP * ML_CHUNK
ML_GROUP = 4


def _mlstm_kernel(u_ref, v_ref, og_ref, cw_ref, bb_ref, ib_ref, brow_ref, irow_ref, g_ref, y_ref,
                  qk_ref, ubuf_ref, ct_ref, m_ref):
    @pl.when(pl.program_id(0) == 0)
    def _():
        ubuf_ref[:SUBLANES, :] = jnp.zeros((SUBLANES, ubuf_ref.shape[1]), F32)
        ct_ref[...] = jnp.zeros_like(ct_ref)
        m_ref[...] = jnp.full_like(m_ref, -jnp.inf)

    taps = cw_ref.shape[0]
    ubuf_ref[SUBLANES:, :] = u_ref[...]
    conv = ubuf_ref[SUBLANES:, :] * cw_ref[taps - 1:taps, :]
    for back in range(1, taps):
        conv = conv + ubuf_ref[pl.ds(SUBLANES - back, ML_ROWS), :] * cw_ref[taps - 1 - back:taps - back, :]
    qk_ref[...] = conv * jax.nn.sigmoid(conv)
    ubuf_ref[:SUBLANES, :] = ubuf_ref[ML_ROWS:, :]

    dh, lc = ML_HEAD_DIM, ML_CHUNK
    si = lax.broadcasted_iota(jnp.int32, (lc, lc), 0)
    ji = lax.broadcasted_iota(jnp.int32, (lc, lc), 1)
    causal = ji <= si
    ones_b = jnp.ones((lc, dh), BF16)

    def group(c, rows, heads):
        cols = [slice(h * dh, (h + 1) * dh) for h in heads]
        q = [qk_ref[rows, cl] * (dh ** -0.5) for cl in cols]
        k = [qk_ref[rows, pl.ds(ML_WIDTH + h * dh, dh)] for h in heads]
        v = [v_ref[rows, cl] for cl in cols]
        bb = [bb_ref[rows, cl] for cl in cols]
        ib = [ib_ref[rows, cl] for cl in cols]
        b_row = [brow_ref[c, h:h + 1, :] for h in heads]
        i_row = [irow_ref[c, h:h + 1, :] for h in heads]
        m_prev = [m_ref[h, 0:1, :] for h in heads]
        ct_prev = [ct_ref[h] for h in heads]
        qb = [x.astype(BF16) for x in q]
        kb = [x.astype(BF16) for x in k]
        s_qk = _each(_bdot_nt, qb, kb)
        inter = _each(_bdot, qb, ct_prev)
        k_t = [jnp.transpose(x).astype(BF16) for x in k]
        log_d = [jnp.where(causal, b[:, :lc] - br + ir, -jnp.inf) for b, br, ir in zip(bb, b_row, i_row)]
        a_log = [b + mp for b, mp in zip(bb, m_prev)]
        m_s = [jnp.maximum(al, jnp.max(ld, axis=-1, keepdims=True)) for al, ld in zip(a_log, log_d)]
        inter_w = [jnp.exp(al - ms) for al, ms in zip(a_log, m_s)]
        qk = [s * jnp.exp(ld - ms[:, :lc]) for s, ld, ms in zip(s_qk, log_d, m_s)]
        intra = [_bdot(a, jnp.concatenate([x.astype(BF16), ones_b], axis=1)) for a, x in zip(qk, v)]
        b_last = [b[lc - 1:lc, :] for b in bb]
        w_end = [bl - b + i for bl, b, i in zip(b_last, bb, ib)]
        m_new = [jnp.maximum(bl + mp, jnp.max(we, axis=0, keepdims=True)) for bl, mp, we in zip(b_last, m_prev, w_end)]
        dec = [jnp.exp(bl + mp - mn) for bl, mp, mn in zip(b_last, m_prev, m_new)]
        wts = [jnp.exp(we - mn) for we, mn in zip(w_end, m_new)]
        upd = [_bdot(kt, jnp.concatenate([x * w, w], axis=1)) for kt, x, w in zip(k_t, v, wts)]
        for h, ctp, d, up, mn in zip(heads, ct_prev, dec, upd, m_new):
            ct_ref[h] = jnp.concatenate([d, d], axis=1) * ctp + up
            m_ref[h, 0:1, :] = mn
        num = [w * a[:, :dh] + b[:, :dh] for w, a, b in zip(inter_w, inter, intra)]
        den = [w * a[:, dh:] + b[:, dh:] for w, a, b in zip(inter_w, inter, intra)]
        hid = [n / jnp.maximum(jnp.abs(d), jnp.exp(-ms)) for n, d, ms in zip(num, den, m_s)]
        hid = [x * jax.nn.sigmoid(og_ref[rows, cl]) for x, cl in zip(hid, cols)]
        mu = [jnp.mean(x, axis=-1, keepdims=True) for x in hid]
        dev = [x - m for x, m in zip(hid, mu)]
        var = [jnp.mean(x * x, axis=-1, keepdims=True) for x in dev]
        for cl, x, vr in zip(cols, dev, var):
            y_ref[rows, cl] = (x * lax.rsqrt(vr + ML_NORM_EPS) * g_ref[:, cl]).astype(y_ref.dtype)

    def chunk(c, carry):
        rows = pl.ds(pl.multiple_of(c * lc, lc), lc)
        for h0 in range(0, ML_HEADS, ML_GROUP):
            group(c, rows, range(h0, h0 + ML_GROUP))
        return carry

    lax.fori_loop(0, ML_CHUNKS_PER_STEP, chunk, 0)


def _mlstm(ml, gates_if, l, p):
    s = ml.shape[0]
    nc = s // ML_CHUNK
    i_pre = gates_if[:, :ML_HEADS] + p['ml_gate_b'][l, 0]
    f_pre = gates_if[:, ML_HEADS:] + p['ml_gate_b'][l, 1]
    lf = jax.nn.log_sigmoid(f_pre)
    b_cum = jnp.cumsum(lf.reshape(nc, ML_CHUNK, ML_HEADS), axis=1)
    b_row = jnp.transpose(b_cum, (0, 2, 1))
    i_row = jnp.transpose(i_pre.reshape(nc, ML_CHUNK, ML_HEADS), (0, 2, 1))
    over_lanes = lambda a: jnp.repeat(a.reshape(s, ML_HEADS), ML_HEAD_DIM, axis=1)
    col = lambda j: pl.BlockSpec((ML_ROWS, ML_WIDTH), lambda i: (i, j))
    grow = pl.BlockSpec((ML_CHUNKS_PER_STEP, ML_HEADS, ML_CHUNK), lambda i: (i, 0, 0))
    conv_w = p['ml_conv'][l]
    return pl.pallas_call(
        _mlstm_kernel,
        grid=(s // ML_ROWS,),
        in_specs=[pl.BlockSpec((ML_ROWS, 2 * ML_WIDTH), lambda i: (i, 0)), col(2), col(3),
                  pl.BlockSpec(conv_w.shape, lambda i: (0, 0)), col(0), col(0), grow, grow,
                  pl.BlockSpec((1, ML_WIDTH), lambda i: (0, 0))],
        out_specs=pl.BlockSpec((ML_ROWS, ML_WIDTH), lambda i: (i, 0)),
        out_shape=jax.ShapeDtypeStruct((s, ML_WIDTH), BF16),
        scratch_shapes=[pltpu.VMEM((ML_ROWS, 2 * ML_WIDTH), F32),
                        pltpu.VMEM((ML_ROWS + SUBLANES, 2 * ML_WIDTH), F32),
                        pltpu.VMEM((ML_HEADS, ML_HEAD_DIM, 2 * ML_HEAD_DIM), F32),
                        pltpu.VMEM((ML_HEADS, SUBLANES, ML_HEAD_DIM), F32)],
        compiler_params=_params("arbitrary"),
        name="mlstm",
    )(ml, ml, ml, conv_w, over_lanes(b_cum), over_lanes(i_pre), b_row, i_row,
      p['ml_ln_g'][l].reshape(1, ML_WIDTH))


def _merge_kernel(x_ref, yr_ref, ya_ref, ym_ref, g0_ref, g1_ref, g2_ref, w0_ref, w1_ref, w2_ref, o_ref, gb_ref):
    @pl.when(pl.program_id(1) == 0)
    def _():
        for b, g_ref in enumerate((g0_ref, g1_ref, g2_ref)):
            gb_ref[b] = jnp.transpose(g_ref[0]).astype(BF16)

    x = x_ref[...]
    gate = lambda b: jax.nn.sigmoid(jnp.dot(x, gb_ref[b], preferred_element_type=F32))
    y = (gate(0) * jnp.dot(yr_ref[...], w0_ref[...], preferred_element_type=F32)
         + gate(1) * jnp.dot(ya_ref[...], w1_ref[...], preferred_element_type=F32)
         + gate(2) * jnp.dot(ym_ref[...], w2_ref[...], preferred_element_type=F32))
    o_ref[...] = y.astype(o_ref.dtype)


def _merge(xb, y_rw, y_at, y_ml, w_in_t, l, wb_rw, wb_at, wb_ml, tm=512, tn=512):
    m, d = xb.shape
    nj = d // tn
    act = lambda a: pl.BlockSpec((tm, a.shape[1]), lambda j, i: (i, 0))
    gate = lambda b: pl.BlockSpec((pl.Element(1), pl.Element(tn), pl.Element(d)),
                                  lambda j, i: (l, pl.multiple_of(COL_GATE + b * d + j * tn, SUBLANES), 0))
    wsp = lambda w: pl.BlockSpec((w.shape[0], tn), lambda j, i: (0, j))
    return pl.pallas_call(
        _merge_kernel,
        grid=(nj, m // tm),
        in_specs=[act(xb), act(y_rw), act(y_at), act(y_ml), gate(0), gate(1), gate(2),
                  wsp(wb_rw), wsp(wb_at), wsp(wb_ml)],
        out_specs=pl.BlockSpec((tm, tn), lambda j, i: (i, j)),
        out_shape=jax.ShapeDtypeStruct((m, d), BF16),
        scratch_shapes=[pltpu.VMEM((N_BRANCHES, d, tn), BF16)],
        compiler_params=_params("arbitrary", "arbitrary"),
        name="merge",
    )(xb, y_rw, y_at, y_ml, w_in_t, w_in_t, w_in_t, wb_rw, wb_at, wb_ml)


MOE_LN_ROWS = 256


def _row_gather(idx_ref, base, n, src_hbm, dst, sem):
    def body(j, carry):
        pltpu.make_async_copy(src_hbm.at[pl.ds(idx_ref[base + j], 1)], dst.at[pl.ds(j, 1)], sem).start()
        return carry
    lax.fori_loop(0, n, body, 0, unroll=8)


def _row_gather_wait(n, src_hbm, dst, sem):
    pltpu.make_async_copy(src_hbm.at[pl.ds(0, n)], dst.at[pl.ds(0, n)], sem).wait()


def _expert_kernel(be_ref, new_ref, used_ref, src_ref, x_hbm, w1_ref, w3_ref, w2_ref, o_ref,
                   xbuf, w1b, w3b, w2b, sem):
    i = pl.program_id(0)
    nblk = pl.num_programs(0)
    slot = i % 2

    @pl.when(i == 0)
    def _():
        _row_gather(src_ref, 0, MOE_ROWS, x_hbm, xbuf.at[0], sem.at[0])

    nxt = jnp.minimum(i + 1, nblk - 1)

    @pl.when((i + 1 < nblk) & (used_ref[nxt] == 1))
    def _():
        _row_gather(src_ref, (i + 1) * MOE_ROWS, MOE_ROWS, x_hbm, xbuf.at[1 - slot], sem.at[1 - slot])

    @pl.when(new_ref[i] == 1)
    def _():
        w1b[...] = w1_ref[...].astype(BF16)
        w3b[...] = w3_ref[...].astype(BF16)
        w2b[...] = w2_ref[...].astype(BF16)

    @pl.when(used_ref[i] == 1)
    def _():
        _row_gather_wait(MOE_ROWS, x_hbm, xbuf.at[slot], sem.at[slot])
        x = xbuf[slot].astype(BF16)
        h1 = jnp.dot(x, w1b[...], preferred_element_type=F32)
        h3 = jnp.dot(x, w3b[...], preferred_element_type=F32)
        hid = (h1 * jax.nn.sigmoid(h1) * h3).astype(BF16)
        o_ref[...] = jnp.dot(hid, w2b[...], preferred_element_type=F32)

    @pl.when(used_ref[i] == 0)
    def _():
        o_ref[...] = jnp.zeros_like(o_ref)


def _experts(x, src, blk_expert, blk_new, blk_used, w1, w3, w2, l):
    rows = src.shape[0]
    d = x.shape[1]
    nblk = rows // MOE_ROWS
    up = pl.BlockSpec((None, None, d, MOE_FF), lambda i, be, nw, us, sr: (l, be[i], 0, 0))
    down = pl.BlockSpec((None, None, MOE_FF, d), lambda i, be, nw, us, sr: (l, be[i], 0, 0))
    return pl.pallas_call(
        _expert_kernel,
        grid_spec=pltpu.PrefetchScalarGridSpec(
            num_scalar_prefetch=4,
            grid=(nblk,),
            in_specs=[pl.BlockSpec(memory_space=pl.ANY), up, up, down],
            out_specs=pl.BlockSpec((MOE_ROWS, d), lambda i, be, nw, us, sr: (i, 0)),
            scratch_shapes=[pltpu.VMEM((2, MOE_ROWS, d), F32),
                            pltpu.VMEM((d, MOE_FF), BF16), pltpu.VMEM((d, MOE_FF), BF16),
                            pltpu.VMEM((MOE_FF, d), BF16),
                            pltpu.SemaphoreType.DMA((2,))]),
        out_shape=jax.ShapeDtypeStruct((rows, d), F32),
        compiler_params=_params("arbitrary"),
        name="experts",
    )(blk_expert, blk_new, blk_used, src, x, w1, w3, w2)


def _moe_ln_kernel(pos_ref, x_ref, w_ref, y_hbm, g_ref, b_ref, o_ref, ob_ref, ybuf, sem):
    i = pl.program_id(0)
    nblk = pl.num_programs(0)
    tm = MOE_LN_ROWS
    slot = i % 2

    def gather(blk, dst, s):
        def body(j, carry):
            for k in range(MOE_TOP_K):
                row = pos_ref[(blk * tm + j) * MOE_TOP_K + k]
                pltpu.make_async_copy(y_hbm.at[pl.ds(row, 1)], dst.at[pl.ds(k * tm + j, 1)], s).start()
            return carry
        lax.fori_loop(0, tm, body, 0, unroll=4)

    @pl.when(i == 0)
    def _():
        gather(0, ybuf.at[0], sem.at[0])

    @pl.when(i + 1 < nblk)
    def _():
        gather(i + 1, ybuf.at[1 - slot], sem.at[1 - slot])

    _row_gather_wait(MOE_TOP_K * tm, y_hbm, ybuf.at[slot], sem.at[slot])
    rows = ybuf.at[slot]
    h = rows[0:tm, :] * w_ref[:, 0:1] + rows[tm:2 * tm, :] * w_ref[:, 1:2]
    out = _layer_norm(DN_ALPHA * x_ref[...] + h, g_ref[...], b_ref[...])
    o_ref[...] = out
    ob_ref[...] = out.astype(BF16)


def _moe_ln(x, y, pos, top_w, g, b):
    t, d = x.shape
    tm = MOE_LN_ROWS
    row = pl.BlockSpec((tm, d), lambda i, pos: (i, 0))
    vec = pl.BlockSpec((1, d), lambda i, pos: (0, 0))
    return pl.pallas_call(
        _moe_ln_kernel,
        grid_spec=pltpu.PrefetchScalarGridSpec(
            num_scalar_prefetch=1,
            grid=(t // tm,),
            in_specs=[row, pl.BlockSpec((tm, MOE_TOP_K), lambda i, pos: (i, 0)),
                      pl.BlockSpec(memory_space=pl.ANY), vec, vec],
            out_specs=[row, row],
            scratch_shapes=[pltpu.VMEM((2, MOE_TOP_K * tm, d), F32),
                            pltpu.SemaphoreType.DMA((2,))]),
        out_shape=[jax.ShapeDtypeStruct((t, d), F32), jax.ShapeDtypeStruct((t, d), BF16)],
        compiler_params=_params("arbitrary"),
        name="moe_ln",
    )(pos, x, top_w, y, g.reshape(1, d), b.reshape(1, d))


def _moe(x, l, p, ln_g, ln_b):
    t, d = x.shape
    n_assign = t * MOE_TOP_K
    w_router = jnp.concatenate([p['moe_w_group'][l], p['moe_w_expert'][l]], axis=1)
    w_router = jnp.pad(w_router, ((0, 0), (0, LANES - w_router.shape[1])))
    logits = _mm(x, w_router, tm=512, exact=True, name="mm_router")
    g_logits = logits[:, :MOE_GROUPS] + p['moe_b_group'][l]
    g_prob = jax.nn.softmax(g_logits, -1)
    g_idx = jnp.argmax(g_logits, -1)
    g_w = jnp.take_along_axis(g_prob, g_idx[:, None], axis=1)
    e_logits = (logits[:, MOE_GROUPS:MOE_GROUPS + MOE_EXPERTS] + p['moe_b_expert'][l]).reshape(
        t, MOE_GROUPS, MOE_EXPERTS_PER_GROUP)
    e_sel = jnp.take_along_axis(e_logits, g_idx[:, None, None], axis=1)[:, 0]
    top_l, top_i = lax.top_k(e_sel, MOE_TOP_K)
    top_w = jax.nn.softmax(top_l, -1) * g_w
    e_flat = (g_idx[:, None] * MOE_EXPERTS_PER_GROUP + top_i).reshape(n_assign).astype(jnp.int32)
    onehot = (e_flat[:, None] == jnp.arange(MOE_EXPERTS, dtype=jnp.int32)[None]).astype(jnp.int32)
    rank = jnp.take_along_axis(jnp.cumsum(onehot, axis=0), e_flat[:, None], axis=1)[:, 0] - 1
    counts = jnp.sum(onehot, axis=0)
    pcounts = (counts + MOE_ROWS - 1) // MOE_ROWS * MOE_ROWS
    pends = jnp.cumsum(pcounts)
    pstarts = pends - pcounts
    pos = pstarts[e_flat] + rank
    n_rows = n_assign + MOE_EXPERTS * MOE_ROWS
    nblk = n_rows // MOE_ROWS
    src = jnp.zeros((n_rows,), jnp.int32).at[pos].set(jnp.arange(n_assign, dtype=jnp.int32) // MOE_TOP_K)
    blk_start = jnp.arange(nblk, dtype=jnp.int32) * MOE_ROWS
    blk_expert = jnp.clip(jnp.searchsorted(pends, blk_start, side='right'), 0, MOE_EXPERTS - 1).astype(jnp.int32)
    blk_used = (blk_start < pends[-1]).astype(jnp.int32)
    blk_new = jnp.concatenate([jnp.ones((1,), jnp.int32), (blk_expert[1:] != blk_expert[:-1]).astype(jnp.int32)])
    y = _experts(x, src, blk_expert, blk_new, blk_used, p['moe_w1'], p['moe_w3'], p['moe_w2'], l)
    return _moe_ln(x, y, pos.astype(jnp.int32), top_w, ln_g, ln_b)


def _mixer(xf, xb, tables, l, p, ln_g, ln_b):
    w_in_t = p['w_in_t']
    bf = lambda w: w.astype(BF16)
    rw = _mm_t(xb, w_in_t, layer=l, row0=COL_RW, n_rows=COL_AT - COL_RW, tn=1024, name="mm_in_rw")
    at = _mm_t(xb, w_in_t, layer=l, row0=COL_AT, n_rows=COL_ML - COL_AT, tn=AT_WIDTH, name="mm_in_at")
    ml = _mm_t(xb, w_in_t, layer=l, row0=COL_ML, n_rows=COL_IF - COL_ML, tn=1024, name="mm_in_ml")
    gates_if = _mm_t(xb, w_in_t, layer=l, row0=COL_IF, n_rows=COL_GATE - COL_IF, name="mm_in_if")
    y_rw = _rwkv7(rw, l, p)
    y_at = _attention(at, tables)
    y_ml = _mlstm(ml, gates_if, l, p)
    wb = p['w_branch'][l]
    y = _merge(xb, y_rw, y_at, y_ml, w_in_t, l, bf(wb[:RW_WIDTH]),
               bf(wb[RW_WIDTH:RW_WIDTH + AT_OUT_WIDTH]), bf(wb[RW_WIDTH + AT_OUT_WIDTH:]))
    return _out_ln(y, bf(p['w_out'][l]), xf, ln_g, ln_b)


def kernel(x, positions, w_in, rw_mu, rw_w0, rw_w1, rw_w2, rw_a0, rw_a1, rw_a2, rw_g1, rw_g2, rw_kk, rw_ka, rw_rk, rw_gn_g, rw_gn_b, ml_conv, ml_gate_b, ml_ln_g, w_branch, w_out, ln1_g, ln1_b, moe_w_group, moe_b_group, moe_w_expert, moe_b_expert, moe_w1, moe_w3, moe_w2, ln2_g, ln2_b):
    p = dict(w_in_t=jnp.swapaxes(w_in, 1, 2), rw_mu=rw_mu, rw_w0=rw_w0, rw_w1=rw_w1, rw_w2=rw_w2, rw_a0=rw_a0, rw_a1=rw_a1,
             rw_a2=rw_a2, rw_g1=rw_g1, rw_g2=rw_g2, rw_kk=rw_kk, rw_ka=rw_ka, rw_rk=rw_rk, rw_gn_g=rw_gn_g,
             rw_gn_b=rw_gn_b, ml_conv=ml_conv, ml_gate_b=ml_gate_b, ml_ln_g=ml_ln_g, w_branch=w_branch,
             w_out=w_out, moe_w_group=moe_w_group, moe_b_group=moe_b_group, moe_w_expert=moe_w_expert,
             moe_b_expert=moe_b_expert, moe_w1=moe_w1, moe_w3=moe_w3, moe_w2=moe_w2)
    batch, seq, d = x.shape
    assert batch == 1
    xf = x.reshape(seq, d)
    xb = xf.astype(BF16)
    tables = _rope_tables(positions.reshape(seq))
    for l in range(w_in.shape[0]):
        xf, xb = _mixer(xf, xb, tables, l, p, ln1_g[l], ln1_b[l])
        xf, xb = _moe(xf, l, p, ln2_g[l], ln2_b[l])
    return xf.reshape(batch, seq, d)
```

```python
import functools

import jax
import jax.numpy as jnp
from jax import lax
from jax.experimental import pallas as pl
from jax.experimental.pallas import tpu as pltpu

F32 = jnp.float32
BF16 = jnp.bfloat16

D_MODEL = 2048
DEPTH = 4
LN_EPS = 1e-5
DN_ALPHA = (2 * DEPTH) ** 0.25

RW_HEADS = 16
RW_HEAD_DIM = 64
RW_WIDTH = RW_HEADS * RW_HEAD_DIM
RW_GN_EPS = 64e-5

AT_GROUPS = ((128, 1), (512, 4), (2048, 16))
AT_HEADS_PER_GROUP = 6
AT_HEADS = AT_HEADS_PER_GROUP * len(AT_GROUPS)
AT_HEAD_DIM = 64
AT_WIDTH = AT_HEADS * AT_HEAD_DIM
AT_OUT_WIDTH = AT_HEADS_PER_GROUP * AT_HEAD_DIM
AT_BLOCK = 128
ROPE_DIMS = AT_HEAD_DIM // 4
ROPE_THETA = 500000.0

ML_HEADS = 8
ML_HEAD_DIM = 128
ML_WIDTH = ML_HEADS * ML_HEAD_DIM
ML_CHUNK = 64
ML_NORM_EPS = 1e-6

N_BRANCHES = 3
COL_RW = 0
COL_AT = 4 * RW_WIDTH
COL_ML = COL_AT + 3 * AT_WIDTH
COL_IF = COL_ML + 4 * ML_WIDTH
COL_GATE = COL_IF + 2 * ML_HEADS
IN_COLS = COL_GATE + N_BRANCHES * D_MODEL

MOE_GROUPS = 4
MOE_EXPERTS_PER_GROUP = 8
MOE_EXPERTS = MOE_GROUPS * MOE_EXPERTS_PER_GROUP
MOE_TOP_K = 2
MOE_FF = 512
MOE_ROWS = 256

LANES = 128
SUBLANES = 8
VMEM_LIMIT = 56 * 1024 * 1024

NT_DIMS = (((1,), (1,)), ((), ()))


def _params(*sem):
    return pltpu.CompilerParams(dimension_semantics=sem, vmem_limit_bytes=VMEM_LIMIT)


def _bdot(a, b):
    return jnp.dot(a.astype(BF16), b.astype(BF16), preferred_element_type=F32)


def _bdot_nt(a, b):
    return lax.dot_general(a.astype(BF16), b.astype(BF16), NT_DIMS, preferred_element_type=F32)


def _each(f, *xs):
    return [f(*a) for a in zip(*xs)]


def _shift_rows(x, prev_row):
    first = lax.broadcasted_iota(jnp.int32, x.shape, 0) == 0
    return jnp.where(first, prev_row, pltpu.roll(x, 1, axis=0))


def _pair_sum(x, first_head):
    lo = jnp.sum(jnp.where(first_head, x, 0.0), axis=-1, keepdims=True)
    hi = jnp.sum(jnp.where(first_head, 0.0, x), axis=-1, keepdims=True)
    return jnp.where(first_head, lo, hi)


def _mm_kernel(a_ref, w_ref, o_ref, wb_ref, *, precision):
    @pl.when(pl.program_id(1) == 0)
    def _():
        wb_ref[...] = w_ref[...].astype(wb_ref.dtype)

    o_ref[...] = jnp.dot(a_ref[...], wb_ref[...], preferred_element_type=F32,
                         precision=precision).astype(o_ref.dtype)


def _mm(a, w, *, layer=None, col0=0, n_cols=None, tm=1024, tn=None, out_dtype=F32, exact=False, name="mm"):
    m, k = a.shape
    n_total = w.shape[-1]
    n_cols = n_total - col0 if n_cols is None else n_cols
    tn = n_cols if tn is None else tn
    tm = min(tm, m)
    assert m % tm == 0 and n_cols % tn == 0 and col0 % tn == 0
    cb0 = col0 // tn
    if layer is None:
        w_spec = pl.BlockSpec((k, tn), lambda j, i: (0, cb0 + j))
    else:
        w_spec = pl.BlockSpec((None, k, tn), lambda j, i: (layer, 0, cb0 + j))
    op_dtype = F32 if exact else BF16
    return pl.pallas_call(
        functools.partial(_mm_kernel, precision=lax.Precision.HIGHEST if exact else None),
        grid=(n_cols // tn, m // tm),
        in_specs=[pl.BlockSpec((tm, k), lambda j, i: (i, 0)), w_spec],
        out_specs=pl.BlockSpec((tm, tn), lambda j, i: (i, j)),
        out_shape=jax.ShapeDtypeStruct((m, n_cols), out_dtype),
        scratch_shapes=[pltpu.VMEM((k, tn), op_dtype)],
        compiler_params=_params("arbitrary", "arbitrary"),
        name=name,
    )(a.astype(op_dtype), w)


def _mm_t_kernel(a_ref, w_ref, o_ref, wb_ref):
    @pl.when(pl.program_id(1) == 0)
    def _():
        wb_ref[...] = jnp.transpose(w_ref[0]).astype(BF16)

    o_ref[...] = jnp.dot(a_ref[...], wb_ref[...], preferred_element_type=F32)


def _mm_t(a, w_t, *, layer, row0, n_rows, tm=1024, tn=None, name="mm_t"):
    m, k = a.shape
    tn = n_rows if tn is None else tn
    tm = min(tm, m)
    assert m % tm == 0 and n_rows % tn == 0 and row0 % SUBLANES == 0
    return pl.pallas_call(
        _mm_t_kernel,
        grid=(n_rows // tn, m // tm),
        in_specs=[pl.BlockSpec((tm, k), lambda j, i: (i, 0)),
                  pl.BlockSpec((pl.Element(1), pl.Element(tn), pl.Element(k)),
                               lambda j, i: (layer, pl.multiple_of(row0 + j * tn, SUBLANES), 0))],
        out_specs=pl.BlockSpec((tm, tn), lambda j, i: (i, j)),
        out_shape=jax.ShapeDtypeStruct((m, n_rows), F32),
        scratch_shapes=[pltpu.VMEM((k, tn), BF16)],
        compiler_params=_params("arbitrary", "arbitrary"),
        name=name,
    )(a, w_t)


def _layer_norm(y, g, b):
    mu = jnp.mean(y, axis=-1, keepdims=True)
    d = y - mu
    var = jnp.mean(d * d, axis=-1, keepdims=True)
    return d * lax.rsqrt(var + LN_EPS) * g + b


def _out_ln_kernel(y_ref, w_ref, x_ref, g_ref, b_ref, o_ref, ob_ref):
    h = jnp.dot(y_ref[...], w_ref[...], preferred_element_type=F32)
    out = _layer_norm(DN_ALPHA * x_ref[...] + h, g_ref[...], b_ref[...])
    o_ref[...] = out
    ob_ref[...] = out.astype(BF16)


def _out_ln(y, w, x, g, b, tm=256):
    m, d = x.shape
    row = pl.BlockSpec((tm, d), lambda i: (i, 0))
    vec = pl.BlockSpec((1, d), lambda i: (0, 0))
    return pl.pallas_call(
        _out_ln_kernel,
        grid=(m // tm,),
        in_specs=[row, pl.BlockSpec((d, d), lambda i: (0, 0)), row, vec, vec],
        out_specs=[row, row],
        out_shape=[jax.ShapeDtypeStruct((m, d), F32), jax.ShapeDtypeStruct((m, d), BF16)],
        compiler_params=_params("arbitrary"),
        name="out_ln",
    )(y, w, x, g.reshape(1, d), b.reshape(1, d))


RW_CHUNK = 64
RW_PAIR = 2 * RW_HEAD_DIM
RW_SUB = 16
RW_CHUNKS_PER_STEP = 2
RW_PREP_ROWS = 256
(RV_MU, RV_W0, RV_A0, RV_KK, RV_KA, RV_GN_G, RV_GN_B, RV_RK, RV_ROWS) = (0, 6, 7, 8, 9, 10, 11, 12, 16)


def _rwkv_prep_kernel(r_ref, k_ref, v_ref, z_ref, rp_ref, kp_ref, vp_ref, zp_ref, vec_ref,
                      w1_ref, w2_ref, a1_ref, a2_ref, g1_ref, g2_ref,
                      ro_ref, lw_ref, ko_ref, vo_ref, kk_ref, b_ref, g_ref):
    first_block = pl.program_id(0) == 0
    vec = lambda j: vec_ref[j:j + 1, :]
    prev = lambda ref: jnp.where(first_block, 0.0, ref[SUBLANES - 1:SUBLANES, :])
    lerp = lambda u, up, j: u + (_shift_rows(u, up) - u) * vec(RV_MU + j)
    r = lerp(r_ref[...], prev(rp_ref), 0)
    k = lerp(k_ref[...], prev(kp_ref), 1)
    v = lerp(v_ref[...], prev(vp_ref), 2)
    z = z_ref[...]
    z_diff = _shift_rows(z, prev(zp_ref)) - z
    xw, xa, xg = z + z_diff * vec(RV_MU + 3), z + z_diff * vec(RV_MU + 4), z + z_diff * vec(RV_MU + 5)
    w_pre = vec(RV_W0) + _bdot(jnp.tanh(_bdot(xw, w1_ref[...])), w2_ref[...])
    softplus = jnp.maximum(-w_pre, 0.0) + jnp.log(1.0 + jnp.exp(-jnp.abs(w_pre)))
    lw_ref[...] = -jnp.exp(-softplus - 0.5)
    a = jax.nn.sigmoid(vec(RV_A0) + _bdot(_bdot(xa, a1_ref[...]), a2_ref[...]))
    g_ref[...] = _bdot(jax.nn.sigmoid(_bdot(xg, g1_ref[...])), g2_ref[...])
    kk = k * vec(RV_KK)
    first_head = lax.broadcasted_iota(jnp.int32, (kk.shape[0], RW_PAIR), 1) < RW_HEAD_DIM
    for p in range(RW_HEADS // 2):
        sl = slice(p * RW_PAIR, (p + 1) * RW_PAIR)
        kk_p = kk[:, sl]
        kk_p = kk_p / jnp.maximum(jnp.sqrt(_pair_sum(kk_p * kk_p, first_head)), 1e-12)
        kk_ref[:, sl] = kk_p
        b_ref[:, sl] = kk_p * a[:, sl]
    ro_ref[...] = r
    ko_ref[...] = k * (1.0 + (a - 1.0) * vec(RV_KA))
    vo_ref[...] = v


def _rwkv_prep(rw, vecs, l, p):
    s = rw.shape[0]
    tm = min(RW_PREP_ROWS, s)
    per = tm // SUBLANES
    cur = lambda j: pl.BlockSpec((tm, RW_WIDTH), lambda i: (i, j))
    prev = lambda j: pl.BlockSpec((SUBLANES, RW_WIDTH), lambda i: (jnp.maximum(i * per - 1, 0), j))
    lora = lambda w: pl.BlockSpec((None,) + w.shape[1:], lambda i: (l, 0, 0))
    ws = [p['rw_w1'], p['rw_w2'], p['rw_a1'], p['rw_a2'], p['rw_g1'], p['rw_g2']]
    out = pl.BlockSpec((tm, RW_WIDTH), lambda i: (i, 0))
    return pl.pallas_call(
        _rwkv_prep_kernel,
        grid=(s // tm,),
        in_specs=[cur(0), cur(1), cur(2), cur(3), prev(0), prev(1), prev(2), prev(3),
                  pl.BlockSpec((RV_ROWS, RW_WIDTH), lambda i: (0, 0))] + [lora(w) for w in ws],
        out_specs=[out] * 7,
        out_shape=[jax.ShapeDtypeStruct((s, RW_WIDTH), F32)] * 7,
        compiler_params=_params("arbitrary"),
        name="rwkv_prep",
    )(rw, rw, rw, rw, rw, rw, rw, rw, vecs, *ws)


def _rwkv_kernel(r_ref, lw_ref, k_ref, v_ref, kk_ref, b_ref, g_ref, vec_ref, o_ref, zt_ref):
    @pl.when(pl.program_id(0) == 0)
    def _():
        zt_ref[...] = jnp.zeros_like(zt_ref)

    c, n2 = RW_CHUNK, RW_PAIR
    tri = (lax.broadcasted_iota(jnp.int32, (c, c), 1) <= lax.broadcasted_iota(jnp.int32, (c, c), 0)).astype(BF16)
    row = lax.broadcasted_iota(jnp.int32, (n2, n2), 0)
    col = lax.broadcasted_iota(jnp.int32, (n2, n2), 1)
    t_idx, s_idx = row % c, col % c
    strict = t_idx > s_idx
    incl = t_idx >= s_idx
    diag_blk = (row // RW_SUB) == (col // RW_SUB)
    eye = (row == col).astype(F32)
    first_head = lax.broadcasted_iota(jnp.int32, (c, n2), 1) < RW_HEAD_DIM

    def embed(x):
        return jnp.concatenate([jnp.where(first_head, x, 0.0), jnp.where(first_head, 0.0, x)], axis=0)

    pairs = range(RW_HEADS // 2)
    sls = [slice(p * n2, (p + 1) * n2) for p in pairs]
    bdot = lambda xs, ys: _each(_bdot, xs, ys)

    lhs, rhs, v_e, kb, gam = [], [], [], [], []
    for ck in range(RW_CHUNKS_PER_STEP):
        rs = slice(ck * c, (ck + 1) * c)
        lw = lw_ref[rs, :]
        lw_hi = lw.astype(BF16)
        rem = lw - lw_hi.astype(F32)
        lw_mid = rem.astype(BF16)
        lw_lo = (rem - lw_mid.astype(F32)).astype(BF16)
        g_in = (jnp.dot(tri, lw_hi, preferred_element_type=F32) + jnp.dot(tri, lw_mid, preferred_element_type=F32)
                + jnp.dot(tri, lw_lo, preferred_element_type=F32))
        g_last = g_in[c - 1:c, :]
        e_neg = jnp.exp(-g_in)
        e_end = jnp.exp(g_last - g_in)
        kkd = kk_ref[rs, :] * jnp.exp(g_in - lw)
        rd = r_ref[rs, :] * jnp.exp(g_in)
        kinv = k_ref[rs, :] * e_neg
        binv = b_ref[rs, :] * e_neg
        kd = k_ref[rs, :] * e_end
        bd = b_ref[rs, :] * e_end
        gam.append(jnp.exp(g_last))
        v_e += [embed(v_ref[rs, sl]) for sl in sls]
        lhs += [jnp.concatenate([embed(kkd[:, sl]), embed(rd[:, sl])], axis=0).astype(BF16) for sl in sls]
        rhs += [jnp.concatenate([embed(kinv[:, sl]), embed(binv[:, sl])], axis=0).astype(BF16) for sl in sls]
        kb += [jnp.concatenate([embed(kd[:, sl]), -embed(bd[:, sl])], axis=0).astype(BF16) for sl in sls]
    v_b = [x.astype(BF16) for x in v_e]
    aa = _each(_bdot_nt, lhs, rhs)
    a_kkk = [jnp.where(strict, x[:n2, :n2], 0.0).astype(BF16) for x in aa]
    a_kkb = [jnp.where(strict, x[:n2, n2:], 0.0) for x in aa]
    a_rk = [jnp.where(incl, x[n2:, :n2], 0.0).astype(BF16) for x in aa]
    a_rb = [jnp.where(incl, x[n2:, n2:], 0.0).astype(BF16) for x in aa]
    nd = [jnp.where(diag_blk, x, 0.0) for x in a_kkb]
    off = [jnp.where(diag_blk, 0.0, x) for x in a_kkb]
    nd2 = bdot(nd, nd)
    akv = bdot(a_kkk, v_b)
    o_v = bdot(a_rk, v_b)
    nd4 = bdot(nd2, nd2)
    p1 = bdot([eye - x for x in nd], [eye + x for x in nd2])
    nd8 = bdot(nd4, nd4)
    p2 = bdot(p1, [eye + x for x in nd4])
    d_inv = bdot(p2, [eye + x for x in nd8])
    e1 = bdot(d_inv, off)
    e2 = bdot(e1, e1)
    t_inv = [x.astype(BF16) for x in bdot(bdot([eye - x for x in e1], [eye + x for x in e2]), d_inv)]

    inv_n = 1.0 / RW_HEAD_DIM
    zt = [zt_ref[p] for p in pairs]
    for ck in range(RW_CHUNKS_PER_STEP):
        rs = slice(ck * c, (ck + 1) * c)
        at = lambda xs: xs[ck * len(sls):(ck + 1) * len(sls)]
        zt_b = [x.astype(BF16) for x in zt]
        x1 = _each(lambda a, z: _bdot_nt(a[:n2], z), at(lhs), zt_b)
        o_z = _each(lambda a, z: _bdot_nt(a[n2:], z), at(lhs), zt_b)
        u = bdot(at(t_inv), [a + b for a, b in zip(x1, at(akv))])
        o_u = bdot(at(a_rb), u)
        vu_t = [jnp.transpose(jnp.concatenate([a, b], axis=0)) for a, b in zip(at(v_e), u)]
        z_up = bdot(vu_t, at(kb))
        zt = [z * gam[ck][:, sl] + zu for z, sl, zu in zip(zt, sls, z_up)]
        for p in pairs:
            sl = sls[p]
            o_e = o_z[p] + at(o_v)[p] - o_u[p]
            out = o_e[:c] + o_e[c:]
            mu = _pair_sum(out, first_head) * inv_n
            dev = out - mu
            var = _pair_sum(dev * dev, first_head) * inv_n
            normed = (dev * lax.rsqrt(var + RW_GN_EPS) * vec_ref[RV_GN_G:RV_GN_G + 1, sl]
                      + vec_ref[RV_GN_B:RV_GN_B + 1, sl])
            bonus = _pair_sum(r_ref[rs, sl] * k_ref[rs, sl] * vec_ref[RV_RK:RV_RK + 1, sl], first_head) * v_ref[rs, sl]
            o_ref[rs, sl] = ((normed + bonus) * g_ref[rs, sl]).astype(o_ref.dtype)
    for p in pairs:
        zt_ref[p] = zt[p]


def _rwkv_scan(r, lw, k, v, kk, b, g, vecs):
    s, width = r.shape
    rows = RW_CHUNK * RW_CHUNKS_PER_STEP
    blk = pl.BlockSpec((rows, width), lambda i: (i, 0))
    return pl.pallas_call(
        _rwkv_kernel,
        grid=(s // rows,),
        in_specs=[blk] * 7 + [pl.BlockSpec((RV_ROWS, width), lambda i: (0, 0))],
        out_specs=blk,
        out_shape=jax.ShapeDtypeStruct((s, width), BF16),
        scratch_shapes=[pltpu.VMEM((RW_HEADS // 2, RW_PAIR, RW_PAIR), F32)],
        compiler_params=_params("arbitrary"),
        name="rwkv_scan",
    )(r, lw, k, v, kk, b, g, vecs)


def _rwkv7(rw, l, p):
    rows = [p['rw_mu'][l], p['rw_w0'][l][None], p['rw_a0'][l][None], p['rw_kk'][l][None], p['rw_ka'][l][None],
            p['rw_gn_g'][l][None], p['rw_gn_b'][l][None], p['rw_rk'][l].reshape(1, RW_WIDTH)]
    vecs = jnp.concatenate(rows + [jnp.zeros((RV_ROWS - RV_RK - 1, RW_WIDTH), F32)], axis=0)
    r, lw, k, v, kk, b, g = _rwkv_prep(rw, vecs, l, p)
    return _rwkv_scan(r, lw, k, v, kk, b, g, vecs)


AT_PREP_ROWS = 512
AT_SLABS = AT_WIDTH // LANES


def _rope_kernel(q_ref, k_ref, c_ref, s1_ref, s2_ref, qo_ref, ko_ref):
    cos, s_dn, s_up = c_ref[...], s1_ref[...], s2_ref[...]

    def rot(x, scale):
        outs = []
        for j in range(AT_SLABS):
            xs = x[:, j * LANES:(j + 1) * LANES]
            y = xs * cos + pltpu.roll(xs, LANES - ROPE_DIMS // 2, axis=1) * s_dn + pltpu.roll(xs, ROPE_DIMS // 2, axis=1) * s_up
            outs.append(y * scale if scale != 1.0 else y)
        return jnp.concatenate(outs, axis=-1)

    qo_ref[...] = rot(q_ref[...], AT_HEAD_DIM ** -0.5)
    ko_ref[...] = rot(k_ref[...], 1.0)


def _rope_tables(positions):
    s = positions.shape[0]
    half = ROPE_DIMS // 2
    inv_freq = ROPE_THETA ** (-jnp.arange(half, dtype=F32) * 2.0 / ROPE_DIMS)
    ang = positions.astype(F32)[:, None] * inv_freq
    cos, sin = jnp.cos(ang), jnp.sin(ang)
    rest = AT_HEAD_DIM - ROPE_DIMS
    zeros, ones = jnp.zeros((s, rest), F32), jnp.ones((s, rest), F32)
    z_half = jnp.zeros((s, half), F32)
    head = lambda parts: jnp.tile(jnp.concatenate(parts, axis=1), (1, LANES // AT_HEAD_DIM))
    return head([cos, cos, ones]), head([-sin, z_half, zeros]), head([z_half, sin, zeros])


def _rope(at, tables):
    s = at.shape[0]
    tm = min(AT_PREP_ROWS, s)
    col = lambda j: pl.BlockSpec((tm, AT_WIDTH), lambda i: (i, j))
    tab = pl.BlockSpec((tm, LANES), lambda i: (i, 0))
    return pl.pallas_call(
        _rope_kernel,
        grid=(s // tm,),
        in_specs=[col(0), col(1), tab, tab, tab],
        out_specs=[col(0)] * 2,
        out_shape=[jax.ShapeDtypeStruct((s, AT_WIDTH), F32)] * 2,
        compiler_params=_params("arbitrary"),
        name="attn_rope",
    )(at, at, *tables)


AT_GROUP_SLABS = AT_OUT_WIDTH // LANES


def _attn_kernel(*refs, dilation):
    ns = AT_GROUP_SLABS
    q_refs, kp_refs, kc_refs, vp_refs, vc_refs = (refs[i * ns:(i + 1) * ns] for i in range(5))
    o_ref, lse_ref, o_s, lse_s = refs[5 * ns:]
    n = pl.program_id(0)
    qi = lax.broadcasted_iota(jnp.int32, (AT_BLOCK, AT_BLOCK), 0)
    kj = lax.broadcasted_iota(jnp.int32, (AT_BLOCK, AT_BLOCK), 1)
    mask_c = kj <= qi
    mask_p = (kj >= qi) & (n > 0)
    per_slab = LANES // AT_HEAD_DIM

    def residue(r, carry):
        rows = pl.ds(r, AT_BLOCK, stride=dilation) if dilation > 1 else slice(None)

        def heads_of(slab_refs):
            slabs = [ref[rows, :].astype(BF16) for ref in slab_refs]
            return [x[:, j * AT_HEAD_DIM:(j + 1) * AT_HEAD_DIM] for x in slabs for j in range(per_slab)]

        q, kc, kp, vc, vp = (heads_of(x) for x in (q_refs, kc_refs, kp_refs, vc_refs, vp_refs))
        s_c = [jnp.where(mask_c, lax.dot_general(a, b, NT_DIMS, preferred_element_type=F32), -jnp.inf)
               for a, b in zip(q, kc)]
        s_p = [jnp.where(mask_p, lax.dot_general(a, b, NT_DIMS, preferred_element_type=F32), -jnp.inf)
               for a, b in zip(q, kp)]
        m = [jnp.maximum(jnp.max(a, axis=-1, keepdims=True), jnp.max(b, axis=-1, keepdims=True))
             for a, b in zip(s_c, s_p)]
        p_c = [jnp.exp(a - mm) for a, mm in zip(s_c, m)]
        p_p = [jnp.exp(a - mm) for a, mm in zip(s_p, m)]
        den = [jnp.sum(a, axis=-1, keepdims=True) + jnp.sum(b, axis=-1, keepdims=True) for a, b in zip(p_c, p_p)]
        o_c = [jnp.dot((a / d).astype(BF16), v, preferred_element_type=F32) for a, d, v in zip(p_c, den, vc)]
        o_p = [jnp.dot((a / d).astype(BF16), v, preferred_element_type=F32) for a, d, v in zip(p_p, den, vp)]
        out = [a + b for a, b in zip(o_c, o_p)]
        lse = [jnp.broadcast_to(mm + jnp.log(d), (AT_BLOCK, AT_HEAD_DIM)) for mm, d in zip(m, den)]
        for t in range(ns):
            o_s[t, rows, :] = jnp.concatenate(out[t * per_slab:(t + 1) * per_slab], axis=-1)
            lse_s[t, rows, :] = jnp.concatenate(lse[t * per_slab:(t + 1) * per_slab], axis=-1)
        return carry

    if dilation > 1:
        lax.fori_loop(0, dilation, residue, 0)
    else:
        residue(0, 0)
    for t in range(ns):
        o_ref[:, t * LANES:(t + 1) * LANES] = o_s[t]
        lse_ref[:, t * LANES:(t + 1) * LANES] = lse_s[t]


def _dilated_attention(q, k, at, g, dilation):
    s = q.shape[0]
    rows = AT_BLOCK * dilation
    assert s % rows == 0
    ns = AT_GROUP_SLABS
    qk_col = g * ns
    v_col = 2 * (AT_WIDTH // LANES) + g * ns
    cur = lambda c: [pl.BlockSpec((rows, LANES), lambda n, t=t: (n, c + t)) for t in range(ns)]
    prev = lambda c: [pl.BlockSpec((rows, LANES), lambda n, t=t: (jnp.maximum(n - 1, 0), c + t)) for t in range(ns)]
    out = pl.BlockSpec((rows, AT_OUT_WIDTH), lambda n: (n, 0))
    return pl.pallas_call(
        functools.partial(_attn_kernel, dilation=dilation),
        grid=(s // rows,),
        in_specs=cur(qk_col) + prev(qk_col) + cur(qk_col) + prev(v_col) + cur(v_col),
        out_specs=[out, out],
        out_shape=[jax.ShapeDtypeStruct((s, AT_OUT_WIDTH), F32)] * 2,
        scratch_shapes=[pltpu.VMEM((ns, rows, LANES), F32)] * 2,
        compiler_params=_params("arbitrary"),
        name=f"attn_d{dilation}",
    )(*([q] * ns + [k] * (2 * ns) + [at] * (2 * ns)))


def _attn_merge_kernel(o0, o1, o2, l0, l1, l2, y_ref):
    m = jnp.maximum(jnp.maximum(l0[...], l1[...]), l2[...])
    e0, e1, e2 = jnp.exp(l0[...] - m), jnp.exp(l1[...] - m), jnp.exp(l2[...] - m)
    tot = e0 + e1 + e2
    y_ref[...] = ((e0 / tot) * o0[...] + (e1 / tot) * o1[...] + (e2 / tot) * o2[...]).astype(BF16)


def _attention(at, tables):
    s = at.shape[0]
    q, k = _rope(at, tables)
    outs, lses = [], []
    for g, (_, dilation) in enumerate(AT_GROUPS):
        o, lse = _dilated_attention(q, k, at, g, dilation)
        outs.append(o)
        lses.append(lse)
    tm = min(AT_PREP_ROWS, s)
    blk = pl.BlockSpec((tm, AT_OUT_WIDTH), lambda i: (i, 0))
    return pl.pallas_call(
        _attn_merge_kernel,
        grid=(s // tm,),
        in_specs=[blk] * 6,
        out_specs=blk,
        out_shape=jax.ShapeDtypeStruct((s, AT_OUT_WIDTH), BF16),
        compiler_params=_params("arbitrary"),
        name="attn_merge",
    )(*outs, *lses)


ML_CHUNKS_PER_STEP = 8
ML_ROWS = ML_CHUNKS_PER_STEP * ML_CHUNK
ML_GROUP = 4


def _mlstm_kernel(u_ref, v_ref, og_ref, cw_ref, bb_ref, ib_ref, brow_ref, irow_ref, g_ref, y_ref,
                  qk_ref, ubuf_ref, ct_ref, m_ref):
    @pl.when(pl.program_id(0) == 0)
    def _():
        ubuf_ref[:SUBLANES, :] = jnp.zeros((SUBLANES, ubuf_ref.shape[1]), F32)
        ct_ref[...] = jnp.zeros_like(ct_ref)
        m_ref[...] = jnp.full_like(m_ref, -jnp.inf)

    taps = cw_ref.shape[0]
    ubuf_ref[SUBLANES:, :] = u_ref[...]
    conv = ubuf_ref[SUBLANES:, :] * cw_ref[taps - 1:taps, :]
    for back in range(1, taps):
        conv = conv + ubuf_ref[pl.ds(SUBLANES - back, ML_ROWS), :] * cw_ref[taps - 1 - back:taps - back, :]
    qk_ref[...] = conv * jax.nn.sigmoid(conv)
    ubuf_ref[:SUBLANES, :] = ubuf_ref[ML_ROWS:, :]

    dh, lc = ML_HEAD_DIM, ML_CHUNK
    si = lax.broadcasted_iota(jnp.int32, (lc, lc), 0)
    ji = lax.broadcasted_iota(jnp.int32, (lc, lc), 1)
    causal = ji <= si
    ones_b = jnp.ones((lc, dh), BF16)

    def group(c, rows, heads):
        cols = [slice(h * dh, (h + 1) * dh) for h in heads]
        q = [qk_ref[rows, cl] * (dh ** -0.5) for cl in cols]
        k = [qk_ref[rows, pl.ds(ML_WIDTH + h * dh, dh)] for h in heads]
        v = [v_ref[rows, cl] for cl in cols]
        bb = [bb_ref[rows, cl] for cl in cols]
        ib = [ib_ref[rows, cl] for cl in cols]
        b_row = [brow_ref[c, h:h + 1, :] for h in heads]
        i_row = [irow_ref[c, h:h + 1, :] for h in heads]
        m_prev = [m_ref[h, 0:1, :] for h in heads]
        ct_prev = [ct_ref[h] for h in heads]
        qb = [x.astype(BF16) for x in q]
        kb = [x.astype(BF16) for x in k]
        s_qk = _each(_bdot_nt, qb, kb)
        inter = _each(_bdot, qb, ct_prev)
        k_t = [jnp.transpose(x).astype(BF16) for x in k]
        log_d = [jnp.where(causal, b[:, :lc] - br + ir, -jnp.inf) for b, br, ir in zip(bb, b_row, i_row)]
        a_log = [b + mp for b, mp in zip(bb, m_prev)]
        m_s = [jnp.maximum(al, jnp.max(ld, axis=-1, keepdims=True)) for al, ld in zip(a_log, log_d)]
        inter_w = [jnp.exp(al - ms) for al, ms in zip(a_log, m_s)]
        qk = [s * jnp.exp(ld - ms[:, :lc]) for s, ld, ms in zip(s_qk, log_d, m_s)]
        intra = [_bdot(a, jnp.concatenate([x.astype(BF16), ones_b], axis=1)) for a, x in zip(qk, v)]
        b_last = [b[lc - 1:lc, :] for b in bb]
        w_end = [bl - b + i for bl, b, i in zip(b_last, bb, ib)]
        m_new = [jnp.maximum(bl + mp, jnp.max(we, axis=0, keepdims=True)) for bl, mp, we in zip(b_last, m_prev, w_end)]
        dec = [jnp.exp(bl + mp - mn) for bl, mp, mn in zip(b_last, m_prev, m_new)]
        wts = [jnp.exp(we - mn) for we, mn in zip(w_end, m_new)]
        upd = [_bdot(kt, jnp.concatenate([x * w, w], axis=1)) for kt, x, w in zip(k_t, v, wts)]
        for h, ctp, d, up, mn in zip(heads, ct_prev, dec, upd, m_new):
            ct_ref[h] = jnp.concatenate([d, d], axis=1) * ctp + up
            m_ref[h, 0:1, :] = mn
        num = [w * a[:, :dh] + b[:, :dh] for w, a, b in zip(inter_w, inter, intra)]
        den = [w * a[:, dh:] + b[:, dh:] for w, a, b in zip(inter_w, inter, intra)]
        hid = [n / jnp.maximum(jnp.abs(d), jnp.exp(-ms)) for n, d, ms in zip(num, den, m_s)]
        hid = [x * jax.nn.sigmoid(og_ref[rows, cl]) for x, cl in zip(hid, cols)]
        mu = [jnp.mean(x, axis=-1, keepdims=True) for x in hid]
        dev = [x - m for x, m in zip(hid, mu)]
        var = [jnp.mean(x * x, axis=-1, keepdims=True) for x in dev]
        for cl, x, vr in zip(cols, dev, var):
            y_ref[rows, cl] = (x * lax.rsqrt(vr + ML_NORM_EPS) * g_ref[:, cl]).astype(y_ref.dtype)

    def chunk(c, carry):
        rows = pl.ds(pl.multiple_of(c * lc, lc), lc)
        for h0 in range(0, ML_HEADS, ML_GROUP):
            group(c, rows, range(h0, h0 + ML_GROUP))
        return carry

    lax.fori_loop(0, ML_CHUNKS_PER_STEP, chunk, 0)


def _mlstm(ml, gates_if, l, p):
    s = ml.shape[0]
    nc = s // ML_CHUNK
    i_pre = gates_if[:, :ML_HEADS] + p['ml_gate_b'][l, 0]
    f_pre = gates_if[:, ML_HEADS:] + p['ml_gate_b'][l, 1]
    lf = jax.nn.log_sigmoid(f_pre)
    b_cum = jnp.cumsum(lf.reshape(nc, ML_CHUNK, ML_HEADS), axis=1)
    b_row = jnp.transpose(b_cum, (0, 2, 1))
    i_row = jnp.transpose(i_pre.reshape(nc, ML_CHUNK, ML_HEADS), (0, 2, 1))
    over_lanes = lambda a: jnp.repeat(a.reshape(s, ML_HEADS), ML_HEAD_DIM, axis=1)
    col = lambda j: pl.BlockSpec((ML_ROWS, ML_WIDTH), lambda i: (i, j))
    grow = pl.BlockSpec((ML_CHUNKS_PER_STEP, ML_HEADS, ML_CHUNK), lambda i: (i, 0, 0))
    conv_w = p['ml_conv'][l]
    return pl.pallas_call(
        _mlstm_kernel,
        grid=(s // ML_ROWS,),
        in_specs=[pl.BlockSpec((ML_ROWS, 2 * ML_WIDTH), lambda i: (i, 0)), col(2), col(3),
                  pl.BlockSpec(conv_w.shape, lambda i: (0, 0)), col(0), col(0), grow, grow,
                  pl.BlockSpec((1, ML_WIDTH), lambda i: (0, 0))],
        out_specs=pl.BlockSpec((ML_ROWS, ML_WIDTH), lambda i: (i, 0)),
        out_shape=jax.ShapeDtypeStruct((s, ML_WIDTH), BF16),
        scratch_shapes=[pltpu.VMEM((ML_ROWS, 2 * ML_WIDTH), F32),
                        pltpu.VMEM((ML_ROWS + SUBLANES, 2 * ML_WIDTH), F32),
                        pltpu.VMEM((ML_HEADS, ML_HEAD_DIM, 2 * ML_HEAD_DIM), F32),
                        pltpu.VMEM((ML_HEADS, SUBLANES, ML_HEAD_DIM), F32)],
        compiler_params=_params("arbitrary"),
        name="mlstm",
    )(ml, ml, ml, conv_w, over_lanes(b_cum), over_lanes(i_pre), b_row, i_row,
      p['ml_ln_g'][l].reshape(1, ML_WIDTH))


def _merge_kernel(x_ref, yr_ref, ya_ref, ym_ref, g0_ref, g1_ref, g2_ref, w0_ref, w1_ref, w2_ref, o_ref, gb_ref):
    @pl.when(pl.program_id(1) == 0)
    def _():
        for b, g_ref in enumerate((g0_ref, g1_ref, g2_ref)):
            gb_ref[b] = jnp.transpose(g_ref[0]).astype(BF16)

    x = x_ref[...]
    gate = lambda b: jax.nn.sigmoid(jnp.dot(x, gb_ref[b], preferred_element_type=F32))
    y = (gate(0) * jnp.dot(yr_ref[...], w0_ref[...], preferred_element_type=F32)
         + gate(1) * jnp.dot(ya_ref[...], w1_ref[...], preferred_element_type=F32)
         + gate(2) * jnp.dot(ym_ref[...], w2_ref[...], preferred_element_type=F32))
    o_ref[...] = y.astype(o_ref.dtype)


def _merge(xb, y_rw, y_at, y_ml, w_in_t, l, wb_rw, wb_at, wb_ml, tm=512, tn=512):
    m, d = xb.shape
    nj = d // tn
    act = lambda a: pl.BlockSpec((tm, a.shape[1]), lambda j, i: (i, 0))
    gate = lambda b: pl.BlockSpec((pl.Element(1), pl.Element(tn), pl.Element(d)),
                                  lambda j, i: (l, pl.multiple_of(COL_GATE + b * d + j * tn, SUBLANES), 0))
    wsp = lambda w: pl.BlockSpec((w.shape[0], tn), lambda j, i: (0, j))
    return pl.pallas_call(
        _merge_kernel,
        grid=(nj, m // tm),
        in_specs=[act(xb), act(y_rw), act(y_at), act(y_ml), gate(0), gate(1), gate(2),
                  wsp(wb_rw), wsp(wb_at), wsp(wb_ml)],
        out_specs=pl.BlockSpec((tm, tn), lambda j, i: (i, j)),
        out_shape=jax.ShapeDtypeStruct((m, d), BF16),
        scratch_shapes=[pltpu.VMEM((N_BRANCHES, d, tn), BF16)],
        compiler_params=_params("arbitrary", "arbitrary"),
        name="merge",
    )(xb, y_rw, y_at, y_ml, w_in_t, w_in_t, w_in_t, wb_rw, wb_at, wb_ml)


MOE_LN_ROWS = 256


def _row_gather(idx_ref, base, n, src_hbm, dst, sem, unroll=False):
    def start(j):
        pltpu.make_async_copy(src_hbm.at[pl.ds(idx_ref[base + j], 1)], dst.at[pl.ds(j, 1)], sem).start()

    if unroll:
        for j in range(n):
            start(j)
    else:
        lax.fori_loop(0, n, lambda j, carry: (start(j), carry)[1], 0, unroll=8)


def _row_gather_wait(n, src_hbm, dst, sem):
    pltpu.make_async_copy(src_hbm.at[pl.ds(0, n)], dst.at[pl.ds(0, n)], sem).wait()


def _expert_kernel(be_ref, new_ref, used_ref, src_ref, x_hbm, w1_ref, w3_ref, w2_ref, o_ref,
                   xbuf, w1b, w3b, w2b, sem):
    i = pl.program_id(0)
    slot = i % 2

    @pl.when(i == 0)
    def _():
        _row_gather(src_ref, 0, MOE_ROWS, x_hbm, xbuf.at[0], sem.at[0])

    @pl.when(new_ref[i] == 1)
    def _():
        w1b[...] = w1_ref[...].astype(BF16)
        w3b[...] = w3_ref[...].astype(BF16)
        w2b[...] = w2_ref[...].astype(BF16)

    @pl.when(used_ref[i] == 1)
    def _():
        _row_gather_wait(MOE_ROWS, x_hbm, xbuf.at[slot], sem.at[slot])
        _row_gather(src_ref, (i + 1) * MOE_ROWS, MOE_ROWS, x_hbm, xbuf.at[1 - slot], sem.at[1 - slot], unroll=True)
        x = xbuf[slot].astype(BF16)
        h1 = jnp.dot(x, w1b[...], preferred_element_type=F32)
        h3 = jnp.dot(x, w3b[...], preferred_element_type=F32)
        hid = (h1 * jax.nn.sigmoid(h1) * h3).astype(BF16)
        o_ref[...] = jnp.dot(hid, w2b[...], preferred_element_type=F32)

    @pl.when(used_ref[i] == 0)
    def _():
        @pl.when(used_ref[jnp.maximum(i - 1, 0)] == 1)
        def _():
            _row_gather_wait(MOE_ROWS, x_hbm, xbuf.at[slot], sem.at[slot])

        o_ref[...] = jnp.zeros_like(o_ref)


def _experts(x, src, blk_expert, blk_new, blk_used, w1, w3, w2, l):
    rows = src.shape[0]
    d = x.shape[1]
    nblk = rows // MOE_ROWS
    up = pl.BlockSpec((None, None, d, MOE_FF), lambda i, be, nw, us, sr: (l, be[i], 0, 0))
    down = pl.BlockSpec((None, None, MOE_FF, d), lambda i, be, nw, us, sr: (l, be[i], 0, 0))
    return pl.pallas_call(
        _expert_kernel,
        grid_spec=pltpu.PrefetchScalarGridSpec(
            num_scalar_prefetch=4,
            grid=(nblk,),
            in_specs=[pl.BlockSpec(memory_space=pl.ANY), up, up, down],
            out_specs=pl.BlockSpec((MOE_ROWS, d), lambda i, be, nw, us, sr: (i, 0)),
            scratch_shapes=[pltpu.VMEM((2, MOE_ROWS, d), F32),
                            pltpu.VMEM((d, MOE_FF), BF16), pltpu.VMEM((d, MOE_FF), BF16),
                            pltpu.VMEM((MOE_FF, d), BF16),
                            pltpu.SemaphoreType.DMA((2,))]),
        out_shape=jax.ShapeDtypeStruct((rows, d), F32),
        compiler_params=_params("arbitrary"),
        name="experts",
    )(blk_expert, blk_new, blk_used, src, x, w1, w3, w2)


def _moe_ln_kernel(pos_ref, x_ref, w_ref, y_hbm, g_ref, b_ref, o_ref, ob_ref, ybuf, sem):
    i = pl.program_id(0)
    nblk = pl.num_programs(0)
    n = MOE_TOP_K * MOE_LN_ROWS
    tm = MOE_LN_ROWS
    slot = i % 2

    @pl.when(i == 0)
    def _():
        _row_gather(pos_ref, 0, n, y_hbm, ybuf.at[0], sem.at[0])

    _row_gather_wait(n, y_hbm, ybuf.at[slot], sem.at[slot])
    _row_gather(pos_ref, (i + 1) * n, n, y_hbm, ybuf.at[1 - slot], sem.at[1 - slot], unroll=True)
    rows = ybuf.at[slot]
    h = rows[0:tm, :] * w_ref[:, 0:1] + rows[tm:2 * tm, :] * w_ref[:, 1:2]
    out = _layer_norm(DN_ALPHA * x_ref[...] + h, g_ref[...], b_ref[...])
    o_ref[...] = out
    ob_ref[...] = out.astype(BF16)

    @pl.when(i == nblk - 1)
    def _():
        _row_gather_wait(n, y_hbm, ybuf.at[1 - slot], sem.at[1 - slot])


def _moe_ln(x, y, pos, top_w, g, b):
    t, d = x.shape
    tm = MOE_LN_ROWS
    pos = jnp.transpose(pos.reshape(t // tm, tm, MOE_TOP_K), (0, 2, 1)).reshape(t * MOE_TOP_K)
    pos = jnp.concatenate([pos, jnp.zeros((tm * MOE_TOP_K,), jnp.int32)])
    row = pl.BlockSpec((tm, d), lambda i, pos: (i, 0))
    vec = pl.BlockSpec((1, d), lambda i, pos: (0, 0))
    return pl.pallas_call(
        _moe_ln_kernel,
        grid_spec=pltpu.PrefetchScalarGridSpec(
            num_scalar_prefetch=1,
            grid=(t // tm,),
            in_specs=[row, pl.BlockSpec((tm, MOE_TOP_K), lambda i, pos: (i, 0)),
                      pl.BlockSpec(memory_space=pl.ANY), vec, vec],
            out_specs=[row, row],
            scratch_shapes=[pltpu.VMEM((2, MOE_TOP_K * tm, d), F32),
                            pltpu.SemaphoreType.DMA((2,))]),
        out_shape=[jax.ShapeDtypeStruct((t, d), F32), jax.ShapeDtypeStruct((t, d), BF16)],
        compiler_params=_params("arbitrary"),
        name="moe_ln",
    )(pos, x, top_w, y, g.reshape(1, d), b.reshape(1, d))


def _moe(x, l, p, ln_g, ln_b):
    t, d = x.shape
    n_assign = t * MOE_TOP_K
    w_router = jnp.concatenate([p['moe_w_group'][l], p['moe_w_expert'][l]], axis=1)
    w_router = jnp.pad(w_router, ((0, 0), (0, LANES - w_router.shape[1])))
    logits = _mm(x, w_router, tm=512, exact=True, name="mm_router")
    g_logits = logits[:, :MOE_GROUPS] + p['moe_b_group'][l]
    g_prob = jax.nn.softmax(g_logits, -1)
    g_idx = jnp.argmax(g_logits, -1)
    g_w = jnp.take_along_axis(g_prob, g_idx[:, None], axis=1)
    e_logits = (logits[:, MOE_GROUPS:MOE_GROUPS + MOE_EXPERTS] + p['moe_b_expert'][l]).reshape(
        t, MOE_GROUPS, MOE_EXPERTS_PER_GROUP)
    e_sel = jnp.take_along_axis(e_logits, g_idx[:, None, None], axis=1)[:, 0]
    top_l, top_i = lax.top_k(e_sel, MOE_TOP_K)
    top_w = jax.nn.softmax(top_l, -1) * g_w
    e_flat = (g_idx[:, None] * MOE_EXPERTS_PER_GROUP + top_i).reshape(n_assign).astype(jnp.int32)
    onehot = (e_flat[:, None] == jnp.arange(MOE_EXPERTS, dtype=jnp.int32)[None]).astype(jnp.int32)
    rank = jnp.take_along_axis(jnp.cumsum(onehot, axis=0), e_flat[:, None], axis=1)[:, 0] - 1
    counts = jnp.sum(onehot, axis=0)
    pcounts = (counts + MOE_ROWS - 1) // MOE_ROWS * MOE_ROWS
    pends = jnp.cumsum(pcounts)
    pstarts = pends - pcounts
    pos = pstarts[e_flat] + rank
    n_rows = n_assign + (MOE_EXPERTS + 1) * MOE_ROWS
    nblk = n_rows // MOE_ROWS
    src = jnp.zeros((n_rows,), jnp.int32).at[pos].set(jnp.arange(n_assign, dtype=jnp.int32) // MOE_TOP_K)
    blk_start = jnp.arange(nblk, dtype=jnp.int32) * MOE_ROWS
    blk_expert = jnp.clip(jnp.searchsorted(pends, blk_start, side='right'), 0, MOE_EXPERTS - 1).astype(jnp.int32)
    blk_used = (blk_start < pends[-1]).astype(jnp.int32)
    blk_new = jnp.concatenate([jnp.ones((1,), jnp.int32), (blk_expert[1:] != blk_expert[:-1]).astype(jnp.int32)])
    y = _experts(x, src, blk_expert, blk_new, blk_used, p['moe_w1'], p['moe_w3'], p['moe_w2'], l)
    return _moe_ln(x, y, pos.astype(jnp.int32).reshape(t, MOE_TOP_K), top_w, ln_g, ln_b)


def _mixer(xf, xb, tables, l, p, ln_g, ln_b):
    w_in_t = p['w_in_t']
    bf = lambda w: w.astype(BF16)
    rw = _mm_t(xb, w_in_t, layer=l, row0=COL_RW, n_rows=COL_AT - COL_RW, tn=1024, name="mm_in_rw")
    at = _mm_t(xb, w_in_t, layer=l, row0=COL_AT, n_rows=COL_ML - COL_AT, tn=AT_WIDTH, name="mm_in_at")
    ml = _mm_t(xb, w_in_t, layer=l, row0=COL_ML, n_rows=COL_IF - COL_ML, tn=1024, name="mm_in_ml")
    gates_if = _mm_t(xb, w_in_t, layer=l, row0=COL_IF, n_rows=COL_GATE - COL_IF, name="mm_in_if")
    y_rw = _rwkv7(rw, l, p)
    y_at = _attention(at, tables)
    y_ml = _mlstm(ml, gates_if, l, p)
    wb = p['w_branch'][l]
    y = _merge(xb, y_rw, y_at, y_ml, w_in_t, l, bf(wb[:RW_WIDTH]),
               bf(wb[RW_WIDTH:RW_WIDTH + AT_OUT_WIDTH]), bf(wb[RW_WIDTH + AT_OUT_WIDTH:]))
    return _out_ln(y, bf(p['w_out'][l]), xf, ln_g, ln_b)


def kernel(x, positions, w_in, rw_mu, rw_w0, rw_w1, rw_w2, rw_a0, rw_a1, rw_a2, rw_g1, rw_g2, rw_kk, rw_ka, rw_rk, rw_gn_g, rw_gn_b, ml_conv, ml_gate_b, ml_ln_g, w_branch, w_out, ln1_g, ln1_b, moe_w_group, moe_b_group, moe_w_expert, moe_b_expert, moe_w1, moe_w3, moe_w2, ln2_g, ln2_b):
    p = dict(w_in_t=jnp.swapaxes(w_in, 1, 2), rw_mu=rw_mu, rw_w0=rw_w0, rw_w1=rw_w1, rw_w2=rw_w2, rw_a0=rw_a0, rw_a1=rw_a1,
             rw_a2=rw_a2, rw_g1=rw_g1, rw_g2=rw_g2, rw_kk=rw_kk, rw_ka=rw_ka, rw_rk=rw_rk, rw_gn_g=rw_gn_g,
             rw_gn_b=rw_gn_b, ml_conv=ml_conv, ml_gate_b=ml_gate_b, ml_ln_g=ml_ln_g, w_branch=w_branch,
             w_out=w_out, moe_w_group=moe_w_group, moe_b_group=moe_b_group, moe_w_expert=moe_w_expert,
             moe_b_expert=moe_b_expert, moe_w1=moe_w1, moe_w3=moe_w3, moe_w2=moe_w2)
    batch, seq, d = x.shape
    assert batch == 1
    xf = x.reshape(seq, d)
    xb = xf.astype(BF16)
    tables = _rope_tables(positions.reshape(seq))
    for l in range(w_in.shape[0]):
        xf, xb = _mixer(xf, xb, tables, l, p, ln1_g[l], ln1_b[l])
        xf, xb = _moe(xf, l, p, ln2_g[l], ln2_b[l])
    return xf.reshape(batch, seq, d)
```

```python
import functools

import jax
import jax.numpy as jnp
from jax import lax
from jax.experimental import pallas as pl
from jax.experimental.pallas import tpu as pltpu

F32 = jnp.float32
BF16 = jnp.bfloat16

D_MODEL = 2048
DEPTH = 4
LN_EPS = 1e-5
DN_ALPHA = (2 * DEPTH) ** 0.25

RW_HEADS = 16
RW_HEAD_DIM = 64
RW_WIDTH = RW_HEADS * RW_HEAD_DIM
RW_GN_EPS = 64e-5

AT_GROUPS = ((128, 1), (512, 4), (2048, 16))
AT_HEADS_PER_GROUP = 6
AT_HEADS = AT_HEADS_PER_GROUP * len(AT_GROUPS)
AT_HEAD_DIM = 64
AT_WIDTH = AT_HEADS * AT_HEAD_DIM
AT_OUT_WIDTH = AT_HEADS_PER_GROUP * AT_HEAD_DIM
AT_BLOCK = 128
ROPE_DIMS = AT_HEAD_DIM // 4
ROPE_THETA = 500000.0

ML_HEADS = 8
ML_HEAD_DIM = 128
ML_WIDTH = ML_HEADS * ML_HEAD_DIM
ML_CHUNK = 64
ML_NORM_EPS = 1e-6

N_BRANCHES = 3
COL_RW = 0
COL_AT = 4 * RW_WIDTH
COL_ML = COL_AT + 3 * AT_WIDTH
COL_IF = COL_ML + 4 * ML_WIDTH
COL_GATE = COL_IF + 2 * ML_HEADS
IN_COLS = COL_GATE + N_BRANCHES * D_MODEL

MOE_GROUPS = 4
MOE_EXPERTS_PER_GROUP = 8
MOE_EXPERTS = MOE_GROUPS * MOE_EXPERTS_PER_GROUP
MOE_TOP_K = 2
MOE_FF = 512
MOE_ROWS = 256

LANES = 128
SUBLANES = 8
VMEM_LIMIT = 56 * 1024 * 1024

NT_DIMS = (((1,), (1,)), ((), ()))


def _params(*sem):
    return pltpu.CompilerParams(dimension_semantics=sem, vmem_limit_bytes=VMEM_LIMIT)


def _bdot(a, b):
    return jnp.dot(a.astype(BF16), b.astype(BF16), preferred_element_type=F32)


def _bdot_nt(a, b):
    return lax.dot_general(a.astype(BF16), b.astype(BF16), NT_DIMS, preferred_element_type=F32)


def _each(f, *xs):
    return [f(*a) for a in zip(*xs)]


def _shift_rows(x, prev_row):
    first = lax.broadcasted_iota(jnp.int32, x.shape, 0) == 0
    return jnp.where(first, prev_row, pltpu.roll(x, 1, axis=0))


def _pair_sum(x, first_head):
    lo = jnp.sum(jnp.where(first_head, x, 0.0), axis=-1, keepdims=True)
    hi = jnp.sum(jnp.where(first_head, 0.0, x), axis=-1, keepdims=True)
    return jnp.where(first_head, lo, hi)


def _mm_split_kernel(a_ref, w_ref, o_ref):
    a, w = a_ref[...], w_ref[...]
    a_hi, w_hi = a.astype(BF16), w.astype(BF16)
    a_lo = (a - a_hi.astype(F32)).astype(BF16)
    w_lo = (w - w_hi.astype(F32)).astype(BF16)
    o_ref[...] = (jnp.dot(a_hi, w_hi, preferred_element_type=F32) + jnp.dot(a_hi, w_lo, preferred_element_type=F32)
                  + jnp.dot(a_lo, w_hi, preferred_element_type=F32))


def _mm_split(a, w, tm=512, name="mm_split"):
    m, k = a.shape
    n = w.shape[1]
    return pl.pallas_call(
        _mm_split_kernel,
        grid=(m // tm,),
        in_specs=[pl.BlockSpec((tm, k), lambda i: (i, 0)), pl.BlockSpec((k, n), lambda i: (0, 0))],
        out_specs=pl.BlockSpec((tm, n), lambda i: (i, 0)),
        out_shape=jax.ShapeDtypeStruct((m, n), F32),
        compiler_params=_params("arbitrary"),
        name=name,
    )(a, w)


def _mm_t_kernel(a_ref, w_ref, o_ref, wb_ref):
    @pl.when(pl.program_id(1) == 0)
    def _():
        wb_ref[...] = jnp.transpose(w_ref[0]).astype(BF16)

    o_ref[...] = jnp.dot(a_ref[...], wb_ref[...], preferred_element_type=F32)


def _mm_t(a, w_t, *, layer, row0, n_rows, tm=1024, tn=None, name="mm_t"):
    m, k = a.shape
    tn = n_rows if tn is None else tn
    tm = min(tm, m)
    assert m % tm == 0 and n_rows % tn == 0 and row0 % SUBLANES == 0
    return pl.pallas_call(
        _mm_t_kernel,
        grid=(n_rows // tn, m // tm),
        in_specs=[pl.BlockSpec((tm, k), lambda j, i: (i, 0)),
                  pl.BlockSpec((pl.Element(1), pl.Element(tn), pl.Element(k)),
                               lambda j, i: (layer, pl.multiple_of(row0 + j * tn, SUBLANES), 0))],
        out_specs=pl.BlockSpec((tm, tn), lambda j, i: (i, j)),
        out_shape=jax.ShapeDtypeStruct((m, n_rows), F32),
        scratch_shapes=[pltpu.VMEM((k, tn), BF16)],
        compiler_params=_params("arbitrary", "arbitrary"),
        name=name,
    )(a, w_t)


def _layer_norm(y, g, b):
    mu = jnp.mean(y, axis=-1, keepdims=True)
    d = y - mu
    var = jnp.mean(d * d, axis=-1, keepdims=True)
    return d * lax.rsqrt(var + LN_EPS) * g + b


def _out_ln_kernel(y_ref, w_ref, x_ref, g_ref, b_ref, o_ref, ob_ref):
    h = jnp.dot(y_ref[...], w_ref[...], preferred_element_type=F32)
    out = _layer_norm(DN_ALPHA * x_ref[...] + h, g_ref[...], b_ref[...])
    o_ref[...] = out
    ob_ref[...] = out.astype(BF16)


def _out_ln(y, w, x, g, b, tm=256):
    m, d = x.shape
    row = pl.BlockSpec((tm, d), lambda i: (i, 0))
    vec = pl.BlockSpec((1, d), lambda i: (0, 0))
    return pl.pallas_call(
        _out_ln_kernel,
        grid=(m // tm,),
        in_specs=[row, pl.BlockSpec((d, d), lambda i: (0, 0)), row, vec, vec],
        out_specs=[row, row],
        out_shape=[jax.ShapeDtypeStruct((m, d), F32), jax.ShapeDtypeStruct((m, d), BF16)],
        compiler_params=_params("arbitrary"),
        name="out_ln",
    )(y, w, x, g.reshape(1, d), b.reshape(1, d))


RW_CHUNK = 64
RW_PAIR = 2 * RW_HEAD_DIM
RW_SUB = 16
RW_CHUNKS_PER_STEP = 2
RW_PREP_ROWS = 256
(RV_MU, RV_W0, RV_A0, RV_KK, RV_KA, RV_GN_G, RV_GN_B, RV_RK, RV_ROWS) = (0, 6, 7, 8, 9, 10, 11, 12, 16)


def _rwkv_prep_kernel(r_ref, k_ref, v_ref, z_ref, rp_ref, kp_ref, vp_ref, zp_ref, vec_ref,
                      w1_ref, w2_ref, a1_ref, a2_ref, g1_ref, g2_ref,
                      ro_ref, lw_ref, ko_ref, vo_ref, kk_ref, b_ref, g_ref):
    first_block = pl.program_id(0) == 0
    vec = lambda j: vec_ref[j:j + 1, :]
    prev = lambda ref: jnp.where(first_block, 0.0, ref[SUBLANES - 1:SUBLANES, :])
    lerp = lambda u, up, j: u + (_shift_rows(u, up) - u) * vec(RV_MU + j)
    r = lerp(r_ref[...], prev(rp_ref), 0)
    k = lerp(k_ref[...], prev(kp_ref), 1)
    v = lerp(v_ref[...], prev(vp_ref), 2)
    z = z_ref[...]
    z_diff = _shift_rows(z, prev(zp_ref)) - z
    xw, xa, xg = z + z_diff * vec(RV_MU + 3), z + z_diff * vec(RV_MU + 4), z + z_diff * vec(RV_MU + 5)
    w_pre = vec(RV_W0) + _bdot(jnp.tanh(_bdot(xw, w1_ref[...])), w2_ref[...])
    softplus = jnp.maximum(-w_pre, 0.0) + jnp.log(1.0 + jnp.exp(-jnp.abs(w_pre)))
    lw_ref[...] = -jnp.exp(-softplus - 0.5)
    a = jax.nn.sigmoid(vec(RV_A0) + _bdot(_bdot(xa, a1_ref[...]), a2_ref[...]))
    g_ref[...] = _bdot(jax.nn.sigmoid(_bdot(xg, g1_ref[...])), g2_ref[...])
    kk = k * vec(RV_KK)
    first_head = lax.broadcasted_iota(jnp.int32, (kk.shape[0], RW_PAIR), 1) < RW_HEAD_DIM
    for p in range(RW_HEADS // 2):
        sl = slice(p * RW_PAIR, (p + 1) * RW_PAIR)
        kk_p = kk[:, sl]
        kk_p = kk_p / jnp.maximum(jnp.sqrt(_pair_sum(kk_p * kk_p, first_head)), 1e-12)
        kk_ref[:, sl] = kk_p
        b_ref[:, sl] = kk_p * a[:, sl]
    ro_ref[...] = r
    ko_ref[...] = k * (1.0 + (a - 1.0) * vec(RV_KA))
    vo_ref[...] = v


def _rwkv_prep(rw, vecs, l, p):
    s = rw.shape[0]
    tm = min(RW_PREP_ROWS, s)
    per = tm // SUBLANES
    cur = lambda j: pl.BlockSpec((tm, RW_WIDTH), lambda i: (i, j))
    prev = lambda j: pl.BlockSpec((SUBLANES, RW_WIDTH), lambda i: (jnp.maximum(i * per - 1, 0), j))
    lora = lambda w: pl.BlockSpec((None,) + w.shape[1:], lambda i: (l, 0, 0))
    ws = [p['rw_w1'], p['rw_w2'], p['rw_a1'], p['rw_a2'], p['rw_g1'], p['rw_g2']]
    out = pl.BlockSpec((tm, RW_WIDTH), lambda i: (i, 0))
    return pl.pallas_call(
        _rwkv_prep_kernel,
        grid=(s // tm,),
        in_specs=[cur(0), cur(1), cur(2), cur(3), prev(0), prev(1), prev(2), prev(3),
                  pl.BlockSpec((RV_ROWS, RW_WIDTH), lambda i: (0, 0))] + [lora(w) for w in ws],
        out_specs=[out] * 7,
        out_shape=[jax.ShapeDtypeStruct((s, RW_WIDTH), F32)] * 7,
        compiler_params=_params("arbitrary"),
        name="rwkv_prep",
    )(rw, rw, rw, rw, rw, rw, rw, rw, vecs, *ws)


def _rwkv_kernel(r_ref, lw_ref, k_ref, v_ref, kk_ref, b_ref, g_ref, vec_ref, o_ref, zt_ref):
    @pl.when(pl.program_id(0) == 0)
    def _():
        zt_ref[...] = jnp.zeros_like(zt_ref)

    c, n2 = RW_CHUNK, RW_PAIR
    tri = (lax.broadcasted_iota(jnp.int32, (c, c), 1) <= lax.broadcasted_iota(jnp.int32, (c, c), 0)).astype(BF16)
    row = lax.broadcasted_iota(jnp.int32, (n2, n2), 0)
    col = lax.broadcasted_iota(jnp.int32, (n2, n2), 1)
    t_idx, s_idx = row % c, col % c
    strict = t_idx > s_idx
    incl = t_idx >= s_idx
    diag_blk = (row // RW_SUB) == (col // RW_SUB)
    eye = (row == col).astype(F32)
    first_head = lax.broadcasted_iota(jnp.int32, (c, n2), 1) < RW_HEAD_DIM

    def embed(x):
        return jnp.concatenate([jnp.where(first_head, x, 0.0), jnp.where(first_head, 0.0, x)], axis=0)

    pairs = range(RW_HEADS // 2)
    sls = [slice(p * n2, (p + 1) * n2) for p in pairs]
    bdot = lambda xs, ys: _each(_bdot, xs, ys)

    lhs, rhs, v_e, kb, gam = [], [], [], [], []
    for ck in range(RW_CHUNKS_PER_STEP):
        rs = slice(ck * c, (ck + 1) * c)
        lw = lw_ref[rs, :]
        lw_hi = lw.astype(BF16)
        rem = lw - lw_hi.astype(F32)
        lw_mid = rem.astype(BF16)
        lw_lo = (rem - lw_mid.astype(F32)).astype(BF16)
        g_in = (jnp.dot(tri, lw_hi, preferred_element_type=F32) + jnp.dot(tri, lw_mid, preferred_element_type=F32)
                + jnp.dot(tri, lw_lo, preferred_element_type=F32))
        g_last = g_in[c - 1:c, :]
        e_neg = jnp.exp(-g_in)
        e_end = jnp.exp(g_last - g_in)
        kkd = kk_ref[rs, :] * jnp.exp(g_in - lw)
        rd = r_ref[rs, :] * jnp.exp(g_in)
        kinv = k_ref[rs, :] * e_neg
        binv = b_ref[rs, :] * e_neg
        kd = k_ref[rs, :] * e_end
        bd = b_ref[rs, :] * e_end
        gam.append(jnp.exp(g_last))
        v_e += [embed(v_ref[rs, sl]) for sl in sls]
        lhs += [jnp.concatenate([embed(kkd[:, sl]), embed(rd[:, sl])], axis=0).astype(BF16) for sl in sls]
        rhs += [jnp.concatenate([embed(kinv[:, sl]), embed(binv[:, sl])], axis=0).astype(BF16) for sl in sls]
        kb += [jnp.concatenate([embed(kd[:, sl]), -embed(bd[:, sl])], axis=0).astype(BF16) for sl in sls]
    v_b = [x.astype(BF16) for x in v_e]
    aa = _each(_bdot_nt, lhs, rhs)
    a_kkk = [jnp.where(strict, x[:n2, :n2], 0.0).astype(BF16) for x in aa]
    a_kkb = [jnp.where(strict, x[:n2, n2:], 0.0) for x in aa]
    a_rk = [jnp.where(incl, x[n2:, :n2], 0.0).astype(BF16) for x in aa]
    a_rb = [jnp.where(incl, x[n2:, n2:], 0.0).astype(BF16) for x in aa]
    nd = [jnp.where(diag_blk, x, 0.0) for x in a_kkb]
    off = [jnp.where(diag_blk, 0.0, x) for x in a_kkb]
    nd2 = bdot(nd, nd)
    akv = bdot(a_kkk, v_b)
    o_v = bdot(a_rk, v_b)
    nd4 = bdot(nd2, nd2)
    p1 = bdot([eye - x for x in nd], [eye + x for x in nd2])
    nd8 = bdot(nd4, nd4)
    p2 = bdot(p1, [eye + x for x in nd4])
    d_inv = bdot(p2, [eye + x for x in nd8])
    e1 = bdot(d_inv, off)
    e2 = bdot(e1, e1)
    t_inv = [x.astype(BF16) for x in bdot(bdot([eye - x for x in e1], [eye + x for x in e2]), d_inv)]

    inv_n = 1.0 / RW_HEAD_DIM
    zt = [zt_ref[p] for p in pairs]
    for ck in range(RW_CHUNKS_PER_STEP):
        rs = slice(ck * c, (ck + 1) * c)
        at = lambda xs: xs[ck * len(sls):(ck + 1) * len(sls)]
        zt_b = [x.astype(BF16) for x in zt]
        x1 = _each(lambda a, z: _bdot_nt(a[:n2], z), at(lhs), zt_b)
        o_z = _each(lambda a, z: _bdot_nt(a[n2:], z), at(lhs), zt_b)
        u = bdot(at(t_inv), [a + b for a, b in zip(x1, at(akv))])
        o_u = bdot(at(a_rb), u)
        vu_t = [jnp.transpose(jnp.concatenate([a, b], axis=0)) for a, b in zip(at(v_e), u)]
        z_up = bdot(vu_t, at(kb))
        zt = [z * gam[ck][:, sl] + zu for z, sl, zu in zip(zt, sls, z_up)]
        for p in pairs:
            sl = sls[p]
            o_e = o_z[p] + at(o_v)[p] - o_u[p]
            out = o_e[:c] + o_e[c:]
            mu = _pair_sum(out, first_head) * inv_n
            dev = out - mu
            var = _pair_sum(dev * dev, first_head) * inv_n
            normed = (dev * lax.rsqrt(var + RW_GN_EPS) * vec_ref[RV_GN_G:RV_GN_G + 1, sl]
                      + vec_ref[RV_GN_B:RV_GN_B + 1, sl])
            bonus = _pair_sum(r_ref[rs, sl] * k_ref[rs, sl] * vec_ref[RV_RK:RV_RK + 1, sl], first_head) * v_ref[rs, sl]
            o_ref[rs, sl] = ((normed + bonus) * g_ref[rs, sl]).astype(o_ref.dtype)
    for p in pairs:
        zt_ref[p] = zt[p]


def _rwkv_scan(r, lw, k, v, kk, b, g, vecs):
    s, width = r.shape
    rows = RW_CHUNK * RW_CHUNKS_PER_STEP
    blk = pl.BlockSpec((rows, width), lambda i: (i, 0))
    return pl.pallas_call(
        _rwkv_kernel,
        grid=(s // rows,),
        in_specs=[blk] * 7 + [pl.BlockSpec((RV_ROWS, width), lambda i: (0, 0))],
        out_specs=blk,
        out_shape=jax.ShapeDtypeStruct((s, width), BF16),
        scratch_shapes=[pltpu.VMEM((RW_HEADS // 2, RW_PAIR, RW_PAIR), F32)],
        compiler_params=_params("arbitrary"),
        name="rwkv_scan",
    )(r, lw, k, v, kk, b, g, vecs)


def _rwkv7(rw, l, p):
    rows = [p['rw_mu'][l], p['rw_w0'][l][None], p['rw_a0'][l][None], p['rw_kk'][l][None], p['rw_ka'][l][None],
            p['rw_gn_g'][l][None], p['rw_gn_b'][l][None], p['rw_rk'][l].reshape(1, RW_WIDTH)]
    vecs = jnp.concatenate(rows + [jnp.zeros((RV_ROWS - RV_RK - 1, RW_WIDTH), F32)], axis=0)
    r, lw, k, v, kk, b, g = _rwkv_prep(rw, vecs, l, p)
    return _rwkv_scan(r, lw, k, v, kk, b, g, vecs)


AT_PREP_ROWS = 512
AT_SLABS = AT_WIDTH // LANES


def _rope_kernel(q_ref, k_ref, c_ref, s1_ref, s2_ref, qo_ref, ko_ref):
    cos, s_dn, s_up = c_ref[...], s1_ref[...], s2_ref[...]

    def rot(x, scale):
        outs = []
        for j in range(AT_SLABS):
            xs = x[:, j * LANES:(j + 1) * LANES]
            y = xs * cos + pltpu.roll(xs, LANES - ROPE_DIMS // 2, axis=1) * s_dn + pltpu.roll(xs, ROPE_DIMS // 2, axis=1) * s_up
            outs.append(y * scale if scale != 1.0 else y)
        return jnp.concatenate(outs, axis=-1)

    qo_ref[...] = rot(q_ref[...], AT_HEAD_DIM ** -0.5)
    ko_ref[...] = rot(k_ref[...], 1.0)


def _rope_tables(positions):
    s = positions.shape[0]
    half = ROPE_DIMS // 2
    inv_freq = ROPE_THETA ** (-jnp.arange(half, dtype=F32) * 2.0 / ROPE_DIMS)
    ang = positions.astype(F32)[:, None] * inv_freq
    cos, sin = jnp.cos(ang), jnp.sin(ang)
    rest = AT_HEAD_DIM - ROPE_DIMS
    zeros, ones = jnp.zeros((s, rest), F32), jnp.ones((s, rest), F32)
    z_half = jnp.zeros((s, half), F32)
    head = lambda parts: jnp.tile(jnp.concatenate(parts, axis=1), (1, LANES // AT_HEAD_DIM))
    return head([cos, cos, ones]), head([-sin, z_half, zeros]), head([z_half, sin, zeros])


def _rope(at, tables):
    s = at.shape[0]
    tm = min(AT_PREP_ROWS, s)
    col = lambda j: pl.BlockSpec((tm, AT_WIDTH), lambda i: (i, j))
    tab = pl.BlockSpec((tm, LANES), lambda i: (i, 0))
    return pl.pallas_call(
        _rope_kernel,
        grid=(s // tm,),
        in_specs=[col(0), col(1), tab, tab, tab],
        out_specs=[col(0)] * 2,
        out_shape=[jax.ShapeDtypeStruct((s, AT_WIDTH), F32)] * 2,
        compiler_params=_params("arbitrary"),
        name="attn_rope",
    )(at, at, *tables)


AT_GROUP_SLABS = AT_OUT_WIDTH // LANES


def _attn_kernel(*refs, dilation):
    ns = AT_GROUP_SLABS
    q_refs, kp_refs, kc_refs, vp_refs, vc_refs = (refs[i * ns:(i + 1) * ns] for i in range(5))
    o_ref, lse_ref, o_s, lse_s = refs[5 * ns:]
    n = pl.program_id(0)
    qi = lax.broadcasted_iota(jnp.int32, (AT_BLOCK, AT_BLOCK), 0)
    kj = lax.broadcasted_iota(jnp.int32, (AT_BLOCK, AT_BLOCK), 1)
    mask_c = kj <= qi
    mask_p = (kj >= qi) & (n > 0)
    per_slab = LANES // AT_HEAD_DIM
    ones_b = jnp.ones((AT_BLOCK, AT_HEAD_DIM), BF16)

    def residue(r, carry):
        rows = pl.ds(r, AT_BLOCK, stride=dilation) if dilation > 1 else slice(None)

        def heads_of(slab_refs):
            slabs = [ref[rows, :].astype(BF16) for ref in slab_refs]
            return [x[:, j * AT_HEAD_DIM:(j + 1) * AT_HEAD_DIM] for x in slabs for j in range(per_slab)]

        q, kc, kp, vc, vp = (heads_of(x) for x in (q_refs, kc_refs, kp_refs, vc_refs, vp_refs))
        s_c = [jnp.where(mask_c, lax.dot_general(a, b, NT_DIMS, preferred_element_type=F32), -jnp.inf)
               for a, b in zip(q, kc)]
        s_p = [jnp.where(mask_p, lax.dot_general(a, b, NT_DIMS, preferred_element_type=F32), -jnp.inf)
               for a, b in zip(q, kp)]
        m = [jnp.maximum(jnp.max(a, axis=-1, keepdims=True), jnp.max(b, axis=-1, keepdims=True))
             for a, b in zip(s_c, s_p)]
        p_c = [jnp.exp(a - mm).astype(BF16) for a, mm in zip(s_c, m)]
        p_p = [jnp.exp(a - mm).astype(BF16) for a, mm in zip(s_p, m)]
        aug = lambda v: jnp.concatenate([v, ones_b], axis=1)
        acc = [jnp.dot(a, aug(v), preferred_element_type=F32) + jnp.dot(b, aug(w), preferred_element_type=F32)
               for a, v, b, w in zip(p_c, vc, p_p, vp)]
        out = [x[:, :AT_HEAD_DIM] / x[:, AT_HEAD_DIM:] for x in acc]
        lse = [mm + jnp.log(x[:, AT_HEAD_DIM:]) for mm, x in zip(m, acc)]
        for t in range(ns):
            o_s[t, rows, :] = jnp.concatenate(out[t * per_slab:(t + 1) * per_slab], axis=-1)
            lse_s[t, rows, :] = jnp.concatenate(lse[t * per_slab:(t + 1) * per_slab], axis=-1)
        return carry

    if dilation > 1:
        lax.fori_loop(0, dilation, residue, 0)
    else:
        residue(0, 0)
    for t in range(ns):
        o_ref[:, t * LANES:(t + 1) * LANES] = o_s[t]
        lse_ref[:, t * LANES:(t + 1) * LANES] = lse_s[t]


def _dilated_attention(q, k, at, g, dilation):
    s = q.shape[0]
    rows = AT_BLOCK * dilation
    assert s % rows == 0
    ns = AT_GROUP_SLABS
    qk_col = g * ns
    v_col = 2 * (AT_WIDTH // LANES) + g * ns
    cur = lambda c: [pl.BlockSpec((rows, LANES), lambda n, t=t: (n, c + t)) for t in range(ns)]
    prev = lambda c: [pl.BlockSpec((rows, LANES), lambda n, t=t: (jnp.maximum(n - 1, 0), c + t)) for t in range(ns)]
    out = pl.BlockSpec((rows, AT_OUT_WIDTH), lambda n: (n, 0))
    return pl.pallas_call(
        functools.partial(_attn_kernel, dilation=dilation),
        grid=(s // rows,),
        in_specs=cur(qk_col) + prev(qk_col) + cur(qk_col) + prev(v_col) + cur(v_col),
        out_specs=[out, out],
        out_shape=[jax.ShapeDtypeStruct((s, AT_OUT_WIDTH), F32)] * 2,
        scratch_shapes=[pltpu.VMEM((ns, rows, LANES), F32)] * 2,
        compiler_params=_params("arbitrary"),
        name=f"attn_d{dilation}",
    )(*([q] * ns + [k] * (2 * ns) + [at] * (2 * ns)))


def _attn_merge_kernel(o0, o1, o2, l0, l1, l2, y_ref):
    m = jnp.maximum(jnp.maximum(l0[...], l1[...]), l2[...])
    e0, e1, e2 = jnp.exp(l0[...] - m), jnp.exp(l1[...] - m), jnp.exp(l2[...] - m)
    tot = e0 + e1 + e2
    y_ref[...] = ((e0 / tot) * o0[...] + (e1 / tot) * o1[...] + (e2 / tot) * o2[...]).astype(BF16)


def _attention(at, tables):
    s = at.shape[0]
    q, k = _rope(at, tables)
    outs, lses = [], []
    for g, (_, dilation) in enumerate(AT_GROUPS):
        o, lse = _dilated_attention(q, k, at, g, dilation)
        outs.append(o)
        lses.append(lse)
    tm = min(AT_PREP_ROWS, s)
    blk = pl.BlockSpec((tm, AT_OUT_WIDTH), lambda i: (i, 0))
    return pl.pallas_call(
        _attn_merge_kernel,
        grid=(s // tm,),
        in_specs=[blk] * 6,
        out_specs=blk,
        out_shape=jax.ShapeDtypeStruct((s, AT_OUT_WIDTH), BF16),
        compiler_params=_params("arbitrary"),
        name="attn_merge",
    )(*outs, *lses)


ML_CHUNKS_PER_STEP = 8
ML_ROWS = ML_CHUNKS_PER_STEP * ML_CHUNK
ML_GROUP = 4


def _mlstm_kernel(u_ref, v_ref, og_ref, cw_ref, bb_ref, ib_ref, brow_ref, irow_ref, g_ref, y_ref,
                  qk_ref, ubuf_ref, ct_ref, m_ref):
    @pl.when(pl.program_id(0) == 0)
    def _():
        ubuf_ref[:SUBLANES, :] = jnp.zeros((SUBLANES, ubuf_ref.shape[1]), F32)
        ct_ref[...] = jnp.zeros_like(ct_ref)
        m_ref[...] = jnp.full_like(m_ref, -jnp.inf)

    taps = cw_ref.shape[0]
    ubuf_ref[SUBLANES:, :] = u_ref[...]
    conv = ubuf_ref[SUBLANES:, :] * cw_ref[taps - 1:taps, :]
    for back in range(1, taps):
        conv = conv + ubuf_ref[pl.ds(SUBLANES - back, ML_ROWS), :] * cw_ref[taps - 1 - back:taps - back, :]
    qk_ref[...] = conv * jax.nn.sigmoid(conv)
    ubuf_ref[:SUBLANES, :] = ubuf_ref[ML_ROWS:, :]

    dh, lc = ML_HEAD_DIM, ML_CHUNK
    si = lax.broadcasted_iota(jnp.int32, (lc, lc), 0)
    ji = lax.broadcasted_iota(jnp.int32, (lc, lc), 1)
    causal = ji <= si
    ones_b = jnp.ones((lc, dh), BF16)

    def group(c, rows, heads):
        cols = [slice(h * dh, (h + 1) * dh) for h in heads]
        q = [qk_ref[rows, cl] * (dh ** -0.5) for cl in cols]
        k = [qk_ref[rows, pl.ds(ML_WIDTH + h * dh, dh)] for h in heads]
        v = [v_ref[rows, cl] for cl in cols]
        bb = [bb_ref[rows, cl] for cl in cols]
        ib = [ib_ref[rows, cl] for cl in cols]
        b_row = [brow_ref[c, h:h + 1, :] for h in heads]
        i_row = [irow_ref[c, h:h + 1, :] for h in heads]
        m_prev = [m_ref[h, 0:1, :] for h in heads]
        ct_prev = [ct_ref[h] for h in heads]
        qb = [x.astype(BF16) for x in q]
        kb = [x.astype(BF16) for x in k]
        s_qk = _each(_bdot_nt, qb, kb)
        inter = _each(_bdot, qb, ct_prev)
        k_t = [jnp.transpose(x).astype(BF16) for x in k]
        log_d = [jnp.where(causal, b[:, :lc] - br + ir, -jnp.inf) for b, br, ir in zip(bb, b_row, i_row)]
        a_log = [b + mp for b, mp in zip(bb, m_prev)]
        m_s = [jnp.maximum(al, jnp.max(ld, axis=-1, keepdims=True)) for al, ld in zip(a_log, log_d)]
        inter_w = [jnp.exp(al - ms) for al, ms in zip(a_log, m_s)]
        qk = [s * jnp.exp(ld - ms[:, :lc]) for s, ld, ms in zip(s_qk, log_d, m_s)]
        intra = [_bdot(a, jnp.concatenate([x.astype(BF16), ones_b], axis=1)) for a, x in zip(qk, v)]
        b_last = [b[lc - 1:lc, :] for b in bb]
        w_end = [bl - b + i for bl, b, i in zip(b_last, bb, ib)]
        m_new = [jnp.maximum(bl + mp, jnp.max(we, axis=0, keepdims=True)) for bl, mp, we in zip(b_last, m_prev, w_end)]
        dec = [jnp.exp(bl + mp - mn) for bl, mp, mn in zip(b_last, m_prev, m_new)]
        wts = [jnp.exp(we - mn) for we, mn in zip(w_end, m_new)]
        upd = [_bdot(kt, jnp.concatenate([x * w, w], axis=1)) for kt, x, w in zip(k_t, v, wts)]
        for h, ctp, d, up, mn in zip(heads, ct_prev, dec, upd, m_new):
            ct_ref[h] = jnp.concatenate([d, d], axis=1) * ctp + up
            m_ref[h, 0:1, :] = mn
        num = [w * a[:, :dh] + b[:, :dh] for w, a, b in zip(inter_w, inter, intra)]
        den = [w * a[:, dh:] + b[:, dh:] for w, a, b in zip(inter_w, inter, intra)]
        hid = [n / jnp.maximum(jnp.abs(d), jnp.exp(-ms)) for n, d, ms in zip(num, den, m_s)]
        hid = [x * jax.nn.sigmoid(og_ref[rows, cl]) for x, cl in zip(hid, cols)]
        mu = [jnp.mean(x, axis=-1, keepdims=True) for x in hid]
        dev = [x - m for x, m in zip(hid, mu)]
        var = [jnp.mean(x * x, axis=-1, keepdims=True) for x in dev]
        for cl, x, vr in zip(cols, dev, var):
            y_ref[rows, cl] = (x * lax.rsqrt(vr + ML_NORM_EPS) * g_ref[:, cl]).astype(y_ref.dtype)

    def chunk(c, carry):
        rows = pl.ds(pl.multiple_of(c * lc, lc), lc)
        for h0 in range(0, ML_HEADS, ML_GROUP):
            group(c, rows, range(h0, h0 + ML_GROUP))
        return carry

    lax.fori_loop(0, ML_CHUNKS_PER_STEP, chunk, 0)


def _mlstm(ml, gates_if, l, p):
    s = ml.shape[0]
    nc = s // ML_CHUNK
    i_pre = gates_if[:, :ML_HEADS] + p['ml_gate_b'][l, 0]
    f_pre = gates_if[:, ML_HEADS:] + p['ml_gate_b'][l, 1]
    lf = jax.nn.log_sigmoid(f_pre)
    b_cum = jnp.cumsum(lf.reshape(nc, ML_CHUNK, ML_HEADS), axis=1)
    b_row = jnp.transpose(b_cum, (0, 2, 1))
    i_row = jnp.transpose(i_pre.reshape(nc, ML_CHUNK, ML_HEADS), (0, 2, 1))
    over_lanes = lambda a: jnp.repeat(a.reshape(s, ML_HEADS), ML_HEAD_DIM, axis=1)
    col = lambda j: pl.BlockSpec((ML_ROWS, ML_WIDTH), lambda i: (i, j))
    grow = pl.BlockSpec((ML_CHUNKS_PER_STEP, ML_HEADS, ML_CHUNK), lambda i: (i, 0, 0))
    conv_w = p['ml_conv'][l]
    return pl.pallas_call(
        _mlstm_kernel,
        grid=(s // ML_ROWS,),
        in_specs=[pl.BlockSpec((ML_ROWS, 2 * ML_WIDTH), lambda i: (i, 0)), col(2), col(3),
                  pl.BlockSpec(conv_w.shape, lambda i: (0, 0)), col(0), col(0), grow, grow,
                  pl.BlockSpec((1, ML_WIDTH), lambda i: (0, 0))],
        out_specs=pl.BlockSpec((ML_ROWS, ML_WIDTH), lambda i: (i, 0)),
        out_shape=jax.ShapeDtypeStruct((s, ML_WIDTH), BF16),
        scratch_shapes=[pltpu.VMEM((ML_ROWS, 2 * ML_WIDTH), F32),
                        pltpu.VMEM((ML_ROWS + SUBLANES, 2 * ML_WIDTH), F32),
                        pltpu.VMEM((ML_HEADS, ML_HEAD_DIM, 2 * ML_HEAD_DIM), F32),
                        pltpu.VMEM((ML_HEADS, SUBLANES, ML_HEAD_DIM), F32)],
        compiler_params=_params("arbitrary"),
        name="mlstm",
    )(ml, ml, ml, conv_w, over_lanes(b_cum), over_lanes(i_pre), b_row, i_row,
      p['ml_ln_g'][l].reshape(1, ML_WIDTH))


def _merge_kernel(x_ref, yr_ref, ya_ref, ym_ref, g0_ref, g1_ref, g2_ref, w0_ref, w1_ref, w2_ref, o_ref, gb_ref):
    @pl.when(pl.program_id(1) == 0)
    def _():
        for b, g_ref in enumerate((g0_ref, g1_ref, g2_ref)):
            gb_ref[b] = jnp.transpose(g_ref[0]).astype(BF16)

    x = x_ref[...]
    gate = lambda b: jax.nn.sigmoid(jnp.dot(x, gb_ref[b], preferred_element_type=F32))
    y = (gate(0) * jnp.dot(yr_ref[...], w0_ref[...], preferred_element_type=F32)
         + gate(1) * jnp.dot(ya_ref[...], w1_ref[...], preferred_element_type=F32)
         + gate(2) * jnp.dot(ym_ref[...], w2_ref[...], preferred_element_type=F32))
    o_ref[...] = y.astype(o_ref.dtype)


def _merge(xb, y_rw, y_at, y_ml, w_in_t, l, wb_rw, wb_at, wb_ml, tm=512, tn=512):
    m, d = xb.shape
    nj = d // tn
    act = lambda a: pl.BlockSpec((tm, a.shape[1]), lambda j, i: (i, 0))
    gate = lambda b: pl.BlockSpec((pl.Element(1), pl.Element(tn), pl.Element(d)),
                                  lambda j, i: (l, pl.multiple_of(COL_GATE + b * d + j * tn, SUBLANES), 0))
    wsp = lambda w: pl.BlockSpec((w.shape[0], tn), lambda j, i: (0, j))
    return pl.pallas_call(
        _merge_kernel,
        grid=(nj, m // tm),
        in_specs=[act(xb), act(y_rw), act(y_at), act(y_ml), gate(0), gate(1), gate(2),
                  wsp(wb_rw), wsp(wb_at), wsp(wb_ml)],
        out_specs=pl.BlockSpec((tm, tn), lambda j, i: (i, j)),
        out_shape=jax.ShapeDtypeStruct((m, d), BF16),
        scratch_shapes=[pltpu.VMEM((N_BRANCHES, d, tn), BF16)],
        compiler_params=_params("arbitrary", "arbitrary"),
        name="merge",
    )(xb, y_rw, y_at, y_ml, w_in_t, w_in_t, w_in_t, wb_rw, wb_at, wb_ml)


MOE_LN_ROWS = 256


def _row_gather(idx_ref, base, n, src_hbm, dst, sem, unroll=False):
    def start(j):
        pltpu.make_async_copy(src_hbm.at[pl.ds(idx_ref[base + j], 1)], dst.at[pl.ds(j, 1)], sem).start()

    if unroll:
        for j in range(n):
            start(j)
    else:
        lax.fori_loop(0, n, lambda j, carry: (start(j), carry)[1], 0, unroll=8)


def _row_gather_wait(n, src_hbm, dst, sem):
    pltpu.make_async_copy(src_hbm.at[pl.ds(0, n)], dst.at[pl.ds(0, n)], sem).wait()


def _expert_kernel(be_ref, new_ref, used_ref, src_ref, x_hbm, w1_ref, w3_ref, w2_ref, o_ref,
                   xbuf, w1b, w3b, w2b, sem):
    i = pl.program_id(0)
    slot = i % 2

    @pl.when(i == 0)
    def _():
        _row_gather(src_ref, 0, MOE_ROWS, x_hbm, xbuf.at[0], sem.at[0])

    @pl.when(new_ref[i] == 1)
    def _():
        w1b[...] = w1_ref[...].astype(BF16)
        w3b[...] = w3_ref[...].astype(BF16)
        w2b[...] = w2_ref[...].astype(BF16)

    @pl.when(used_ref[i] == 1)
    def _():
        _row_gather_wait(MOE_ROWS, x_hbm, xbuf.at[slot], sem.at[slot])
        _row_gather(src_ref, (i + 1) * MOE_ROWS, MOE_ROWS, x_hbm, xbuf.at[1 - slot], sem.at[1 - slot], unroll=True)
        x = xbuf[slot].astype(BF16)
        h1 = jnp.dot(x, w1b[...], preferred_element_type=F32)
        h3 = jnp.dot(x, w3b[...], preferred_element_type=F32)
        hid = (h1 * jax.nn.sigmoid(h1) * h3).astype(BF16)
        o_ref[...] = jnp.dot(hid, w2b[...], preferred_element_type=F32)

    @pl.when(used_ref[i] == 0)
    def _():
        @pl.when(used_ref[jnp.maximum(i - 1, 0)] == 1)
        def _():
            _row_gather_wait(MOE_ROWS, x_hbm, xbuf.at[slot], sem.at[slot])

        o_ref[...] = jnp.zeros_like(o_ref)


def _experts(x, src, blk_expert, blk_new, blk_used, w1, w3, w2, l):
    rows = src.shape[0]
    d = x.shape[1]
    nblk = rows // MOE_ROWS
    up = pl.BlockSpec((None, None, d, MOE_FF), lambda i, be, nw, us, sr: (l, be[i], 0, 0))
    down = pl.BlockSpec((None, None, MOE_FF, d), lambda i, be, nw, us, sr: (l, be[i], 0, 0))
    return pl.pallas_call(
        _expert_kernel,
        grid_spec=pltpu.PrefetchScalarGridSpec(
            num_scalar_prefetch=4,
            grid=(nblk,),
            in_specs=[pl.BlockSpec(memory_space=pl.ANY), up, up, down],
            out_specs=pl.BlockSpec((MOE_ROWS, d), lambda i, be, nw, us, sr: (i, 0)),
            scratch_shapes=[pltpu.VMEM((2, MOE_ROWS, d), F32),
                            pltpu.VMEM((d, MOE_FF), BF16), pltpu.VMEM((d, MOE_FF), BF16),
                            pltpu.VMEM((MOE_FF, d), BF16),
                            pltpu.SemaphoreType.DMA((2,))]),
        out_shape=jax.ShapeDtypeStruct((rows, d), F32),
        compiler_params=_params("arbitrary"),
        name="experts",
    )(blk_expert, blk_new, blk_used, src, x, w1, w3, w2)


def _moe_ln_kernel(pos_ref, x_ref, w_ref, y_hbm, g_ref, b_ref, o_ref, ob_ref, ybuf, sem):
    i = pl.program_id(0)
    nblk = pl.num_programs(0)
    n = MOE_TOP_K * MOE_LN_ROWS
    tm = MOE_LN_ROWS
    slot = i % 2

    @pl.when(i == 0)
    def _():
        _row_gather(pos_ref, 0, n, y_hbm, ybuf.at[0], sem.at[0])

    _row_gather_wait(n, y_hbm, ybuf.at[slot], sem.at[slot])
    _row_gather(pos_ref, (i + 1) * n, n, y_hbm, ybuf.at[1 - slot], sem.at[1 - slot], unroll=True)
    rows = ybuf.at[slot]
    h = rows[0:tm, :] * w_ref[:, 0:1] + rows[tm:2 * tm, :] * w_ref[:, 1:2]
    out = _layer_norm(DN_ALPHA * x_ref[...] + h, g_ref[...], b_ref[...])
    o_ref[...] = out
    ob_ref[...] = out.astype(BF16)

    @pl.when(i == nblk - 1)
    def _():
        _row_gather_wait(n, y_hbm, ybuf.at[1 - slot], sem.at[1 - slot])


def _moe_ln(x, y, pos, top_w, g, b):
    t, d = x.shape
    tm = MOE_LN_ROWS
    pos = jnp.transpose(pos.reshape(t // tm, tm, MOE_TOP_K), (0, 2, 1)).reshape(t * MOE_TOP_K)
    pos = jnp.concatenate([pos, jnp.zeros((tm * MOE_TOP_K,), jnp.int32)])
    row = pl.BlockSpec((tm, d), lambda i, pos: (i, 0))
    vec = pl.BlockSpec((1, d), lambda i, pos: (0, 0))
    return pl.pallas_call(
        _moe_ln_kernel,
        grid_spec=pltpu.PrefetchScalarGridSpec(
            num_scalar_prefetch=1,
            grid=(t // tm,),
            in_specs=[row, pl.BlockSpec((tm, MOE_TOP_K), lambda i, pos: (i, 0)),
                      pl.BlockSpec(memory_space=pl.ANY), vec, vec],
            out_specs=[row, row],
            scratch_shapes=[pltpu.VMEM((2, MOE_TOP_K * tm, d), F32),
                            pltpu.SemaphoreType.DMA((2,))]),
        out_shape=[jax.ShapeDtypeStruct((t, d), F32), jax.ShapeDtypeStruct((t, d), BF16)],
        compiler_params=_params("arbitrary"),
        name="moe_ln",
    )(pos, x, top_w, y, g.reshape(1, d), b.reshape(1, d))


def _moe(x, l, p, ln_g, ln_b):
    t, d = x.shape
    n_assign = t * MOE_TOP_K
    w_router = jnp.concatenate([p['moe_w_group'][l], p['moe_w_expert'][l]], axis=1)
    w_router = jnp.pad(w_router, ((0, 0), (0, LANES - w_router.shape[1])))
    logits = _mm_split(x, w_router, name="mm_router")
    g_logits = logits[:, :MOE_GROUPS] + p['moe_b_group'][l]
    g_prob = jax.nn.softmax(g_logits, -1)
    g_idx = jnp.argmax(g_logits, -1)
    g_w = jnp.take_along_axis(g_prob, g_idx[:, None], axis=1)
    e_logits = (logits[:, MOE_GROUPS:MOE_GROUPS + MOE_EXPERTS] + p['moe_b_expert'][l]).reshape(
        t, MOE_GROUPS, MOE_EXPERTS_PER_GROUP)
    e_sel = jnp.take_along_axis(e_logits, g_idx[:, None, None], axis=1)[:, 0]
    top_l, top_i = lax.top_k(e_sel, MOE_TOP_K)
    top_w = jax.nn.softmax(top_l, -1) * g_w
    e_flat = (g_idx[:, None] * MOE_EXPERTS_PER_GROUP + top_i).reshape(n_assign).astype(jnp.int32)
    onehot = (e_flat[:, None] == jnp.arange(MOE_EXPERTS, dtype=jnp.int32)[None]).astype(jnp.int32)
    rank = jnp.take_along_axis(jnp.cumsum(onehot, axis=0), e_flat[:, None], axis=1)[:, 0] - 1
    counts = jnp.sum(onehot, axis=0)
    pcounts = (counts + MOE_ROWS - 1) // MOE_ROWS * MOE_ROWS
    pends = jnp.cumsum(pcounts)
    pstarts = pends - pcounts
    pos = pstarts[e_flat] + rank
    n_rows = n_assign + (MOE_EXPERTS + 1) * MOE_ROWS
    nblk = n_rows // MOE_ROWS
    src = jnp.zeros((n_rows,), jnp.int32).at[pos].set(jnp.arange(n_assign, dtype=jnp.int32) // MOE_TOP_K)
    blk_start = jnp.arange(nblk, dtype=jnp.int32) * MOE_ROWS
    blk_expert = jnp.clip(jnp.searchsorted(pends, blk_start, side='right'), 0, MOE_EXPERTS - 1).astype(jnp.int32)
    blk_used = (blk_start < pends[-1]).astype(jnp.int32)
    blk_new = jnp.concatenate([jnp.ones((1,), jnp.int32), (blk_expert[1:] != blk_expert[:-1]).astype(jnp.int32)])
    y = _experts(x, src, blk_expert, blk_new, blk_used, p['moe_w1'], p['moe_w3'], p['moe_w2'], l)
    return _moe_ln(x, y, pos.astype(jnp.int32).reshape(t, MOE_TOP_K), top_w, ln_g, ln_b)


def _mixer(xf, xb, tables, l, p, ln_g, ln_b):
    w_in_t = p['w_in_t']
    bf = lambda w: w.astype(BF16)
    rw = _mm_t(xb, w_in_t, layer=l, row0=COL_RW, n_rows=COL_AT - COL_RW, tn=1024, name="mm_in_rw")
    at = _mm_t(xb, w_in_t, layer=l, row0=COL_AT, n_rows=COL_ML - COL_AT, tn=AT_WIDTH, name="mm_in_at")
    ml = _mm_t(xb, w_in_t, layer=l, row0=COL_ML, n_rows=COL_IF - COL_ML, tn=1024, name="mm_in_ml")
    gates_if = _mm_t(xb, w_in_t, layer=l, row0=COL_IF, n_rows=COL_GATE - COL_IF, name="mm_in_if")
    y_rw = _rwkv7(rw, l, p)
    y_at = _attention(at, tables)
    y_ml = _mlstm(ml, gates_if, l, p)
    wb = p['w_branch'][l]
    y = _merge(xb, y_rw, y_at, y_ml, w_in_t, l, bf(wb[:RW_WIDTH]),
               bf(wb[RW_WIDTH:RW_WIDTH + AT_OUT_WIDTH]), bf(wb[RW_WIDTH + AT_OUT_WIDTH:]))
    return _out_ln(y, bf(p['w_out'][l]), xf, ln_g, ln_b)


def kernel(x, positions, w_in, rw_mu, rw_w0, rw_w1, rw_w2, rw_a0, rw_a1, rw_a2, rw_g1, rw_g2, rw_kk, rw_ka, rw_rk, rw_gn_g, rw_gn_b, ml_conv, ml_gate_b, ml_ln_g, w_branch, w_out, ln1_g, ln1_b, moe_w_group, moe_b_group, moe_w_expert, moe_b_expert, moe_w1, moe_w3, moe_w2, ln2_g, ln2_b):
    p = dict(w_in_t=jnp.swapaxes(w_in, 1, 2), rw_mu=rw_mu, rw_w0=rw_w0, rw_w1=rw_w1, rw_w2=rw_w2, rw_a0=rw_a0, rw_a1=rw_a1,
             rw_a2=rw_a2, rw_g1=rw_g1, rw_g2=rw_g2, rw_kk=rw_kk, rw_ka=rw_ka, rw_rk=rw_rk, rw_gn_g=rw_gn_g,
             rw_gn_b=rw_gn_b, ml_conv=ml_conv, ml_gate_b=ml_gate_b, ml_ln_g=ml_ln_g, w_branch=w_branch,
             w_out=w_out, moe_w_group=moe_w_group, moe_b_group=moe_b_group, moe_w_expert=moe_w_expert,
             moe_b_expert=moe_b_expert, moe_w1=moe_w1, moe_w3=moe_w3, moe_w2=moe_w2)
    batch, seq, d = x.shape
    assert batch == 1
    xf = x.reshape(seq, d)
    xb = xf.astype(BF16)
    tables = _rope_tables(positions.reshape(seq))
    for l in range(w_in.shape[0]):
        xf, xb = _mixer(xf, xb, tables, l, p, ln1_g[l], ln1_b[l])
        xf, xb = _moe(xf, l, p, ln2_g[l], ln2_b[l])
    return xf.reshape(batch, seq, d)
```

```python
import functools

import jax
import jax.numpy as jnp
from jax import lax
from jax.experimental import pallas as pl
from jax.experimental.pallas import tpu as pltpu

F32 = jnp.float32
BF16 = jnp.bfloat16

D_MODEL = 2048
DEPTH = 4
LN_EPS = 1e-5
DN_ALPHA = (2 * DEPTH) ** 0.25

RW_HEADS = 16
RW_HEAD_DIM = 64
RW_WIDTH = RW_HEADS * RW_HEAD_DIM
RW_GN_EPS = 64e-5

AT_GROUPS = ((128, 1), (512, 4), (2048, 16))
AT_HEADS_PER_GROUP = 6
AT_HEADS = AT_HEADS_PER_GROUP * len(AT_GROUPS)
AT_HEAD_DIM = 64
AT_WIDTH = AT_HEADS * AT_HEAD_DIM
AT_OUT_WIDTH = AT_HEADS_PER_GROUP * AT_HEAD_DIM
AT_BLOCK = 128
ROPE_DIMS = AT_HEAD_DIM // 4
ROPE_THETA = 500000.0

ML_HEADS = 8
ML_HEAD_DIM = 128
ML_WIDTH = ML_HEADS * ML_HEAD_DIM
ML_CHUNK = 64
ML_NORM_EPS = 1e-6

N_BRANCHES = 3
COL_RW = 0
COL_AT = 4 * RW_WIDTH
COL_ML = COL_AT + 3 * AT_WIDTH
COL_IF = COL_ML + 4 * ML_WIDTH
COL_GATE = COL_IF + 2 * ML_HEADS
IN_COLS = COL_GATE + N_BRANCHES * D_MODEL

MOE_GROUPS = 4
MOE_EXPERTS_PER_GROUP = 8
MOE_EXPERTS = MOE_GROUPS * MOE_EXPERTS_PER_GROUP
MOE_TOP_K = 2
MOE_FF = 512
MOE_ROWS = 256

LANES = 128
SUBLANES = 8
VMEM_LIMIT = 56 * 1024 * 1024

NT_DIMS = (((1,), (1,)), ((), ()))


def _params(*sem):
    return pltpu.CompilerParams(dimension_semantics=sem, vmem_limit_bytes=VMEM_LIMIT)


def _bdot(a, b):
    return jnp.dot(a.astype(BF16), b.astype(BF16), preferred_element_type=F32)


def _bdot_nt(a, b):
    return lax.dot_general(a.astype(BF16), b.astype(BF16), NT_DIMS, preferred_element_type=F32)


def _each(f, *xs):
    return [f(*a) for a in zip(*xs)]


def _shift_rows(x, prev_row):
    first = lax.broadcasted_iota(jnp.int32, x.shape, 0) == 0
    return jnp.where(first, prev_row, pltpu.roll(x, 1, axis=0))


def _pair_sum(x, first_head):
    lo = jnp.sum(jnp.where(first_head, x, 0.0), axis=-1, keepdims=True)
    hi = jnp.sum(jnp.where(first_head, 0.0, x), axis=-1, keepdims=True)
    return jnp.where(first_head, lo, hi)


def _mm_split_kernel(a_ref, w_ref, o_ref):
    a, w = a_ref[...], w_ref[...]
    a_hi, w_hi = a.astype(BF16), w.astype(BF16)
    a_lo = (a - a_hi.astype(F32)).astype(BF16)
    w_lo = (w - w_hi.astype(F32)).astype(BF16)
    o_ref[...] = (jnp.dot(a_hi, w_hi, preferred_element_type=F32) + jnp.dot(a_hi, w_lo, preferred_element_type=F32)
                  + jnp.dot(a_lo, w_hi, preferred_element_type=F32))


def _mm_split(a, w, tm=512, name="mm_split"):
    m, k = a.shape
    n = w.shape[1]
    return pl.pallas_call(
        _mm_split_kernel,
        grid=(m // tm,),
        in_specs=[pl.BlockSpec((tm, k), lambda i: (i, 0)), pl.BlockSpec((k, n), lambda i: (0, 0))],
        out_specs=pl.BlockSpec((tm, n), lambda i: (i, 0)),
        out_shape=jax.ShapeDtypeStruct((m, n), F32),
        compiler_params=_params("arbitrary"),
        name=name,
    )(a, w)


def _mm_t_kernel(a_ref, w_ref, o_ref, wb_ref):
    @pl.when(pl.program_id(1) == 0)
    def _():
        wb_ref[...] = jnp.transpose(w_ref[0]).astype(BF16)

    o_ref[...] = jnp.dot(a_ref[...], wb_ref[...], preferred_element_type=F32)


def _mm_t(a, w_t, *, layer, row0, n_rows, tm=1024, tn=None, name="mm_t"):
    m, k = a.shape
    tn = n_rows if tn is None else tn
    tm = min(tm, m)
    assert m % tm == 0 and n_rows % tn == 0 and row0 % SUBLANES == 0
    return pl.pallas_call(
        _mm_t_kernel,
        grid=(n_rows // tn, m // tm),
        in_specs=[pl.BlockSpec((tm, k), lambda j, i: (i, 0)),
                  pl.BlockSpec((pl.Element(1), pl.Element(tn), pl.Element(k)),
                               lambda j, i: (layer, pl.multiple_of(row0 + j * tn, SUBLANES), 0))],
        out_specs=pl.BlockSpec((tm, tn), lambda j, i: (i, j)),
        out_shape=jax.ShapeDtypeStruct((m, n_rows), F32),
        scratch_shapes=[pltpu.VMEM((k, tn), BF16)],
        compiler_params=_params("arbitrary", "arbitrary"),
        name=name,
    )(a, w_t)


def _layer_norm(y, g, b):
    mu = jnp.mean(y, axis=-1, keepdims=True)
    d = y - mu
    var = jnp.mean(d * d, axis=-1, keepdims=True)
    return d * lax.rsqrt(var + LN_EPS) * g + b


def _out_ln_kernel(y_ref, w_ref, x_ref, g_ref, b_ref, o_ref, ob_ref):
    h = jnp.dot(y_ref[...], w_ref[...], preferred_element_type=F32)
    out = _layer_norm(DN_ALPHA * x_ref[...] + h, g_ref[...], b_ref[...])
    o_ref[...] = out
    ob_ref[...] = out.astype(BF16)


def _out_ln(y, w, x, g, b, tm=256):
    m, d = x.shape
    row = pl.BlockSpec((tm, d), lambda i: (i, 0))
    vec = pl.BlockSpec((1, d), lambda i: (0, 0))
    return pl.pallas_call(
        _out_ln_kernel,
        grid=(m // tm,),
        in_specs=[row, pl.BlockSpec((d, d), lambda i: (0, 0)), row, vec, vec],
        out_specs=[row, row],
        out_shape=[jax.ShapeDtypeStruct((m, d), F32), jax.ShapeDtypeStruct((m, d), BF16)],
        compiler_params=_params("arbitrary"),
        name="out_ln",
    )(y, w, x, g.reshape(1, d), b.reshape(1, d))


RW_CHUNK = 64
RW_PAIR = 2 * RW_HEAD_DIM
RW_SUB = 16
RW_CHUNKS_PER_STEP = 2
RW_PREP_ROWS = 256
(RV_MU, RV_W0, RV_A0, RV_KK, RV_KA, RV_GN_G, RV_GN_B, RV_RK, RV_ROWS) = (0, 6, 7, 8, 9, 10, 11, 12, 16)


def _rwkv_prep_kernel(r_ref, k_ref, v_ref, z_ref, rp_ref, kp_ref, vp_ref, zp_ref, vec_ref,
                      w1_ref, w2_ref, a1_ref, a2_ref, g1_ref, g2_ref,
                      ro_ref, lw_ref, ko_ref, vo_ref, kk_ref, b_ref, g_ref):
    first_block = pl.program_id(0) == 0
    vec = lambda j: vec_ref[j:j + 1, :]
    prev = lambda ref: jnp.where(first_block, 0.0, ref[SUBLANES - 1:SUBLANES, :])
    lerp = lambda u, up, j: u + (_shift_rows(u, up) - u) * vec(RV_MU + j)
    r = lerp(r_ref[...], prev(rp_ref), 0)
    k = lerp(k_ref[...], prev(kp_ref), 1)
    v = lerp(v_ref[...], prev(vp_ref), 2)
    z = z_ref[...]
    z_diff = _shift_rows(z, prev(zp_ref)) - z
    xw, xa, xg = z + z_diff * vec(RV_MU + 3), z + z_diff * vec(RV_MU + 4), z + z_diff * vec(RV_MU + 5)
    w_pre = vec(RV_W0) + _bdot(jnp.tanh(_bdot(xw, w1_ref[...])), w2_ref[...])
    softplus = jnp.maximum(-w_pre, 0.0) + jnp.log(1.0 + jnp.exp(-jnp.abs(w_pre)))
    lw_ref[...] = -jnp.exp(-softplus - 0.5)
    a = jax.nn.sigmoid(vec(RV_A0) + _bdot(_bdot(xa, a1_ref[...]), a2_ref[...]))
    g_ref[...] = _bdot(jax.nn.sigmoid(_bdot(xg, g1_ref[...])), g2_ref[...])
    kk = k * vec(RV_KK)
    first_head = lax.broadcasted_iota(jnp.int32, (kk.shape[0], RW_PAIR), 1) < RW_HEAD_DIM
    for p in range(RW_HEADS // 2):
        sl = slice(p * RW_PAIR, (p + 1) * RW_PAIR)
        kk_p = kk[:, sl]
        kk_p = kk_p / jnp.maximum(jnp.sqrt(_pair_sum(kk_p * kk_p, first_head)), 1e-12)
        kk_ref[:, sl] = kk_p
        b_ref[:, sl] = kk_p * a[:, sl]
    ro_ref[...] = r
    ko_ref[...] = k * (1.0 + (a - 1.0) * vec(RV_KA))
    vo_ref[...] = v


def _rwkv_prep(rw, vecs, l, p):
    s = rw.shape[0]
    tm = min(RW_PREP_ROWS, s)
    per = tm // SUBLANES
    cur = lambda j: pl.BlockSpec((tm, RW_WIDTH), lambda i: (i, j))
    prev = lambda j: pl.BlockSpec((SUBLANES, RW_WIDTH), lambda i: (jnp.maximum(i * per - 1, 0), j))
    lora = lambda w: pl.BlockSpec((None,) + w.shape[1:], lambda i: (l, 0, 0))
    ws = [p['rw_w1'], p['rw_w2'], p['rw_a1'], p['rw_a2'], p['rw_g1'], p['rw_g2']]
    out = pl.BlockSpec((tm, RW_WIDTH), lambda i: (i, 0))
    return pl.pallas_call(
        _rwkv_prep_kernel,
        grid=(s // tm,),
        in_specs=[cur(0), cur(1), cur(2), cur(3), prev(0), prev(1), prev(2), prev(3),
                  pl.BlockSpec((RV_ROWS, RW_WIDTH), lambda i: (0, 0))] + [lora(w) for w in ws],
        out_specs=[out] * 7,
        out_shape=[jax.ShapeDtypeStruct((s, RW_WIDTH), F32)] * 7,
        compiler_params=_params("arbitrary"),
        name="rwkv_prep",
    )(rw, rw, rw, rw, rw, rw, rw, rw, vecs, *ws)


def _rwkv_kernel(r_ref, lw_ref, k_ref, v_ref, kk_ref, b_ref, g_ref, vec_ref, o_ref, zt_ref):
    @pl.when(pl.program_id(0) == 0)
    def _():
        zt_ref[...] = jnp.zeros_like(zt_ref)

    c, n2 = RW_CHUNK, RW_PAIR
    tri = (lax.broadcasted_iota(jnp.int32, (c, c), 1) <= lax.broadcasted_iota(jnp.int32, (c, c), 0)).astype(BF16)
    row = lax.broadcasted_iota(jnp.int32, (n2, n2), 0)
    col = lax.broadcasted_iota(jnp.int32, (n2, n2), 1)
    t_idx, s_idx = row % c, col % c
    strict = t_idx > s_idx
    incl = t_idx >= s_idx
    diag_blk = (row // RW_SUB) == (col // RW_SUB)
    eye = (row == col).astype(F32)
    first_head = lax.broadcasted_iota(jnp.int32, (c, n2), 1) < RW_HEAD_DIM

    def embed(x):
        return jnp.concatenate([jnp.where(first_head, x, 0.0), jnp.where(first_head, 0.0, x)], axis=0)

    pairs = range(RW_HEADS // 2)
    sls = [slice(p * n2, (p + 1) * n2) for p in pairs]
    bdot = lambda xs, ys: _each(_bdot, xs, ys)

    lhs, rhs, v_e, kb, gam = [], [], [], [], []
    for ck in range(RW_CHUNKS_PER_STEP):
        rs = slice(ck * c, (ck + 1) * c)
        lw = lw_ref[rs, :]
        lw_hi = lw.astype(BF16)
        rem = lw - lw_hi.astype(F32)
        lw_mid = rem.astype(BF16)
        lw_lo = (rem - lw_mid.astype(F32)).astype(BF16)
        g_in = (jnp.dot(tri, lw_hi, preferred_element_type=F32) + jnp.dot(tri, lw_mid, preferred_element_type=F32)
                + jnp.dot(tri, lw_lo, preferred_element_type=F32))
        g_last = g_in[c - 1:c, :]
        e_neg = jnp.exp(-g_in)
        e_end = jnp.exp(g_last - g_in)
        kkd = kk_ref[rs, :] * jnp.exp(g_in - lw)
        rd = r_ref[rs, :] * jnp.exp(g_in)
        kinv = k_ref[rs, :] * e_neg
        binv = b_ref[rs, :] * e_neg
        kd = k_ref[rs, :] * e_end
        bd = b_ref[rs, :] * e_end
        gam.append(jnp.exp(g_last))
        v_e += [embed(v_ref[rs, sl]) for sl in sls]
        lhs += [jnp.concatenate([embed(kkd[:, sl]), embed(rd[:, sl])], axis=0).astype(BF16) for sl in sls]
        rhs += [jnp.concatenate([embed(kinv[:, sl]), embed(binv[:, sl])], axis=0).astype(BF16) for sl in sls]
        kb += [jnp.concatenate([embed(kd[:, sl]), -embed(bd[:, sl])], axis=0).astype(BF16) for sl in sls]
    v_b = [x.astype(BF16) for x in v_e]
    aa = _each(_bdot_nt, lhs, rhs)
    a_kkk = [jnp.where(strict, x[:n2, :n2], 0.0).astype(BF16) for x in aa]
    a_kkb = [jnp.where(strict, x[:n2, n2:], 0.0) for x in aa]
    a_rk = [jnp.where(incl, x[n2:, :n2], 0.0).astype(BF16) for x in aa]
    a_rb = [jnp.where(incl, x[n2:, n2:], 0.0).astype(BF16) for x in aa]
    nd = [jnp.where(diag_blk, x, 0.0) for x in a_kkb]
    off = [jnp.where(diag_blk, 0.0, x) for x in a_kkb]
    nd2 = bdot(nd, nd)
    akv = bdot(a_kkk, v_b)
    o_v = bdot(a_rk, v_b)
    nd4 = bdot(nd2, nd2)
    p1 = bdot([eye - x for x in nd], [eye + x for x in nd2])
    nd8 = bdot(nd4, nd4)
    p2 = bdot(p1, [eye + x for x in nd4])
    d_inv = bdot(p2, [eye + x for x in nd8])
    e1 = bdot(d_inv, off)
    e2 = bdot(e1, e1)
    t_inv = [x.astype(BF16) for x in bdot(bdot([eye - x for x in e1], [eye + x for x in e2]), d_inv)]

    inv_n = 1.0 / RW_HEAD_DIM
    zt = [zt_ref[p] for p in pairs]
    for ck in range(RW_CHUNKS_PER_STEP):
        rs = slice(ck * c, (ck + 1) * c)
        at = lambda xs: xs[ck * len(sls):(ck + 1) * len(sls)]
        zt_b = [x.astype(BF16) for x in zt]
        x1 = _each(lambda a, z: _bdot_nt(a[:n2], z), at(lhs), zt_b)
        o_z = _each(lambda a, z: _bdot_nt(a[n2:], z), at(lhs), zt_b)
        u = bdot(at(t_inv), [a + b for a, b in zip(x1, at(akv))])
        o_u = bdot(at(a_rb), u)
        vu_t = [jnp.transpose(jnp.concatenate([a, b], axis=0)) for a, b in zip(at(v_e), u)]
        z_up = bdot(vu_t, at(kb))
        zt = [z * gam[ck][:, sl] + zu for z, sl, zu in zip(zt, sls, z_up)]
        for p in pairs:
            sl = sls[p]
            o_e = o_z[p] + at(o_v)[p] - o_u[p]
            out = o_e[:c] + o_e[c:]
            mu = _pair_sum(out, first_head) * inv_n
            dev = out - mu
            var = _pair_sum(dev * dev, first_head) * inv_n
            normed = (dev * lax.rsqrt(var + RW_GN_EPS) * vec_ref[RV_GN_G:RV_GN_G + 1, sl]
                      + vec_ref[RV_GN_B:RV_GN_B + 1, sl])
            bonus = _pair_sum(r_ref[rs, sl] * k_ref[rs, sl] * vec_ref[RV_RK:RV_RK + 1, sl], first_head) * v_ref[rs, sl]
            o_ref[rs, sl] = ((normed + bonus) * g_ref[rs, sl]).astype(o_ref.dtype)
    for p in pairs:
        zt_ref[p] = zt[p]


def _rwkv_scan(r, lw, k, v, kk, b, g, vecs):
    s, width = r.shape
    rows = RW_CHUNK * RW_CHUNKS_PER_STEP
    blk = pl.BlockSpec((rows, width), lambda i: (i, 0))
    return pl.pallas_call(
        _rwkv_kernel,
        grid=(s // rows,),
        in_specs=[blk] * 7 + [pl.BlockSpec((RV_ROWS, width), lambda i: (0, 0))],
        out_specs=blk,
        out_shape=jax.ShapeDtypeStruct((s, width), BF16),
        scratch_shapes=[pltpu.VMEM((RW_HEADS // 2, RW_PAIR, RW_PAIR), F32)],
        compiler_params=_params("arbitrary"),
        name="rwkv_scan",
    )(r, lw, k, v, kk, b, g, vecs)


def _rwkv7(rw, l, p):
    rows = [p['rw_mu'][l], p['rw_w0'][l][None], p['rw_a0'][l][None], p['rw_kk'][l][None], p['rw_ka'][l][None],
            p['rw_gn_g'][l][None], p['rw_gn_b'][l][None], p['rw_rk'][l].reshape(1, RW_WIDTH)]
    vecs = jnp.concatenate(rows + [jnp.zeros((RV_ROWS - RV_RK - 1, RW_WIDTH), F32)], axis=0)
    r, lw, k, v, kk, b, g = _rwkv_prep(rw, vecs, l, p)
    return _rwkv_scan(r, lw, k, v, kk, b, g, vecs)


AT_PREP_ROWS = 512
AT_SLABS = AT_WIDTH // LANES


def _rope_kernel(q_ref, k_ref, c_ref, s1_ref, s2_ref, qo_ref, ko_ref):
    cos, s_dn, s_up = c_ref[...], s1_ref[...], s2_ref[...]

    def rot(x, scale):
        outs = []
        for j in range(AT_SLABS):
            xs = x[:, j * LANES:(j + 1) * LANES]
            y = xs * cos + pltpu.roll(xs, LANES - ROPE_DIMS // 2, axis=1) * s_dn + pltpu.roll(xs, ROPE_DIMS // 2, axis=1) * s_up
            outs.append(y * scale if scale != 1.0 else y)
        return jnp.concatenate(outs, axis=-1)

    qo_ref[...] = rot(q_ref[...], AT_HEAD_DIM ** -0.5)
    ko_ref[...] = rot(k_ref[...], 1.0)


def _rope_tables(positions):
    s = positions.shape[0]
    half = ROPE_DIMS // 2
    inv_freq = ROPE_THETA ** (-jnp.arange(half, dtype=F32) * 2.0 / ROPE_DIMS)
    ang = positions.astype(F32)[:, None] * inv_freq
    cos, sin = jnp.cos(ang), jnp.sin(ang)
    rest = AT_HEAD_DIM - ROPE_DIMS
    zeros, ones = jnp.zeros((s, rest), F32), jnp.ones((s, rest), F32)
    z_half = jnp.zeros((s, half), F32)
    head = lambda parts: jnp.tile(jnp.concatenate(parts, axis=1), (1, LANES // AT_HEAD_DIM))
    return head([cos, cos, ones]), head([-sin, z_half, zeros]), head([z_half, sin, zeros])


def _rope(at, tables):
    s = at.shape[0]
    tm = min(AT_PREP_ROWS, s)
    col = lambda j: pl.BlockSpec((tm, AT_WIDTH), lambda i: (i, j))
    tab = pl.BlockSpec((tm, LANES), lambda i: (i, 0))
    return pl.pallas_call(
        _rope_kernel,
        grid=(s // tm,),
        in_specs=[col(0), col(1), tab, tab, tab],
        out_specs=[col(0)] * 2,
        out_shape=[jax.ShapeDtypeStruct((s, AT_WIDTH), F32)] * 2,
        compiler_params=_params("arbitrary"),
        name="attn_rope",
    )(at, at, *tables)


AT_GROUP_SLABS = AT_OUT_WIDTH // LANES


def _attn_kernel(*refs, dilation):
    ns = AT_GROUP_SLABS
    q_refs, kp_refs, kc_refs, vp_refs, vc_refs = (refs[i * ns:(i + 1) * ns] for i in range(5))
    o_ref, lse_ref, o_s, lse_s = refs[5 * ns:]
    n = pl.program_id(0)
    qi = lax.broadcasted_iota(jnp.int32, (AT_BLOCK, AT_BLOCK), 0)
    kj = lax.broadcasted_iota(jnp.int32, (AT_BLOCK, AT_BLOCK), 1)
    mask_c = kj <= qi
    mask_p = (kj >= qi) & (n > 0)
    per_slab = LANES // AT_HEAD_DIM
    ones_b = jnp.ones((AT_BLOCK, AT_HEAD_DIM), BF16)

    def residue(r, carry):
        rows = pl.ds(r, AT_BLOCK, stride=dilation) if dilation > 1 else slice(None)

        def heads_of(slab_refs):
            slabs = [ref[rows, :].astype(BF16) for ref in slab_refs]
            return [x[:, j * AT_HEAD_DIM:(j + 1) * AT_HEAD_DIM] for x in slabs for j in range(per_slab)]

        q, kc, kp, vc, vp = (heads_of(x) for x in (q_refs, kc_refs, kp_refs, vc_refs, vp_refs))
        s_c = [jnp.where(mask_c, lax.dot_general(a, b, NT_DIMS, preferred_element_type=F32), -jnp.inf)
               for a, b in zip(q, kc)]
        s_p = [jnp.where(mask_p, lax.dot_general(a, b, NT_DIMS, preferred_element_type=F32), -jnp.inf)
               for a, b in zip(q, kp)]
        m = [jnp.maximum(jnp.max(a, axis=-1, keepdims=True), jnp.max(b, axis=-1, keepdims=True))
             for a, b in zip(s_c, s_p)]
        p_c = [jnp.exp(a - mm).astype(BF16) for a, mm in zip(s_c, m)]
        p_p = [jnp.exp(a - mm).astype(BF16) for a, mm in zip(s_p, m)]
        aug = lambda v: jnp.concatenate([v, ones_b], axis=1)
        acc = [jnp.dot(a, aug(v), preferred_element_type=F32) + jnp.dot(b, aug(w), preferred_element_type=F32)
               for a, v, b, w in zip(p_c, vc, p_p, vp)]
        out = [x[:, :AT_HEAD_DIM] / x[:, AT_HEAD_DIM:] for x in acc]
        lse = [mm + jnp.log(x[:, AT_HEAD_DIM:]) for mm, x in zip(m, acc)]
        for t in range(ns):
            o_s[t, rows, :] = jnp.concatenate(out[t * per_slab:(t + 1) * per_slab], axis=-1)
            lse_s[t, rows, :] = jnp.concatenate(lse[t * per_slab:(t + 1) * per_slab], axis=-1)
        return carry

    if dilation > 1:
        lax.fori_loop(0, dilation, residue, 0)
    else:
        residue(0, 0)
    for t in range(ns):
        o_ref[:, t * LANES:(t + 1) * LANES] = o_s[t]
        lse_ref[:, t * LANES:(t + 1) * LANES] = lse_s[t]


def _dilated_attention(q, k, at, g, dilation):
    s = q.shape[0]
    rows = AT_BLOCK * dilation
    assert s % rows == 0
    ns = AT_GROUP_SLABS
    qk_col = g * ns
    v_col = 2 * (AT_WIDTH // LANES) + g * ns
    cur = lambda c: [pl.BlockSpec((rows, LANES), lambda n, t=t: (n, c + t)) for t in range(ns)]
    prev = lambda c: [pl.BlockSpec((rows, LANES), lambda n, t=t: (jnp.maximum(n - 1, 0), c + t)) for t in range(ns)]
    out = pl.BlockSpec((rows, AT_OUT_WIDTH), lambda n: (n, 0))
    return pl.pallas_call(
        functools.partial(_attn_kernel, dilation=dilation),
        grid=(s // rows,),
        in_specs=cur(qk_col) + prev(qk_col) + cur(qk_col) + prev(v_col) + cur(v_col),
        out_specs=[out, out],
        out_shape=[jax.ShapeDtypeStruct((s, AT_OUT_WIDTH), F32)] * 2,
        scratch_shapes=[pltpu.VMEM((ns, rows, LANES), F32)] * 2,
        compiler_params=_params("arbitrary"),
        name=f"attn_d{dilation}",
    )(*([q] * ns + [k] * (2 * ns) + [at] * (2 * ns)))


def _attn_merge_kernel(o0, o1, o2, l0, l1, l2, y_ref):
    m = jnp.maximum(jnp.maximum(l0[...], l1[...]), l2[...])
    e0, e1, e2 = jnp.exp(l0[...] - m), jnp.exp(l1[...] - m), jnp.exp(l2[...] - m)
    tot = e0 + e1 + e2
    y_ref[...] = ((e0 / tot) * o0[...] + (e1 / tot) * o1[...] + (e2 / tot) * o2[...]).astype(BF16)


def _attention(at, tables):
    s = at.shape[0]
    q, k = _rope(at, tables)
    outs, lses = [], []
    for g, (_, dilation) in enumerate(AT_GROUPS):
        o, lse = _dilated_attention(q, k, at, g, dilation)
        outs.append(o)
        lses.append(lse)
    tm = min(AT_PREP_ROWS, s)
    blk = pl.BlockSpec((tm, AT_OUT_WIDTH), lambda i: (i, 0))
    return pl.pallas_call(
        _attn_merge_kernel,
        grid=(s // tm,),
        in_specs=[blk] * 6,
        out_specs=blk,
        out_shape=jax.ShapeDtypeStruct((s, AT_OUT_WIDTH), BF16),
        compiler_params=_params("arbitrary"),
        name="attn_merge",
    )(*outs, *lses)


ML_CHUNKS_PER_STEP = 8
ML_ROWS = ML_CHUNKS_PER_STEP * ML_CHUNK
ML_GROUP = 4


def _mlstm_kernel(u_ref, v_ref, og_ref, cw_ref, bb_ref, ib_ref, brow_ref, irow_ref, g_ref, y_ref,
                  qk_ref, ubuf_ref, ct_ref, m_ref):
    @pl.when(pl.program_id(0) == 0)
    def _():
        ubuf_ref[:SUBLANES, :] = jnp.zeros((SUBLANES, ubuf_ref.shape[1]), F32)
        ct_ref[...] = jnp.zeros_like(ct_ref)
        m_ref[...] = jnp.full_like(m_ref, -jnp.inf)

    taps = cw_ref.shape[0]
    ubuf_ref[SUBLANES:, :] = u_ref[...]
    conv = ubuf_ref[SUBLANES:, :] * cw_ref[taps - 1:taps, :]
    for back in range(1, taps):
        conv = conv + ubuf_ref[pl.ds(SUBLANES - back, ML_ROWS), :] * cw_ref[taps - 1 - back:taps - back, :]
    qk_ref[...] = conv * jax.nn.sigmoid(conv)
    ubuf_ref[:SUBLANES, :] = ubuf_ref[ML_ROWS:, :]

    dh, lc = ML_HEAD_DIM, ML_CHUNK
    si = lax.broadcasted_iota(jnp.int32, (lc, lc), 0)
    ji = lax.broadcasted_iota(jnp.int32, (lc, lc), 1)
    causal = ji <= si
    ones_b = jnp.ones((lc, dh), BF16)

    def group(c, rows, heads):
        cols = [slice(h * dh, (h + 1) * dh) for h in heads]
        q = [qk_ref[rows, cl] * (dh ** -0.5) for cl in cols]
        k = [qk_ref[rows, pl.ds(ML_WIDTH + h * dh, dh)] for h in heads]
        v = [v_ref[rows, cl] for cl in cols]
        bb = [bb_ref[rows, cl] for cl in cols]
        ib = [ib_ref[rows, cl] for cl in cols]
        b_row = [brow_ref[c, h:h + 1, :] for h in heads]
        i_row = [irow_ref[c, h:h + 1, :] for h in heads]
        m_prev = [m_ref[h, 0:1, :] for h in heads]
        ct_prev = [ct_ref[h] for h in heads]
        qb = [x.astype(BF16) for x in q]
        kb = [x.astype(BF16) for x in k]
        s_qk = _each(_bdot_nt, qb, kb)
        inter = _each(_bdot, qb, ct_prev)
        k_t = [jnp.transpose(x).astype(BF16) for x in k]
        log_d = [jnp.where(causal, b[:, :lc] - br + ir, -jnp.inf) for b, br, ir in zip(bb, b_row, i_row)]
        a_log = [b + mp for b, mp in zip(bb, m_prev)]
        m_s = [jnp.maximum(al, jnp.max(ld, axis=-1, keepdims=True)) for al, ld in zip(a_log, log_d)]
        inter_w = [jnp.exp(al - ms) for al, ms in zip(a_log, m_s)]
        qk = [s * jnp.exp(ld - ms[:, :lc]) for s, ld, ms in zip(s_qk, log_d, m_s)]
        intra = [_bdot(a, jnp.concatenate([x.astype(BF16), ones_b], axis=1)) for a, x in zip(qk, v)]
        b_last = [b[lc - 1:lc, :] for b in bb]
        w_end = [bl - b + i for bl, b, i in zip(b_last, bb, ib)]
        m_new = [jnp.maximum(bl + mp, jnp.max(we, axis=0, keepdims=True)) for bl, mp, we in zip(b_last, m_prev, w_end)]
        dec = [jnp.exp(bl + mp - mn) for bl, mp, mn in zip(b_last, m_prev, m_new)]
        wts = [jnp.exp(we - mn) for we, mn in zip(w_end, m_new)]
        upd = [_bdot(kt, jnp.concatenate([x * w, w], axis=1)) for kt, x, w in zip(k_t, v, wts)]
        for h, ctp, d, up, mn in zip(heads, ct_prev, dec, upd, m_new):
            ct_ref[h] = jnp.concatenate([d, d], axis=1) * ctp + up
            m_ref[h, 0:1, :] = mn
        num = [w * a[:, :dh] + b[:, :dh] for w, a, b in zip(inter_w, inter, intra)]
        den = [w * a[:, dh:] + b[:, dh:] for w, a, b in zip(inter_w, inter, intra)]
        hid = [n / jnp.maximum(jnp.abs(d), jnp.exp(-ms)) for n, d, ms in zip(num, den, m_s)]
        hid = [x * jax.nn.sigmoid(og_ref[rows, cl]) for x, cl in zip(hid, cols)]
        mu = [jnp.mean(x, axis=-1, keepdims=True) for x in hid]
        dev = [x - m for x, m in zip(hid, mu)]
        var = [jnp.mean(x * x, axis=-1, keepdims=True) for x in dev]
        for cl, x, vr in zip(cols, dev, var):
            y_ref[rows, cl] = (x * lax.rsqrt(vr + ML_NORM_EPS) * g_ref[:, cl]).astype(y_ref.dtype)

    def chunk(c, carry):
        rows = pl.ds(pl.multiple_of(c * lc, lc), lc)
        for h0 in range(0, ML_HEADS, ML_GROUP):
            group(c, rows, range(h0, h0 + ML_GROUP))
        return carry

    lax.fori_loop(0, ML_CHUNKS_PER_STEP, chunk, 0)


def _mlstm(ml, gates_if, l, p):
    s = ml.shape[0]
    nc = s // ML_CHUNK
    i_pre = gates_if[:, :ML_HEADS] + p['ml_gate_b'][l, 0]
    f_pre = gates_if[:, ML_HEADS:] + p['ml_gate_b'][l, 1]
    lf = jax.nn.log_sigmoid(f_pre)
    b_cum = jnp.cumsum(lf.reshape(nc, ML_CHUNK, ML_HEADS), axis=1)
    b_row = jnp.transpose(b_cum, (0, 2, 1))
    i_row = jnp.transpose(i_pre.reshape(nc, ML_CHUNK, ML_HEADS), (0, 2, 1))
    over_lanes = lambda a: jnp.repeat(a.reshape(s, ML_HEADS), ML_HEAD_DIM, axis=1)
    col = lambda j: pl.BlockSpec((ML_ROWS, ML_WIDTH), lambda i: (i, j))
    grow = pl.BlockSpec((ML_CHUNKS_PER_STEP, ML_HEADS, ML_CHUNK), lambda i: (i, 0, 0))
    conv_w = p['ml_conv'][l]
    return pl.pallas_call(
        _mlstm_kernel,
        grid=(s // ML_ROWS,),
        in_specs=[pl.BlockSpec((ML_ROWS, 2 * ML_WIDTH), lambda i: (i, 0)), col(2), col(3),
                  pl.BlockSpec(conv_w.shape, lambda i: (0, 0)), col(0), col(0), grow, grow,
                  pl.BlockSpec((1, ML_WIDTH), lambda i: (0, 0))],
        out_specs=pl.BlockSpec((ML_ROWS, ML_WIDTH), lambda i: (i, 0)),
        out_shape=jax.ShapeDtypeStruct((s, ML_WIDTH), BF16),
        scratch_shapes=[pltpu.VMEM((ML_ROWS, 2 * ML_WIDTH), F32),
                        pltpu.VMEM((ML_ROWS + SUBLANES, 2 * ML_WIDTH), F32),
                        pltpu.VMEM((ML_HEADS, ML_HEAD_DIM, 2 * ML_HEAD_DIM), F32),
                        pltpu.VMEM((ML_HEADS, SUBLANES, ML_HEAD_DIM), F32)],
        compiler_params=_params("arbitrary"),
        name="mlstm",
    )(ml, ml, ml, conv_w, over_lanes(b_cum), over_lanes(i_pre), b_row, i_row,
      p['ml_ln_g'][l].reshape(1, ML_WIDTH))


def _merge_kernel(x_ref, yr_ref, ya_ref, ym_ref, g0_ref, g1_ref, g2_ref, w0_ref, w1_ref, w2_ref, o_ref, gb_ref):
    @pl.when(pl.program_id(1) == 0)
    def _():
        for b, g_ref in enumerate((g0_ref, g1_ref, g2_ref)):
            gb_ref[b] = jnp.transpose(g_ref[0]).astype(BF16)

    x = x_ref[...]
    gate = lambda b: jax.nn.sigmoid(jnp.dot(x, gb_ref[b], preferred_element_type=F32))
    y = (gate(0) * jnp.dot(yr_ref[...], w0_ref[...], preferred_element_type=F32)
         + gate(1) * jnp.dot(ya_ref[...], w1_ref[...], preferred_element_type=F32)
         + gate(2) * jnp.dot(ym_ref[...], w2_ref[...], preferred_element_type=F32))
    o_ref[...] = y.astype(o_ref.dtype)


def _merge(xb, y_rw, y_at, y_ml, w_in_t, l, wb_rw, wb_at, wb_ml, tm=512, tn=512):
    m, d = xb.shape
    nj = d // tn
    act = lambda a: pl.BlockSpec((tm, a.shape[1]), lambda j, i: (i, 0))
    gate = lambda b: pl.BlockSpec((pl.Element(1), pl.Element(tn), pl.Element(d)),
                                  lambda j, i: (l, pl.multiple_of(COL_GATE + b * d + j * tn, SUBLANES), 0))
    wsp = lambda w: pl.BlockSpec((w.shape[0], tn), lambda j, i: (0, j))
    return pl.pallas_call(
        _merge_kernel,
        grid=(nj, m // tm),
        in_specs=[act(xb), act(y_rw), act(y_at), act(y_ml), gate(0), gate(1), gate(2),
                  wsp(wb_rw), wsp(wb_at), wsp(wb_ml)],
        out_specs=pl.BlockSpec((tm, tn), lambda j, i: (i, j)),
        out_shape=jax.ShapeDtypeStruct((m, d), BF16),
        scratch_shapes=[pltpu.VMEM((N_BRANCHES, d, tn), BF16)],
        compiler_params=_params("arbitrary", "arbitrary"),
        name="merge",
    )(xb, y_rw, y_at, y_ml, w_in_t, w_in_t, w_in_t, wb_rw, wb_at, wb_ml)


MOE_LN_ROWS = 256


def _row_gather(idx_ref, base, n, src_hbm, dst, sem, unroll=False):
    def start(j):
        pltpu.make_async_copy(src_hbm.at[pl.ds(idx_ref[base + j], 1)], dst.at[pl.ds(j, 1)], sem).start()

    if unroll:
        for j in range(n):
            start(j)
    else:
        lax.fori_loop(0, n, lambda j, carry: (start(j), carry)[1], 0, unroll=8)


def _row_gather_wait(n, src_hbm, dst, sem):
    pltpu.make_async_copy(src_hbm.at[pl.ds(0, n)], dst.at[pl.ds(0, n)], sem).wait()


def _expert_kernel(new_ref, used_ref, wslot_ref, wnext_ref, src_ref, x_hbm, w1_hbm, w3_hbm, w2_hbm, o_ref,
                   xbuf, w1f, w3f, w2f, w1b, w3b, w2b, sem, wsem, *, layer):
    i = pl.program_id(0)
    slot = i % 2

    def weight_copies(e, ws):
        return [pltpu.make_async_copy(w_hbm.at[layer, e], buf.at[ws], wsem.at[ws])
                for w_hbm, buf in ((w1_hbm, w1f), (w3_hbm, w3f), (w2_hbm, w2f))]

    @pl.when(i == 0)
    def _():
        _row_gather(src_ref, 0, MOE_ROWS, x_hbm, xbuf.at[0], sem.at[0])
        for cp in weight_copies(wnext_ref[0], 0):
            cp.start()

    @pl.when(new_ref[i] == 1)
    def _():
        ws = wslot_ref[i]
        for cp in weight_copies(0, ws):
            cp.wait()

        @pl.when(wnext_ref[i + 1] >= 0)
        def _():
            for cp in weight_copies(wnext_ref[i + 1], 1 - ws):
                cp.start()

        w1b[...] = w1f[ws].astype(BF16)
        w3b[...] = w3f[ws].astype(BF16)
        w2b[...] = w2f[ws].astype(BF16)

    @pl.when(used_ref[i] == 1)
    def _():
        _row_gather_wait(MOE_ROWS, x_hbm, xbuf.at[slot], sem.at[slot])
        _row_gather(src_ref, (i + 1) * MOE_ROWS, MOE_ROWS, x_hbm, xbuf.at[1 - slot], sem.at[1 - slot], unroll=True)
        x = xbuf[slot].astype(BF16)
        h1 = jnp.dot(x, w1b[...], preferred_element_type=F32)
        h3 = jnp.dot(x, w3b[...], preferred_element_type=F32)
        hid = (h1 * jax.nn.sigmoid(h1) * h3).astype(BF16)
        o_ref[...] = jnp.dot(hid, w2b[...], preferred_element_type=F32)

    @pl.when(used_ref[i] == 0)
    def _():
        @pl.when(used_ref[jnp.maximum(i - 1, 0)] == 1)
        def _():
            _row_gather_wait(MOE_ROWS, x_hbm, xbuf.at[slot], sem.at[slot])

        o_ref[...] = jnp.zeros_like(o_ref)


def _experts(x, src, blk_expert, blk_used, w1, w3, w2, l):
    rows = src.shape[0]
    d = x.shape[1]
    nblk = rows // MOE_ROWS
    changed = jnp.concatenate([jnp.ones((1,), jnp.int32), (blk_expert[1:] != blk_expert[:-1]).astype(jnp.int32)])
    blk_new = changed * blk_used
    wslot = (jnp.cumsum(blk_new) - 1) % 2
    idx = jnp.arange(nblk, dtype=jnp.int32)
    first_after = lax.cummin(jnp.where(blk_new == 1, idx, nblk), reverse=True)
    nxt = jnp.concatenate([first_after[1:], jnp.full((1,), nblk, jnp.int32)])
    wnext = jnp.where(nxt < nblk, blk_expert[jnp.minimum(nxt, nblk - 1)], -1)
    wnext = jnp.concatenate([blk_expert[:1], wnext]).astype(jnp.int32)
    hbm = pl.BlockSpec(memory_space=pl.ANY)
    return pl.pallas_call(
        functools.partial(_expert_kernel, layer=l),
        grid_spec=pltpu.PrefetchScalarGridSpec(
            num_scalar_prefetch=5,
            grid=(nblk,),
            in_specs=[hbm, hbm, hbm, hbm],
            out_specs=pl.BlockSpec((MOE_ROWS, d), lambda i, *_: (i, 0)),
            scratch_shapes=[pltpu.VMEM((2, MOE_ROWS, d), F32),
                            pltpu.VMEM((2, d, MOE_FF), F32), pltpu.VMEM((2, d, MOE_FF), F32),
                            pltpu.VMEM((2, MOE_FF, d), F32),
                            pltpu.VMEM((d, MOE_FF), BF16), pltpu.VMEM((d, MOE_FF), BF16),
                            pltpu.VMEM((MOE_FF, d), BF16),
                            pltpu.SemaphoreType.DMA((2,)), pltpu.SemaphoreType.DMA((2,))]),
        out_shape=jax.ShapeDtypeStruct((rows, d), F32),
        compiler_params=_params("arbitrary"),
        name="experts",
    )(blk_new, blk_used, wslot.astype(jnp.int32), wnext, src, x, w1, w3, w2)


def _moe_ln_kernel(pos_ref, x_ref, w_ref, y_hbm, g_ref, b_ref, o_ref, ob_ref, ybuf, sem):
    i = pl.program_id(0)
    nblk = pl.num_programs(0)
    n = MOE_TOP_K * MOE_LN_ROWS
    tm = MOE_LN_ROWS
    slot = i % 2

    @pl.when(i == 0)
    def _():
        _row_gather(pos_ref, 0, n, y_hbm, ybuf.at[0], sem.at[0])

    _row_gather_wait(n, y_hbm, ybuf.at[slot], sem.at[slot])
    _row_gather(pos_ref, (i + 1) * n, n, y_hbm, ybuf.at[1 - slot], sem.at[1 - slot], unroll=True)
    rows = ybuf.at[slot]
    h = rows[0:tm, :] * w_ref[:, 0:1] + rows[tm:2 * tm, :] * w_ref[:, 1:2]
    out = _layer_norm(DN_ALPHA * x_ref[...] + h, g_ref[...], b_ref[...])
    o_ref[...] = out
    ob_ref[...] = out.astype(BF16)

    @pl.when(i == nblk - 1)
    def _():
        _row_gather_wait(n, y_hbm, ybuf.at[1 - slot], sem.at[1 - slot])


def _moe_ln(x, y, pos, top_w, g, b):
    t, d = x.shape
    tm = MOE_LN_ROWS
    pos = jnp.transpose(pos.reshape(t // tm, tm, MOE_TOP_K), (0, 2, 1)).reshape(t * MOE_TOP_K)
    pos = jnp.concatenate([pos, jnp.zeros((tm * MOE_TOP_K,), jnp.int32)])
    row = pl.BlockSpec((tm, d), lambda i, pos: (i, 0))
    vec = pl.BlockSpec((1, d), lambda i, pos: (0, 0))
    return pl.pallas_call(
        _moe_ln_kernel,
        grid_spec=pltpu.PrefetchScalarGridSpec(
            num_scalar_prefetch=1,
            grid=(t // tm,),
            in_specs=[row, pl.BlockSpec((tm, MOE_TOP_K), lambda i, pos: (i, 0)),
                      pl.BlockSpec(memory_space=pl.ANY), vec, vec],
            out_specs=[row, row],
            scratch_shapes=[pltpu.VMEM((2, MOE_TOP_K * tm, d), F32),
                            pltpu.SemaphoreType.DMA((2,))]),
        out_shape=[jax.ShapeDtypeStruct((t, d), F32), jax.ShapeDtypeStruct((t, d), BF16)],
        compiler_params=_params("arbitrary"),
        name="moe_ln",
    )(pos, x, top_w, y, g.reshape(1, d), b.reshape(1, d))


def _moe(x, l, p, ln_g, ln_b):
    t, d = x.shape
    n_assign = t * MOE_TOP_K
    w_router = jnp.concatenate([p['moe_w_group'][l], p['moe_w_expert'][l]], axis=1)
    w_router = jnp.pad(w_router, ((0, 0), (0, LANES - w_router.shape[1])))
    logits = _mm_split(x, w_router, name="mm_router")
    g_logits = logits[:, :MOE_GROUPS] + p['moe_b_group'][l]
    g_prob = jax.nn.softmax(g_logits, -1)
    g_idx = jnp.argmax(g_logits, -1)
    g_w = jnp.take_along_axis(g_prob, g_idx[:, None], axis=1)
    e_logits = (logits[:, MOE_GROUPS:MOE_GROUPS + MOE_EXPERTS] + p['moe_b_expert'][l]).reshape(
        t, MOE_GROUPS, MOE_EXPERTS_PER_GROUP)
    e_sel = jnp.take_along_axis(e_logits, g_idx[:, None, None], axis=1)[:, 0]
    top_l, top_i = lax.top_k(e_sel, MOE_TOP_K)
    top_w = jax.nn.softmax(top_l, -1) * g_w
    e_flat = (g_idx[:, None] * MOE_EXPERTS_PER_GROUP + top_i).reshape(n_assign).astype(jnp.int32)
    onehot = (e_flat[:, None] == jnp.arange(MOE_EXPERTS, dtype=jnp.int32)[None]).astype(jnp.int32)
    rank = jnp.take_along_axis(jnp.cumsum(onehot, axis=0), e_flat[:, None], axis=1)[:, 0] - 1
    counts = jnp.sum(onehot, axis=0)
    pcounts = (counts + MOE_ROWS - 1) // MOE_ROWS * MOE_ROWS
    pends = jnp.cumsum(pcounts)
    pstarts = pends - pcounts
    pos = pstarts[e_flat] + rank
    n_rows = n_assign + (MOE_EXPERTS + 1) * MOE_ROWS
    nblk = n_rows // MOE_ROWS
    src = jnp.zeros((n_rows,), jnp.int32).at[pos].set(jnp.arange(n_assign, dtype=jnp.int32) // MOE_TOP_K)
    blk_start = jnp.arange(nblk, dtype=jnp.int32) * MOE_ROWS
    blk_expert = jnp.clip(jnp.searchsorted(pends, blk_start, side='right'), 0, MOE_EXPERTS - 1).astype(jnp.int32)
    blk_used = (blk_start < pends[-1]).astype(jnp.int32)
    y = _experts(x, src, blk_expert, blk_used, p['moe_w1'], p['moe_w3'], p['moe_w2'], l)
    return _moe_ln(x, y, pos.astype(jnp.int32).reshape(t, MOE_TOP_K), top_w, ln_g, ln_b)


def _mixer(xf, xb, tables, l, p, ln_g, ln_b):
    w_in_t = p['w_in_t']
    bf = lambda w: w.astype(BF16)
    rw = _mm_t(xb, w_in_t, layer=l, row0=COL_RW, n_rows=COL_AT - COL_RW, tn=1024, name="mm_in_rw")
    at = _mm_t(xb, w_in_t, layer=l, row0=COL_AT, n_rows=COL_ML - COL_AT, tn=AT_WIDTH, name="mm_in_at")
    ml = _mm_t(xb, w_in_t, layer=l, row0=COL_ML, n_rows=COL_IF - COL_ML, tn=1024, name="mm_in_ml")
    gates_if = _mm_t(xb, w_in_t, layer=l, row0=COL_IF, n_rows=COL_GATE - COL_IF, name="mm_in_if")
    y_rw = _rwkv7(rw, l, p)
    y_at = _attention(at, tables)
    y_ml = _mlstm(ml, gates_if, l, p)
    wb = p['w_branch'][l]
    y = _merge(xb, y_rw, y_at, y_ml, w_in_t, l, bf(wb[:RW_WIDTH]),
               bf(wb[RW_WIDTH:RW_WIDTH + AT_OUT_WIDTH]), bf(wb[RW_WIDTH + AT_OUT_WIDTH:]))
    return _out_ln(y, bf(p['w_out'][l]), xf, ln_g, ln_b)


def kernel(x, positions, w_in, rw_mu, rw_w0, rw_w1, rw_w2, rw_a0, rw_a1, rw_a2, rw_g1, rw_g2, rw_kk, rw_ka, rw_rk, rw_gn_g, rw_gn_b, ml_conv, ml_gate_b, ml_ln_g, w_branch, w_out, ln1_g, ln1_b, moe_w_group, moe_b_group, moe_w_expert, moe_b_expert, moe_w1, moe_w3, moe_w2, ln2_g, ln2_b):
    p = dict(w_in_t=jnp.swapaxes(w_in, 1, 2), rw_mu=rw_mu, rw_w0=rw_w0, rw_w1=rw_w1, rw_w2=rw_w2, rw_a0=rw_a0, rw_a1=rw_a1,
             rw_a2=rw_a2, rw_g1=rw_g1, rw_g2=rw_g2, rw_kk=rw_kk, rw_ka=rw_ka, rw_rk=rw_rk, rw_gn_g=rw_gn_g,
             rw_gn_b=rw_gn_b, ml_conv=ml_conv, ml_gate_b=ml_gate_b, ml_ln_g=ml_ln_g, w_branch=w_branch,
             w_out=w_out, moe_w_group=moe_w_group, moe_b_group=moe_b_group, moe_w_expert=moe_w_expert,
             moe_b_expert=moe_b_expert, moe_w1=moe_w1, moe_w3=moe_w3, moe_w2=moe_w2)
    batch, seq, d = x.shape
    assert batch == 1
    xf = x.reshape(seq, d)
    xb = xf.astype(BF16)
    tables = _rope_tables(positions.reshape(seq))
    for l in range(w_in.shape[0]):
        xf, xb = _mixer(xf, xb, tables, l, p, ln1_g[l], ln1_b[l])
        xf, xb = _moe(xf, l, p, ln2_g[l], ln2_b[l])
    return xf.reshape(batch, seq, d)
```

```python
import functools

import jax
import jax.numpy as jnp
from jax import lax
from jax.experimental import pallas as pl
from jax.experimental.pallas import tpu as pltpu

F32 = jnp.float32
BF16 = jnp.bfloat16

D_MODEL = 2048
DEPTH = 4
LN_EPS = 1e-5
DN_ALPHA = (2 * DEPTH) ** 0.25

RW_HEADS = 16
RW_HEAD_DIM = 64
RW_WIDTH = RW_HEADS * RW_HEAD_DIM
RW_GN_EPS = 64e-5

AT_GROUPS = ((128, 1), (512, 4), (2048, 16))
AT_HEADS_PER_GROUP = 6
AT_HEADS = AT_HEADS_PER_GROUP * len(AT_GROUPS)
AT_HEAD_DIM = 64
AT_WIDTH = AT_HEADS * AT_HEAD_DIM
AT_OUT_WIDTH = AT_HEADS_PER_GROUP * AT_HEAD_DIM
AT_BLOCK = 128
ROPE_DIMS = AT_HEAD_DIM // 4
ROPE_THETA = 500000.0

ML_HEADS = 8
ML_HEAD_DIM = 128
ML_WIDTH = ML_HEADS * ML_HEAD_DIM
ML_CHUNK = 64
ML_NORM_EPS = 1e-6

N_BRANCHES = 3
COL_RW = 0
COL_AT = 4 * RW_WIDTH
COL_ML = COL_AT + 3 * AT_WIDTH
COL_IF = COL_ML + 4 * ML_WIDTH
COL_GATE = COL_IF + 2 * ML_HEADS
IN_COLS = COL_GATE + N_BRANCHES * D_MODEL

MOE_GROUPS = 4
MOE_EXPERTS_PER_GROUP = 8
MOE_EXPERTS = MOE_GROUPS * MOE_EXPERTS_PER_GROUP
MOE_TOP_K = 2
MOE_FF = 512
MOE_ROWS = 256

LANES = 128
SUBLANES = 8
VMEM_LIMIT = 56 * 1024 * 1024

NT_DIMS = (((1,), (1,)), ((), ()))


def _params(*sem):
    return pltpu.CompilerParams(dimension_semantics=sem, vmem_limit_bytes=VMEM_LIMIT)


def _bdot(a, b):
    return jnp.dot(a.astype(BF16), b.astype(BF16), preferred_element_type=F32)


def _bdot_nt(a, b):
    return lax.dot_general(a.astype(BF16), b.astype(BF16), NT_DIMS, preferred_element_type=F32)


def _each(f, *xs):
    return [f(*a) for a in zip(*xs)]


def _shift_rows(x, prev_row):
    first = lax.broadcasted_iota(jnp.int32, x.shape, 0) == 0
    return jnp.where(first, prev_row, pltpu.roll(x, 1, axis=0))


def _pair_sum(x, first_head):
    lo = jnp.sum(jnp.where(first_head, x, 0.0), axis=-1, keepdims=True)
    hi = jnp.sum(jnp.where(first_head, 0.0, x), axis=-1, keepdims=True)
    return jnp.where(first_head, lo, hi)


def _mm_split_kernel(a_ref, w_ref, o_ref):
    a, w = a_ref[...], w_ref[...]
    a_hi, w_hi = a.astype(BF16), w.astype(BF16)
    a_lo = (a - a_hi.astype(F32)).astype(BF16)
    w_lo = (w - w_hi.astype(F32)).astype(BF16)
    o_ref[...] = (jnp.dot(a_hi, w_hi, preferred_element_type=F32) + jnp.dot(a_hi, w_lo, preferred_element_type=F32)
                  + jnp.dot(a_lo, w_hi, preferred_element_type=F32))


def _mm_split(a, w, tm=512, name="mm_split"):
    m, k = a.shape
    n = w.shape[1]
    return pl.pallas_call(
        _mm_split_kernel,
        grid=(m // tm,),
        in_specs=[pl.BlockSpec((tm, k), lambda i: (i, 0)), pl.BlockSpec((k, n), lambda i: (0, 0))],
        out_specs=pl.BlockSpec((tm, n), lambda i: (i, 0)),
        out_shape=jax.ShapeDtypeStruct((m, n), F32),
        compiler_params=_params("arbitrary"),
        name=name,
    )(a, w)


def _mm_t_kernel(a_ref, w_ref, o_ref, wb_ref):
    @pl.when(pl.program_id(1) == 0)
    def _():
        wb_ref[...] = jnp.transpose(w_ref[0]).astype(BF16)

    o_ref[...] = jnp.dot(a_ref[...], wb_ref[...], preferred_element_type=F32)


def _mm_t(a, w_t, *, layer, row0, n_rows, tm=1024, tn=None, name="mm_t"):
    m, k = a.shape
    tn = n_rows if tn is None else tn
    tm = min(tm, m)
    assert m % tm == 0 and n_rows % tn == 0 and row0 % SUBLANES == 0
    return pl.pallas_call(
        _mm_t_kernel,
        grid=(n_rows // tn, m // tm),
        in_specs=[pl.BlockSpec((tm, k), lambda j, i: (i, 0)),
                  pl.BlockSpec((pl.Element(1), pl.Element(tn), pl.Element(k)),
                               lambda j, i: (layer, pl.multiple_of(row0 + j * tn, SUBLANES), 0))],
        out_specs=pl.BlockSpec((tm, tn), lambda j, i: (i, j)),
        out_shape=jax.ShapeDtypeStruct((m, n_rows), F32),
        scratch_shapes=[pltpu.VMEM((k, tn), BF16)],
        compiler_params=_params("arbitrary", "arbitrary"),
        name=name,
    )(a, w_t)


def _layer_norm(y, g, b):
    mu = jnp.mean(y, axis=-1, keepdims=True)
    d = y - mu
    var = jnp.mean(d * d, axis=-1, keepdims=True)
    return d * lax.rsqrt(var + LN_EPS) * g + b


def _out_ln_kernel(y_ref, w_ref, x_ref, g_ref, b_ref, o_ref, ob_ref):
    h = jnp.dot(y_ref[...], w_ref[...], preferred_element_type=F32)
    out = _layer_norm(DN_ALPHA * x_ref[...] + h, g_ref[...], b_ref[...])
    o_ref[...] = out
    ob_ref[...] = out.astype(BF16)


def _out_ln(y, w, x, g, b, tm=256):
    m, d = x.shape
    row = pl.BlockSpec((tm, d), lambda i: (i, 0))
    vec = pl.BlockSpec((1, d), lambda i: (0, 0))
    return pl.pallas_call(
        _out_ln_kernel,
        grid=(m // tm,),
        in_specs=[row, pl.BlockSpec((d, d), lambda i: (0, 0)), row, vec, vec],
        out_specs=[row, row],
        out_shape=[jax.ShapeDtypeStruct((m, d), F32), jax.ShapeDtypeStruct((m, d), BF16)],
        compiler_params=_params("arbitrary"),
        name="out_ln",
    )(y, w, x, g.reshape(1, d), b.reshape(1, d))


RW_CHUNK = 64
RW_PAIR = 2 * RW_HEAD_DIM
RW_SUB = 16
RW_CHUNKS_PER_STEP = 2
RW_PREP_ROWS = 256
(RV_MU, RV_W0, RV_A0, RV_KK, RV_KA, RV_GN_G, RV_GN_B, RV_RK, RV_ROWS) = (0, 6, 7, 8, 9, 10, 11, 12, 16)


def _rwkv_prep_kernel(r_ref, k_ref, v_ref, z_ref, rp_ref, kp_ref, vp_ref, zp_ref, vec_ref,
                      w1_ref, w2_ref, a1_ref, a2_ref, g1_ref, g2_ref,
                      ro_ref, lw_ref, ko_ref, vo_ref, kk_ref, b_ref, g_ref):
    first_block = pl.program_id(0) == 0
    vec = lambda j: vec_ref[j:j + 1, :]
    prev = lambda ref: jnp.where(first_block, 0.0, ref[SUBLANES - 1:SUBLANES, :])
    lerp = lambda u, up, j: u + (_shift_rows(u, up) - u) * vec(RV_MU + j)
    r = lerp(r_ref[...], prev(rp_ref), 0)
    k = lerp(k_ref[...], prev(kp_ref), 1)
    v = lerp(v_ref[...], prev(vp_ref), 2)
    z = z_ref[...]
    z_diff = _shift_rows(z, prev(zp_ref)) - z
    xw, xa, xg = z + z_diff * vec(RV_MU + 3), z + z_diff * vec(RV_MU + 4), z + z_diff * vec(RV_MU + 5)
    w_pre = vec(RV_W0) + _bdot(jnp.tanh(_bdot(xw, w1_ref[...])), w2_ref[...])
    softplus = jnp.maximum(-w_pre, 0.0) + jnp.log(1.0 + jnp.exp(-jnp.abs(w_pre)))
    lw_ref[...] = -jnp.exp(-softplus - 0.5)
    a = jax.nn.sigmoid(vec(RV_A0) + _bdot(_bdot(xa, a1_ref[...]), a2_ref[...]))
    g_ref[...] = _bdot(jax.nn.sigmoid(_bdot(xg, g1_ref[...])), g2_ref[...])
    kk = k * vec(RV_KK)
    first_head = lax.broadcasted_iota(jnp.int32, (kk.shape[0], RW_PAIR), 1) < RW_HEAD_DIM
    for p in range(RW_HEADS // 2):
        sl = slice(p * RW_PAIR, (p + 1) * RW_PAIR)
        kk_p = kk[:, sl]
        kk_p = kk_p / jnp.maximum(jnp.sqrt(_pair_sum(kk_p * kk_p, first_head)), 1e-12)
        kk_ref[:, sl] = kk_p
        b_ref[:, sl] = kk_p * a[:, sl]
    ro_ref[...] = r
    ko_ref[...] = k * (1.0 + (a - 1.0) * vec(RV_KA))
    vo_ref[...] = v


def _rwkv_prep(rw, vecs, l, p):
    s = rw.shape[0]
    tm = min(RW_PREP_ROWS, s)
    per = tm // SUBLANES
    cur = lambda j: pl.BlockSpec((tm, RW_WIDTH), lambda i: (i, j))
    prev = lambda j: pl.BlockSpec((SUBLANES, RW_WIDTH), lambda i: (jnp.maximum(i * per - 1, 0), j))
    lora = lambda w: pl.BlockSpec((None,) + w.shape[1:], lambda i: (l, 0, 0))
    ws = [p['rw_w1'], p['rw_w2'], p['rw_a1'], p['rw_a2'], p['rw_g1'], p['rw_g2']]
    out = pl.BlockSpec((tm, RW_WIDTH), lambda i: (i, 0))
    return pl.pallas_call(
        _rwkv_prep_kernel,
        grid=(s // tm,),
        in_specs=[cur(0), cur(1), cur(2), cur(3), prev(0), prev(1), prev(2), prev(3),
                  pl.BlockSpec((RV_ROWS, RW_WIDTH), lambda i: (0, 0))] + [lora(w) for w in ws],
        out_specs=[out] * 7,
        out_shape=[jax.ShapeDtypeStruct((s, RW_WIDTH), F32)] * 7,
        compiler_params=_params("arbitrary"),
        name="rwkv_prep",
    )(rw, rw, rw, rw, rw, rw, rw, rw, vecs, *ws)


def _rwkv_kernel(r_ref, lw_ref, k_ref, v_ref, kk_ref, b_ref, g_ref, vec_ref, o_ref, zt_ref):
    @pl.when(pl.program_id(0) == 0)
    def _():
        zt_ref[...] = jnp.zeros_like(zt_ref)

    c, n2 = RW_CHUNK, RW_PAIR
    tri = (lax.broadcasted_iota(jnp.int32, (c, c), 1) <= lax.broadcasted_iota(jnp.int32, (c, c), 0)).astype(BF16)
    row = lax.broadcasted_iota(jnp.int32, (n2, n2), 0)
    col = lax.broadcasted_iota(jnp.int32, (n2, n2), 1)
    t_idx, s_idx = row % c, col % c
    strict = t_idx > s_idx
    incl = t_idx >= s_idx
    diag_blk = (row // RW_SUB) == (col // RW_SUB)
    eye = (row == col).astype(F32)
    first_head = lax.broadcasted_iota(jnp.int32, (c, n2), 1) < RW_HEAD_DIM

    def embed(x):
        return jnp.concatenate([jnp.where(first_head, x, 0.0), jnp.where(first_head, 0.0, x)], axis=0)

    pairs = range(RW_HEADS // 2)
    sls = [slice(p * n2, (p + 1) * n2) for p in pairs]
    bdot = lambda xs, ys: _each(_bdot, xs, ys)

    lhs, rhs, v_e, kb, gam = [], [], [], [], []
    for ck in range(RW_CHUNKS_PER_STEP):
        rs = slice(ck * c, (ck + 1) * c)
        lw = lw_ref[rs, :]
        lw_hi = lw.astype(BF16)
        rem = lw - lw_hi.astype(F32)
        lw_mid = rem.astype(BF16)
        lw_lo = (rem - lw_mid.astype(F32)).astype(BF16)
        g_in = (jnp.dot(tri, lw_hi, preferred_element_type=F32) + jnp.dot(tri, lw_mid, preferred_element_type=F32)
                + jnp.dot(tri, lw_lo, preferred_element_type=F32))
        g_last = g_in[c - 1:c, :]
        e_neg = jnp.exp(-g_in)
        e_end = jnp.exp(g_last - g_in)
        kkd = kk_ref[rs, :] * jnp.exp(g_in - lw)
        rd = r_ref[rs, :] * jnp.exp(g_in)
        kinv = k_ref[rs, :] * e_neg
        binv = b_ref[rs, :] * e_neg
        kd = k_ref[rs, :] * e_end
        bd = b_ref[rs, :] * e_end
        gam.append(jnp.exp(g_last))
        v_e += [embed(v_ref[rs, sl]) for sl in sls]
        lhs += [jnp.concatenate([embed(kkd[:, sl]), embed(rd[:, sl])], axis=0).astype(BF16) for sl in sls]
        rhs += [jnp.concatenate([embed(kinv[:, sl]), embed(binv[:, sl])], axis=0).astype(BF16) for sl in sls]
        kb += [jnp.concatenate([embed(kd[:, sl]), -embed(bd[:, sl])], axis=0).astype(BF16) for sl in sls]
    v_b = [x.astype(BF16) for x in v_e]
    aa = _each(_bdot_nt, lhs, rhs)
    a_kkk = [jnp.where(strict, x[:n2, :n2], 0.0).astype(BF16) for x in aa]
    a_kkb = [jnp.where(strict, x[:n2, n2:], 0.0) for x in aa]
    a_rk = [jnp.where(incl, x[n2:, :n2], 0.0).astype(BF16) for x in aa]
    a_rb = [jnp.where(incl, x[n2:, n2:], 0.0).astype(BF16) for x in aa]
    nd = [jnp.where(diag_blk, x, 0.0) for x in a_kkb]
    off = [jnp.where(diag_blk, 0.0, x) for x in a_kkb]
    nd2 = bdot(nd, nd)
    akv = bdot(a_kkk, v_b)
    o_v = bdot(a_rk, v_b)
    nd4 = bdot(nd2, nd2)
    p1 = bdot([eye - x for x in nd], [eye + x for x in nd2])
    nd8 = bdot(nd4, nd4)
    p2 = bdot(p1, [eye + x for x in nd4])
    d_inv = bdot(p2, [eye + x for x in nd8])
    e1 = bdot(d_inv, off)
    e2 = bdot(e1, e1)
    t_inv = [x.astype(BF16) for x in bdot(bdot([eye - x for x in e1], [eye + x for x in e2]), d_inv)]

    inv_n = 1.0 / RW_HEAD_DIM
    zt = [zt_ref[p] for p in pairs]
    for ck in range(RW_CHUNKS_PER_STEP):
        rs = slice(ck * c, (ck + 1) * c)
        at = lambda xs: xs[ck * len(sls):(ck + 1) * len(sls)]
        zt_b = [x.astype(BF16) for x in zt]
        x1 = _each(lambda a, z: _bdot_nt(a[:n2], z), at(lhs), zt_b)
        o_z = _each(lambda a, z: _bdot_nt(a[n2:], z), at(lhs), zt_b)
        u = bdot(at(t_inv), [a + b for a, b in zip(x1, at(akv))])
        o_u = bdot(at(a_rb), u)
        vu_t = [jnp.transpose(jnp.concatenate([a, b], axis=0)) for a, b in zip(at(v_e), u)]
        z_up = bdot(vu_t, at(kb))
        zt = [z * gam[ck][:, sl] + zu for z, sl, zu in zip(zt, sls, z_up)]
        for p in pairs:
            sl = sls[p]
            o_e = o_z[p] + at(o_v)[p] - o_u[p]
            out = o_e[:c] + o_e[c:]
            mu = _pair_sum(out, first_head) * inv_n
            dev = out - mu
            var = _pair_sum(dev * dev, first_head) * inv_n
            normed = (dev * lax.rsqrt(var + RW_GN_EPS) * vec_ref[RV_GN_G:RV_GN_G + 1, sl]
                      + vec_ref[RV_GN_B:RV_GN_B + 1, sl])
            bonus = _pair_sum(r_ref[rs, sl] * k_ref[rs, sl] * vec_ref[RV_RK:RV_RK + 1, sl], first_head) * v_ref[rs, sl]
            o_ref[rs, sl] = ((normed + bonus) * g_ref[rs, sl]).astype(o_ref.dtype)
    for p in pairs:
        zt_ref[p] = zt[p]


def _rwkv_scan(r, lw, k, v, kk, b, g, vecs):
    s, width = r.shape
    rows = RW_CHUNK * RW_CHUNKS_PER_STEP
    blk = pl.BlockSpec((rows, width), lambda i: (i, 0))
    return pl.pallas_call(
        _rwkv_kernel,
        grid=(s // rows,),
        in_specs=[blk] * 7 + [pl.BlockSpec((RV_ROWS, width), lambda i: (0, 0))],
        out_specs=blk,
        out_shape=jax.ShapeDtypeStruct((s, width), BF16),
        scratch_shapes=[pltpu.VMEM((RW_HEADS // 2, RW_PAIR, RW_PAIR), F32)],
        compiler_params=_params("arbitrary"),
        name="rwkv_scan",
    )(r, lw, k, v, kk, b, g, vecs)


def _rwkv7(rw, l, p):
    rows = [p['rw_mu'][l], p['rw_w0'][l][None], p['rw_a0'][l][None], p['rw_kk'][l][None], p['rw_ka'][l][None],
            p['rw_gn_g'][l][None], p['rw_gn_b'][l][None], p['rw_rk'][l].reshape(1, RW_WIDTH)]
    vecs = jnp.concatenate(rows + [jnp.zeros((RV_ROWS - RV_RK - 1, RW_WIDTH), F32)], axis=0)
    r, lw, k, v, kk, b, g = _rwkv_prep(rw, vecs, l, p)
    return _rwkv_scan(r, lw, k, v, kk, b, g, vecs)


AT_PREP_ROWS = 512
AT_SLABS = AT_WIDTH // LANES


def _rope_kernel(q_ref, k_ref, c_ref, s1_ref, s2_ref, qo_ref, ko_ref):
    cos, s_dn, s_up = c_ref[...], s1_ref[...], s2_ref[...]

    def rot(x, scale):
        outs = []
        for j in range(AT_SLABS):
            xs = x[:, j * LANES:(j + 1) * LANES]
            y = xs * cos + pltpu.roll(xs, LANES - ROPE_DIMS // 2, axis=1) * s_dn + pltpu.roll(xs, ROPE_DIMS // 2, axis=1) * s_up
            outs.append(y * scale if scale != 1.0 else y)
        return jnp.concatenate(outs, axis=-1)

    qo_ref[...] = rot(q_ref[...], AT_HEAD_DIM ** -0.5)
    ko_ref[...] = rot(k_ref[...], 1.0)


def _rope_tables(positions):
    s = positions.shape[0]
    half = ROPE_DIMS // 2
    inv_freq = ROPE_THETA ** (-jnp.arange(half, dtype=F32) * 2.0 / ROPE_DIMS)
    ang = positions.astype(F32)[:, None] * inv_freq
    cos, sin = jnp.cos(ang), jnp.sin(ang)
    rest = AT_HEAD_DIM - ROPE_DIMS
    zeros, ones = jnp.zeros((s, rest), F32), jnp.ones((s, rest), F32)
    z_half = jnp.zeros((s, half), F32)
    head = lambda parts: jnp.tile(jnp.concatenate(parts, axis=1), (1, LANES // AT_HEAD_DIM))
    return head([cos, cos, ones]), head([-sin, z_half, zeros]), head([z_half, sin, zeros])


def _rope(at, tables):
    s = at.shape[0]
    tm = min(AT_PREP_ROWS, s)
    col = lambda j: pl.BlockSpec((tm, AT_WIDTH), lambda i: (i, j))
    tab = pl.BlockSpec((tm, LANES), lambda i: (i, 0))
    return pl.pallas_call(
        _rope_kernel,
        grid=(s // tm,),
        in_specs=[col(0), col(1), tab, tab, tab],
        out_specs=[col(0)] * 2,
        out_shape=[jax.ShapeDtypeStruct((s, AT_WIDTH), F32)] * 2,
        compiler_params=_params("arbitrary"),
        name="attn_rope",
    )(at, at, *tables)


AT_GROUP_SLABS = AT_OUT_WIDTH // LANES


def _attn_kernel(*refs, dilation):
    ns = AT_GROUP_SLABS
    q_refs, kp_refs, kc_refs, vp_refs, vc_refs = (refs[i * ns:(i + 1) * ns] for i in range(5))
    o_ref, lse_ref, o_s, lse_s = refs[5 * ns:]
    n = pl.program_id(0)
    qi = lax.broadcasted_iota(jnp.int32, (AT_BLOCK, AT_BLOCK), 0)
    kj = lax.broadcasted_iota(jnp.int32, (AT_BLOCK, AT_BLOCK), 1)
    mask_c = kj <= qi
    mask_p = (kj >= qi) & (n > 0)
    per_slab = LANES // AT_HEAD_DIM
    ones_b = jnp.ones((AT_BLOCK, AT_HEAD_DIM), BF16)

    def residue(r, carry):
        rows = pl.ds(r, AT_BLOCK, stride=dilation) if dilation > 1 else slice(None)

        def heads_of(slab_refs):
            slabs = [ref[rows, :].astype(BF16) for ref in slab_refs]
            return [x[:, j * AT_HEAD_DIM:(j + 1) * AT_HEAD_DIM] for x in slabs for j in range(per_slab)]

        q, kc, kp, vc, vp = (heads_of(x) for x in (q_refs, kc_refs, kp_refs, vc_refs, vp_refs))
        s_c = [jnp.where(mask_c, lax.dot_general(a, b, NT_DIMS, preferred_element_type=F32), -jnp.inf)
               for a, b in zip(q, kc)]
        s_p = [jnp.where(mask_p, lax.dot_general(a, b, NT_DIMS, preferred_element_type=F32), -jnp.inf)
               for a, b in zip(q, kp)]
        m = [jnp.maximum(jnp.max(a, axis=-1, keepdims=True), jnp.max(b, axis=-1, keepdims=True))
             for a, b in zip(s_c, s_p)]
        p_c = [jnp.exp(a - mm).astype(BF16) for a, mm in zip(s_c, m)]
        p_p = [jnp.exp(a - mm).astype(BF16) for a, mm in zip(s_p, m)]
        aug = lambda v: jnp.concatenate([v, ones_b], axis=1)
        acc = [jnp.dot(a, aug(v), preferred_element_type=F32) + jnp.dot(b, aug(w), preferred_element_type=F32)
               for a, v, b, w in zip(p_c, vc, p_p, vp)]
        out = [x[:, :AT_HEAD_DIM] / x[:, AT_HEAD_DIM:] for x in acc]
        lse = [mm + jnp.log(x[:, AT_HEAD_DIM:]) for mm, x in zip(m, acc)]
        for t in range(ns):
            o_s[t, rows, :] = jnp.concatenate(out[t * per_slab:(t + 1) * per_slab], axis=-1)
            lse_s[t, rows, :] = jnp.concatenate(lse[t * per_slab:(t + 1) * per_slab], axis=-1)
        return carry

    if dilation > 1:
        lax.fori_loop(0, dilation, residue, 0)
    else:
        residue(0, 0)
    for t in range(ns):
        o_ref[:, t * LANES:(t + 1) * LANES] = o_s[t]
        lse_ref[:, t * LANES:(t + 1) * LANES] = lse_s[t]


def _dilated_attention(q, k, at, g, dilation):
    s = q.shape[0]
    rows = AT_BLOCK * dilation
    assert s % rows == 0
    ns = AT_GROUP_SLABS
    qk_col = g * ns
    v_col = 2 * (AT_WIDTH // LANES) + g * ns
    cur = lambda c: [pl.BlockSpec((rows, LANES), lambda n, t=t: (n, c + t)) for t in range(ns)]
    prev = lambda c: [pl.BlockSpec((rows, LANES), lambda n, t=t: (jnp.maximum(n - 1, 0), c + t)) for t in range(ns)]
    out = pl.BlockSpec((rows, AT_OUT_WIDTH), lambda n: (n, 0))
    return pl.pallas_call(
        functools.partial(_attn_kernel, dilation=dilation),
        grid=(s // rows,),
        in_specs=cur(qk_col) + prev(qk_col) + cur(qk_col) + prev(v_col) + cur(v_col),
        out_specs=[out, out],
        out_shape=[jax.ShapeDtypeStruct((s, AT_OUT_WIDTH), F32)] * 2,
        scratch_shapes=[pltpu.VMEM((ns, rows, LANES), F32)] * 2,
        compiler_params=_params("arbitrary"),
        name=f"attn_d{dilation}",
    )(*([q] * ns + [k] * (2 * ns) + [at] * (2 * ns)))


def _attn_merge_kernel(o0, o1, o2, l0, l1, l2, y_ref):
    m = jnp.maximum(jnp.maximum(l0[...], l1[...]), l2[...])
    e0, e1, e2 = jnp.exp(l0[...] - m), jnp.exp(l1[...] - m), jnp.exp(l2[...] - m)
    tot = e0 + e1 + e2
    y_ref[...] = ((e0 / tot) * o0[...] + (e1 / tot) * o1[...] + (e2 / tot) * o2[...]).astype(BF16)


def _attention(at, tables):
    s = at.shape[0]
    q, k = _rope(at, tables)
    outs, lses = [], []
    for g, (_, dilation) in enumerate(AT_GROUPS):
        o, lse = _dilated_attention(q, k, at, g, dilation)
        outs.append(o)
        lses.append(lse)
    tm = min(AT_PREP_ROWS, s)
    blk = pl.BlockSpec((tm, AT_OUT_WIDTH), lambda i: (i, 0))
    return pl.pallas_call(
        _attn_merge_kernel,
        grid=(s // tm,),
        in_specs=[blk] * 6,
        out_specs=blk,
        out_shape=jax.ShapeDtypeStruct((s, AT_OUT_WIDTH), BF16),
        compiler_params=_params("arbitrary"),
        name="attn_merge",
    )(*outs, *lses)


ML_CHUNKS_PER_STEP = 8
ML_ROWS = ML_CHUNKS_PER_STEP * ML_CHUNK
ML_GROUP = 4


def _mlstm_kernel(u_ref, v_ref, og_ref, cw_ref, bb_ref, ib_ref, brow_ref, irow_ref, g_ref, y_ref,
                  qk_ref, ubuf_ref, ct_ref, m_ref):
    @pl.when(pl.program_id(0) == 0)
    def _():
        ubuf_ref[:SUBLANES, :] = jnp.zeros((SUBLANES, ubuf_ref.shape[1]), F32)
        ct_ref[...] = jnp.zeros_like(ct_ref)
        m_ref[...] = jnp.full_like(m_ref, -jnp.inf)

    taps = cw_ref.shape[0]
    ubuf_ref[SUBLANES:, :] = u_ref[...]
    conv = ubuf_ref[SUBLANES:, :] * cw_ref[taps - 1:taps, :]
    for back in range(1, taps):
        conv = conv + ubuf_ref[pl.ds(SUBLANES - back, ML_ROWS), :] * cw_ref[taps - 1 - back:taps - back, :]
    qk_ref[...] = conv * jax.nn.sigmoid(conv)
    ubuf_ref[:SUBLANES, :] = ubuf_ref[ML_ROWS:, :]

    dh, lc = ML_HEAD_DIM, ML_CHUNK
    si = lax.broadcasted_iota(jnp.int32, (lc, lc), 0)
    ji = lax.broadcasted_iota(jnp.int32, (lc, lc), 1)
    causal = ji <= si
    ones_b = jnp.ones((lc, dh), BF16)

    def group(c, rows, heads):
        cols = [slice(h * dh, (h + 1) * dh) for h in heads]
        q = [qk_ref[rows, cl] * (dh ** -0.5) for cl in cols]
        k = [qk_ref[rows, pl.ds(ML_WIDTH + h * dh, dh)] for h in heads]
        v = [v_ref[rows, cl] for cl in cols]
        bb = [bb_ref[rows, cl] for cl in cols]
        ib = [ib_ref[rows, cl] for cl in cols]
        b_row = [brow_ref[c, h:h + 1, :] for h in heads]
        i_row = [irow_ref[c, h:h + 1, :] for h in heads]
        m_prev = [m_ref[h, 0:1, :] for h in heads]
        ct_prev = [ct_ref[h] for h in heads]
        qb = [x.astype(BF16) for x in q]
        kb = [x.astype(BF16) for x in k]
        s_qk = _each(_bdot_nt, qb, kb)
        inter = _each(_bdot, qb, ct_prev)
        k_t = [jnp.transpose(x).astype(BF16) for x in k]
        log_d = [jnp.where(causal, b[:, :lc] - br + ir, -jnp.inf) for b, br, ir in zip(bb, b_row, i_row)]
        a_log = [b + mp for b, mp in zip(bb, m_prev)]
        m_s = [jnp.maximum(al, jnp.max(ld, axis=-1, keepdims=True)) for al, ld in zip(a_log, log_d)]
        inter_w = [jnp.exp(al - ms) for al, ms in zip(a_log, m_s)]
        qk = [s * jnp.exp(ld - ms[:, :lc]) for s, ld, ms in zip(s_qk, log_d, m_s)]
        intra = [_bdot(a, jnp.concatenate([x.astype(BF16), ones_b], axis=1)) for a, x in zip(qk, v)]
        b_last = [b[lc - 1:lc, :] for b in bb]
        w_end = [bl - b + i for bl, b, i in zip(b_last, bb, ib)]
        m_new = [jnp.maximum(bl + mp, jnp.max(we, axis=0, keepdims=True)) for bl, mp, we in zip(b_last, m_prev, w_end)]
        dec = [jnp.exp(bl + mp - mn) for bl, mp, mn in zip(b_last, m_prev, m_new)]
        wts = [jnp.exp(we - mn) for we, mn in zip(w_end, m_new)]
        upd = [_bdot(kt, jnp.concatenate([x * w, w], axis=1)) for kt, x, w in zip(k_t, v, wts)]
        for h, ctp, d, up, mn in zip(heads, ct_prev, dec, upd, m_new):
            ct_ref[h] = jnp.concatenate([d, d], axis=1) * ctp + up
            m_ref[h, 0:1, :] = mn
        num = [w * a[:, :dh] + b[:, :dh] for w, a, b in zip(inter_w, inter, intra)]
        den = [w * a[:, dh:] + b[:, dh:] for w, a, b in zip(inter_w, inter, intra)]
        hid = [n / jnp.maximum(jnp.abs(d), jnp.exp(-ms)) for n, d, ms in zip(num, den, m_s)]
        hid = [x * jax.nn.sigmoid(og_ref[rows, cl]) for x, cl in zip(hid, cols)]
        mu = [jnp.mean(x, axis=-1, keepdims=True) for x in hid]
        dev = [x - m for x, m in zip(hid, mu)]
        var = [jnp.mean(x * x, axis=-1, keepdims=True) for x in dev]
        for cl, x, vr in zip(cols, dev, var):
            y_ref[rows, cl] = (x * lax.rsqrt(vr + ML_NORM_EPS) * g_ref[:, cl]).astype(y_ref.dtype)

    def chunk(c, carry):
        rows = pl.ds(pl.multiple_of(c * lc, lc), lc)
        for h0 in range(0, ML_HEADS, ML_GROUP):
            group(c, rows, range(h0, h0 + ML_GROUP))
        return carry

    lax.fori_loop(0, ML_CHUNKS_PER_STEP, chunk, 0)


def _mlstm(ml, gates_if, l, p):
    s = ml.shape[0]
    nc = s // ML_CHUNK
    i_pre = gates_if[:, :ML_HEADS] + p['ml_gate_b'][l, 0]
    f_pre = gates_if[:, ML_HEADS:] + p['ml_gate_b'][l, 1]
    lf = jax.nn.log_sigmoid(f_pre)
    b_cum = jnp.cumsum(lf.reshape(nc, ML_CHUNK, ML_HEADS), axis=1)
    b_row = jnp.transpose(b_cum, (0, 2, 1))
    i_row = jnp.transpose(i_pre.reshape(nc, ML_CHUNK, ML_HEADS), (0, 2, 1))
    over_lanes = lambda a: jnp.repeat(a.reshape(s, ML_HEADS), ML_HEAD_DIM, axis=1)
    col = lambda j: pl.BlockSpec((ML_ROWS, ML_WIDTH), lambda i: (i, j))
    grow = pl.BlockSpec((ML_CHUNKS_PER_STEP, ML_HEADS, ML_CHUNK), lambda i: (i, 0, 0))
    conv_w = p['ml_conv'][l]
    return pl.pallas_call(
        _mlstm_kernel,
        grid=(s // ML_ROWS,),
        in_specs=[pl.BlockSpec((ML_ROWS, 2 * ML_WIDTH), lambda i: (i, 0)), col(2), col(3),
                  pl.BlockSpec(conv_w.shape, lambda i: (0, 0)), col(0), col(0), grow, grow,
                  pl.BlockSpec((1, ML_WIDTH), lambda i: (0, 0))],
        out_specs=pl.BlockSpec((ML_ROWS, ML_WIDTH), lambda i: (i, 0)),
        out_shape=jax.ShapeDtypeStruct((s, ML_WIDTH), BF16),
        scratch_shapes=[pltpu.VMEM((ML_ROWS, 2 * ML_WIDTH), F32),
                        pltpu.VMEM((ML_ROWS + SUBLANES, 2 * ML_WIDTH), F32),
                        pltpu.VMEM((ML_HEADS, ML_HEAD_DIM, 2 * ML_HEAD_DIM), F32),
                        pltpu.VMEM((ML_HEADS, SUBLANES, ML_HEAD_DIM), F32)],
        compiler_params=_params("arbitrary"),
        name="mlstm",
    )(ml, ml, ml, conv_w, over_lanes(b_cum), over_lanes(i_pre), b_row, i_row,
      p['ml_ln_g'][l].reshape(1, ML_WIDTH))


def _merge_kernel(x_ref, yr_ref, ya_ref, ym_ref, g0_ref, g1_ref, g2_ref, w0_ref, w1_ref, w2_ref, o_ref, gb_ref):
    @pl.when(pl.program_id(1) == 0)
    def _():
        for b, g_ref in enumerate((g0_ref, g1_ref, g2_ref)):
            gb_ref[b] = jnp.transpose(g_ref[0]).astype(BF16)

    x = x_ref[...]
    gate = lambda b: jax.nn.sigmoid(jnp.dot(x, gb_ref[b], preferred_element_type=F32))
    y = (gate(0) * jnp.dot(yr_ref[...], w0_ref[...], preferred_element_type=F32)
         + gate(1) * jnp.dot(ya_ref[...], w1_ref[...], preferred_element_type=F32)
         + gate(2) * jnp.dot(ym_ref[...], w2_ref[...], preferred_element_type=F32))
    o_ref[...] = y.astype(o_ref.dtype)


def _merge(xb, y_rw, y_at, y_ml, w_in_t, l, wb_rw, wb_at, wb_ml, tm=512, tn=512):
    m, d = xb.shape
    nj = d // tn
    act = lambda a: pl.BlockSpec((tm, a.shape[1]), lambda j, i: (i, 0))
    gate = lambda b: pl.BlockSpec((pl.Element(1), pl.Element(tn), pl.Element(d)),
                                  lambda j, i: (l, pl.multiple_of(COL_GATE + b * d + j * tn, SUBLANES), 0))
    wsp = lambda w: pl.BlockSpec((w.shape[0], tn), lambda j, i: (0, j))
    return pl.pallas_call(
        _merge_kernel,
        grid=(nj, m // tm),
        in_specs=[act(xb), act(y_rw), act(y_at), act(y_ml), gate(0), gate(1), gate(2),
                  wsp(wb_rw), wsp(wb_at), wsp(wb_ml)],
        out_specs=pl.BlockSpec((tm, tn), lambda j, i: (i, j)),
        out_shape=jax.ShapeDtypeStruct((m, d), BF16),
        scratch_shapes=[pltpu.VMEM((N_BRANCHES, d, tn), BF16)],
        compiler_params=_params("arbitrary", "arbitrary"),
        name="merge",
    )(xb, y_rw, y_at, y_ml, w_in_t, w_in_t, w_in_t, wb_rw, wb_at, wb_ml)


MOE_LN_ROWS = 256


def _row_gather(idx_ref, base, n, src_hbm, dst, sem, unroll=False):
    def start(j):
        pltpu.make_async_copy(src_hbm.at[pl.ds(idx_ref[base + j], 1)], dst.at[pl.ds(j, 1)], sem).start()

    if unroll:
        for j in range(n):
            start(j)
    else:
        lax.fori_loop(0, n, lambda j, carry: (start(j), carry)[1], 0, unroll=8)


def _row_gather_wait(n, src_hbm, dst, sem):
    pltpu.make_async_copy(src_hbm.at[pl.ds(0, n)], dst.at[pl.ds(0, n)], sem).wait()


MOE_AHEAD = 2


def _expert_kernel(be_ref, new_ref, used_ref, src_ref, x_hbm, w1_ref, w3_ref, w2_ref, o_ref,
                   xbuf, w1b, w3b, w2b, sem):
    i = pl.program_id(0)
    slots = MOE_AHEAD + 1
    slot = i % slots

    @pl.when(i == 0)
    def _():
        for j in range(MOE_AHEAD):
            _row_gather(src_ref, j * MOE_ROWS, MOE_ROWS, x_hbm, xbuf.at[j], sem.at[j])

    @pl.when(new_ref[i] == 1)
    def _():
        w1b[...] = w1_ref[...].astype(BF16)
        w3b[...] = w3_ref[...].astype(BF16)
        w2b[...] = w2_ref[...].astype(BF16)

    @pl.when(used_ref[i] == 1)
    def _():
        _row_gather_wait(MOE_ROWS, x_hbm, xbuf.at[slot], sem.at[slot])
        ahead = (i + MOE_AHEAD) % slots
        _row_gather(src_ref, (i + MOE_AHEAD) * MOE_ROWS, MOE_ROWS, x_hbm, xbuf.at[ahead], sem.at[ahead], unroll=True)
        x = xbuf[slot].astype(BF16)
        h1 = jnp.dot(x, w1b[...], preferred_element_type=F32)
        h3 = jnp.dot(x, w3b[...], preferred_element_type=F32)
        hid = (h1 * jax.nn.sigmoid(h1) * h3).astype(BF16)
        o_ref[...] = jnp.dot(hid, w2b[...], preferred_element_type=F32)

    @pl.when(used_ref[i] == 0)
    def _():
        @pl.when((i < MOE_AHEAD) | (used_ref[jnp.maximum(i - MOE_AHEAD, 0)] == 1))
        def _():
            _row_gather_wait(MOE_ROWS, x_hbm, xbuf.at[slot], sem.at[slot])

        o_ref[...] = jnp.zeros_like(o_ref)


def _experts(x, src, blk_expert, blk_new, blk_used, w1, w3, w2, l):
    rows = src.shape[0]
    d = x.shape[1]
    nblk = rows // MOE_ROWS
    up = pl.BlockSpec((None, None, d, MOE_FF), lambda i, be, nw, us, sr: (l, be[i], 0, 0))
    down = pl.BlockSpec((None, None, MOE_FF, d), lambda i, be, nw, us, sr: (l, be[i], 0, 0))
    return pl.pallas_call(
        _expert_kernel,
        grid_spec=pltpu.PrefetchScalarGridSpec(
            num_scalar_prefetch=4,
            grid=(nblk,),
            in_specs=[pl.BlockSpec(memory_space=pl.ANY), up, up, down],
            out_specs=pl.BlockSpec((MOE_ROWS, d), lambda i, be, nw, us, sr: (i, 0)),
            scratch_shapes=[pltpu.VMEM((MOE_AHEAD + 1, MOE_ROWS, d), F32),
                            pltpu.VMEM((d, MOE_FF), BF16), pltpu.VMEM((d, MOE_FF), BF16),
                            pltpu.VMEM((MOE_FF, d), BF16),
                            pltpu.SemaphoreType.DMA((MOE_AHEAD + 1,))]),
        out_shape=jax.ShapeDtypeStruct((rows, d), F32),
        compiler_params=_params("arbitrary"),
        name="experts",
    )(blk_expert, blk_new, blk_used, src, x, w1, w3, w2)


def _moe_ln_kernel(pos_ref, x_ref, w_ref, y_hbm, g_ref, b_ref, o_ref, ob_ref, ybuf, sem):
    i = pl.program_id(0)
    nblk = pl.num_programs(0)
    n = MOE_TOP_K * MOE_LN_ROWS
    tm = MOE_LN_ROWS
    slot = i % 2

    @pl.when(i == 0)
    def _():
        _row_gather(pos_ref, 0, n, y_hbm, ybuf.at[0], sem.at[0])

    _row_gather_wait(n, y_hbm, ybuf.at[slot], sem.at[slot])
    _row_gather(pos_ref, (i + 1) * n, n, y_hbm, ybuf.at[1 - slot], sem.at[1 - slot], unroll=True)
    rows = ybuf.at[slot]
    h = rows[0:tm, :] * w_ref[:, 0:1] + rows[tm:2 * tm, :] * w_ref[:, 1:2]
    out = _layer_norm(DN_ALPHA * x_ref[...] + h, g_ref[...], b_ref[...])
    o_ref[...] = out
    ob_ref[...] = out.astype(BF16)

    @pl.when(i == nblk - 1)
    def _():
        _row_gather_wait(n, y_hbm, ybuf.at[1 - slot], sem.at[1 - slot])


def _moe_ln(x, y, pos, top_w, g, b):
    t, d = x.shape
    tm = MOE_LN_ROWS
    pos = jnp.transpose(pos.reshape(t // tm, tm, MOE_TOP_K), (0, 2, 1)).reshape(t * MOE_TOP_K)
    pos = jnp.concatenate([pos, jnp.zeros((tm * MOE_TOP_K,), jnp.int32)])
    row = pl.BlockSpec((tm, d), lambda i, pos: (i, 0))
    vec = pl.BlockSpec((1, d), lambda i, pos: (0, 0))
    return pl.pallas_call(
        _moe_ln_kernel,
        grid_spec=pltpu.PrefetchScalarGridSpec(
            num_scalar_prefetch=1,
            grid=(t // tm,),
            in_specs=[row, pl.BlockSpec((tm, MOE_TOP_K), lambda i, pos: (i, 0)),
                      pl.BlockSpec(memory_space=pl.ANY), vec, vec],
            out_specs=[row, row],
            scratch_shapes=[pltpu.VMEM((2, MOE_TOP_K * tm, d), F32),
                            pltpu.SemaphoreType.DMA((2,))]),
        out_shape=[jax.ShapeDtypeStruct((t, d), F32), jax.ShapeDtypeStruct((t, d), BF16)],
        compiler_params=_params("arbitrary"),
        name="moe_ln",
    )(pos, x, top_w, y, g.reshape(1, d), b.reshape(1, d))


def _moe(x, l, p, ln_g, ln_b):
    t, d = x.shape
    n_assign = t * MOE_TOP_K
    w_router = jnp.concatenate([p['moe_w_group'][l], p['moe_w_expert'][l]], axis=1)
    w_router = jnp.pad(w_router, ((0, 0), (0, LANES - w_router.shape[1])))
    logits = _mm_split(x, w_router, name="mm_router")
    g_logits = logits[:, :MOE_GROUPS] + p['moe_b_group'][l]
    g_prob = jax.nn.softmax(g_logits, -1)
    g_idx = jnp.argmax(g_logits, -1)
    g_w = jnp.take_along_axis(g_prob, g_idx[:, None], axis=1)
    e_logits = (logits[:, MOE_GROUPS:MOE_GROUPS + MOE_EXPERTS] + p['moe_b_expert'][l]).reshape(
        t, MOE_GROUPS, MOE_EXPERTS_PER_GROUP)
    e_sel = jnp.take_along_axis(e_logits, g_idx[:, None, None], axis=1)[:, 0]
    top_l, top_i = lax.top_k(e_sel, MOE_TOP_K)
    top_w = jax.nn.softmax(top_l, -1) * g_w
    e_flat = (g_idx[:, None] * MOE_EXPERTS_PER_GROUP + top_i).reshape(n_assign).astype(jnp.int32)
    onehot = (e_flat[:, None] == jnp.arange(MOE_EXPERTS, dtype=jnp.int32)[None]).astype(jnp.int32)
    rank = jnp.take_along_axis(jnp.cumsum(onehot, axis=0), e_flat[:, None], axis=1)[:, 0] - 1
    counts = jnp.sum(onehot, axis=0)
    pcounts = (counts + MOE_ROWS - 1) // MOE_ROWS * MOE_ROWS
    pends = jnp.cumsum(pcounts)
    pstarts = pends - pcounts
    pos = pstarts[e_flat] + rank
    n_rows = n_assign + (MOE_EXPERTS + MOE_AHEAD) * MOE_ROWS
    nblk = n_rows // MOE_ROWS
    src = jnp.zeros((n_rows,), jnp.int32).at[pos].set(jnp.arange(n_assign, dtype=jnp.int32) // MOE_TOP_K)
    blk_start = jnp.arange(nblk, dtype=jnp.int32) * MOE_ROWS
    blk_expert = jnp.clip(jnp.searchsorted(pends, blk_start, side='right'), 0, MOE_EXPERTS - 1).astype(jnp.int32)
    blk_used = (blk_start < pends[-1]).astype(jnp.int32)
    blk_new = jnp.concatenate([jnp.ones((1,), jnp.int32), (blk_expert[1:] != blk_expert[:-1]).astype(jnp.int32)])
    y = _experts(x, src, blk_expert, blk_new, blk_used, p['moe_w1'], p['moe_w3'], p['moe_w2'], l)
    return _moe_ln(x, y, pos.astype(jnp.int32).reshape(t, MOE_TOP_K), top_w, ln_g, ln_b)


def _mixer(xf, xb, tables, l, p, ln_g, ln_b):
    w_in_t = p['w_in_t']
    bf = lambda w: w.astype(BF16)
    rw = _mm_t(xb, w_in_t, layer=l, row0=COL_RW, n_rows=COL_AT - COL_RW, tn=1024, name="mm_in_rw")
    at = _mm_t(xb, w_in_t, layer=l, row0=COL_AT, n_rows=COL_ML - COL_AT, tn=AT_WIDTH, name="mm_in_at")
    ml = _mm_t(xb, w_in_t, layer=l, row0=COL_ML, n_rows=COL_IF - COL_ML, tn=1024, name="mm_in_ml")
    gates_if = _mm_t(xb, w_in_t, layer=l, row0=COL_IF, n_rows=COL_GATE - COL_IF, name="mm_in_if")
    y_rw = _rwkv7(rw, l, p)
    y_at = _attention(at, tables)
    y_ml = _mlstm(ml, gates_if, l, p)
    wb = p['w_branch'][l]
    y = _merge(xb, y_rw, y_at, y_ml, w_in_t, l, bf(wb[:RW_WIDTH]),
               bf(wb[RW_WIDTH:RW_WIDTH + AT_OUT_WIDTH]), bf(wb[RW_WIDTH + AT_OUT_WIDTH:]))
    return _out_ln(y, bf(p['w_out'][l]), xf, ln_g, ln_b)


def kernel(x, positions, w_in, rw_mu, rw_w0, rw_w1, rw_w2, rw_a0, rw_a1, rw_a2, rw_g1, rw_g2, rw_kk, rw_ka, rw_rk, rw_gn_g, rw_gn_b, ml_conv, ml_gate_b, ml_ln_g, w_branch, w_out, ln1_g, ln1_b, moe_w_group, moe_b_group, moe_w_expert, moe_b_expert, moe_w1, moe_w3, moe_w2, ln2_g, ln2_b):
    p = dict(w_in_t=jnp.swapaxes(w_in, 1, 2), rw_mu=rw_mu, rw_w0=rw_w0, rw_w1=rw_w1, rw_w2=rw_w2, rw_a0=rw_a0, rw_a1=rw_a1,
             rw_a2=rw_a2, rw_g1=rw_g1, rw_g2=rw_g2, rw_kk=rw_kk, rw_ka=rw_ka, rw_rk=rw_rk, rw_gn_g=rw_gn_g,
             rw_gn_b=rw_gn_b, ml_conv=ml_conv, ml_gate_b=ml_gate_b, ml_ln_g=ml_ln_g, w_branch=w_branch,
             w_out=w_out, moe_w_group=moe_w_group, moe_b_group=moe_b_group, moe_w_expert=moe_w_expert,
             moe_b_expert=moe_b_expert, moe_w1=moe_w1, moe_w3=moe_w3, moe_w2=moe_w2)
    batch, seq, d = x.shape
    assert batch == 1
    xf = x.reshape(seq, d)
    xb = xf.astype(BF16)
    tables = _rope_tables(positions.reshape(seq))
    for l in range(w_in.shape[0]):
        xf, xb = _mixer(xf, xb, tables, l, p, ln1_g[l], ln1_b[l])
        xf, xb = _moe(xf, l, p, ln2_g[l], ln2_b[l])
    return xf.reshape(batch, seq, d)
```

```python
import functools

import jax
import jax.numpy as jnp
from jax import lax
from jax.experimental import pallas as pl
from jax.experimental.pallas import tpu as pltpu

F32 = jnp.float32
BF16 = jnp.bfloat16

D_MODEL = 2048
DEPTH = 4
LN_EPS = 1e-5
DN_ALPHA = (2 * DEPTH) ** 0.25

RW_HEADS = 16
RW_HEAD_DIM = 64
RW_WIDTH = RW_HEADS * RW_HEAD_DIM
RW_GN_EPS = 64e-5

AT_GROUPS = ((128, 1), (512, 4), (2048, 16))
AT_HEADS_PER_GROUP = 6
AT_HEADS = AT_HEADS_PER_GROUP * len(AT_GROUPS)
AT_HEAD_DIM = 64
AT_WIDTH = AT_HEADS * AT_HEAD_DIM
AT_OUT_WIDTH = AT_HEADS_PER_GROUP * AT_HEAD_DIM
AT_BLOCK = 128
ROPE_DIMS = AT_HEAD_DIM // 4
ROPE_THETA = 500000.0

ML_HEADS = 8
ML_HEAD_DIM = 128
ML_WIDTH = ML_HEADS * ML_HEAD_DIM
ML_CHUNK = 64
ML_NORM_EPS = 1e-6

N_BRANCHES = 3
COL_RW = 0
COL_AT = 4 * RW_WIDTH
COL_ML = COL_AT + 3 * AT_WIDTH
COL_IF = COL_ML + 4 * ML_WIDTH
COL_GATE = COL_IF + 2 * ML_HEADS
IN_COLS = COL_GATE + N_BRANCHES * D_MODEL

MOE_GROUPS = 4
MOE_EXPERTS_PER_GROUP = 8
MOE_EXPERTS = MOE_GROUPS * MOE_EXPERTS_PER_GROUP
MOE_TOP_K = 2
MOE_FF = 512
MOE_ROWS = 256

LANES = 128
SUBLANES = 8
VMEM_LIMIT = 56 * 1024 * 1024

NT_DIMS = (((1,), (1,)), ((), ()))


def _params(*sem):
    return pltpu.CompilerParams(dimension_semantics=sem, vmem_limit_bytes=VMEM_LIMIT)


def _bdot(a, b):
    return jnp.dot(a.astype(BF16), b.astype(BF16), preferred_element_type=F32)


def _bdot_nt(a, b):
    return lax.dot_general(a.astype(BF16), b.astype(BF16), NT_DIMS, preferred_element_type=F32)


def _each(f, *xs):
    return [f(*a) for a in zip(*xs)]


def _shift_rows(x, prev_row):
    first = lax.broadcasted_iota(jnp.int32, x.shape, 0) == 0
    return jnp.where(first, prev_row, pltpu.roll(x, 1, axis=0))


def _pair_sum(x, first_head):
    lo = jnp.sum(jnp.where(first_head, x, 0.0), axis=-1, keepdims=True)
    hi = jnp.sum(jnp.where(first_head, 0.0, x), axis=-1, keepdims=True)
    return jnp.where(first_head, lo, hi)


def _mm_split_kernel(a_ref, w_ref, o_ref):
    a, w = a_ref[...], w_ref[...]
    a_hi, w_hi = a.astype(BF16), w.astype(BF16)
    a_lo = (a - a_hi.astype(F32)).astype(BF16)
    w_lo = (w - w_hi.astype(F32)).astype(BF16)
    o_ref[...] = (jnp.dot(a_hi, w_hi, preferred_element_type=F32) + jnp.dot(a_hi, w_lo, preferred_element_type=F32)
                  + jnp.dot(a_lo, w_hi, preferred_element_type=F32))


def _mm_split(a, w, tm=512, name="mm_split"):
    m, k = a.shape
    n = w.shape[1]
    return pl.pallas_call(
        _mm_split_kernel,
        grid=(m // tm,),
        in_specs=[pl.BlockSpec((tm, k), lambda i: (i, 0)), pl.BlockSpec((k, n), lambda i: (0, 0))],
        out_specs=pl.BlockSpec((tm, n), lambda i: (i, 0)),
        out_shape=jax.ShapeDtypeStruct((m, n), F32),
        compiler_params=_params("arbitrary"),
        name=name,
    )(a, w)


def _mm_t_kernel(a_ref, w_ref, o_ref, wb_ref):
    @pl.when(pl.program_id(1) == 0)
    def _():
        wb_ref[...] = jnp.transpose(w_ref[0]).astype(BF16)

    o_ref[...] = jnp.dot(a_ref[...], wb_ref[...], preferred_element_type=F32)


def _mm_t(a, w_t, *, layer, row0, n_rows, tm=1024, tn=None, name="mm_t"):
    m, k = a.shape
    tn = n_rows if tn is None else tn
    tm = min(tm, m)
    assert m % tm == 0 and n_rows % tn == 0 and row0 % SUBLANES == 0
    return pl.pallas_call(
        _mm_t_kernel,
        grid=(n_rows // tn, m // tm),
        in_specs=[pl.BlockSpec((tm, k), lambda j, i: (i, 0)),
                  pl.BlockSpec((pl.Element(1), pl.Element(tn), pl.Element(k)),
                               lambda j, i: (layer, pl.multiple_of(row0 + j * tn, SUBLANES), 0))],
        out_specs=pl.BlockSpec((tm, tn), lambda j, i: (i, j)),
        out_shape=jax.ShapeDtypeStruct((m, n_rows), F32),
        scratch_shapes=[pltpu.VMEM((k, tn), BF16)],
        compiler_params=_params("arbitrary", "arbitrary"),
        name=name,
    )(a, w_t)


def _layer_norm(y, g, b):
    mu = jnp.mean(y, axis=-1, keepdims=True)
    d = y - mu
    var = jnp.mean(d * d, axis=-1, keepdims=True)
    return d * lax.rsqrt(var + LN_EPS) * g + b


def _out_ln_kernel(y_ref, w_ref, x_ref, g_ref, b_ref, o_ref, ob_ref):
    h = jnp.dot(y_ref[...], w_ref[...], preferred_element_type=F32)
    out = _layer_norm(DN_ALPHA * x_ref[...] + h, g_ref[...], b_ref[...])
    o_ref[...] = out
    ob_ref[...] = out.astype(BF16)


def _out_ln(y, w, x, g, b, tm=256):
    m, d = x.shape
    row = pl.BlockSpec((tm, d), lambda i: (i, 0))
    vec = pl.BlockSpec((1, d), lambda i: (0, 0))
    return pl.pallas_call(
        _out_ln_kernel,
        grid=(m // tm,),
        in_specs=[row, pl.BlockSpec((d, d), lambda i: (0, 0)), row, vec, vec],
        out_specs=[row, row],
        out_shape=[jax.ShapeDtypeStruct((m, d), F32), jax.ShapeDtypeStruct((m, d), BF16)],
        compiler_params=_params("arbitrary"),
        name="out_ln",
    )(y, w, x, g.reshape(1, d), b.reshape(1, d))


RW_CHUNK = 64
RW_PAIR = 2 * RW_HEAD_DIM
RW_SUB = 16
RW_CHUNKS_PER_STEP = 2
RW_PREP_ROWS = 256
(RV_MU, RV_W0, RV_A0, RV_KK, RV_KA, RV_GN_G, RV_GN_B, RV_RK, RV_ROWS) = (0, 6, 7, 8, 9, 10, 11, 12, 16)


def _rwkv_prep_kernel(r_ref, k_ref, v_ref, z_ref, rp_ref, kp_ref, vp_ref, zp_ref, vec_ref,
                      w1_ref, w2_ref, a1_ref, a2_ref, g1_ref, g2_ref,
                      ro_ref, lw_ref, ko_ref, vo_ref, kk_ref, b_ref, g_ref):
    first_block = pl.program_id(0) == 0
    vec = lambda j: vec_ref[j:j + 1, :]
    prev = lambda ref: jnp.where(first_block, 0.0, ref[SUBLANES - 1:SUBLANES, :])
    lerp = lambda u, up, j: u + (_shift_rows(u, up) - u) * vec(RV_MU + j)
    r = lerp(r_ref[...], prev(rp_ref), 0)
    k = lerp(k_ref[...], prev(kp_ref), 1)
    v = lerp(v_ref[...], prev(vp_ref), 2)
    z = z_ref[...]
    z_diff = _shift_rows(z, prev(zp_ref)) - z
    xw, xa, xg = z + z_diff * vec(RV_MU + 3), z + z_diff * vec(RV_MU + 4), z + z_diff * vec(RV_MU + 5)
    w_pre = vec(RV_W0) + _bdot(jnp.tanh(_bdot(xw, w1_ref[...])), w2_ref[...])
    softplus = jnp.maximum(-w_pre, 0.0) + jnp.log(1.0 + jnp.exp(-jnp.abs(w_pre)))
    lw_ref[...] = -jnp.exp(-softplus - 0.5)
    a = jax.nn.sigmoid(vec(RV_A0) + _bdot(_bdot(xa, a1_ref[...]), a2_ref[...]))
    g_ref[...] = _bdot(jax.nn.sigmoid(_bdot(xg, g1_ref[...])), g2_ref[...])
    kk = k * vec(RV_KK)
    first_head = lax.broadcasted_iota(jnp.int32, (kk.shape[0], RW_PAIR), 1) < RW_HEAD_DIM
    for p in range(RW_HEADS // 2):
        sl = slice(p * RW_PAIR, (p + 1) * RW_PAIR)
        kk_p = kk[:, sl]
        kk_p = kk_p / jnp.maximum(jnp.sqrt(_pair_sum(kk_p * kk_p, first_head)), 1e-12)
        kk_ref[:, sl] = kk_p
        b_ref[:, sl] = kk_p * a[:, sl]
    ro_ref[...] = r
    ko_ref[...] = k * (1.0 + (a - 1.0) * vec(RV_KA))
    vo_ref[...] = v


def _rwkv_prep(rw, vecs, l, p):
    s = rw.shape[0]
    tm = min(RW_PREP_ROWS, s)
    per = tm // SUBLANES
    cur = lambda j: pl.BlockSpec((tm, RW_WIDTH), lambda i: (i, j))
    prev = lambda j: pl.BlockSpec((SUBLANES, RW_WIDTH), lambda i: (jnp.maximum(i * per - 1, 0), j))
    lora = lambda w: pl.BlockSpec((None,) + w.shape[1:], lambda i: (l, 0, 0))
    ws = [p['rw_w1'], p['rw_w2'], p['rw_a1'], p['rw_a2'], p['rw_g1'], p['rw_g2']]
    out = pl.BlockSpec((tm, RW_WIDTH), lambda i: (i, 0))
    return pl.pallas_call(
        _rwkv_prep_kernel,
        grid=(s // tm,),
        in_specs=[cur(0), cur(1), cur(2), cur(3), prev(0), prev(1), prev(2), prev(3),
                  pl.BlockSpec((RV_ROWS, RW_WIDTH), lambda i: (0, 0))] + [lora(w) for w in ws],
        out_specs=[out] * 7,
        out_shape=[jax.ShapeDtypeStruct((s, RW_WIDTH), F32)] * 7,
        compiler_params=_params("arbitrary"),
        name="rwkv_prep",
    )(rw, rw, rw, rw, rw, rw, rw, rw, vecs, *ws)


def _rwkv_kernel(r_ref, lw_ref, k_ref, v_ref, kk_ref, b_ref, g_ref, vec_ref, o_ref, zt_ref):
    @pl.when(pl.program_id(0) == 0)
    def _():
        zt_ref[...] = jnp.zeros_like(zt_ref)

    c, n2 = RW_CHUNK, RW_PAIR
    tri = (lax.broadcasted_iota(jnp.int32, (c, c), 1) <= lax.broadcasted_iota(jnp.int32, (c, c), 0)).astype(BF16)
    row = lax.broadcasted_iota(jnp.int32, (n2, n2), 0)
    col = lax.broadcasted_iota(jnp.int32, (n2, n2), 1)
    t_idx, s_idx = row % c, col % c
    strict = t_idx > s_idx
    incl = t_idx >= s_idx
    diag_blk = (row // RW_SUB) == (col // RW_SUB)
    eye = (row == col).astype(F32)
    first_head = lax.broadcasted_iota(jnp.int32, (c, n2), 1) < RW_HEAD_DIM

    def embed(x):
        return jnp.concatenate([jnp.where(first_head, x, 0.0), jnp.where(first_head, 0.0, x)], axis=0)

    pairs = range(RW_HEADS // 2)
    sls = [slice(p * n2, (p + 1) * n2) for p in pairs]
    bdot = lambda xs, ys: _each(_bdot, xs, ys)

    lhs, rhs, v_e, kb, gam = [], [], [], [], []
    for ck in range(RW_CHUNKS_PER_STEP):
        rs = slice(ck * c, (ck + 1) * c)
        lw = lw_ref[rs, :]
        lw_hi = lw.astype(BF16)
        rem = lw - lw_hi.astype(F32)
        lw_mid = rem.astype(BF16)
        lw_lo = (rem - lw_mid.astype(F32)).astype(BF16)
        g_in = (jnp.dot(tri, lw_hi, preferred_element_type=F32) + jnp.dot(tri, lw_mid, preferred_element_type=F32)
                + jnp.dot(tri, lw_lo, preferred_element_type=F32))
        g_last = g_in[c - 1:c, :]
        e_neg = jnp.exp(-g_in)
        e_end = jnp.exp(g_last - g_in)
        kkd = kk_ref[rs, :] * jnp.exp(g_in - lw)
        rd = r_ref[rs, :] * jnp.exp(g_in)
        kinv = k_ref[rs, :] * e_neg
        binv = b_ref[rs, :] * e_neg
        kd = k_ref[rs, :] * e_end
        bd = b_ref[rs, :] * e_end
        gam.append(jnp.exp(g_last))
        v_e += [embed(v_ref[rs, sl]) for sl in sls]
        lhs += [jnp.concatenate([embed(kkd[:, sl]), embed(rd[:, sl])], axis=0).astype(BF16) for sl in sls]
        rhs += [jnp.concatenate([embed(kinv[:, sl]), embed(binv[:, sl])], axis=0).astype(BF16) for sl in sls]
        kb += [jnp.concatenate([embed(kd[:, sl]), -embed(bd[:, sl])], axis=0).astype(BF16) for sl in sls]
    v_b = [x.astype(BF16) for x in v_e]
    aa = _each(_bdot_nt, lhs, rhs)
    a_kkk = [jnp.where(strict, x[:n2, :n2], 0.0).astype(BF16) for x in aa]
    a_kkb = [jnp.where(strict, x[:n2, n2:], 0.0) for x in aa]
    a_rk = [jnp.where(incl, x[n2:, :n2], 0.0).astype(BF16) for x in aa]
    a_rb = [jnp.where(incl, x[n2:, n2:], 0.0).astype(BF16) for x in aa]
    nd = [jnp.where(diag_blk, x, 0.0) for x in a_kkb]
    off = [jnp.where(diag_blk, 0.0, x) for x in a_kkb]
    nd2 = bdot(nd, nd)
    akv = bdot(a_kkk, v_b)
    o_v = bdot(a_rk, v_b)
    nd4 = bdot(nd2, nd2)
    p1 = bdot([eye - x for x in nd], [eye + x for x in nd2])
    nd8 = bdot(nd4, nd4)
    p2 = bdot(p1, [eye + x for x in nd4])
    d_inv = bdot(p2, [eye + x for x in nd8])
    e1 = bdot(d_inv, off)
    e2 = bdot(e1, e1)
    t_inv = [x.astype(BF16) for x in bdot(bdot([eye - x for x in e1], [eye + x for x in e2]), d_inv)]

    inv_n = 1.0 / RW_HEAD_DIM
    zt = [zt_ref[p] for p in pairs]
    for ck in range(RW_CHUNKS_PER_STEP):
        rs = slice(ck * c, (ck + 1) * c)
        at = lambda xs: xs[ck * len(sls):(ck + 1) * len(sls)]
        zt_b = [x.astype(BF16) for x in zt]
        x1 = _each(lambda a, z: _bdot_nt(a[:n2], z), at(lhs), zt_b)
        o_z = _each(lambda a, z: _bdot_nt(a[n2:], z), at(lhs), zt_b)
        u = bdot(at(t_inv), [a + b for a, b in zip(x1, at(akv))])
        o_u = bdot(at(a_rb), u)
        vu_t = [jnp.transpose(jnp.concatenate([a, b], axis=0)) for a, b in zip(at(v_e), u)]
        z_up = bdot(vu_t, at(kb))
        zt = [z * gam[ck][:, sl] + zu for z, sl, zu in zip(zt, sls, z_up)]
        for p in pairs:
            sl = sls[p]
            o_e = o_z[p] + at(o_v)[p] - o_u[p]
            out = o_e[:c] + o_e[c:]
            mu = _pair_sum(out, first_head) * inv_n
            dev = out - mu
            var = _pair_sum(dev * dev, first_head) * inv_n
            normed = (dev * lax.rsqrt(var + RW_GN_EPS) * vec_ref[RV_GN_G:RV_GN_G + 1, sl]
                      + vec_ref[RV_GN_B:RV_GN_B + 1, sl])
            bonus = _pair_sum(r_ref[rs, sl] * k_ref[rs, sl] * vec_ref[RV_RK:RV_RK + 1, sl], first_head) * v_ref[rs, sl]
            o_ref[rs, sl] = ((normed + bonus) * g_ref[rs, sl]).astype(o_ref.dtype)
    for p in pairs:
        zt_ref[p] = zt[p]


def _rwkv_scan(r, lw, k, v, kk, b, g, vecs):
    s, width = r.shape
    rows = RW_CHUNK * RW_CHUNKS_PER_STEP
    blk = pl.BlockSpec((rows, width), lambda i: (i, 0))
    return pl.pallas_call(
        _rwkv_kernel,
        grid=(s // rows,),
        in_specs=[blk] * 7 + [pl.BlockSpec((RV_ROWS, width), lambda i: (0, 0))],
        out_specs=blk,
        out_shape=jax.ShapeDtypeStruct((s, width), BF16),
        scratch_shapes=[pltpu.VMEM((RW_HEADS // 2, RW_PAIR, RW_PAIR), F32)],
        compiler_params=_params("arbitrary"),
        name="rwkv_scan",
    )(r, lw, k, v, kk, b, g, vecs)


def _rwkv7(rw, l, p):
    rows = [p['rw_mu'][l], p['rw_w0'][l][None], p['rw_a0'][l][None], p['rw_kk'][l][None], p['rw_ka'][l][None],
            p['rw_gn_g'][l][None], p['rw_gn_b'][l][None], p['rw_rk'][l].reshape(1, RW_WIDTH)]
    vecs = jnp.concatenate(rows + [jnp.zeros((RV_ROWS - RV_RK - 1, RW_WIDTH), F32)], axis=0)
    r, lw, k, v, kk, b, g = _rwkv_prep(rw, vecs, l, p)
    return _rwkv_scan(r, lw, k, v, kk, b, g, vecs)


AT_PREP_ROWS = 512
AT_SLABS = AT_WIDTH // LANES


def _rope_kernel(q_ref, k_ref, c_ref, s1_ref, s2_ref, qo_ref, ko_ref):
    cos, s_dn, s_up = c_ref[...], s1_ref[...], s2_ref[...]

    def rot(x, scale):
        outs = []
        for j in range(AT_SLABS):
            xs = x[:, j * LANES:(j + 1) * LANES]
            y = xs * cos + pltpu.roll(xs, LANES - ROPE_DIMS // 2, axis=1) * s_dn + pltpu.roll(xs, ROPE_DIMS // 2, axis=1) * s_up
            outs.append(y * scale if scale != 1.0 else y)
        return jnp.concatenate(outs, axis=-1)

    qo_ref[...] = rot(q_ref[...], AT_HEAD_DIM ** -0.5)
    ko_ref[...] = rot(k_ref[...], 1.0)


def _rope_tables(positions):
    s = positions.shape[0]
    half = ROPE_DIMS // 2
    inv_freq = ROPE_THETA ** (-jnp.arange(half, dtype=F32) * 2.0 / ROPE_DIMS)
    ang = positions.astype(F32)[:, None] * inv_freq
    cos, sin = jnp.cos(ang), jnp.sin(ang)
    rest = AT_HEAD_DIM - ROPE_DIMS
    zeros, ones = jnp.zeros((s, rest), F32), jnp.ones((s, rest), F32)
    z_half = jnp.zeros((s, half), F32)
    head = lambda parts: jnp.tile(jnp.concatenate(parts, axis=1), (1, LANES // AT_HEAD_DIM))
    return head([cos, cos, ones]), head([-sin, z_half, zeros]), head([z_half, sin, zeros])


def _rope(at, tables):
    s = at.shape[0]
    tm = min(AT_PREP_ROWS, s)
    col = lambda j: pl.BlockSpec((tm, AT_WIDTH), lambda i: (i, j))
    tab = pl.BlockSpec((tm, LANES), lambda i: (i, 0))
    return pl.pallas_call(
        _rope_kernel,
        grid=(s // tm,),
        in_specs=[col(0), col(1), tab, tab, tab],
        out_specs=[col(0)] * 2,
        out_shape=[jax.ShapeDtypeStruct((s, AT_WIDTH), F32)] * 2,
        compiler_params=_params("arbitrary"),
        name="attn_rope",
    )(at, at, *tables)


AT_GROUP_SLABS = AT_OUT_WIDTH // LANES


def _attn_kernel(*refs, dilation):
    ns = AT_GROUP_SLABS
    q_refs, kp_refs, kc_refs, vp_refs, vc_refs = (refs[i * ns:(i + 1) * ns] for i in range(5))
    o_ref, lse_ref, o_s, lse_s = refs[5 * ns:]
    n = pl.program_id(0)
    qi = lax.broadcasted_iota(jnp.int32, (AT_BLOCK, AT_BLOCK), 0)
    kj = lax.broadcasted_iota(jnp.int32, (AT_BLOCK, AT_BLOCK), 1)
    mask_c = kj <= qi
    mask_p = (kj >= qi) & (n > 0)
    per_slab = LANES // AT_HEAD_DIM
    ones_b = jnp.ones((AT_BLOCK, AT_HEAD_DIM), BF16)

    def residue(r, carry):
        rows = pl.ds(r, AT_BLOCK, stride=dilation) if dilation > 1 else slice(None)

        def heads_of(slab_refs):
            slabs = [ref[rows, :].astype(BF16) for ref in slab_refs]
            return [x[:, j * AT_HEAD_DIM:(j + 1) * AT_HEAD_DIM] for x in slabs for j in range(per_slab)]

        q, kc, kp, vc, vp = (heads_of(x) for x in (q_refs, kc_refs, kp_refs, vc_refs, vp_refs))
        s_c = [jnp.where(mask_c, lax.dot_general(a, b, NT_DIMS, preferred_element_type=F32), -jnp.inf)
               for a, b in zip(q, kc)]
        s_p = [jnp.where(mask_p, lax.dot_general(a, b, NT_DIMS, preferred_element_type=F32), -jnp.inf)
               for a, b in zip(q, kp)]
        m = [jnp.maximum(jnp.max(a, axis=-1, keepdims=True), jnp.max(b, axis=-1, keepdims=True))
             for a, b in zip(s_c, s_p)]
        p_c = [jnp.exp(a - mm).astype(BF16) for a, mm in zip(s_c, m)]
        p_p = [jnp.exp(a - mm).astype(BF16) for a, mm in zip(s_p, m)]
        aug = lambda v: jnp.concatenate([v, ones_b], axis=1)
        acc = [jnp.dot(a, aug(v), preferred_element_type=F32) + jnp.dot(b, aug(w), preferred_element_type=F32)
               for a, v, b, w in zip(p_c, vc, p_p, vp)]
        out = [x[:, :AT_HEAD_DIM] / x[:, AT_HEAD_DIM:] for x in acc]
        lse = [mm + jnp.log(x[:, AT_HEAD_DIM:]) for mm, x in zip(m, acc)]
        for t in range(ns):
            o_s[t, rows, :] = jnp.concatenate(out[t * per_slab:(t + 1) * per_slab], axis=-1)
            lse_s[t, rows, :] = jnp.concatenate(lse[t * per_slab:(t + 1) * per_slab], axis=-1)
        return carry

    if dilation > 1:
        lax.fori_loop(0, dilation, residue, 0)
    else:
        residue(0, 0)
    for t in range(ns):
        o_ref[:, t * LANES:(t + 1) * LANES] = o_s[t]
        lse_ref[:, t * LANES:(t + 1) * LANES] = lse_s[t]


def _dilated_attention(q, k, at, g, dilation):
    s = q.shape[0]
    rows = AT_BLOCK * dilation
    assert s % rows == 0
    ns = AT_GROUP_SLABS
    qk_col = g * ns
    v_col = 2 * (AT_WIDTH // LANES) + g * ns
    cur = lambda c: [pl.BlockSpec((rows, LANES), lambda n, t=t: (n, c + t)) for t in range(ns)]
    prev = lambda c: [pl.BlockSpec((rows, LANES), lambda n, t=t: (jnp.maximum(n - 1, 0), c + t)) for t in range(ns)]
    out = pl.BlockSpec((rows, AT_OUT_WIDTH), lambda n: (n, 0))
    return pl.pallas_call(
        functools.partial(_attn_kernel, dilation=dilation),
        grid=(s // rows,),
        in_specs=cur(qk_col) + prev(qk_col) + cur(qk_col) + prev(v_col) + cur(v_col),
        out_specs=[out, out],
        out_shape=[jax.ShapeDtypeStruct((s, AT_OUT_WIDTH), F32)] * 2,
        scratch_shapes=[pltpu.VMEM((ns, rows, LANES), F32)] * 2,
        compiler_params=_params("arbitrary"),
        name=f"attn_d{dilation}",
    )(*([q] * ns + [k] * (2 * ns) + [at] * (2 * ns)))


def _attn_merge_kernel(o0, o1, o2, l0, l1, l2, y_ref):
    m = jnp.maximum(jnp.maximum(l0[...], l1[...]), l2[...])
    e0, e1, e2 = jnp.exp(l0[...] - m), jnp.exp(l1[...] - m), jnp.exp(l2[...] - m)
    tot = e0 + e1 + e2
    y_ref[...] = ((e0 / tot) * o0[...] + (e1 / tot) * o1[...] + (e2 / tot) * o2[...]).astype(BF16)


def _attention(at, tables):
    s = at.shape[0]
    q, k = _rope(at, tables)
    outs, lses = [], []
    for g, (_, dilation) in enumerate(AT_GROUPS):
        o, lse = _dilated_attention(q, k, at, g, dilation)
        outs.append(o)
        lses.append(lse)
    tm = min(AT_PREP_ROWS, s)
    blk = pl.BlockSpec((tm, AT_OUT_WIDTH), lambda i: (i, 0))
    return pl.pallas_call(
        _attn_merge_kernel,
        grid=(s // tm,),
        in_specs=[blk] * 6,
        out_specs=blk,
        out_shape=jax.ShapeDtypeStruct((s, AT_OUT_WIDTH), BF16),
        compiler_params=_params("arbitrary"),
        name="attn_merge",
    )(*outs, *lses)


ML_CHUNKS_PER_STEP = 8
ML_ROWS = ML_CHUNKS_PER_STEP * ML_CHUNK
ML_GROUP = 4


def _mlstm_kernel(u_ref, v_ref, og_ref, cw_ref, bb_ref, ib_ref, brow_ref, irow_ref, g_ref, y_ref,
                  qk_ref, ubuf_ref, ct_ref, m_ref):
    @pl.when(pl.program_id(0) == 0)
    def _():
        ubuf_ref[:SUBLANES, :] = jnp.zeros((SUBLANES, ubuf_ref.shape[1]), F32)
        ct_ref[...] = jnp.zeros_like(ct_ref)
        m_ref[...] = jnp.full_like(m_ref, -jnp.inf)

    taps = cw_ref.shape[0]
    ubuf_ref[SUBLANES:, :] = u_ref[...]
    conv = ubuf_ref[SUBLANES:, :] * cw_ref[taps - 1:taps, :]
    for back in range(1, taps):
        conv = conv + ubuf_ref[pl.ds(SUBLANES - back, ML_ROWS), :] * cw_ref[taps - 1 - back:taps - back, :]
    qk_ref[...] = conv * jax.nn.sigmoid(conv)
    ubuf_ref[:SUBLANES, :] = ubuf_ref[ML_ROWS:, :]

    dh, lc = ML_HEAD_DIM, ML_CHUNK
    si = lax.broadcasted_iota(jnp.int32, (lc, lc), 0)
    ji = lax.broadcasted_iota(jnp.int32, (lc, lc), 1)
    causal = ji <= si
    ones_b = jnp.ones((lc, dh), BF16)

    def group(c, rows, heads):
        cols = [slice(h * dh, (h + 1) * dh) for h in heads]
        q = [qk_ref[rows, cl] * (dh ** -0.5) for cl in cols]
        k = [qk_ref[rows, pl.ds(ML_WIDTH + h * dh, dh)] for h in heads]
        v = [v_ref[rows, cl] for cl in cols]
        bb = [bb_ref[rows, cl] for cl in cols]
        ib = [ib_ref[rows, cl] for cl in cols]
        b_row = [brow_ref[c, h:h + 1, :] for h in heads]
        i_row = [irow_ref[c, h:h + 1, :] for h in heads]
        m_prev = [m_ref[h, 0:1, :] for h in heads]
        ct_prev = [ct_ref[h] for h in heads]
        qb = [x.astype(BF16) for x in q]
        kb = [x.astype(BF16) for x in k]
        s_qk = _each(_bdot_nt, qb, kb)
        inter = _each(_bdot, qb, ct_prev)
        k_t = [jnp.transpose(x).astype(BF16) for x in k]
        log_d = [jnp.where(causal, b[:, :lc] - br + ir, -jnp.inf) for b, br, ir in zip(bb, b_row, i_row)]
        a_log = [b + mp for b, mp in zip(bb, m_prev)]
        m_s = [jnp.maximum(al, jnp.max(ld, axis=-1, keepdims=True)) for al, ld in zip(a_log, log_d)]
        inter_w = [jnp.exp(al - ms) for al, ms in zip(a_log, m_s)]
        qk = [s * jnp.exp(ld - ms[:, :lc]) for s, ld, ms in zip(s_qk, log_d, m_s)]
        intra = [_bdot(a, jnp.concatenate([x.astype(BF16), ones_b], axis=1)) for a, x in zip(qk, v)]
        b_last = [b[lc - 1:lc, :] for b in bb]
        w_end = [bl - b + i for bl, b, i in zip(b_last, bb, ib)]
        m_new = [jnp.maximum(bl + mp, jnp.max(we, axis=0, keepdims=True)) for bl, mp, we in zip(b_last, m_prev, w_end)]
        dec = [jnp.exp(bl + mp - mn) for bl, mp, mn in zip(b_last, m_prev, m_new)]
        wts = [jnp.exp(we - mn) for we, mn in zip(w_end, m_new)]
        upd = [_bdot(kt, jnp.concatenate([x * w, w], axis=1)) for kt, x, w in zip(k_t, v, wts)]
        for h, ctp, d, up, mn in zip(heads, ct_prev, dec, upd, m_new):
            ct_ref[h] = jnp.concatenate([d, d], axis=1) * ctp + up
            m_ref[h, 0:1, :] = mn
        num = [w * a[:, :dh] + b[:, :dh] for w, a, b in zip(inter_w, inter, intra)]
        den = [w * a[:, dh:] + b[:, dh:] for w, a, b in zip(inter_w, inter, intra)]
        hid = [n / jnp.maximum(jnp.abs(d), jnp.exp(-ms)) for n, d, ms in zip(num, den, m_s)]
        hid = [x * jax.nn.sigmoid(og_ref[rows, cl]) for x, cl in zip(hid, cols)]
        mu = [jnp.mean(x, axis=-1, keepdims=True) for x in hid]
        dev = [x - m for x, m in zip(hid, mu)]
        var = [jnp.mean(x * x, axis=-1, keepdims=True) for x in dev]
        for cl, x, vr in zip(cols, dev, var):
            y_ref[rows, cl] = (x * lax.rsqrt(vr + ML_NORM_EPS) * g_ref[:, cl]).astype(y_ref.dtype)

    def chunk(c, carry):
        rows = pl.ds(pl.multiple_of(c * lc, lc), lc)
        for h0 in range(0, ML_HEADS, ML_GROUP):
            group(c, rows, range(h0, h0 + ML_GROUP))
        return carry

    lax.fori_loop(0, ML_CHUNKS_PER_STEP, chunk, 0)


def _mlstm(ml, gates_if, l, p):
    s = ml.shape[0]
    nc = s // ML_CHUNK
    i_pre = gates_if[:, :ML_HEADS] + p['ml_gate_b'][l, 0]
    f_pre = gates_if[:, ML_HEADS:] + p['ml_gate_b'][l, 1]
    lf = jax.nn.log_sigmoid(f_pre)
    b_cum = jnp.cumsum(lf.reshape(nc, ML_CHUNK, ML_HEADS), axis=1)
    b_row = jnp.transpose(b_cum, (0, 2, 1))
    i_row = jnp.transpose(i_pre.reshape(nc, ML_CHUNK, ML_HEADS), (0, 2, 1))
    over_lanes = lambda a: jnp.repeat(a.reshape(s, ML_HEADS), ML_HEAD_DIM, axis=1)
    col = lambda j: pl.BlockSpec((ML_ROWS, ML_WIDTH), lambda i: (i, j))
    grow = pl.BlockSpec((ML_CHUNKS_PER_STEP, ML_HEADS, ML_CHUNK), lambda i: (i, 0, 0))
    conv_w = p['ml_conv'][l]
    return pl.pallas_call(
        _mlstm_kernel,
        grid=(s // ML_ROWS,),
        in_specs=[pl.BlockSpec((ML_ROWS, 2 * ML_WIDTH), lambda i: (i, 0)), col(2), col(3),
                  pl.BlockSpec(conv_w.shape, lambda i: (0, 0)), col(0), col(0), grow, grow,
                  pl.BlockSpec((1, ML_WIDTH), lambda i: (0, 0))],
        out_specs=pl.BlockSpec((ML_ROWS, ML_WIDTH), lambda i: (i, 0)),
        out_shape=jax.ShapeDtypeStruct((s, ML_WIDTH), BF16),
        scratch_shapes=[pltpu.VMEM((ML_ROWS, 2 * ML_WIDTH), F32),
                        pltpu.VMEM((ML_ROWS + SUBLANES, 2 * ML_WIDTH), F32),
                        pltpu.VMEM((ML_HEADS, ML_HEAD_DIM, 2 * ML_HEAD_DIM), F32),
                        pltpu.VMEM((ML_HEADS, SUBLANES, ML_HEAD_DIM), F32)],
        compiler_params=_params("arbitrary"),
        name="mlstm",
    )(ml, ml, ml, conv_w, over_lanes(b_cum), over_lanes(i_pre), b_row, i_row,
      p['ml_ln_g'][l].reshape(1, ML_WIDTH))


def _merge_kernel(x_ref, yr_ref, ya_ref, ym_ref, g0_ref, g1_ref, g2_ref, w0_ref, w1_ref, w2_ref, o_ref, gb_ref):
    @pl.when(pl.program_id(1) == 0)
    def _():
        for b, g_ref in enumerate((g0_ref, g1_ref, g2_ref)):
            gb_ref[b] = jnp.transpose(g_ref[0]).astype(BF16)

    x = x_ref[...]
    gate = lambda b: jax.nn.sigmoid(jnp.dot(x, gb_ref[b], preferred_element_type=F32))
    y = (gate(0) * jnp.dot(yr_ref[...], w0_ref[...], preferred_element_type=F32)
         + gate(1) * jnp.dot(ya_ref[...], w1_ref[...], preferred_element_type=F32)
         + gate(2) * jnp.dot(ym_ref[...], w2_ref[...], preferred_element_type=F32))
    o_ref[...] = y.astype(o_ref.dtype)


def _merge(xb, y_rw, y_at, y_ml, w_in_t, l, wb_rw, wb_at, wb_ml, tm=512, tn=512):
    m, d = xb.shape
    nj = d // tn
    act = lambda a: pl.BlockSpec((tm, a.shape[1]), lambda j, i: (i, 0))
    gate = lambda b: pl.BlockSpec((pl.Element(1), pl.Element(tn), pl.Element(d)),
                                  lambda j, i: (l, pl.multiple_of(COL_GATE + b * d + j * tn, SUBLANES), 0))
    wsp = lambda w: pl.BlockSpec((w.shape[0], tn), lambda j, i: (0, j))
    return pl.pallas_call(
        _merge_kernel,
        grid=(nj, m // tm),
        in_specs=[act(xb), act(y_rw), act(y_at), act(y_ml), gate(0), gate(1), gate(2),
                  wsp(wb_rw), wsp(wb_at), wsp(wb_ml)],
        out_specs=pl.BlockSpec((tm, tn), lambda j, i: (i, j)),
        out_shape=jax.ShapeDtypeStruct((m, d), BF16),
        scratch_shapes=[pltpu.VMEM((N_BRANCHES, d, tn), BF16)],
        compiler_params=_params("arbitrary", "arbitrary"),
        name="merge",
    )(xb, y_rw, y_at, y_ml, w_in_t, w_in_t, w_in_t, wb_rw, wb_at, wb_ml)


MOE_LN_ROWS = 256
MOE_LN_AHEAD = 2


def _row_gather(idx_ref, base, n, src_hbm, dst, sem, unroll=False):
    def start(j):
        pltpu.make_async_copy(src_hbm.at[pl.ds(idx_ref[base + j], 1)], dst.at[pl.ds(j, 1)], sem).start()

    if unroll:
        for j in range(n):
            start(j)
    else:
        lax.fori_loop(0, n, lambda j, carry: (start(j), carry)[1], 0, unroll=8)


def _row_gather_wait(n, src_hbm, dst, sem):
    pltpu.make_async_copy(src_hbm.at[pl.ds(0, n)], dst.at[pl.ds(0, n)], sem).wait()


MOE_AHEAD = 3


def _expert_kernel(be_ref, new_ref, used_ref, src_ref, x_hbm, w1_ref, w3_ref, w2_ref, o_ref,
                   xbuf, w1b, w3b, w2b, sem):
    i = pl.program_id(0)
    slots = MOE_AHEAD + 1
    slot = i % slots

    @pl.when(i == 0)
    def _():
        for j in range(MOE_AHEAD):
            _row_gather(src_ref, j * MOE_ROWS, MOE_ROWS, x_hbm, xbuf.at[j], sem.at[j])

    @pl.when(new_ref[i] == 1)
    def _():
        w1b[...] = w1_ref[...].astype(BF16)
        w3b[...] = w3_ref[...].astype(BF16)
        w2b[...] = w2_ref[...].astype(BF16)

    @pl.when(used_ref[i] == 1)
    def _():
        _row_gather_wait(MOE_ROWS, x_hbm, xbuf.at[slot], sem.at[slot])
        ahead = (i + MOE_AHEAD) % slots
        _row_gather(src_ref, (i + MOE_AHEAD) * MOE_ROWS, MOE_ROWS, x_hbm, xbuf.at[ahead], sem.at[ahead], unroll=True)
        x = xbuf[slot].astype(BF16)
        h1 = jnp.dot(x, w1b[...], preferred_element_type=F32)
        h3 = jnp.dot(x, w3b[...], preferred_element_type=F32)
        hid = (h1 * jax.nn.sigmoid(h1) * h3).astype(BF16)
        o_ref[...] = jnp.dot(hid, w2b[...], preferred_element_type=F32)

    @pl.when(used_ref[i] == 0)
    def _():
        @pl.when((i < MOE_AHEAD) | (used_ref[jnp.maximum(i - MOE_AHEAD, 0)] == 1))
        def _():
            _row_gather_wait(MOE_ROWS, x_hbm, xbuf.at[slot], sem.at[slot])

        o_ref[...] = jnp.zeros_like(o_ref)


def _experts(x, src, blk_expert, blk_new, blk_used, w1, w3, w2, l):
    rows = src.shape[0]
    d = x.shape[1]
    nblk = rows // MOE_ROWS
    up = pl.BlockSpec((None, None, d, MOE_FF), lambda i, be, nw, us, sr: (l, be[i], 0, 0))
    down = pl.BlockSpec((None, None, MOE_FF, d), lambda i, be, nw, us, sr: (l, be[i], 0, 0))
    return pl.pallas_call(
        _expert_kernel,
        grid_spec=pltpu.PrefetchScalarGridSpec(
            num_scalar_prefetch=4,
            grid=(nblk,),
            in_specs=[pl.BlockSpec(memory_space=pl.ANY), up, up, down],
            out_specs=pl.BlockSpec((MOE_ROWS, d), lambda i, be, nw, us, sr: (i, 0)),
            scratch_shapes=[pltpu.VMEM((MOE_AHEAD + 1, MOE_ROWS, d), F32),
                            pltpu.VMEM((d, MOE_FF), BF16), pltpu.VMEM((d, MOE_FF), BF16),
                            pltpu.VMEM((MOE_FF, d), BF16),
                            pltpu.SemaphoreType.DMA((MOE_AHEAD + 1,))]),
        out_shape=jax.ShapeDtypeStruct((rows, d), F32),
        compiler_params=_params("arbitrary"),
        name="experts",
    )(blk_expert, blk_new, blk_used, src, x, w1, w3, w2)


def _moe_ln_kernel(pos_ref, x_ref, w_ref, y_hbm, g_ref, b_ref, o_ref, ob_ref, ybuf, sem):
    i = pl.program_id(0)
    nblk = pl.num_programs(0)
    n = MOE_TOP_K * MOE_LN_ROWS
    tm = MOE_LN_ROWS
    slots = MOE_LN_AHEAD + 1
    slot = i % slots

    @pl.when(i == 0)
    def _():
        for j in range(MOE_LN_AHEAD):
            _row_gather(pos_ref, j * n, n, y_hbm, ybuf.at[j], sem.at[j])

    _row_gather_wait(n, y_hbm, ybuf.at[slot], sem.at[slot])
    ahead = (i + MOE_LN_AHEAD) % slots
    _row_gather(pos_ref, (i + MOE_LN_AHEAD) * n, n, y_hbm, ybuf.at[ahead], sem.at[ahead], unroll=True)
    rows = ybuf.at[slot]
    h = rows[0:tm, :] * w_ref[:, 0:1] + rows[tm:2 * tm, :] * w_ref[:, 1:2]
    out = _layer_norm(DN_ALPHA * x_ref[...] + h, g_ref[...], b_ref[...])
    o_ref[...] = out
    ob_ref[...] = out.astype(BF16)

    @pl.when(i == nblk - 1)
    def _():
        for j in range(1, MOE_LN_AHEAD + 1):
            _row_gather_wait(n, y_hbm, ybuf.at[(i + j) % slots], sem.at[(i + j) % slots])


def _moe_ln(x, y, pos, top_w, g, b):
    t, d = x.shape
    tm = MOE_LN_ROWS
    pos = jnp.transpose(pos.reshape(t // tm, tm, MOE_TOP_K), (0, 2, 1)).reshape(t * MOE_TOP_K)
    pos = jnp.concatenate([pos, jnp.zeros((MOE_LN_AHEAD * tm * MOE_TOP_K,), jnp.int32)])
    row = pl.BlockSpec((tm, d), lambda i, pos: (i, 0))
    vec = pl.BlockSpec((1, d), lambda i, pos: (0, 0))
    return pl.pallas_call(
        _moe_ln_kernel,
        grid_spec=pltpu.PrefetchScalarGridSpec(
            num_scalar_prefetch=1,
            grid=(t // tm,),
            in_specs=[row, pl.BlockSpec((tm, MOE_TOP_K), lambda i, pos: (i, 0)),
                      pl.BlockSpec(memory_space=pl.ANY), vec, vec],
            out_specs=[row, row],
            scratch_shapes=[pltpu.VMEM((MOE_LN_AHEAD + 1, MOE_TOP_K * tm, d), F32),
                            pltpu.SemaphoreType.DMA((MOE_LN_AHEAD + 1,))]),
        out_shape=[jax.ShapeDtypeStruct((t, d), F32), jax.ShapeDtypeStruct((t, d), BF16)],
        compiler_params=_params("arbitrary"),
        name="moe_ln",
    )(pos, x, top_w, y, g.reshape(1, d), b.reshape(1, d))


def _moe(x, l, p, ln_g, ln_b):
    t, d = x.shape
    n_assign = t * MOE_TOP_K
    w_router = jnp.concatenate([p['moe_w_group'][l], p['moe_w_expert'][l]], axis=1)
    w_router = jnp.pad(w_router, ((0, 0), (0, LANES - w_router.shape[1])))
    logits = _mm_split(x, w_router, name="mm_router")
    g_logits = logits[:, :MOE_GROUPS] + p['moe_b_group'][l]
    g_prob = jax.nn.softmax(g_logits, -1)
    g_idx = jnp.argmax(g_logits, -1)
    g_w = jnp.take_along_axis(g_prob, g_idx[:, None], axis=1)
    e_logits = (logits[:, MOE_GROUPS:MOE_GROUPS + MOE_EXPERTS] + p['moe_b_expert'][l]).reshape(
        t, MOE_GROUPS, MOE_EXPERTS_PER_GROUP)
    e_sel = jnp.take_along_axis(e_logits, g_idx[:, None, None], axis=1)[:, 0]
    top_l, top_i = lax.top_k(e_sel, MOE_TOP_K)
    top_w = jax.nn.softmax(top_l, -1) * g_w
    e_flat = (g_idx[:, None] * MOE_EXPERTS_PER_GROUP + top_i).reshape(n_assign).astype(jnp.int32)
    onehot = (e_flat[:, None] == jnp.arange(MOE_EXPERTS, dtype=jnp.int32)[None]).astype(jnp.int32)
    rank = jnp.take_along_axis(jnp.cumsum(onehot, axis=0), e_flat[:, None], axis=1)[:, 0] - 1
    counts = jnp.sum(onehot, axis=0)
    pcounts = (counts + MOE_ROWS - 1) // MOE_ROWS * MOE_ROWS
    pends = jnp.cumsum(pcounts)
    pstarts = pends - pcounts
    pos = pstarts[e_flat] + rank
    n_rows = n_assign + (MOE_EXPERTS + MOE_AHEAD) * MOE_ROWS
    nblk = n_rows // MOE_ROWS
    src = jnp.zeros((n_rows,), jnp.int32).at[pos].set(jnp.arange(n_assign, dtype=jnp.int32) // MOE_TOP_K)
    blk_start = jnp.arange(nblk, dtype=jnp.int32) * MOE_ROWS
    blk_expert = jnp.clip(jnp.searchsorted(pends, blk_start, side='right'), 0, MOE_EXPERTS - 1).astype(jnp.int32)
    blk_used = (blk_start < pends[-1]).astype(jnp.int32)
    blk_new = jnp.concatenate([jnp.ones((1,), jnp.int32), (blk_expert[1:] != blk_expert[:-1]).astype(jnp.int32)])
    y = _experts(x, src, blk_expert, blk_new, blk_used, p['moe_w1'], p['moe_w3'], p['moe_w2'], l)
    return _moe_ln(x, y, pos.astype(jnp.int32).reshape(t, MOE_TOP_K), top_w, ln_g, ln_b)


def _mixer(xf, xb, tables, l, p, ln_g, ln_b):
    w_in_t = p['w_in_t']
    bf = lambda w: w.astype(BF16)
    rw = _mm_t(xb, w_in_t, layer=l, row0=COL_RW, n_rows=COL_AT - COL_RW, tn=1024, name="mm_in_rw")
    at = _mm_t(xb, w_in_t, layer=l, row0=COL_AT, n_rows=COL_ML - COL_AT, tn=AT_WIDTH, name="mm_in_at")
    ml = _mm_t(xb, w_in_t, layer=l, row0=COL_ML, n_rows=COL_IF - COL_ML, tn=1024, name="mm_in_ml")
    gates_if = _mm_t(xb, w_in_t, layer=l, row0=COL_IF, n_rows=COL_GATE - COL_IF, name="mm_in_if")
    y_rw = _rwkv7(rw, l, p)
    y_at = _attention(at, tables)
    y_ml = _mlstm(ml, gates_if, l, p)
    wb = p['w_branch'][l]
    y = _merge(xb, y_rw, y_at, y_ml, w_in_t, l, bf(wb[:RW_WIDTH]),
               bf(wb[RW_WIDTH:RW_WIDTH + AT_OUT_WIDTH]), bf(wb[RW_WIDTH + AT_OUT_WIDTH:]))
    return _out_ln(y, bf(p['w_out'][l]), xf, ln_g, ln_b)


def kernel(x, positions, w_in, rw_mu, rw_w0, rw_w1, rw_w2, rw_a0, rw_a1, rw_a2, rw_g1, rw_g2, rw_kk, rw_ka, rw_rk, rw_gn_g, rw_gn_b, ml_conv, ml_gate_b, ml_ln_g, w_branch, w_out, ln1_g, ln1_b, moe_w_group, moe_b_group, moe_w_expert, moe_b_expert, moe_w1, moe_w3, moe_w2, ln2_g, ln2_b):
    p = dict(w_in_t=jnp.swapaxes(w_in, 1, 2), rw_mu=rw_mu, rw_w0=rw_w0, rw_w1=rw_w1, rw_w2=rw_w2, rw_a0=rw_a0, rw_a1=rw_a1,
             rw_a2=rw_a2, rw_g1=rw_g1, rw_g2=rw_g2, rw_kk=rw_kk, rw_ka=rw_ka, rw_rk=rw_rk, rw_gn_g=rw_gn_g,
             rw_gn_b=rw_gn_b, ml_conv=ml_conv, ml_gate_b=ml_gate_b, ml_ln_g=ml_ln_g, w_branch=w_branch,
             w_out=w_out, moe_w_group=moe_w_group, moe_b_group=moe_b_group, moe_w_expert=moe_w_expert,
             moe_b_expert=moe_b_expert, moe_w1=moe_w1, moe_w3=moe_w3, moe_w2=moe_w2)
    batch, seq, d = x.shape
    assert batch == 1
    xf = x.reshape(seq, d)
    xb = xf.astype(BF16)
    tables = _rope_tables(positions.reshape(seq))
    for l in range(w_in.shape[0]):
        xf, xb = _mixer(xf, xb, tables, l, p, ln1_g[l], ln1_b[l])
        xf, xb = _moe(xf, l, p, ln2_g[l], ln2_b[l])
    return xf.reshape(batch, seq, d)
```

```python
import functools

import jax
import jax.numpy as jnp
from jax import lax
from jax.experimental import pallas as pl
from jax.experimental.pallas import tpu as pltpu

F32 = jnp.float32
BF16 = jnp.bfloat16

D_MODEL = 2048
DEPTH = 4
LN_EPS = 1e-5
DN_ALPHA = (2 * DEPTH) ** 0.25

RW_HEADS = 16
RW_HEAD_DIM = 64
RW_WIDTH = RW_HEADS * RW_HEAD_DIM
RW_GN_EPS = 64e-5

AT_GROUPS = ((128, 1), (512, 4), (2048, 16))
AT_HEADS_PER_GROUP = 6
AT_HEADS = AT_HEADS_PER_GROUP * len(AT_GROUPS)
AT_HEAD_DIM = 64
AT_WIDTH = AT_HEADS * AT_HEAD_DIM
AT_OUT_WIDTH = AT_HEADS_PER_GROUP * AT_HEAD_DIM
AT_BLOCK = 128
ROPE_DIMS = AT_HEAD_DIM // 4
ROPE_THETA = 500000.0

ML_HEADS = 8
ML_HEAD_DIM = 128
ML_WIDTH = ML_HEADS * ML_HEAD_DIM
ML_CHUNK = 64
ML_NORM_EPS = 1e-6

N_BRANCHES = 3
COL_RW = 0
COL_AT = 4 * RW_WIDTH
COL_ML = COL_AT + 3 * AT_WIDTH
COL_IF = COL_ML + 4 * ML_WIDTH
COL_GATE = COL_IF + 2 * ML_HEADS
IN_COLS = COL_GATE + N_BRANCHES * D_MODEL

MOE_GROUPS = 4
MOE_EXPERTS_PER_GROUP = 8
MOE_EXPERTS = MOE_GROUPS * MOE_EXPERTS_PER_GROUP
MOE_TOP_K = 2
MOE_FF = 512
MOE_ROWS = 256

LANES = 128
SUBLANES = 8
VMEM_LIMIT = 56 * 1024 * 1024

NT_DIMS = (((1,), (1,)), ((), ()))


def _params(*sem):
    return pltpu.CompilerParams(dimension_semantics=sem, vmem_limit_bytes=VMEM_LIMIT)


def _bdot(a, b):
    return jnp.dot(a.astype(BF16), b.astype(BF16), preferred_element_type=F32)


def _bdot_nt(a, b):
    return lax.dot_general(a.astype(BF16), b.astype(BF16), NT_DIMS, preferred_element_type=F32)


def _each(f, *xs):
    return [f(*a) for a in zip(*xs)]


def _shift_rows(x, prev_row):
    first = lax.broadcasted_iota(jnp.int32, x.shape, 0) == 0
    return jnp.where(first, prev_row, pltpu.roll(x, 1, axis=0))


def _pair_sum(x, first_head):
    lo = jnp.sum(jnp.where(first_head, x, 0.0), axis=-1, keepdims=True)
    hi = jnp.sum(jnp.where(first_head, 0.0, x), axis=-1, keepdims=True)
    return jnp.where(first_head, lo, hi)


def _mm_split_kernel(a_ref, w_ref, o_ref):
    a, w = a_ref[...], w_ref[...]
    a_hi, w_hi = a.astype(BF16), w.astype(BF16)
    a_lo = (a - a_hi.astype(F32)).astype(BF16)
    w_lo = (w - w_hi.astype(F32)).astype(BF16)
    o_ref[...] = (jnp.dot(a_hi, w_hi, preferred_element_type=F32) + jnp.dot(a_hi, w_lo, preferred_element_type=F32)
                  + jnp.dot(a_lo, w_hi, preferred_element_type=F32))


def _mm_split(a, w, tm=512, name="mm_split"):
    m, k = a.shape
    n = w.shape[1]
    return pl.pallas_call(
        _mm_split_kernel,
        grid=(m // tm,),
        in_specs=[pl.BlockSpec((tm, k), lambda i: (i, 0)), pl.BlockSpec((k, n), lambda i: (0, 0))],
        out_specs=pl.BlockSpec((tm, n), lambda i: (i, 0)),
        out_shape=jax.ShapeDtypeStruct((m, n), F32),
        compiler_params=_params("arbitrary"),
        name=name,
    )(a, w)


def _mm_t_kernel(a_ref, w_ref, o_ref, wb_ref):
    @pl.when(pl.program_id(1) == 0)
    def _():
        wb_ref[...] = jnp.transpose(w_ref[0]).astype(BF16)

    o_ref[...] = jnp.dot(a_ref[...], wb_ref[...], preferred_element_type=F32)


def _mm_t(a, w_t, *, layer, row0, n_rows, tm=1024, tn=None, name="mm_t"):
    m, k = a.shape
    tn = n_rows if tn is None else tn
    tm = min(tm, m)
    assert m % tm == 0 and n_rows % tn == 0 and row0 % SUBLANES == 0
    return pl.pallas_call(
        _mm_t_kernel,
        grid=(n_rows // tn, m // tm),
        in_specs=[pl.BlockSpec((tm, k), lambda j, i: (i, 0)),
                  pl.BlockSpec((pl.Element(1), pl.Element(tn), pl.Element(k)),
                               lambda j, i: (layer, pl.multiple_of(row0 + j * tn, SUBLANES), 0))],
        out_specs=pl.BlockSpec((tm, tn), lambda j, i: (i, j)),
        out_shape=jax.ShapeDtypeStruct((m, n_rows), F32),
        scratch_shapes=[pltpu.VMEM((k, tn), BF16)],
        compiler_params=_params("arbitrary", "arbitrary"),
        name=name,
    )(a, w_t)


def _layer_norm(y, g, b):
    mu = jnp.mean(y, axis=-1, keepdims=True)
    d = y - mu
    var = jnp.mean(d * d, axis=-1, keepdims=True)
    return d * lax.rsqrt(var + LN_EPS) * g + b


def _out_ln_kernel(y_ref, w_ref, x_ref, g_ref, b_ref, o_ref, ob_ref):
    h = jnp.dot(y_ref[...], w_ref[...], preferred_element_type=F32)
    out = _layer_norm(DN_ALPHA * x_ref[...] + h, g_ref[...], b_ref[...])
    o_ref[...] = out
    ob_ref[...] = out.astype(BF16)


def _out_ln(y, w, x, g, b, tm=256):
    m, d = x.shape
    row = pl.BlockSpec((tm, d), lambda i: (i, 0))
    vec = pl.BlockSpec((1, d), lambda i: (0, 0))
    return pl.pallas_call(
        _out_ln_kernel,
        grid=(m // tm,),
        in_specs=[row, pl.BlockSpec((d, d), lambda i: (0, 0)), row, vec, vec],
        out_specs=[row, row],
        out_shape=[jax.ShapeDtypeStruct((m, d), F32), jax.ShapeDtypeStruct((m, d), BF16)],
        compiler_params=_params("arbitrary"),
        name="out_ln",
    )(y, w, x, g.reshape(1, d), b.reshape(1, d))


RW_CHUNK = 64
RW_PAIR = 2 * RW_HEAD_DIM
RW_SUB = 16
RW_CHUNKS_PER_STEP = 2
RW_PREP_ROWS = 256
(RV_MU, RV_W0, RV_A0, RV_KK, RV_KA, RV_GN_G, RV_GN_B, RV_RK, RV_ROWS) = (0, 6, 7, 8, 9, 10, 11, 12, 16)


def _rwkv_prep_kernel(r_ref, k_ref, v_ref, z_ref, rp_ref, kp_ref, vp_ref, zp_ref, vec_ref,
                      w1_ref, w2_ref, a1_ref, a2_ref, g1_ref, g2_ref,
                      ro_ref, lw_ref, ko_ref, vo_ref, kk_ref, b_ref, g_ref):
    first_block = pl.program_id(0) == 0
    vec = lambda j: vec_ref[j:j + 1, :]
    prev = lambda ref: jnp.where(first_block, 0.0, ref[SUBLANES - 1:SUBLANES, :])
    lerp = lambda u, up, j: u + (_shift_rows(u, up) - u) * vec(RV_MU + j)
    r = lerp(r_ref[...], prev(rp_ref), 0)
    k = lerp(k_ref[...], prev(kp_ref), 1)
    v = lerp(v_ref[...], prev(vp_ref), 2)
    z = z_ref[...]
    z_diff = _shift_rows(z, prev(zp_ref)) - z
    xw, xa, xg = z + z_diff * vec(RV_MU + 3), z + z_diff * vec(RV_MU + 4), z + z_diff * vec(RV_MU + 5)
    w_pre = vec(RV_W0) + _bdot(jnp.tanh(_bdot(xw, w1_ref[...])), w2_ref[...])
    softplus = jnp.maximum(-w_pre, 0.0) + jnp.log(1.0 + jnp.exp(-jnp.abs(w_pre)))
    lw_ref[...] = -jnp.exp(-softplus - 0.5)
    a = jax.nn.sigmoid(vec(RV_A0) + _bdot(_bdot(xa, a1_ref[...]), a2_ref[...]))
    g_ref[...] = _bdot(jax.nn.sigmoid(_bdot(xg, g1_ref[...])), g2_ref[...])
    kk = k * vec(RV_KK)
    first_head = lax.broadcasted_iota(jnp.int32, (kk.shape[0], RW_PAIR), 1) < RW_HEAD_DIM
    for p in range(RW_HEADS // 2):
        sl = slice(p * RW_PAIR, (p + 1) * RW_PAIR)
        kk_p = kk[:, sl]
        kk_p = kk_p / jnp.maximum(jnp.sqrt(_pair_sum(kk_p * kk_p, first_head)), 1e-12)
        kk_ref[:, sl] = kk_p
        b_ref[:, sl] = kk_p * a[:, sl]
    ro_ref[...] = r
    ko_ref[...] = k * (1.0 + (a - 1.0) * vec(RV_KA))
    vo_ref[...] = v


def _rwkv_prep(rw, vecs, l, p):
    s = rw.shape[0]
    tm = min(RW_PREP_ROWS, s)
    per = tm // SUBLANES
    cur = lambda j: pl.BlockSpec((tm, RW_WIDTH), lambda i: (i, j))
    prev = lambda j: pl.BlockSpec((SUBLANES, RW_WIDTH), lambda i: (jnp.maximum(i * per - 1, 0), j))
    lora = lambda w: pl.BlockSpec((None,) + w.shape[1:], lambda i: (l, 0, 0))
    ws = [p['rw_w1'], p['rw_w2'], p['rw_a1'], p['rw_a2'], p['rw_g1'], p['rw_g2']]
    out = pl.BlockSpec((tm, RW_WIDTH), lambda i: (i, 0))
    return pl.pallas_call(
        _rwkv_prep_kernel,
        grid=(s // tm,),
        in_specs=[cur(0), cur(1), cur(2), cur(3), prev(0), prev(1), prev(2), prev(3),
                  pl.BlockSpec((RV_ROWS, RW_WIDTH), lambda i: (0, 0))] + [lora(w) for w in ws],
        out_specs=[out] * 7,
        out_shape=[jax.ShapeDtypeStruct((s, RW_WIDTH), F32)] * 7,
        compiler_params=_params("arbitrary"),
        name="rwkv_prep",
    )(rw, rw, rw, rw, rw, rw, rw, rw, vecs, *ws)


def _rwkv_kernel(r_ref, lw_ref, k_ref, v_ref, kk_ref, b_ref, g_ref, vec_ref, o_ref, zt_ref):
    @pl.when(pl.program_id(0) == 0)
    def _():
        zt_ref[...] = jnp.zeros_like(zt_ref)

    c, n2 = RW_CHUNK, RW_PAIR
    tri = (lax.broadcasted_iota(jnp.int32, (c, c), 1) <= lax.broadcasted_iota(jnp.int32, (c, c), 0)).astype(BF16)
    row = lax.broadcasted_iota(jnp.int32, (n2, n2), 0)
    col = lax.broadcasted_iota(jnp.int32, (n2, n2), 1)
    t_idx, s_idx = row % c, col % c
    strict = t_idx > s_idx
    incl = t_idx >= s_idx
    diag_blk = (row // RW_SUB) == (col // RW_SUB)
    eye = (row == col).astype(F32)
    first_head = lax.broadcasted_iota(jnp.int32, (c, n2), 1) < RW_HEAD_DIM

    def embed(x):
        return jnp.concatenate([jnp.where(first_head, x, 0.0), jnp.where(first_head, 0.0, x)], axis=0)

    pairs = range(RW_HEADS // 2)
    sls = [slice(p * n2, (p + 1) * n2) for p in pairs]
    bdot = lambda xs, ys: _each(_bdot, xs, ys)

    lhs, rhs, v_e, kb, gam = [], [], [], [], []
    for ck in range(RW_CHUNKS_PER_STEP):
        rs = slice(ck * c, (ck + 1) * c)
        lw = lw_ref[rs, :]
        lw_hi = lw.astype(BF16)
        rem = lw - lw_hi.astype(F32)
        lw_mid = rem.astype(BF16)
        lw_lo = (rem - lw_mid.astype(F32)).astype(BF16)
        g_in = (jnp.dot(tri, lw_hi, preferred_element_type=F32) + jnp.dot(tri, lw_mid, preferred_element_type=F32)
                + jnp.dot(tri, lw_lo, preferred_element_type=F32))
        g_last = g_in[c - 1:c, :]
        e_neg = jnp.exp(-g_in)
        e_end = jnp.exp(g_last - g_in)
        kkd = kk_ref[rs, :] * jnp.exp(g_in - lw)
        rd = r_ref[rs, :] * jnp.exp(g_in)
        kinv = k_ref[rs, :] * e_neg
        binv = b_ref[rs, :] * e_neg
        kd = k_ref[rs, :] * e_end
        bd = b_ref[rs, :] * e_end
        gam.append(jnp.exp(g_last))
        v_e += [embed(v_ref[rs, sl]) for sl in sls]
        lhs += [jnp.concatenate([embed(kkd[:, sl]), embed(rd[:, sl])], axis=0).astype(BF16) for sl in sls]
        rhs += [jnp.concatenate([embed(kinv[:, sl]), embed(binv[:, sl])], axis=0).astype(BF16) for sl in sls]
        kb += [jnp.concatenate([embed(kd[:, sl]), -embed(bd[:, sl])], axis=0).astype(BF16) for sl in sls]
    v_b = [x.astype(BF16) for x in v_e]
    aa = _each(_bdot_nt, lhs, rhs)
    a_kkk = [jnp.where(strict, x[:n2, :n2], 0.0).astype(BF16) for x in aa]
    a_kkb = [jnp.where(strict, x[:n2, n2:], 0.0) for x in aa]
    a_rk = [jnp.where(incl, x[n2:, :n2], 0.0).astype(BF16) for x in aa]
    a_rb = [jnp.where(incl, x[n2:, n2:], 0.0).astype(BF16) for x in aa]
    nd = [jnp.where(diag_blk, x, 0.0) for x in a_kkb]
    off = [jnp.where(diag_blk, 0.0, x) for x in a_kkb]
    nd2 = bdot(nd, nd)
    akv = bdot(a_kkk, v_b)
    o_v = bdot(a_rk, v_b)
    nd4 = bdot(nd2, nd2)
    p1 = bdot([eye - x for x in nd], [eye + x for x in nd2])
    nd8 = bdot(nd4, nd4)
    p2 = bdot(p1, [eye + x for x in nd4])
    d_inv = bdot(p2, [eye + x for x in nd8])
    e1 = bdot(d_inv, off)
    e2 = bdot(e1, e1)
    t_inv = [x.astype(BF16) for x in bdot(bdot([eye - x for x in e1], [eye + x for x in e2]), d_inv)]

    inv_n = 1.0 / RW_HEAD_DIM
    zt = [zt_ref[p] for p in pairs]
    for ck in range(RW_CHUNKS_PER_STEP):
        rs = slice(ck * c, (ck + 1) * c)
        at = lambda xs: xs[ck * len(sls):(ck + 1) * len(sls)]
        zt_b = [x.astype(BF16) for x in zt]
        x1 = _each(lambda a, z: _bdot_nt(a[:n2], z), at(lhs), zt_b)
        o_z = _each(lambda a, z: _bdot_nt(a[n2:], z), at(lhs), zt_b)
        u = bdot(at(t_inv), [a + b for a, b in zip(x1, at(akv))])
        o_u = bdot(at(a_rb), u)
        vu_t = [jnp.transpose(jnp.concatenate([a, b], axis=0)) for a, b in zip(at(v_e), u)]
        z_up = bdot(vu_t, at(kb))
        zt = [z * gam[ck][:, sl] + zu for z, sl, zu in zip(zt, sls, z_up)]
        for p in pairs:
            sl = sls[p]
            o_e = o_z[p] + at(o_v)[p] - o_u[p]
            out = o_e[:c] + o_e[c:]
            mu = _pair_sum(out, first_head) * inv_n
            dev = out - mu
            var = _pair_sum(dev * dev, first_head) * inv_n
            normed = (dev * lax.rsqrt(var + RW_GN_EPS) * vec_ref[RV_GN_G:RV_GN_G + 1, sl]
                      + vec_ref[RV_GN_B:RV_GN_B + 1, sl])
            bonus = _pair_sum(r_ref[rs, sl] * k_ref[rs, sl] * vec_ref[RV_RK:RV_RK + 1, sl], first_head) * v_ref[rs, sl]
            o_ref[rs, sl] = ((normed + bonus) * g_ref[rs, sl]).astype(o_ref.dtype)
    for p in pairs:
        zt_ref[p] = zt[p]


def _rwkv_scan(r, lw, k, v, kk, b, g, vecs):
    s, width = r.shape
    rows = RW_CHUNK * RW_CHUNKS_PER_STEP
    blk = pl.BlockSpec((rows, width), lambda i: (i, 0))
    return pl.pallas_call(
        _rwkv_kernel,
        grid=(s // rows,),
        in_specs=[blk] * 7 + [pl.BlockSpec((RV_ROWS, width), lambda i: (0, 0))],
        out_specs=blk,
        out_shape=jax.ShapeDtypeStruct((s, width), BF16),
        scratch_shapes=[pltpu.VMEM((RW_HEADS // 2, RW_PAIR, RW_PAIR), F32)],
        compiler_params=_params("arbitrary"),
        name="rwkv_scan",
    )(r, lw, k, v, kk, b, g, vecs)


def _rwkv7(rw, l, p):
    rows = [p['rw_mu'][l], p['rw_w0'][l][None], p['rw_a0'][l][None], p['rw_kk'][l][None], p['rw_ka'][l][None],
            p['rw_gn_g'][l][None], p['rw_gn_b'][l][None], p['rw_rk'][l].reshape(1, RW_WIDTH)]
    vecs = jnp.concatenate(rows + [jnp.zeros((RV_ROWS - RV_RK - 1, RW_WIDTH), F32)], axis=0)
    r, lw, k, v, kk, b, g = _rwkv_prep(rw, vecs, l, p)
    return _rwkv_scan(r, lw, k, v, kk, b, g, vecs)


AT_PREP_ROWS = 512
AT_SLABS = AT_WIDTH // LANES


def _rope_kernel(q_ref, k_ref, c_ref, s1_ref, s2_ref, qo_ref, ko_ref):
    cos, s_dn, s_up = c_ref[...], s1_ref[...], s2_ref[...]

    def rot(x, scale):
        outs = []
        for j in range(AT_SLABS):
            xs = x[:, j * LANES:(j + 1) * LANES]
            y = xs * cos + pltpu.roll(xs, LANES - ROPE_DIMS // 2, axis=1) * s_dn + pltpu.roll(xs, ROPE_DIMS // 2, axis=1) * s_up
            outs.append(y * scale if scale != 1.0 else y)
        return jnp.concatenate(outs, axis=-1)

    qo_ref[...] = rot(q_ref[...], AT_HEAD_DIM ** -0.5)
    ko_ref[...] = rot(k_ref[...], 1.0)


def _rope_tables(positions):
    s = positions.shape[0]
    half = ROPE_DIMS // 2
    inv_freq = ROPE_THETA ** (-jnp.arange(half, dtype=F32) * 2.0 / ROPE_DIMS)
    ang = positions.astype(F32)[:, None] * inv_freq
    cos, sin = jnp.cos(ang), jnp.sin(ang)
    rest = AT_HEAD_DIM - ROPE_DIMS
    zeros, ones = jnp.zeros((s, rest), F32), jnp.ones((s, rest), F32)
    z_half = jnp.zeros((s, half), F32)
    head = lambda parts: jnp.tile(jnp.concatenate(parts, axis=1), (1, LANES // AT_HEAD_DIM))
    return head([cos, cos, ones]), head([-sin, z_half, zeros]), head([z_half, sin, zeros])


def _rope(at, tables):
    s = at.shape[0]
    tm = min(AT_PREP_ROWS, s)
    col = lambda j: pl.BlockSpec((tm, AT_WIDTH), lambda i: (i, j))
    tab = pl.BlockSpec((tm, LANES), lambda i: (i, 0))
    return pl.pallas_call(
        _rope_kernel,
        grid=(s // tm,),
        in_specs=[col(0), col(1), tab, tab, tab],
        out_specs=[col(0)] * 2,
        out_shape=[jax.ShapeDtypeStruct((s, AT_WIDTH), F32)] * 2,
        compiler_params=_params("arbitrary"),
        name="attn_rope",
    )(at, at, *tables)


AT_GROUP_SLABS = AT_OUT_WIDTH // LANES


def _attn_kernel(*refs, dilation):
    ns = AT_GROUP_SLABS
    q_refs, kp_refs, kc_refs, vp_refs, vc_refs = (refs[i * ns:(i + 1) * ns] for i in range(5))
    o_ref, lse_ref, o_s, lse_s = refs[5 * ns:]
    n = pl.program_id(0)
    qi = lax.broadcasted_iota(jnp.int32, (AT_BLOCK, AT_BLOCK), 0)
    kj = lax.broadcasted_iota(jnp.int32, (AT_BLOCK, AT_BLOCK), 1)
    mask_c = kj <= qi
    mask_p = (kj >= qi) & (n > 0)
    per_slab = LANES // AT_HEAD_DIM
    ones_b = jnp.ones((AT_BLOCK, AT_HEAD_DIM), BF16)

    def residue(r, carry):
        rows = pl.ds(r, AT_BLOCK, stride=dilation) if dilation > 1 else slice(None)

        def heads_of(slab_refs):
            slabs = [ref[rows, :].astype(BF16) for ref in slab_refs]
            return [x[:, j * AT_HEAD_DIM:(j + 1) * AT_HEAD_DIM] for x in slabs for j in range(per_slab)]

        q, kc, kp, vc, vp = (heads_of(x) for x in (q_refs, kc_refs, kp_refs, vc_refs, vp_refs))
        s_c = [jnp.where(mask_c, lax.dot_general(a, b, NT_DIMS, preferred_element_type=F32), -jnp.inf)
               for a, b in zip(q, kc)]
        s_p = [jnp.where(mask_p, lax.dot_general(a, b, NT_DIMS, preferred_element_type=F32), -jnp.inf)
               for a, b in zip(q, kp)]
        m = [jnp.maximum(jnp.max(a, axis=-1, keepdims=True), jnp.max(b, axis=-1, keepdims=True))
             for a, b in zip(s_c, s_p)]
        p_c = [jnp.exp(a - mm).astype(BF16) for a, mm in zip(s_c, m)]
        p_p = [jnp.exp(a - mm).astype(BF16) for a, mm in zip(s_p, m)]
        aug = lambda v: jnp.concatenate([v, ones_b], axis=1)
        acc = [jnp.dot(a, aug(v), preferred_element_type=F32) + jnp.dot(b, aug(w), preferred_element_type=F32)
               for a, v, b, w in zip(p_c, vc, p_p, vp)]
        out = [x[:, :AT_HEAD_DIM] / x[:, AT_HEAD_DIM:] for x in acc]
        lse = [mm + jnp.log(x[:, AT_HEAD_DIM:]) for mm, x in zip(m, acc)]
        for t in range(ns):
            o_s[t, rows, :] = jnp.concatenate(out[t * per_slab:(t + 1) * per_slab], axis=-1)
            lse_s[t, rows, :] = jnp.concatenate(lse[t * per_slab:(t + 1) * per_slab], axis=-1)
        return carry

    if dilation > 1:
        lax.fori_loop(0, dilation, residue, 0)
    else:
        residue(0, 0)
    for t in range(ns):
        o_ref[:, t * LANES:(t + 1) * LANES] = o_s[t]
        lse_ref[:, t * LANES:(t + 1) * LANES] = lse_s[t]


def _dilated_attention(q, k, at, g, dilation):
    s = q.shape[0]
    rows = AT_BLOCK * dilation
    assert s % rows == 0
    ns = AT_GROUP_SLABS
    qk_col = g * ns
    v_col = 2 * (AT_WIDTH // LANES) + g * ns
    cur = lambda c: [pl.BlockSpec((rows, LANES), lambda n, t=t: (n, c + t)) for t in range(ns)]
    prev = lambda c: [pl.BlockSpec((rows, LANES), lambda n, t=t: (jnp.maximum(n - 1, 0), c + t)) for t in range(ns)]
    out = pl.BlockSpec((rows, AT_OUT_WIDTH), lambda n: (n, 0))
    return pl.pallas_call(
        functools.partial(_attn_kernel, dilation=dilation),
        grid=(s // rows,),
        in_specs=cur(qk_col) + prev(qk_col) + cur(qk_col) + prev(v_col) + cur(v_col),
        out_specs=[out, out],
        out_shape=[jax.ShapeDtypeStruct((s, AT_OUT_WIDTH), F32)] * 2,
        scratch_shapes=[pltpu.VMEM((ns, rows, LANES), F32)] * 2,
        compiler_params=_params("arbitrary"),
        name=f"attn_d{dilation}",
    )(*([q] * ns + [k] * (2 * ns) + [at] * (2 * ns)))


def _attn_merge_kernel(o0, o1, o2, l0, l1, l2, y_ref):
    m = jnp.maximum(jnp.maximum(l0[...], l1[...]), l2[...])
    e0, e1, e2 = jnp.exp(l0[...] - m), jnp.exp(l1[...] - m), jnp.exp(l2[...] - m)
    tot = e0 + e1 + e2
    y_ref[...] = ((e0 / tot) * o0[...] + (e1 / tot) * o1[...] + (e2 / tot) * o2[...]).astype(BF16)


def _attention(at, tables):
    s = at.shape[0]
    q, k = _rope(at, tables)
    outs, lses = [], []
    for g, (_, dilation) in enumerate(AT_GROUPS):
        o, lse = _dilated_attention(q, k, at, g, dilation)
        outs.append(o)
        lses.append(lse)
    tm = min(AT_PREP_ROWS, s)
    blk = pl.BlockSpec((tm, AT_OUT_WIDTH), lambda i: (i, 0))
    return pl.pallas_call(
        _attn_merge_kernel,
        grid=(s // tm,),
        in_specs=[blk] * 6,
        out_specs=blk,
        out_shape=jax.ShapeDtypeStruct((s, AT_OUT_WIDTH), BF16),
        compiler_params=_params("arbitrary"),
        name="attn_merge",
    )(*outs, *lses)


ML_CHUNKS_PER_STEP = 8
ML_ROWS = ML_CHUNKS_PER_STEP * ML_CHUNK
ML_GROUP = 4


def _mlstm_kernel(u_ref, v_ref, og_ref, cw_ref, bb_ref, ib_ref, brow_ref, irow_ref, g_ref, y_ref,
                  qk_ref, ubuf_ref, ct_ref, m_ref):
    @pl.when(pl.program_id(0) == 0)
    def _():
        ubuf_ref[:SUBLANES, :] = jnp.zeros((SUBLANES, ubuf_ref.shape[1]), F32)
        ct_ref[...] = jnp.zeros_like(ct_ref)
        m_ref[...] = jnp.full_like(m_ref, -jnp.inf)

    taps = cw_ref.shape[0]
    ubuf_ref[SUBLANES:, :] = u_ref[...]
    conv = ubuf_ref[SUBLANES:, :] * cw_ref[taps - 1:taps, :]
    for back in range(1, taps):
        conv = conv + ubuf_ref[pl.ds(SUBLANES - back, ML_ROWS), :] * cw_ref[taps - 1 - back:taps - back, :]
    qk_ref[...] = conv * jax.nn.sigmoid(conv)
    ubuf_ref[:SUBLANES, :] = ubuf_ref[ML_ROWS:, :]

    dh, lc = ML_HEAD_DIM, ML_CHUNK
    si = lax.broadcasted_iota(jnp.int32, (lc, lc), 0)
    ji = lax.broadcasted_iota(jnp.int32, (lc, lc), 1)
    causal = ji <= si
    ones_b = jnp.ones((lc, dh), BF16)

    def group(c, rows, heads):
        cols = [slice(h * dh, (h + 1) * dh) for h in heads]
        q = [qk_ref[rows, cl] * (dh ** -0.5) for cl in cols]
        k = [qk_ref[rows, pl.ds(ML_WIDTH + h * dh, dh)] for h in heads]
        v = [v_ref[rows, cl] for cl in cols]
        bb = [bb_ref[rows, cl] for cl in cols]
        ib = [ib_ref[rows, cl] for cl in cols]
        b_row = [brow_ref[c, h:h + 1, :] for h in heads]
        i_row = [irow_ref[c, h:h + 1, :] for h in heads]
        m_prev = [m_ref[h, 0:1, :] for h in heads]
        ct_prev = [ct_ref[h] for h in heads]
        qb = [x.astype(BF16) for x in q]
        kb = [x.astype(BF16) for x in k]
        s_qk = _each(_bdot_nt, qb, kb)
        inter = _each(_bdot, qb, ct_prev)
        k_t = [jnp.transpose(x).astype(BF16) for x in k]
        log_d = [jnp.where(causal, b[:, :lc] - br + ir, -jnp.inf) for b, br, ir in zip(bb, b_row, i_row)]
        a_log = [b + mp for b, mp in zip(bb, m_prev)]
        m_s = [jnp.maximum(al, jnp.max(ld, axis=-1, keepdims=True)) for al, ld in zip(a_log, log_d)]
        inter_w = [jnp.exp(al - ms) for al, ms in zip(a_log, m_s)]
        qk = [s * jnp.exp(ld - ms[:, :lc]) for s, ld, ms in zip(s_qk, log_d, m_s)]
        intra = [_bdot(a, jnp.concatenate([x.astype(BF16), ones_b], axis=1)) for a, x in zip(qk, v)]
        b_last = [b[lc - 1:lc, :] for b in bb]
        w_end = [bl - b + i for bl, b, i in zip(b_last, bb, ib)]
        m_new = [jnp.maximum(bl + mp, jnp.max(we, axis=0, keepdims=True)) for bl, mp, we in zip(b_last, m_prev, w_end)]
        dec = [jnp.exp(bl + mp - mn) for bl, mp, mn in zip(b_last, m_prev, m_new)]
        wts = [jnp.exp(we - mn) for we, mn in zip(w_end, m_new)]
        upd = [_bdot(kt, jnp.concatenate([x * w, w], axis=1)) for kt, x, w in zip(k_t, v, wts)]
        for h, ctp, d, up, mn in zip(heads, ct_prev, dec, upd, m_new):
            ct_ref[h] = jnp.concatenate([d, d], axis=1) * ctp + up
            m_ref[h, 0:1, :] = mn
        num = [w * a[:, :dh] + b[:, :dh] for w, a, b in zip(inter_w, inter, intra)]
        den = [w * a[:, dh:] + b[:, dh:] for w, a, b in zip(inter_w, inter, intra)]
        hid = [n / jnp.maximum(jnp.abs(d), jnp.exp(-ms)) for n, d, ms in zip(num, den, m_s)]
        hid = [x * jax.nn.sigmoid(og_ref[rows, cl]) for x, cl in zip(hid, cols)]
        mu = [jnp.mean(x, axis=-1, keepdims=True) for x in hid]
        dev = [x - m for x, m in zip(hid, mu)]
        var = [jnp.mean(x * x, axis=-1, keepdims=True) for x in dev]
        for cl, x, vr in zip(cols, dev, var):
            y_ref[rows, cl] = (x * lax.rsqrt(vr + ML_NORM_EPS) * g_ref[:, cl]).astype(y_ref.dtype)

    def chunk(c, carry):
        rows = pl.ds(pl.multiple_of(c * lc, lc), lc)
        for h0 in range(0, ML_HEADS, ML_GROUP):
            group(c, rows, range(h0, h0 + ML_GROUP))
        return carry

    lax.fori_loop(0, ML_CHUNKS_PER_STEP, chunk, 0)


def _mlstm(ml, gates_if, l, p):
    s = ml.shape[0]
    nc = s // ML_CHUNK
    i_pre = gates_if[:, :ML_HEADS] + p['ml_gate_b'][l, 0]
    f_pre = gates_if[:, ML_HEADS:] + p['ml_gate_b'][l, 1]
    lf = jax.nn.log_sigmoid(f_pre)
    b_cum = jnp.cumsum(lf.reshape(nc, ML_CHUNK, ML_HEADS), axis=1)
    b_row = jnp.transpose(b_cum, (0, 2, 1))
    i_row = jnp.transpose(i_pre.reshape(nc, ML_CHUNK, ML_HEADS), (0, 2, 1))
    over_lanes = lambda a: jnp.repeat(a.reshape(s, ML_HEADS), ML_HEAD_DIM, axis=1)
    col = lambda j: pl.BlockSpec((ML_ROWS, ML_WIDTH), lambda i: (i, j))
    grow = pl.BlockSpec((ML_CHUNKS_PER_STEP, ML_HEADS, ML_CHUNK), lambda i: (i, 0, 0))
    conv_w = p['ml_conv'][l]
    return pl.pallas_call(
        _mlstm_kernel,
        grid=(s // ML_ROWS,),
        in_specs=[pl.BlockSpec((ML_ROWS, 2 * ML_WIDTH), lambda i: (i, 0)), col(2), col(3),
                  pl.BlockSpec(conv_w.shape, lambda i: (0, 0)), col(0), col(0), grow, grow,
                  pl.BlockSpec((1, ML_WIDTH), lambda i: (0, 0))],
        out_specs=pl.BlockSpec((ML_ROWS, ML_WIDTH), lambda i: (i, 0)),
        out_shape=jax.ShapeDtypeStruct((s, ML_WIDTH), BF16),
        scratch_shapes=[pltpu.VMEM((ML_ROWS, 2 * ML_WIDTH), F32),
                        pltpu.VMEM((ML_ROWS + SUBLANES, 2 * ML_WIDTH), F32),
                        pltpu.VMEM((ML_HEADS, ML_HEAD_DIM, 2 * ML_HEAD_DIM), F32),
                        pltpu.VMEM((ML_HEADS, SUBLANES, ML_HEAD_DIM), F32)],
        compiler_params=_params("arbitrary"),
        name="mlstm",
    )(ml, ml, ml, conv_w, over_lanes(b_cum), over_lanes(i_pre), b_row, i_row,
      p['ml_ln_g'][l].reshape(1, ML_WIDTH))


def _merge_kernel(x_ref, yr_ref, ya_ref, ym_ref, g0_ref, g1_ref, g2_ref, w0_ref, w1_ref, w2_ref, o_ref, gb_ref):
    @pl.when(pl.program_id(1) == 0)
    def _():
        for b, g_ref in enumerate((g0_ref, g1_ref, g2_ref)):
            gb_ref[b] = jnp.transpose(g_ref[0]).astype(BF16)

    x = x_ref[...]
    gate = lambda b: jax.nn.sigmoid(jnp.dot(x, gb_ref[b], preferred_element_type=F32))
    y = (gate(0) * jnp.dot(yr_ref[...], w0_ref[...], preferred_element_type=F32)
         + gate(1) * jnp.dot(ya_ref[...], w1_ref[...], preferred_element_type=F32)
         + gate(2) * jnp.dot(ym_ref[...], w2_ref[...], preferred_element_type=F32))
    o_ref[...] = y.astype(o_ref.dtype)


def _merge(xb, y_rw, y_at, y_ml, w_in_t, l, wb_rw, wb_at, wb_ml, tm=512, tn=512):
    m, d = xb.shape
    nj = d // tn
    act = lambda a: pl.BlockSpec((tm, a.shape[1]), lambda j, i: (i, 0))
    gate = lambda b: pl.BlockSpec((pl.Element(1), pl.Element(tn), pl.Element(d)),
                                  lambda j, i: (l, pl.multiple_of(COL_GATE + b * d + j * tn, SUBLANES), 0))
    wsp = lambda w: pl.BlockSpec((w.shape[0], tn), lambda j, i: (0, j))
    return pl.pallas_call(
        _merge_kernel,
        grid=(nj, m // tm),
        in_specs=[act(xb), act(y_rw), act(y_at), act(y_ml), gate(0), gate(1), gate(2),
                  wsp(wb_rw), wsp(wb_at), wsp(wb_ml)],
        out_specs=pl.BlockSpec((tm, tn), lambda j, i: (i, j)),
        out_shape=jax.ShapeDtypeStruct((m, d), BF16),
        scratch_shapes=[pltpu.VMEM((N_BRANCHES, d, tn), BF16)],
        compiler_params=_params("arbitrary", "arbitrary"),
        name="merge",
    )(xb, y_rw, y_at, y_ml, w_in_t, w_in_t, w_in_t, wb_rw, wb_at, wb_ml)


MOE_LN_ROWS = 256


def _row_gather(idx_ref, base, n, src_hbm, dst, sem, unroll=False, both_queues=False):
    def start(j, priority=0):
        pltpu.make_async_copy(src_hbm.at[pl.ds(idx_ref[base + j], 1)], dst.at[pl.ds(j, 1)], sem).start(
            priority=priority)

    if unroll:
        for j in range(n):
            start(j, j % 2 if both_queues else 0)
    else:
        lax.fori_loop(0, n, lambda j, carry: (start(j), carry)[1], 0, unroll=8)


def _row_gather_wait(n, src_hbm, dst, sem):
    pltpu.make_async_copy(src_hbm.at[pl.ds(0, n)], dst.at[pl.ds(0, n)], sem).wait()


MOE_AHEAD = 2


def _expert_kernel(be_ref, new_ref, used_ref, src_ref, x_hbm, w1_ref, w3_ref, w2_ref, o_ref,
                   xbuf, w1b, w3b, w2b, sem):
    i = pl.program_id(0)
    slots = MOE_AHEAD + 1
    slot = i % slots

    @pl.when(i == 0)
    def _():
        for j in range(MOE_AHEAD):
            _row_gather(src_ref, j * MOE_ROWS, MOE_ROWS, x_hbm, xbuf.at[j], sem.at[j])

    @pl.when(new_ref[i] == 1)
    def _():
        w1b[...] = w1_ref[...].astype(BF16)
        w3b[...] = w3_ref[...].astype(BF16)
        w2b[...] = w2_ref[...].astype(BF16)

    @pl.when(used_ref[i] == 1)
    def _():
        _row_gather_wait(MOE_ROWS, x_hbm, xbuf.at[slot], sem.at[slot])
        ahead = (i + MOE_AHEAD) % slots
        _row_gather(src_ref, (i + MOE_AHEAD) * MOE_ROWS, MOE_ROWS, x_hbm, xbuf.at[ahead], sem.at[ahead], unroll=True)
        x = xbuf[slot].astype(BF16)
        h1 = jnp.dot(x, w1b[...], preferred_element_type=F32)
        h3 = jnp.dot(x, w3b[...], preferred_element_type=F32)
        hid = (h1 * jax.nn.sigmoid(h1) * h3).astype(BF16)
        o_ref[...] = jnp.dot(hid, w2b[...], preferred_element_type=F32)

    @pl.when(used_ref[i] == 0)
    def _():
        @pl.when((i < MOE_AHEAD) | (used_ref[jnp.maximum(i - MOE_AHEAD, 0)] == 1))
        def _():
            _row_gather_wait(MOE_ROWS, x_hbm, xbuf.at[slot], sem.at[slot])

        o_ref[...] = jnp.zeros_like(o_ref)


def _experts(x, src, blk_expert, blk_new, blk_used, w1, w3, w2, l):
    rows = src.shape[0]
    d = x.shape[1]
    nblk = rows // MOE_ROWS
    up = pl.BlockSpec((None, None, d, MOE_FF), lambda i, be, nw, us, sr: (l, be[i], 0, 0))
    down = pl.BlockSpec((None, None, MOE_FF, d), lambda i, be, nw, us, sr: (l, be[i], 0, 0))
    return pl.pallas_call(
        _expert_kernel,
        grid_spec=pltpu.PrefetchScalarGridSpec(
            num_scalar_prefetch=4,
            grid=(nblk,),
            in_specs=[pl.BlockSpec(memory_space=pl.ANY), up, up, down],
            out_specs=pl.BlockSpec((MOE_ROWS, d), lambda i, be, nw, us, sr: (i, 0)),
            scratch_shapes=[pltpu.VMEM((MOE_AHEAD + 1, MOE_ROWS, d), F32),
                            pltpu.VMEM((d, MOE_FF), BF16), pltpu.VMEM((d, MOE_FF), BF16),
                            pltpu.VMEM((MOE_FF, d), BF16),
                            pltpu.SemaphoreType.DMA((MOE_AHEAD + 1,))]),
        out_shape=jax.ShapeDtypeStruct((rows, d), F32),
        compiler_params=_params("arbitrary"),
        name="experts",
    )(blk_expert, blk_new, blk_used, src, x, w1, w3, w2)


def _moe_ln_kernel(pos_ref, x_ref, w_ref, y_hbm, g_ref, b_ref, o_ref, ob_ref, ybuf, sem):
    i = pl.program_id(0)
    nblk = pl.num_programs(0)
    n = MOE_TOP_K * MOE_LN_ROWS
    tm = MOE_LN_ROWS
    slot = i % 2

    @pl.when(i == 0)
    def _():
        _row_gather(pos_ref, 0, n, y_hbm, ybuf.at[0], sem.at[0])

    _row_gather_wait(n, y_hbm, ybuf.at[slot], sem.at[slot])
    _row_gather(pos_ref, (i + 1) * n, n, y_hbm, ybuf.at[1 - slot], sem.at[1 - slot], unroll=True, both_queues=True)
    rows = ybuf.at[slot]
    h = rows[0:tm, :] * w_ref[:, 0:1] + rows[tm:2 * tm, :] * w_ref[:, 1:2]
    out = _layer_norm(DN_ALPHA * x_ref[...] + h, g_ref[...], b_ref[...])
    o_ref[...] = out
    ob_ref[...] = out.astype(BF16)

    @pl.when(i == nblk - 1)
    def _():
        _row_gather_wait(n, y_hbm, ybuf.at[1 - slot], sem.at[1 - slot])


def _moe_ln(x, y, pos, top_w, g, b):
    t, d = x.shape
    tm = MOE_LN_ROWS
    pos = jnp.transpose(pos.reshape(t // tm, tm, MOE_TOP_K), (0, 2, 1)).reshape(t * MOE_TOP_K)
    pos = jnp.concatenate([pos, jnp.zeros((tm * MOE_TOP_K,), jnp.int32)])
    row = pl.BlockSpec((tm, d), lambda i, pos: (i, 0))
    vec = pl.BlockSpec((1, d), lambda i, pos: (0, 0))
    return pl.pallas_call(
        _moe_ln_kernel,
        grid_spec=pltpu.PrefetchScalarGridSpec(
            num_scalar_prefetch=1,
            grid=(t // tm,),
            in_specs=[row, pl.BlockSpec((tm, MOE_TOP_K), lambda i, pos: (i, 0)),
                      pl.BlockSpec(memory_space=pl.ANY), vec, vec],
            out_specs=[row, row],
            scratch_shapes=[pltpu.VMEM((2, MOE_TOP_K * tm, d), F32),
                            pltpu.SemaphoreType.DMA((2,))]),
        out_shape=[jax.ShapeDtypeStruct((t, d), F32), jax.ShapeDtypeStruct((t, d), BF16)],
        compiler_params=_params("arbitrary"),
        name="moe_ln",
    )(pos, x, top_w, y, g.reshape(1, d), b.reshape(1, d))


def _moe(x, l, p, ln_g, ln_b):
    t, d = x.shape
    n_assign = t * MOE_TOP_K
    w_router = jnp.concatenate([p['moe_w_group'][l], p['moe_w_expert'][l]], axis=1)
    w_router = jnp.pad(w_router, ((0, 0), (0, LANES - w_router.shape[1])))
    logits = _mm_split(x, w_router, name="mm_router")
    g_logits = logits[:, :MOE_GROUPS] + p['moe_b_group'][l]
    g_prob = jax.nn.softmax(g_logits, -1)
    g_idx = jnp.argmax(g_logits, -1)
    g_w = jnp.take_along_axis(g_prob, g_idx[:, None], axis=1)
    e_logits = (logits[:, MOE_GROUPS:MOE_GROUPS + MOE_EXPERTS] + p['moe_b_expert'][l]).reshape(
        t, MOE_GROUPS, MOE_EXPERTS_PER_GROUP)
    e_sel = jnp.take_along_axis(e_logits, g_idx[:, None, None], axis=1)[:, 0]
    top_l, top_i = lax.top_k(e_sel, MOE_TOP_K)
    top_w = jax.nn.softmax(top_l, -1) * g_w
    e_flat = (g_idx[:, None] * MOE_EXPERTS_PER_GROUP + top_i).reshape(n_assign).astype(jnp.int32)
    onehot = (e_flat[:, None] == jnp.arange(MOE_EXPERTS, dtype=jnp.int32)[None]).astype(jnp.int32)
    rank = jnp.take_along_axis(jnp.cumsum(onehot, axis=0), e_flat[:, None], axis=1)[:, 0] - 1
    counts = jnp.sum(onehot, axis=0)
    pcounts = (counts + MOE_ROWS - 1) // MOE_ROWS * MOE_ROWS
    pends = jnp.cumsum(pcounts)
    pstarts = pends - pcounts
    pos = pstarts[e_flat] + rank
    n_rows = n_assign + (MOE_EXPERTS + MOE_AHEAD) * MOE_ROWS
    nblk = n_rows // MOE_ROWS
    src = jnp.zeros((n_rows,), jnp.int32).at[pos].set(jnp.arange(n_assign, dtype=jnp.int32) // MOE_TOP_K)
    blk_start = jnp.arange(nblk, dtype=jnp.int32) * MOE_ROWS
    blk_expert = jnp.clip(jnp.searchsorted(pends, blk_start, side='right'), 0, MOE_EXPERTS - 1).astype(jnp.int32)
    blk_used = (blk_start < pends[-1]).astype(jnp.int32)
    blk_new = jnp.concatenate([jnp.ones((1,), jnp.int32), (blk_expert[1:] != blk_expert[:-1]).astype(jnp.int32)])
    y = _experts(x, src, blk_expert, blk_new, blk_used, p['moe_w1'], p['moe_w3'], p['moe_w2'], l)
    return _moe_ln(x, y, pos.astype(jnp.int32).reshape(t, MOE_TOP_K), top_w, ln_g, ln_b)


def _mixer(xf, xb, tables, l, p, ln_g, ln_b):
    w_in_t = p['w_in_t']
    bf = lambda w: w.astype(BF16)
    rw = _mm_t(xb, w_in_t, layer=l, row0=COL_RW, n_rows=COL_AT - COL_RW, tn=1024, name="mm_in_rw")
    at = _mm_t(xb, w_in_t, layer=l, row0=COL_AT, n_rows=COL_ML - COL_AT, tn=AT_WIDTH, name="mm_in_at")
    ml = _mm_t(xb, w_in_t, layer=l, row0=COL_ML, n_rows=COL_IF - COL_ML, tn=1024, name="mm_in_ml")
    gates_if = _mm_t(xb, w_in_t, layer=l, row0=COL_IF, n_rows=COL_GATE - COL_IF, name="mm_in_if")
    y_rw = _rwkv7(rw, l, p)
    y_at = _attention(at, tables)
    y_ml = _mlstm(ml, gates_if, l, p)
    wb = p['w_branch'][l]
    y = _merge(xb, y_rw, y_at, y_ml, w_in_t, l, bf(wb[:RW_WIDTH]),
               bf(wb[RW_WIDTH:RW_WIDTH + AT_OUT_WIDTH]), bf(wb[RW_WIDTH + AT_OUT_WIDTH:]))
    return _out_ln(y, bf(p['w_out'][l]), xf, ln_g, ln_b)


def kernel(x, positions, w_in, rw_mu, rw_w0, rw_w1, rw_w2, rw_a0, rw_a1, rw_a2, rw_g1, rw_g2, rw_kk, rw_ka, rw_rk, rw_gn_g, rw_gn_b, ml_conv, ml_gate_b, ml_ln_g, w_branch, w_out, ln1_g, ln1_b, moe_w_group, moe_b_group, moe_w_expert, moe_b_expert, moe_w1, moe_w3, moe_w2, ln2_g, ln2_b):
    p = dict(w_in_t=jnp.swapaxes(w_in, 1, 2), rw_mu=rw_mu, rw_w0=rw_w0, rw_w1=rw_w1, rw_w2=rw_w2, rw_a0=rw_a0, rw_a1=rw_a1,
             rw_a2=rw_a2, rw_g1=rw_g1, rw_g2=rw_g2, rw_kk=rw_kk, rw_ka=rw_ka, rw_rk=rw_rk, rw_gn_g=rw_gn_g,
             rw_gn_b=rw_gn_b, ml_conv=ml_conv, ml_gate_b=ml_gate_b, ml_ln_g=ml_ln_g, w_branch=w_branch,
             w_out=w_out, moe_w_group=moe_w_group, moe_b_group=moe_b_group, moe_w_expert=moe_w_expert,
             moe_b_expert=moe_b_expert, moe_w1=moe_w1, moe_w3=moe_w3, moe_w2=moe_w2)
    batch, seq, d = x.shape
    assert batch == 1
    xf = x.reshape(seq, d)
    xb = xf.astype(BF16)
    tables = _rope_tables(positions.reshape(seq))
    for l in range(w_in.shape[0]):
        xf, xb = _mixer(xf, xb, tables, l, p, ln1_g[l], ln1_b[l])
        xf, xb = _moe(xf, l, p, ln2_g[l], ln2_b[l])
    return xf.reshape(batch, seq, d)
```

```python
import functools

import jax
import jax.numpy as jnp
from jax import lax
from jax.experimental import pallas as pl
from jax.experimental.pallas import tpu as pltpu

F32 = jnp.float32
BF16 = jnp.bfloat16

D_MODEL = 2048
DEPTH = 4
LN_EPS = 1e-5
DN_ALPHA = (2 * DEPTH) ** 0.25

RW_HEADS = 16
RW_HEAD_DIM = 64
RW_WIDTH = RW_HEADS * RW_HEAD_DIM
RW_GN_EPS = 64e-5

AT_GROUPS = ((128, 1), (512, 4), (2048, 16))
AT_HEADS_PER_GROUP = 6
AT_HEADS = AT_HEADS_PER_GROUP * len(AT_GROUPS)
AT_HEAD_DIM = 64
AT_WIDTH = AT_HEADS * AT_HEAD_DIM
AT_OUT_WIDTH = AT_HEADS_PER_GROUP * AT_HEAD_DIM
AT_BLOCK = 128
ROPE_DIMS = AT_HEAD_DIM // 4
ROPE_THETA = 500000.0

ML_HEADS = 8
ML_HEAD_DIM = 128
ML_WIDTH = ML_HEADS * ML_HEAD_DIM
ML_CHUNK = 64
ML_NORM_EPS = 1e-6

N_BRANCHES = 3
COL_RW = 0
COL_AT = 4 * RW_WIDTH
COL_ML = COL_AT + 3 * AT_WIDTH
COL_IF = COL_ML + 4 * ML_WIDTH
COL_GATE = COL_IF + 2 * ML_HEADS
IN_COLS = COL_GATE + N_BRANCHES * D_MODEL

MOE_GROUPS = 4
MOE_EXPERTS_PER_GROUP = 8
MOE_EXPERTS = MOE_GROUPS * MOE_EXPERTS_PER_GROUP
MOE_TOP_K = 2
MOE_FF = 512
MOE_ROWS = 256

LANES = 128
SUBLANES = 8
VMEM_LIMIT = 56 * 1024 * 1024

NT_DIMS = (((1,), (1,)), ((), ()))


def _params(*sem):
    return pltpu.CompilerParams(dimension_semantics=sem, vmem_limit_bytes=VMEM_LIMIT)


def _bdot(a, b):
    return jnp.dot(a.astype(BF16), b.astype(BF16), preferred_element_type=F32)


def _bdot_nt(a, b):
    return lax.dot_general(a.astype(BF16), b.astype(BF16), NT_DIMS, preferred_element_type=F32)


def _each(f, *xs):
    return [f(*a) for a in zip(*xs)]


def _shift_rows(x, prev_row):
    first = lax.broadcasted_iota(jnp.int32, x.shape, 0) == 0
    return jnp.where(first, prev_row, pltpu.roll(x, 1, axis=0))


def _pair_sum(x, first_head):
    lo = jnp.sum(jnp.where(first_head, x, 0.0), axis=-1, keepdims=True)
    hi = jnp.sum(jnp.where(first_head, 0.0, x), axis=-1, keepdims=True)
    return jnp.where(first_head, lo, hi)


def _mm_split_kernel(a_ref, w_ref, o_ref):
    a, w = a_ref[...], w_ref[...]
    a_hi, w_hi = a.astype(BF16), w.astype(BF16)
    a_lo = (a - a_hi.astype(F32)).astype(BF16)
    w_lo = (w - w_hi.astype(F32)).astype(BF16)
    o_ref[...] = (jnp.dot(a_hi, w_hi, preferred_element_type=F32) + jnp.dot(a_hi, w_lo, preferred_element_type=F32)
                  + jnp.dot(a_lo, w_hi, preferred_element_type=F32))


def _mm_split(a, w, tm=512, name="mm_split"):
    m, k = a.shape
    n = w.shape[1]
    return pl.pallas_call(
        _mm_split_kernel,
        grid=(m // tm,),
        in_specs=[pl.BlockSpec((tm, k), lambda i: (i, 0)), pl.BlockSpec((k, n), lambda i: (0, 0))],
        out_specs=pl.BlockSpec((tm, n), lambda i: (i, 0)),
        out_shape=jax.ShapeDtypeStruct((m, n), F32),
        compiler_params=_params("arbitrary"),
        name=name,
    )(a, w)


def _mm_t_kernel(a_ref, w_ref, o_ref, wb_ref):
    @pl.when(pl.program_id(1) == 0)
    def _():
        wb_ref[...] = jnp.transpose(w_ref[0]).astype(BF16)

    o_ref[...] = jnp.dot(a_ref[...], wb_ref[...], preferred_element_type=F32)


def _mm_t(a, w_t, *, layer, row0, n_rows, tm=1024, tn=None, name="mm_t"):
    m, k = a.shape
    tn = n_rows if tn is None else tn
    tm = min(tm, m)
    assert m % tm == 0 and n_rows % tn == 0 and row0 % SUBLANES == 0
    return pl.pallas_call(
        _mm_t_kernel,
        grid=(n_rows // tn, m // tm),
        in_specs=[pl.BlockSpec((tm, k), lambda j, i: (i, 0)),
                  pl.BlockSpec((pl.Element(1), pl.Element(tn), pl.Element(k)),
                               lambda j, i: (layer, pl.multiple_of(row0 + j * tn, SUBLANES), 0))],
        out_specs=pl.BlockSpec((tm, tn), lambda j, i: (i, j)),
        out_shape=jax.ShapeDtypeStruct((m, n_rows), F32),
        scratch_shapes=[pltpu.VMEM((k, tn), BF16)],
        compiler_params=_params("arbitrary", "arbitrary"),
        name=name,
    )(a, w_t)


def _layer_norm(y, g, b):
    mu = jnp.mean(y, axis=-1, keepdims=True)
    d = y - mu
    var = jnp.mean(d * d, axis=-1, keepdims=True)
    return d * lax.rsqrt(var + LN_EPS) * g + b


def _out_ln_kernel(y_ref, w_ref, x_ref, g_ref, b_ref, o_ref, ob_ref):
    h = jnp.dot(y_ref[...], w_ref[...], preferred_element_type=F32)
    out = _layer_norm(DN_ALPHA * x_ref[...] + h, g_ref[...], b_ref[...])
    o_ref[...] = out
    ob_ref[...] = out.astype(BF16)


def _out_ln(y, w, x, g, b, tm=256):
    m, d = x.shape
    row = pl.BlockSpec((tm, d), lambda i: (i, 0))
    vec = pl.BlockSpec((1, d), lambda i: (0, 0))
    return pl.pallas_call(
        _out_ln_kernel,
        grid=(m // tm,),
        in_specs=[row, pl.BlockSpec((d, d), lambda i: (0, 0)), row, vec, vec],
        out_specs=[row, row],
        out_shape=[jax.ShapeDtypeStruct((m, d), F32), jax.ShapeDtypeStruct((m, d), BF16)],
        compiler_params=_params("arbitrary"),
        name="out_ln",
    )(y, w, x, g.reshape(1, d), b.reshape(1, d))


RW_CHUNK = 64
RW_PAIR = 2 * RW_HEAD_DIM
RW_SUB = 16
RW_CHUNKS_PER_STEP = 4
RW_PREP_ROWS = 256
(RV_MU, RV_W0, RV_A0, RV_KK, RV_KA, RV_GN_G, RV_GN_B, RV_RK, RV_ROWS) = (0, 6, 7, 8, 9, 10, 11, 12, 16)


def _rwkv_prep_kernel(r_ref, k_ref, v_ref, z_ref, rp_ref, kp_ref, vp_ref, zp_ref, vec_ref,
                      w1_ref, w2_ref, a1_ref, a2_ref, g1_ref, g2_ref,
                      ro_ref, lw_ref, ko_ref, vo_ref, kk_ref, b_ref, g_ref):
    first_block = pl.program_id(0) == 0
    vec = lambda j: vec_ref[j:j + 1, :]
    prev = lambda ref: jnp.where(first_block, 0.0, ref[SUBLANES - 1:SUBLANES, :])
    lerp = lambda u, up, j: u + (_shift_rows(u, up) - u) * vec(RV_MU + j)
    r = lerp(r_ref[...], prev(rp_ref), 0)
    k = lerp(k_ref[...], prev(kp_ref), 1)
    v = lerp(v_ref[...], prev(vp_ref), 2)
    z = z_ref[...]
    z_diff = _shift_rows(z, prev(zp_ref)) - z
    xw, xa, xg = z + z_diff * vec(RV_MU + 3), z + z_diff * vec(RV_MU + 4), z + z_diff * vec(RV_MU + 5)
    w_pre = vec(RV_W0) + _bdot(jnp.tanh(_bdot(xw, w1_ref[...])), w2_ref[...])
    softplus = jnp.maximum(-w_pre, 0.0) + jnp.log(1.0 + jnp.exp(-jnp.abs(w_pre)))
    lw_ref[...] = -jnp.exp(-softplus - 0.5)
    a = jax.nn.sigmoid(vec(RV_A0) + _bdot(_bdot(xa, a1_ref[...]), a2_ref[...]))
    g_ref[...] = _bdot(jax.nn.sigmoid(_bdot(xg, g1_ref[...])), g2_ref[...])
    kk = k * vec(RV_KK)
    first_head = lax.broadcasted_iota(jnp.int32, (kk.shape[0], RW_PAIR), 1) < RW_HEAD_DIM
    for p in range(RW_HEADS // 2):
        sl = slice(p * RW_PAIR, (p + 1) * RW_PAIR)
        kk_p = kk[:, sl]
        kk_p = kk_p / jnp.maximum(jnp.sqrt(_pair_sum(kk_p * kk_p, first_head)), 1e-12)
        kk_ref[:, sl] = kk_p
        b_ref[:, sl] = kk_p * a[:, sl]
    ro_ref[...] = r
    ko_ref[...] = k * (1.0 + (a - 1.0) * vec(RV_KA))
    vo_ref[...] = v


def _rwkv_prep(rw, vecs, l, p):
    s = rw.shape[0]
    tm = min(RW_PREP_ROWS, s)
    per = tm // SUBLANES
    cur = lambda j: pl.BlockSpec((tm, RW_WIDTH), lambda i: (i, j))
    prev = lambda j: pl.BlockSpec((SUBLANES, RW_WIDTH), lambda i: (jnp.maximum(i * per - 1, 0), j))
    lora = lambda w: pl.BlockSpec((None,) + w.shape[1:], lambda i: (l, 0, 0))
    ws = [p['rw_w1'], p['rw_w2'], p['rw_a1'], p['rw_a2'], p['rw_g1'], p['rw_g2']]
    out = pl.BlockSpec((tm, RW_WIDTH), lambda i: (i, 0))
    return pl.pallas_call(
        _rwkv_prep_kernel,
        grid=(s // tm,),
        in_specs=[cur(0), cur(1), cur(2), cur(3), prev(0), prev(1), prev(2), prev(3),
                  pl.BlockSpec((RV_ROWS, RW_WIDTH), lambda i: (0, 0))] + [lora(w) for w in ws],
        out_specs=[out] * 7,
        out_shape=[jax.ShapeDtypeStruct((s, RW_WIDTH), F32)] * 7,
        compiler_params=_params("arbitrary"),
        name="rwkv_prep",
    )(rw, rw, rw, rw, rw, rw, rw, rw, vecs, *ws)


def _rwkv_kernel(r_ref, lw_ref, k_ref, v_ref, kk_ref, b_ref, g_ref, vec_ref, o_ref, zt_ref):
    @pl.when(pl.program_id(0) == 0)
    def _():
        zt_ref[...] = jnp.zeros_like(zt_ref)

    c, n2 = RW_CHUNK, RW_PAIR
    tri = (lax.broadcasted_iota(jnp.int32, (c, c), 1) <= lax.broadcasted_iota(jnp.int32, (c, c), 0)).astype(BF16)
    row = lax.broadcasted_iota(jnp.int32, (n2, n2), 0)
    col = lax.broadcasted_iota(jnp.int32, (n2, n2), 1)
    t_idx, s_idx = row % c, col % c
    strict = t_idx > s_idx
    incl = t_idx >= s_idx
    diag_blk = (row // RW_SUB) == (col // RW_SUB)
    eye = (row == col).astype(F32)
    first_head = lax.broadcasted_iota(jnp.int32, (c, n2), 1) < RW_HEAD_DIM

    def embed(x):
        return jnp.concatenate([jnp.where(first_head, x, 0.0), jnp.where(first_head, 0.0, x)], axis=0)

    pairs = range(RW_HEADS // 2)
    sls = [slice(p * n2, (p + 1) * n2) for p in pairs]
    bdot = lambda xs, ys: _each(_bdot, xs, ys)

    lhs, rhs, v_e, kb, gam = [], [], [], [], []
    for ck in range(RW_CHUNKS_PER_STEP):
        rs = slice(ck * c, (ck + 1) * c)
        lw = lw_ref[rs, :]
        lw_hi = lw.astype(BF16)
        rem = lw - lw_hi.astype(F32)
        lw_mid = rem.astype(BF16)
        lw_lo = (rem - lw_mid.astype(F32)).astype(BF16)
        g_in = (jnp.dot(tri, lw_hi, preferred_element_type=F32) + jnp.dot(tri, lw_mid, preferred_element_type=F32)
                + jnp.dot(tri, lw_lo, preferred_element_type=F32))
        g_last = g_in[c - 1:c, :]
        e_neg = jnp.exp(-g_in)
        e_end = jnp.exp(g_last - g_in)
        kkd = kk_ref[rs, :] * jnp.exp(g_in - lw)
        rd = r_ref[rs, :] * jnp.exp(g_in)
        kinv = k_ref[rs, :] * e_neg
        binv = b_ref[rs, :] * e_neg
        kd = k_ref[rs, :] * e_end
        bd = b_ref[rs, :] * e_end
        gam.append(jnp.exp(g_last))
        v_e += [embed(v_ref[rs, sl]) for sl in sls]
        lhs += [jnp.concatenate([embed(kkd[:, sl]), embed(rd[:, sl])], axis=0).astype(BF16) for sl in sls]
        rhs += [jnp.concatenate([embed(kinv[:, sl]), embed(binv[:, sl])], axis=0).astype(BF16) for sl in sls]
        kb += [jnp.concatenate([embed(kd[:, sl]), -embed(bd[:, sl])], axis=0).astype(BF16) for sl in sls]
    v_b = [x.astype(BF16) for x in v_e]
    aa = _each(_bdot_nt, lhs, rhs)
    a_kkk = [jnp.where(strict, x[:n2, :n2], 0.0).astype(BF16) for x in aa]
    a_kkb = [jnp.where(strict, x[:n2, n2:], 0.0) for x in aa]
    a_rk = [jnp.where(incl, x[n2:, :n2], 0.0).astype(BF16) for x in aa]
    a_rb = [jnp.where(incl, x[n2:, n2:], 0.0).astype(BF16) for x in aa]
    nd = [jnp.where(diag_blk, x, 0.0) for x in a_kkb]
    off = [jnp.where(diag_blk, 0.0, x) for x in a_kkb]
    nd2 = bdot(nd, nd)
    akv = bdot(a_kkk, v_b)
    o_v = bdot(a_rk, v_b)
    nd4 = bdot(nd2, nd2)
    p1 = bdot([eye - x for x in nd], [eye + x for x in nd2])
    nd8 = bdot(nd4, nd4)
    p2 = bdot(p1, [eye + x for x in nd4])
    d_inv = bdot(p2, [eye + x for x in nd8])
    e1 = bdot(d_inv, off)
    e2 = bdot(e1, e1)
    t_inv = [x.astype(BF16) for x in bdot(bdot([eye - x for x in e1], [eye + x for x in e2]), d_inv)]

    inv_n = 1.0 / RW_HEAD_DIM
    zt = [zt_ref[p] for p in pairs]
    for ck in range(RW_CHUNKS_PER_STEP):
        rs = slice(ck * c, (ck + 1) * c)
        at = lambda xs: xs[ck * len(sls):(ck + 1) * len(sls)]
        zt_b = [x.astype(BF16) for x in zt]
        x1 = _each(lambda a, z: _bdot_nt(a[:n2], z), at(lhs), zt_b)
        o_z = _each(lambda a, z: _bdot_nt(a[n2:], z), at(lhs), zt_b)
        u = bdot(at(t_inv), [a + b for a, b in zip(x1, at(akv))])
        o_u = bdot(at(a_rb), u)
        vu_t = [jnp.transpose(jnp.concatenate([a, b], axis=0)) for a, b in zip(at(v_e), u)]
        z_up = bdot(vu_t, at(kb))
        zt = [z * gam[ck][:, sl] + zu for z, sl, zu in zip(zt, sls, z_up)]
        for p in pairs:
            sl = sls[p]
            o_e = o_z[p] + at(o_v)[p] - o_u[p]
            out = o_e[:c] + o_e[c:]
            mu = _pair_sum(out, first_head) * inv_n
            dev = out - mu
            var = _pair_sum(dev * dev, first_head) * inv_n
            normed = (dev * lax.rsqrt(var + RW_GN_EPS) * vec_ref[RV_GN_G:RV_GN_G + 1, sl]
                      + vec_ref[RV_GN_B:RV_GN_B + 1, sl])
            bonus = _pair_sum(r_ref[rs, sl] * k_ref[rs, sl] * vec_ref[RV_RK:RV_RK + 1, sl], first_head) * v_ref[rs, sl]
            o_ref[rs, sl] = ((normed + bonus) * g_ref[rs, sl]).astype(o_ref.dtype)
    for p in pairs:
        zt_ref[p] = zt[p]


def _rwkv_scan(r, lw, k, v, kk, b, g, vecs):
    s, width = r.shape
    rows = RW_CHUNK * RW_CHUNKS_PER_STEP
    blk = pl.BlockSpec((rows, width), lambda i: (i, 0))
    return pl.pallas_call(
        _rwkv_kernel,
        grid=(s // rows,),
        in_specs=[blk] * 7 + [pl.BlockSpec((RV_ROWS, width), lambda i: (0, 0))],
        out_specs=blk,
        out_shape=jax.ShapeDtypeStruct((s, width), BF16),
        scratch_shapes=[pltpu.VMEM((RW_HEADS // 2, RW_PAIR, RW_PAIR), F32)],
        compiler_params=_params("arbitrary"),
        name="rwkv_scan",
    )(r, lw, k, v, kk, b, g, vecs)


def _rwkv7(rw, l, p):
    rows = [p['rw_mu'][l], p['rw_w0'][l][None], p['rw_a0'][l][None], p['rw_kk'][l][None], p['rw_ka'][l][None],
            p['rw_gn_g'][l][None], p['rw_gn_b'][l][None], p['rw_rk'][l].reshape(1, RW_WIDTH)]
    vecs = jnp.concatenate(rows + [jnp.zeros((RV_ROWS - RV_RK - 1, RW_WIDTH), F32)], axis=0)
    r, lw, k, v, kk, b, g = _rwkv_prep(rw, vecs, l, p)
    return _rwkv_scan(r, lw, k, v, kk, b, g, vecs)


AT_PREP_ROWS = 512
AT_SLABS = AT_WIDTH // LANES


def _rope_kernel(q_ref, k_ref, c_ref, s1_ref, s2_ref, qo_ref, ko_ref):
    cos, s_dn, s_up = c_ref[...], s1_ref[...], s2_ref[...]

    def rot(x, scale):
        outs = []
        for j in range(AT_SLABS):
            xs = x[:, j * LANES:(j + 1) * LANES]
            y = xs * cos + pltpu.roll(xs, LANES - ROPE_DIMS // 2, axis=1) * s_dn + pltpu.roll(xs, ROPE_DIMS // 2, axis=1) * s_up
            outs.append(y * scale if scale != 1.0 else y)
        return jnp.concatenate(outs, axis=-1)

    qo_ref[...] = rot(q_ref[...], AT_HEAD_DIM ** -0.5)
    ko_ref[...] = rot(k_ref[...], 1.0)


def _rope_tables(positions):
    s = positions.shape[0]
    half = ROPE_DIMS // 2
    inv_freq = ROPE_THETA ** (-jnp.arange(half, dtype=F32) * 2.0 / ROPE_DIMS)
    ang = positions.astype(F32)[:, None] * inv_freq
    cos, sin = jnp.cos(ang), jnp.sin(ang)
    rest = AT_HEAD_DIM - ROPE_DIMS
    zeros, ones = jnp.zeros((s, rest), F32), jnp.ones((s, rest), F32)
    z_half = jnp.zeros((s, half), F32)
    head = lambda parts: jnp.tile(jnp.concatenate(parts, axis=1), (1, LANES // AT_HEAD_DIM))
    return head([cos, cos, ones]), head([-sin, z_half, zeros]), head([z_half, sin, zeros])


def _rope(at, tables):
    s = at.shape[0]
    tm = min(AT_PREP_ROWS, s)
    col = lambda j: pl.BlockSpec((tm, AT_WIDTH), lambda i: (i, j))
    tab = pl.BlockSpec((tm, LANES), lambda i: (i, 0))
    return pl.pallas_call(
        _rope_kernel,
        grid=(s // tm,),
        in_specs=[col(0), col(1), tab, tab, tab],
        out_specs=[col(0)] * 2,
        out_shape=[jax.ShapeDtypeStruct((s, AT_WIDTH), F32)] * 2,
        compiler_params=_params("arbitrary"),
        name="attn_rope",
    )(at, at, *tables)


AT_GROUP_SLABS = AT_OUT_WIDTH // LANES


def _attn_kernel(*refs, dilation):
    ns = AT_GROUP_SLABS
    q_refs, kp_refs, kc_refs, vp_refs, vc_refs = (refs[i * ns:(i + 1) * ns] for i in range(5))
    o_ref, lse_ref, o_s, lse_s = refs[5 * ns:]
    n = pl.program_id(0)
    qi = lax.broadcasted_iota(jnp.int32, (AT_BLOCK, AT_BLOCK), 0)
    kj = lax.broadcasted_iota(jnp.int32, (AT_BLOCK, AT_BLOCK), 1)
    mask_c = kj <= qi
    mask_p = (kj >= qi) & (n > 0)
    per_slab = LANES // AT_HEAD_DIM
    ones_b = jnp.ones((AT_BLOCK, AT_HEAD_DIM), BF16)

    def residue(r, carry):
        rows = pl.ds(r, AT_BLOCK, stride=dilation) if dilation > 1 else slice(None)

        def heads_of(slab_refs):
            slabs = [ref[rows, :].astype(BF16) for ref in slab_refs]
            return [x[:, j * AT_HEAD_DIM:(j + 1) * AT_HEAD_DIM] for x in slabs for j in range(per_slab)]

        q, kc, kp, vc, vp = (heads_of(x) for x in (q_refs, kc_refs, kp_refs, vc_refs, vp_refs))
        s_c = [jnp.where(mask_c, lax.dot_general(a, b, NT_DIMS, preferred_element_type=F32), -jnp.inf)
               for a, b in zip(q, kc)]
        s_p = [jnp.where(mask_p, lax.dot_general(a, b, NT_DIMS, preferred_element_type=F32), -jnp.inf)
               for a, b in zip(q, kp)]
        m = [jnp.maximum(jnp.max(a, axis=-1, keepdims=True), jnp.max(b, axis=-1, keepdims=True))
             for a, b in zip(s_c, s_p)]
        p_c = [jnp.exp(a - mm).astype(BF16) for a, mm in zip(s_c, m)]
        p_p = [jnp.exp(a - mm).astype(BF16) for a, mm in zip(s_p, m)]
        aug = lambda v: jnp.concatenate([v, ones_b], axis=1)
        acc = [jnp.dot(a, aug(v), preferred_element_type=F32) + jnp.dot(b, aug(w), preferred_element_type=F32)
               for a, v, b, w in zip(p_c, vc, p_p, vp)]
        out = [x[:, :AT_HEAD_DIM] / x[:, AT_HEAD_DIM:] for x in acc]
        lse = [mm + jnp.log(x[:, AT_HEAD_DIM:]) for mm, x in zip(m, acc)]
        for t in range(ns):
            o_s[t, rows, :] = jnp.concatenate(out[t * per_slab:(t + 1) * per_slab], axis=-1)
            lse_s[t, rows, :] = jnp.concatenate(lse[t * per_slab:(t + 1) * per_slab], axis=-1)
        return carry

    if dilation > 1:
        lax.fori_loop(0, dilation, residue, 0)
    else:
        residue(0, 0)
    for t in range(ns):
        o_ref[:, t * LANES:(t + 1) * LANES] = o_s[t]
        lse_ref[:, t * LANES:(t + 1) * LANES] = lse_s[t]


def _dilated_attention(q, k, at, g, dilation):
    s = q.shape[0]
    rows = AT_BLOCK * dilation
    assert s % rows == 0
    ns = AT_GROUP_SLABS
    qk_col = g * ns
    v_col = 2 * (AT_WIDTH // LANES) + g * ns
    cur = lambda c: [pl.BlockSpec((rows, LANES), lambda n, t=t: (n, c + t)) for t in range(ns)]
    prev = lambda c: [pl.BlockSpec((rows, LANES), lambda n, t=t: (jnp.maximum(n - 1, 0), c + t)) for t in range(ns)]
    out = pl.BlockSpec((rows, AT_OUT_WIDTH), lambda n: (n, 0))
    return pl.pallas_call(
        functools.partial(_attn_kernel, dilation=dilation),
        grid=(s // rows,),
        in_specs=cur(qk_col) + prev(qk_col) + cur(qk_col) + prev(v_col) + cur(v_col),
        out_specs=[out, out],
        out_shape=[jax.ShapeDtypeStruct((s, AT_OUT_WIDTH), F32)] * 2,
        scratch_shapes=[pltpu.VMEM((ns, rows, LANES), F32)] * 2,
        compiler_params=_params("arbitrary"),
        name=f"attn_d{dilation}",
    )(*([q] * ns + [k] * (2 * ns) + [at] * (2 * ns)))


def _attn_merge_kernel(o0, o1, o2, l0, l1, l2, y_ref):
    m = jnp.maximum(jnp.maximum(l0[...], l1[...]), l2[...])
    e0, e1, e2 = jnp.exp(l0[...] - m), jnp.exp(l1[...] - m), jnp.exp(l2[...] - m)
    tot = e0 + e1 + e2
    y_ref[...] = ((e0 / tot) * o0[...] + (e1 / tot) * o1[...] + (e2 / tot) * o2[...]).astype(BF16)


def _attention(at, tables):
    s = at.shape[0]
    q, k = _rope(at, tables)
    outs, lses = [], []
    for g, (_, dilation) in enumerate(AT_GROUPS):
        o, lse = _dilated_attention(q, k, at, g, dilation)
        outs.append(o)
        lses.append(lse)
    tm = min(AT_PREP_ROWS, s)
    blk = pl.BlockSpec((tm, AT_OUT_WIDTH), lambda i: (i, 0))
    return pl.pallas_call(
        _attn_merge_kernel,
        grid=(s // tm,),
        in_specs=[blk] * 6,
        out_specs=blk,
        out_shape=jax.ShapeDtypeStruct((s, AT_OUT_WIDTH), BF16),
        compiler_params=_params("arbitrary"),
        name="attn_merge",
    )(*outs, *lses)


ML_CHUNKS_PER_STEP = 8
ML_ROWS = ML_CHUNKS_PER_STEP * ML_CHUNK
ML_GROUP = 8


def _mlstm_kernel(u_ref, v_ref, og_ref, cw_ref, bb_ref, ib_ref, brow_ref, irow_ref, g_ref, y_ref,
                  qk_ref, ubuf_ref, ct_ref, m_ref):
    @pl.when(pl.program_id(0) == 0)
    def _():
        ubuf_ref[:SUBLANES, :] = jnp.zeros((SUBLANES, ubuf_ref.shape[1]), F32)
        ct_ref[...] = jnp.zeros_like(ct_ref)
        m_ref[...] = jnp.full_like(m_ref, -jnp.inf)

    taps = cw_ref.shape[0]
    ubuf_ref[SUBLANES:, :] = u_ref[...]
    conv = ubuf_ref[SUBLANES:, :] * cw_ref[taps - 1:taps, :]
    for back in range(1, taps):
        conv = conv + ubuf_ref[pl.ds(SUBLANES - back, ML_ROWS), :] * cw_ref[taps - 1 - back:taps - back, :]
    qk_ref[...] = conv * jax.nn.sigmoid(conv)
    ubuf_ref[:SUBLANES, :] = ubuf_ref[ML_ROWS:, :]

    dh, lc = ML_HEAD_DIM, ML_CHUNK
    si = lax.broadcasted_iota(jnp.int32, (lc, lc), 0)
    ji = lax.broadcasted_iota(jnp.int32, (lc, lc), 1)
    causal = ji <= si
    ones_b = jnp.ones((lc, dh), BF16)

    def group(c, rows, heads):
        cols = [slice(h * dh, (h + 1) * dh) for h in heads]
        q = [qk_ref[rows, cl] * (dh ** -0.5) for cl in cols]
        k = [qk_ref[rows, pl.ds(ML_WIDTH + h * dh, dh)] for h in heads]
        v = [v_ref[rows, cl] for cl in cols]
        bb = [bb_ref[rows, cl] for cl in cols]
        ib = [ib_ref[rows, cl] for cl in cols]
        b_row = [brow_ref[c, h:h + 1, :] for h in heads]
        i_row = [irow_ref[c, h:h + 1, :] for h in heads]
        m_prev = [m_ref[h, 0:1, :] for h in heads]
        ct_prev = [ct_ref[h] for h in heads]
        qb = [x.astype(BF16) for x in q]
        kb = [x.astype(BF16) for x in k]
        s_qk = _each(_bdot_nt, qb, kb)
        inter = _each(_bdot, qb, ct_prev)
        k_t = [jnp.transpose(x).astype(BF16) for x in k]
        log_d = [jnp.where(causal, b[:, :lc] - br + ir, -jnp.inf) for b, br, ir in zip(bb, b_row, i_row)]
        a_log = [b + mp for b, mp in zip(bb, m_prev)]
        m_s = [jnp.maximum(al, jnp.max(ld, axis=-1, keepdims=True)) for al, ld in zip(a_log, log_d)]
        inter_w = [jnp.exp(al - ms) for al, ms in zip(a_log, m_s)]
        qk = [s * jnp.exp(ld - ms[:, :lc]) for s, ld, ms in zip(s_qk, log_d, m_s)]
        intra = [_bdot(a, jnp.concatenate([x.astype(BF16), ones_b], axis=1)) for a, x in zip(qk, v)]
        b_last = [b[lc - 1:lc, :] for b in bb]
        w_end = [bl - b + i for bl, b, i in zip(b_last, bb, ib)]
        m_new = [jnp.maximum(bl + mp, jnp.max(we, axis=0, keepdims=True)) for bl, mp, we in zip(b_last, m_prev, w_end)]
        dec = [jnp.exp(bl + mp - mn) for bl, mp, mn in zip(b_last, m_prev, m_new)]
        wts = [jnp.exp(we - mn) for we, mn in zip(w_end, m_new)]
        upd = [_bdot(kt, jnp.concatenate([x * w, w], axis=1)) for kt, x, w in zip(k_t, v, wts)]
        for h, ctp, d, up, mn in zip(heads, ct_prev, dec, upd, m_new):
            ct_ref[h] = jnp.concatenate([d, d], axis=1) * ctp + up
            m_ref[h, 0:1, :] = mn
        num = [w * a[:, :dh] + b[:, :dh] for w, a, b in zip(inter_w, inter, intra)]
        den = [w * a[:, dh:] + b[:, dh:] for w, a, b in zip(inter_w, inter, intra)]
        hid = [n / jnp.maximum(jnp.abs(d), jnp.exp(-ms)) for n, d, ms in zip(num, den, m_s)]
        hid = [x * jax.nn.sigmoid(og_ref[rows, cl]) for x, cl in zip(hid, cols)]
        mu = [jnp.mean(x, axis=-1, keepdims=True) for x in hid]
        dev = [x - m for x, m in zip(hid, mu)]
        var = [jnp.mean(x * x, axis=-1, keepdims=True) for x in dev]
        for cl, x, vr in zip(cols, dev, var):
            y_ref[rows, cl] = (x * lax.rsqrt(vr + ML_NORM_EPS) * g_ref[:, cl]).astype(y_ref.dtype)

    def chunk(c, carry):
        rows = pl.ds(pl.multiple_of(c * lc, lc), lc)
        for h0 in range(0, ML_HEADS, ML_GROUP):
            group(c, rows, range(h0, h0 + ML_GROUP))
        return carry

    lax.fori_loop(0, ML_CHUNKS_PER_STEP, chunk, 0)


def _mlstm(ml, gates_if, l, p):
    s = ml.shape[0]
    nc = s // ML_CHUNK
    i_pre = gates_if[:, :ML_HEADS] + p['ml_gate_b'][l, 0]
    f_pre = gates_if[:, ML_HEADS:] + p['ml_gate_b'][l, 1]
    lf = jax.nn.log_sigmoid(f_pre)
    b_cum = jnp.cumsum(lf.reshape(nc, ML_CHUNK, ML_HEADS), axis=1)
    b_row = jnp.transpose(b_cum, (0, 2, 1))
    i_row = jnp.transpose(i_pre.reshape(nc, ML_CHUNK, ML_HEADS), (0, 2, 1))
    over_lanes = lambda a: jnp.repeat(a.reshape(s, ML_HEADS), ML_HEAD_DIM, axis=1)
    col = lambda j: pl.BlockSpec((ML_ROWS, ML_WIDTH), lambda i: (i, j))
    grow = pl.BlockSpec((ML_CHUNKS_PER_STEP, ML_HEADS, ML_CHUNK), lambda i: (i, 0, 0))
    conv_w = p['ml_conv'][l]
    return pl.pallas_call(
        _mlstm_kernel,
        grid=(s // ML_ROWS,),
        in_specs=[pl.BlockSpec((ML_ROWS, 2 * ML_WIDTH), lambda i: (i, 0)), col(2), col(3),
                  pl.BlockSpec(conv_w.shape, lambda i: (0, 0)), col(0), col(0), grow, grow,
                  pl.BlockSpec((1, ML_WIDTH), lambda i: (0, 0))],
        out_specs=pl.BlockSpec((ML_ROWS, ML_WIDTH), lambda i: (i, 0)),
        out_shape=jax.ShapeDtypeStruct((s, ML_WIDTH), BF16),
        scratch_shapes=[pltpu.VMEM((ML_ROWS, 2 * ML_WIDTH), F32),
                        pltpu.VMEM((ML_ROWS + SUBLANES, 2 * ML_WIDTH), F32),
                        pltpu.VMEM((ML_HEADS, ML_HEAD_DIM, 2 * ML_HEAD_DIM), F32),
                        pltpu.VMEM((ML_HEADS, SUBLANES, ML_HEAD_DIM), F32)],
        compiler_params=_params("arbitrary"),
        name="mlstm",
    )(ml, ml, ml, conv_w, over_lanes(b_cum), over_lanes(i_pre), b_row, i_row,
      p['ml_ln_g'][l].reshape(1, ML_WIDTH))


def _merge_kernel(x_ref, yr_ref, ya_ref, ym_ref, g0_ref, g1_ref, g2_ref, w0_ref, w1_ref, w2_ref, o_ref, gb_ref):
    @pl.when(pl.program_id(1) == 0)
    def _():
        for b, g_ref in enumerate((g0_ref, g1_ref, g2_ref)):
            gb_ref[b] = jnp.transpose(g_ref[0]).astype(BF16)

    x = x_ref[...]
    gate = lambda b: jax.nn.sigmoid(jnp.dot(x, gb_ref[b], preferred_element_type=F32))
    y = (gate(0) * jnp.dot(yr_ref[...], w0_ref[...], preferred_element_type=F32)
         + gate(1) * jnp.dot(ya_ref[...], w1_ref[...], preferred_element_type=F32)
         + gate(2) * jnp.dot(ym_ref[...], w2_ref[...], preferred_element_type=F32))
    o_ref[...] = y.astype(o_ref.dtype)


def _merge(xb, y_rw, y_at, y_ml, w_in_t, l, wb_rw, wb_at, wb_ml, tm=512, tn=512):
    m, d = xb.shape
    nj = d // tn
    act = lambda a: pl.BlockSpec((tm, a.shape[1]), lambda j, i: (i, 0))
    gate = lambda b: pl.BlockSpec((pl.Element(1), pl.Element(tn), pl.Element(d)),
                                  lambda j, i: (l, pl.multiple_of(COL_GATE + b * d + j * tn, SUBLANES), 0))
    wsp = lambda w: pl.BlockSpec((w.shape[0], tn), lambda j, i: (0, j))
    return pl.pallas_call(
        _merge_kernel,
        grid=(nj, m // tm),
        in_specs=[act(xb), act(y_rw), act(y_at), act(y_ml), gate(0), gate(1), gate(2),
                  wsp(wb_rw), wsp(wb_at), wsp(wb_ml)],
        out_specs=pl.BlockSpec((tm, tn), lambda j, i: (i, j)),
        out_shape=jax.ShapeDtypeStruct((m, d), BF16),
        scratch_shapes=[pltpu.VMEM((N_BRANCHES, d, tn), BF16)],
        compiler_params=_params("arbitrary", "arbitrary"),
        name="merge",
    )(xb, y_rw, y_at, y_ml, w_in_t, w_in_t, w_in_t, wb_rw, wb_at, wb_ml)


MOE_LN_ROWS = 256


def _row_gather(idx_ref, base, n, src_hbm, dst, sem, unroll=False):
    def start(j):
        pltpu.make_async_copy(src_hbm.at[pl.ds(idx_ref[base + j], 1)], dst.at[pl.ds(j, 1)], sem).start()

    if unroll:
        for j in range(n):
            start(j)
    else:
        lax.fori_loop(0, n, lambda j, carry: (start(j), carry)[1], 0, unroll=8)


def _row_gather_wait(n, src_hbm, dst, sem):
    pltpu.make_async_copy(src_hbm.at[pl.ds(0, n)], dst.at[pl.ds(0, n)], sem).wait()


MOE_AHEAD = 2


def _expert_kernel(be_ref, new_ref, used_ref, src_ref, x_hbm, w1_ref, w3_ref, w2_ref, o_ref,
                   xbuf, w1b, w3b, w2b, sem):
    i = pl.program_id(0)
    slots = MOE_AHEAD + 1
    slot = i % slots

    @pl.when(i == 0)
    def _():
        for j in range(MOE_AHEAD):
            _row_gather(src_ref, j * MOE_ROWS, MOE_ROWS, x_hbm, xbuf.at[j], sem.at[j])

    @pl.when(new_ref[i] == 1)
    def _():
        w1b[...] = w1_ref[...].astype(BF16)
        w3b[...] = w3_ref[...].astype(BF16)
        w2b[...] = w2_ref[...].astype(BF16)

    @pl.when(used_ref[i] == 1)
    def _():
        _row_gather_wait(MOE_ROWS, x_hbm, xbuf.at[slot], sem.at[slot])
        ahead = (i + MOE_AHEAD) % slots
        _row_gather(src_ref, (i + MOE_AHEAD) * MOE_ROWS, MOE_ROWS, x_hbm, xbuf.at[ahead], sem.at[ahead], unroll=True)
        x = xbuf[slot].astype(BF16)
        h1 = jnp.dot(x, w1b[...], preferred_element_type=F32)
        h3 = jnp.dot(x, w3b[...], preferred_element_type=F32)
        hid = (h1 * jax.nn.sigmoid(h1) * h3).astype(BF16)
        o_ref[...] = jnp.dot(hid, w2b[...], preferred_element_type=F32)

    @pl.when(used_ref[i] == 0)
    def _():
        @pl.when((i < MOE_AHEAD) | (used_ref[jnp.maximum(i - MOE_AHEAD, 0)] == 1))
        def _():
            _row_gather_wait(MOE_ROWS, x_hbm, xbuf.at[slot], sem.at[slot])

        o_ref[...] = jnp.zeros_like(o_ref)


def _experts(x, src, blk_expert, blk_new, blk_used, w1, w3, w2, l):
    rows = src.shape[0]
    d = x.shape[1]
    nblk = rows // MOE_ROWS
    up = pl.BlockSpec((None, None, d, MOE_FF), lambda i, be, nw, us, sr: (l, be[i], 0, 0))
    down = pl.BlockSpec((None, None, MOE_FF, d), lambda i, be, nw, us, sr: (l, be[i], 0, 0))
    return pl.pallas_call(
        _expert_kernel,
        grid_spec=pltpu.PrefetchScalarGridSpec(
            num_scalar_prefetch=4,
            grid=(nblk,),
            in_specs=[pl.BlockSpec(memory_space=pl.ANY), up, up, down],
            out_specs=pl.BlockSpec((MOE_ROWS, d), lambda i, be, nw, us, sr: (i, 0)),
            scratch_shapes=[pltpu.VMEM((MOE_AHEAD + 1, MOE_ROWS, d), F32),
                            pltpu.VMEM((d, MOE_FF), BF16), pltpu.VMEM((d, MOE_FF), BF16),
                            pltpu.VMEM((MOE_FF, d), BF16),
                            pltpu.SemaphoreType.DMA((MOE_AHEAD + 1,))]),
        out_shape=jax.ShapeDtypeStruct((rows, d), F32),
        compiler_params=_params("arbitrary"),
        name="experts",
    )(blk_expert, blk_new, blk_used, src, x, w1, w3, w2)


def _moe_ln_kernel(pos_ref, x_ref, w_ref, y_hbm, g_ref, b_ref, o_ref, ob_ref, ybuf, sem):
    i = pl.program_id(0)
    nblk = pl.num_programs(0)
    n = MOE_TOP_K * MOE_LN_ROWS
    tm = MOE_LN_ROWS
    slot = i % 2

    @pl.when(i == 0)
    def _():
        _row_gather(pos_ref, 0, n, y_hbm, ybuf.at[0], sem.at[0])

    _row_gather_wait(n, y_hbm, ybuf.at[slot], sem.at[slot])
    _row_gather(pos_ref, (i + 1) * n, n, y_hbm, ybuf.at[1 - slot], sem.at[1 - slot], unroll=True)
    rows = ybuf.at[slot]
    h = rows[0:tm, :] * w_ref[:, 0:1] + rows[tm:2 * tm, :] * w_ref[:, 1:2]
    out = _layer_norm(DN_ALPHA * x_ref[...] + h, g_ref[...], b_ref[...])
    o_ref[...] = out
    ob_ref[...] = out.astype(BF16)

    @pl.when(i == nblk - 1)
    def _():
        _row_gather_wait(n, y_hbm, ybuf.at[1 - slot], sem.at[1 - slot])


def _moe_ln(x, y, pos, top_w, g, b):
    t, d = x.shape
    tm = MOE_LN_ROWS
    pos = jnp.transpose(pos.reshape(t // tm, tm, MOE_TOP_K), (0, 2, 1)).reshape(t * MOE_TOP_K)
    pos = jnp.concatenate([pos, jnp.zeros((tm * MOE_TOP_K,), jnp.int32)])
    row = pl.BlockSpec((tm, d), lambda i, pos: (i, 0))
    vec = pl.BlockSpec((1, d), lambda i, pos: (0, 0))
    return pl.pallas_call(
        _moe_ln_kernel,
        grid_spec=pltpu.PrefetchScalarGridSpec(
            num_scalar_prefetch=1,
            grid=(t // tm,),
            in_specs=[row, pl.BlockSpec((tm, MOE_TOP_K), lambda i, pos: (i, 0)),
                      pl.BlockSpec(memory_space=pl.ANY), vec, vec],
            out_specs=[row, row],
            scratch_shapes=[pltpu.VMEM((2, MOE_TOP_K * tm, d), F32),
                            pltpu.SemaphoreType.DMA((2,))]),
        out_shape=[jax.ShapeDtypeStruct((t, d), F32), jax.ShapeDtypeStruct((t, d), BF16)],
        compiler_params=_params("arbitrary"),
        name="moe_ln",
    )(pos, x, top_w, y, g.reshape(1, d), b.reshape(1, d))


def _moe(x, l, p, ln_g, ln_b):
    t, d = x.shape
    n_assign = t * MOE_TOP_K
    w_router = jnp.concatenate([p['moe_w_group'][l], p['moe_w_expert'][l]], axis=1)
    w_router = jnp.pad(w_router, ((0, 0), (0, LANES - w_router.shape[1])))
    logits = _mm_split(x, w_router, name="mm_router")
    g_logits = logits[:, :MOE_GROUPS] + p['moe_b_group'][l]
    g_prob = jax.nn.softmax(g_logits, -1)
    g_idx = jnp.argmax(g_logits, -1)
    g_w = jnp.take_along_axis(g_prob, g_idx[:, None], axis=1)
    e_logits = (logits[:, MOE_GROUPS:MOE_GROUPS + MOE_EXPERTS] + p['moe_b_expert'][l]).reshape(
        t, MOE_GROUPS, MOE_EXPERTS_PER_GROUP)
    e_sel = jnp.take_along_axis(e_logits, g_idx[:, None, None], axis=1)[:, 0]
    top_l, top_i = lax.top_k(e_sel, MOE_TOP_K)
    top_w = jax.nn.softmax(top_l, -1) * g_w
    e_flat = (g_idx[:, None] * MOE_EXPERTS_PER_GROUP + top_i).reshape(n_assign).astype(jnp.int32)
    onehot = (e_flat[:, None] == jnp.arange(MOE_EXPERTS, dtype=jnp.int32)[None]).astype(jnp.int32)
    rank = jnp.take_along_axis(jnp.cumsum(onehot, axis=0), e_flat[:, None], axis=1)[:, 0] - 1
    counts = jnp.sum(onehot, axis=0)
    pcounts = (counts + MOE_ROWS - 1) // MOE_ROWS * MOE_ROWS
    pends = jnp.cumsum(pcounts)
    pstarts = pends - pcounts
    pos = pstarts[e_flat] + rank
    n_rows = n_assign + (MOE_EXPERTS + MOE_AHEAD) * MOE_ROWS
    nblk = n_rows // MOE_ROWS
    src = jnp.zeros((n_rows,), jnp.int32).at[pos].set(jnp.arange(n_assign, dtype=jnp.int32) // MOE_TOP_K)
    blk_start = jnp.arange(nblk, dtype=jnp.int32) * MOE_ROWS
    blk_expert = jnp.clip(jnp.searchsorted(pends, blk_start, side='right'), 0, MOE_EXPERTS - 1).astype(jnp.int32)
    blk_used = (blk_start < pends[-1]).astype(jnp.int32)
    blk_new = jnp.concatenate([jnp.ones((1,), jnp.int32), (blk_expert[1:] != blk_expert[:-1]).astype(jnp.int32)])
    y = _experts(x, src, blk_expert, blk_new, blk_used, p['moe_w1'], p['moe_w3'], p['moe_w2'], l)
    return _moe_ln(x, y, pos.astype(jnp.int32).reshape(t, MOE_TOP_K), top_w, ln_g, ln_b)


def _mixer(xf, xb, tables, l, p, ln_g, ln_b):
    w_in_t = p['w_in_t']
    bf = lambda w: w.astype(BF16)
    rw = _mm_t(xb, w_in_t, layer=l, row0=COL_RW, n_rows=COL_AT - COL_RW, tn=1024, name="mm_in_rw")
    at = _mm_t(xb, w_in_t, layer=l, row0=COL_AT, n_rows=COL_ML - COL_AT, tn=AT_WIDTH, name="mm_in_at")
    ml = _mm_t(xb, w_in_t, layer=l, row0=COL_ML, n_rows=COL_IF - COL_ML, tn=1024, name="mm_in_ml")
    gates_if = _mm_t(xb, w_in_t, layer=l, row0=COL_IF, n_rows=COL_GATE - COL_IF, name="mm_in_if")
    y_rw = _rwkv7(rw, l, p)
    y_at = _attention(at, tables)
    y_ml = _mlstm(ml, gates_if, l, p)
    wb = p['w_branch'][l]
    y = _merge(xb, y_rw, y_at, y_ml, w_in_t, l, bf(wb[:RW_WIDTH]),
               bf(wb[RW_WIDTH:RW_WIDTH + AT_OUT_WIDTH]), bf(wb[RW_WIDTH + AT_OUT_WIDTH:]))
    return _out_ln(y, bf(p['w_out'][l]), xf, ln_g, ln_b)


def kernel(x, positions, w_in, rw_mu, rw_w0, rw_w1, rw_w2, rw_a0, rw_a1, rw_a2, rw_g1, rw_g2, rw_kk, rw_ka, rw_rk, rw_gn_g, rw_gn_b, ml_conv, ml_gate_b, ml_ln_g, w_branch, w_out, ln1_g, ln1_b, moe_w_group, moe_b_group, moe_w_expert, moe_b_expert, moe_w1, moe_w3, moe_w2, ln2_g, ln2_b):
    p = dict(w_in_t=jnp.swapaxes(w_in, 1, 2), rw_mu=rw_mu, rw_w0=rw_w0, rw_w1=rw_w1, rw_w2=rw_w2, rw_a0=rw_a0, rw_a1=rw_a1,
             rw_a2=rw_a2, rw_g1=rw_g1, rw_g2=rw_g2, rw_kk=rw_kk, rw_ka=rw_ka, rw_rk=rw_rk, rw_gn_g=rw_gn_g,
             rw_gn_b=rw_gn_b, ml_conv=ml_conv, ml_gate_b=ml_gate_b, ml_ln_g=ml_ln_g, w_branch=w_branch,
             w_out=w_out, moe_w_group=moe_w_group, moe_b_group=moe_b_group, moe_w_expert=moe_w_expert,
             moe_b_expert=moe_b_expert, moe_w1=moe_w1, moe_w3=moe_w3, moe_w2=moe_w2)
    batch, seq, d = x.shape
    assert batch == 1
    xf = x.reshape(seq, d)
    xb = xf.astype(BF16)
    tables = _rope_tables(positions.reshape(seq))
    for l in range(w_in.shape[0]):
        xf, xb = _mixer(xf, xb, tables, l, p, ln1_g[l], ln1_b[l])
        xf, xb = _moe(xf, l, p, ln2_g[l], ln2_b[l])
    return xf.reshape(batch, seq, d)
```
